```python
import math
import jax, jax.numpy as jnp
from jax import lax
import numpy as np

D_MODEL = 2048
BATCH = 8
SEQ = 8192
DEPTH = 1

N_HEADS = 32
HEAD_DIM = 64
N_KV_HEADS = 4
GROUP = N_HEADS // N_KV_HEADS
D_ATTN = N_HEADS * HEAD_DIM
D_KV = N_KV_HEADS * HEAD_DIM
WINDOW = 128
BLOCK = 128
NUM_BUCKETS = 32
MAX_DISTANCE = 128
D_RNN = 2560
N_RNN_BLOCKS = 20
RNN_BLOCK = D_RNN // N_RNN_BLOCKS
RNN_CONV = 4
RG_C = 8.0
D_FF = 3 * D_MODEL
FFN_CONV = 3
IN_SPLITS = (D_ATTN, D_KV, D_KV, D_RNN, D_RNN, D_MODEL, D_MODEL)
D_IN = D_ATTN + 2 * D_KV + 2 * D_RNN + 2 * D_MODEL
EPS = 1e-6
NEG_INF = -1e30

kernel_name = "hybrid_rglru_swa_sink_convffn_adaln"


def rmsnorm(x, g):
    xf = x.astype(jnp.float32)
    y = xf * lax.rsqrt(jnp.mean(xf * xf, axis=-1, keepdims=True) + EPS)
    return (y * g.astype(jnp.float32)).astype(x.dtype)


def causal_dwconv(x, w, b):
    k = w.shape[0]
    y = lax.conv_general_dilated(
        x, w[:, None, :].astype(x.dtype), window_strides=(1,), padding=[(k - 1, 0)],
        dimension_numbers=("NWC", "WIO", "NWC"), feature_group_count=x.shape[-1])
    return y + b


def t5_bucket(dist):
    max_exact = NUM_BUCKETS // 2
    d = jnp.maximum(dist, 1).astype(jnp.float32)
    large = max_exact + (jnp.log(d / max_exact) / math.log(MAX_DISTANCE / max_exact)
                         * (NUM_BUCKETS - max_exact)).astype(jnp.int32)
    large = jnp.minimum(large, NUM_BUCKETS - 1)
    return jnp.where(dist < max_exact, dist, large)


def band_bias_and_mask(rel_bias, n_blocks):
    qi = jnp.arange(BLOCK)[:, None]
    kj = jnp.arange(2 * BLOCK)[None, :]
    dist = qi + BLOCK - kj
    bucket = t5_bucket(jnp.maximum(dist, 0))
    bias = jnp.transpose(rel_bias[bucket], (2, 0, 1)).astype(jnp.float32)
    in_window = (dist >= 0) & (dist < WINDOW)
    key_exists = (jnp.arange(n_blocks)[:, None, None] > 0) | (kj >= BLOCK)[None]
    mask = in_window[None] & key_exists
    return bias.reshape(N_KV_HEADS, GROUP, BLOCK, 2 * BLOCK), mask


def swa_attention(q, k, v, sinks, bias, mask):
    b, s, _ = q.shape
    nb = s // BLOCK
    qb = q.reshape(b, nb, BLOCK, N_KV_HEADS, GROUP, HEAD_DIM)
    kb = k.reshape(b, nb, BLOCK, N_KV_HEADS, HEAD_DIM)
    vb = v.reshape(b, nb, BLOCK, N_KV_HEADS, HEAD_DIM)
    pad = ((0, 0), (1, 0), (0, 0), (0, 0), (0, 0))
    k_band = jnp.concatenate([jnp.pad(kb, pad)[:, :-1], kb], axis=2)
    v_band = jnp.concatenate([jnp.pad(vb, pad)[:, :-1], vb], axis=2)
    scores = jnp.einsum("bnqhgd,bnkhd->bnhgqk", qb, k_band,
                        preferred_element_type=jnp.float32) * (HEAD_DIM ** -0.5)
    scores = jnp.where(mask[None, :, None, None], scores + bias[None, None], NEG_INF)
    sink = sinks.astype(jnp.float32).reshape(N_KV_HEADS, GROUP)[None, None, :, :, None, None]
    m = jnp.maximum(jnp.max(scores, axis=-1, keepdims=True), sink)
    p = jnp.exp(scores - m)
    probs = p / (jnp.sum(p, axis=-1, keepdims=True) + jnp.exp(sink - m))
    out = jnp.einsum("bnhgqk,bnkhd->bnqhgd", probs.astype(v.dtype), v_band)
    return out.reshape(b, s, D_ATTN)


def _lin_combine(e1, e2):
    a1, b1 = e1
    a2, b2 = e2
    return (a1 * a2, a2 * b1 + b2)


def rg_lru(x, w_a, b_a, w_i, b_i, lam):
    b, s, _ = x.shape
    xb = x.reshape(b, s, N_RNN_BLOCKS, RNN_BLOCK)
    r = jax.nn.sigmoid((jnp.einsum("bsni,nij->bsnj", xb, w_a).reshape(b, s, D_RNN) + b_a).astype(jnp.float32))
    i = jax.nn.sigmoid((jnp.einsum("bsni,nij->bsnj", xb, w_i).reshape(b, s, D_RNN) + b_i).astype(jnp.float32))
    log_a = -RG_C * r * jax.nn.softplus(-lam.astype(jnp.float32))
    a = jnp.exp(log_a)
    mult = jnp.sqrt(-jnp.expm1(2.0 * log_a))
    mult = jnp.where(jnp.arange(s)[None, :, None] == 0, 1.0, mult)
    inp = mult * i * x.astype(jnp.float32)
    _, h = lax.associative_scan(_lin_combine, (a, inp), axis=1)
    return h.astype(x.dtype)


def _fwd_setup_inputs(seed: int = 0) -> dict:
    key = jax.random.key(seed)
    ks = jax.random.split(key, 26)
    L = DEPTH

    def nrm(k, shape, scale):
        return jax.random.normal(k, shape, jnp.float32) * scale

    a_init = jax.random.uniform(ks[14], (L, D_RNN), jnp.float32, 0.9, 0.999) ** (1.0 / RG_C)
    return {
        "x": nrm(ks[0], (BATCH, SEQ, D_MODEL), 1.0),
        "c": nrm(ks[1], (BATCH, D_MODEL), 1.0),
        "w_ada": nrm(ks[2], (L, D_MODEL, 6 * D_MODEL), D_MODEL ** -0.5),
        "b_ada": nrm(ks[3], (L, 6 * D_MODEL), 0.01),
        "norm1": 1.0 + nrm(ks[4], (L, D_MODEL), 0.02),
        "w_in": nrm(ks[5], (L, D_MODEL, D_IN), D_MODEL ** -0.5),
        "rnn_conv_w": nrm(ks[6], (L, RNN_CONV, D_RNN), RNN_CONV ** -0.5),
        "rnn_conv_b": nrm(ks[7], (L, D_RNN), 0.01),
        "w_rg_a": nrm(ks[8], (L, N_RNN_BLOCKS, RNN_BLOCK, RNN_BLOCK), RNN_BLOCK ** -0.5),
        "b_rg_a": nrm(ks[9], (L, D_RNN), 0.1),
        "w_rg_i": nrm(ks[10], (L, N_RNN_BLOCKS, RNN_BLOCK, RNN_BLOCK), RNN_BLOCK ** -0.5),
        "b_rg_i": nrm(ks[11], (L, D_RNN), 0.1),
        "rg_lambda": jnp.log(a_init) - jnp.log1p(-a_init),
        "w_o_rnn": nrm(ks[12], (L, D_RNN, D_MODEL), D_RNN ** -0.5),
        "w_o_attn": nrm(ks[13], (L, D_ATTN, D_MODEL), D_ATTN ** -0.5),
        "attn_sinks": nrm(ks[15], (L, N_HEADS), 0.5),
        "rel_bias": nrm(ks[16], (NUM_BUCKETS, N_HEADS), 0.2),
        "w_out": nrm(ks[17], (L, D_MODEL, D_MODEL), D_MODEL ** -0.5),
        "norm2": 1.0 + nrm(ks[18], (L, D_MODEL), 0.02),
        "w_up": nrm(ks[19], (L, D_MODEL, 2 * D_FF), D_MODEL ** -0.5),
        "ffn_conv_w": nrm(ks[20], (L, FFN_CONV, 2 * D_FF), FFN_CONV ** -0.5),
        "ffn_conv_b": nrm(ks[21], (L, 2 * D_FF), 0.01),
        "w_down": nrm(ks[22], (L, D_FF, D_MODEL), D_FF ** -0.5),
        "norm_f": 1.0 + nrm(ks[23], (D_MODEL,), 0.02),
    }


def _fwd_reference(x, c, w_ada, b_ada, norm1, w_in, rnn_conv_w, rnn_conv_b, w_rg_a, b_rg_a,
              w_rg_i, b_rg_i, rg_lambda, w_o_rnn, w_o_attn, attn_sinks, rel_bias, w_out,
              norm2, w_up, ffn_conv_w, ffn_conv_b, w_down, norm_f):
    n_blocks = x.shape[1] // BLOCK
    bias, mask = band_bias_and_mask(rel_bias, n_blocks)
    split_idx = np.cumsum(IN_SPLITS)[:-1].tolist()
    h = x
    for l in range(DEPTH):
        mod = jax.nn.silu(c) @ w_ada[l] + b_ada[l]
        shift1, scale1, gate1, shift2, scale2, gate2 = [m[:, None, :] for m in jnp.split(mod, 6, axis=-1)]

        u = rmsnorm(h, norm1[l]) * (1.0 + scale1) + shift1
        proj = u @ w_in[l]
        q, k, v, xr, gr, ga_logit, gr_logit = jnp.split(proj, split_idx, axis=-1)
        y_attn = swa_attention(q, k, v, attn_sinks[l], bias, mask) @ w_o_attn[l]
        xr = causal_dwconv(xr, rnn_conv_w[l], rnn_conv_b[l])
        hr = rg_lru(xr, w_rg_a[l], b_rg_a[l], w_rg_i[l], b_rg_i[l], rg_lambda[l])
        y_rnn = (hr * jax.nn.gelu(gr)) @ w_o_rnn[l]
        merged = jax.nn.sigmoid(ga_logit) * y_attn + jax.nn.sigmoid(gr_logit) * y_rnn
        h = h + gate1 * (merged @ w_out[l])

        u2 = rmsnorm(h, norm2[l]) * (1.0 + scale2) + shift2
        up = causal_dwconv(u2 @ w_up[l], ffn_conv_w[l], ffn_conv_b[l])
        g, val = jnp.split(up, 2, axis=-1)
        h = h + gate2 * ((jax.nn.gelu(g) * val) @ w_down[l])
    return rmsnorm(h, norm_f)


import jax as _jax
import jax.numpy as _jnp

TWIN_FORMAT = 'train_step'
FWD_PARAMS = ['x', 'c', 'w_ada', 'b_ada', 'norm1', 'w_in', 'rnn_conv_w', 'rnn_conv_b', 'w_rg_a', 'b_rg_a', 'w_rg_i', 'b_rg_i', 'rg_lambda', 'w_o_rnn', 'w_o_attn', 'attn_sinks', 'rel_bias', 'w_out', 'norm2', 'w_up', 'ffn_conv_w', 'ffn_conv_b', 'w_down', 'norm_f']
TWIN_WEIGHTS = ['w_ada', 'b_ada', 'norm1', 'w_in', 'rnn_conv_w', 'rnn_conv_b', 'w_rg_a', 'b_rg_a', 'w_rg_i', 'b_rg_i', 'rg_lambda', 'w_o_rnn', 'w_o_attn', 'attn_sinks', 'rel_bias', 'w_out', 'norm2', 'w_up', 'ffn_conv_w', 'ffn_conv_b', 'w_down', 'norm_f']
TWIN_DIFF_INPUT = 'x'
TWIN_INPUTS = ['x', 'c', 'w_ada', 'b_ada', 'norm1', 'w_in', 'rnn_conv_w', 'rnn_conv_b', 'w_rg_a', 'b_rg_a', 'w_rg_i', 'b_rg_i', 'rg_lambda', 'w_o_rnn', 'w_o_attn', 'attn_sinks', 'rel_bias', 'w_out', 'norm2', 'w_up', 'ffn_conv_w', 'ffn_conv_b', 'w_down', 'norm_f', 'loss_target', 'm_w_ada', 'm_b_ada', 'm_norm1', 'm_w_in', 'm_rnn_conv_w', 'm_rnn_conv_b', 'm_w_rg_a', 'm_b_rg_a', 'm_w_rg_i', 'm_b_rg_i', 'm_rg_lambda', 'm_w_o_rnn', 'm_w_o_attn', 'm_attn_sinks', 'm_rel_bias', 'm_w_out', 'm_norm2', 'm_w_up', 'm_ffn_conv_w', 'm_ffn_conv_b', 'm_w_down', 'm_norm_f', 'v_w_ada', 'v_b_ada', 'v_norm1', 'v_w_in', 'v_rnn_conv_w', 'v_rnn_conv_b', 'v_w_rg_a', 'v_b_rg_a', 'v_w_rg_i', 'v_b_rg_i', 'v_rg_lambda', 'v_w_o_rnn', 'v_w_o_attn', 'v_attn_sinks', 'v_rel_bias', 'v_w_out', 'v_norm2', 'v_w_up', 'v_ffn_conv_w', 'v_ffn_conv_b', 'v_w_down', 'v_norm_f']
TWIN_OUTPUTS = ['loss', 'grad_x', 'grad_w_ada', 'grad_b_ada', 'grad_norm1', 'grad_w_in', 'grad_rnn_conv_w', 'grad_rnn_conv_b', 'grad_w_rg_a', 'grad_b_rg_a', 'grad_w_rg_i', 'grad_b_rg_i', 'grad_rg_lambda', 'grad_w_o_rnn', 'grad_w_o_attn', 'grad_attn_sinks', 'grad_rel_bias', 'grad_w_out', 'grad_norm2', 'grad_w_up', 'grad_ffn_conv_w', 'grad_ffn_conv_b', 'grad_w_down', 'grad_norm_f', 'delta_w_ada', 'delta_b_ada', 'delta_norm1', 'delta_w_in', 'delta_rnn_conv_w', 'delta_rnn_conv_b', 'delta_w_rg_a', 'delta_b_rg_a', 'delta_w_rg_i', 'delta_b_rg_i', 'delta_rg_lambda', 'delta_w_o_rnn', 'delta_w_o_attn', 'delta_attn_sinks', 'delta_rel_bias', 'delta_w_out', 'delta_norm2', 'delta_w_up', 'delta_ffn_conv_w', 'delta_ffn_conv_b', 'delta_w_down', 'delta_norm_f', 'new_m_w_ada', 'new_m_b_ada', 'new_m_norm1', 'new_m_w_in', 'new_m_rnn_conv_w', 'new_m_rnn_conv_b', 'new_m_w_rg_a', 'new_m_b_rg_a', 'new_m_w_rg_i', 'new_m_b_rg_i', 'new_m_rg_lambda', 'new_m_w_o_rnn', 'new_m_w_o_attn', 'new_m_attn_sinks', 'new_m_rel_bias', 'new_m_w_out', 'new_m_norm2', 'new_m_w_up', 'new_m_ffn_conv_w', 'new_m_ffn_conv_b', 'new_m_w_down', 'new_m_norm_f', 'new_v_w_ada', 'new_v_b_ada', 'new_v_norm1', 'new_v_w_in', 'new_v_rnn_conv_w', 'new_v_rnn_conv_b', 'new_v_w_rg_a', 'new_v_b_rg_a', 'new_v_w_rg_i', 'new_v_b_rg_i', 'new_v_rg_lambda', 'new_v_w_o_rnn', 'new_v_w_o_attn', 'new_v_attn_sinks', 'new_v_rel_bias', 'new_v_w_out', 'new_v_norm2', 'new_v_w_up', 'new_v_ffn_conv_w', 'new_v_ffn_conv_b', 'new_v_w_down', 'new_v_norm_f']
TWIN_LEAF_KINDS = {'loss': 'loss', 'grad_x': 'grad_x', 'grad_w_ada': 'grad_w', 'grad_b_ada': 'grad_w', 'grad_norm1': 'grad_w', 'grad_w_in': 'grad_w', 'grad_rnn_conv_w': 'grad_w', 'grad_rnn_conv_b': 'grad_w', 'grad_w_rg_a': 'grad_w', 'grad_b_rg_a': 'grad_w', 'grad_w_rg_i': 'grad_w', 'grad_b_rg_i': 'grad_w', 'grad_rg_lambda': 'grad_w', 'grad_w_o_rnn': 'grad_w', 'grad_w_o_attn': 'grad_w', 'grad_attn_sinks': 'grad_w', 'grad_rel_bias': 'grad_w', 'grad_w_out': 'grad_w', 'grad_norm2': 'grad_w', 'grad_w_up': 'grad_w', 'grad_ffn_conv_w': 'grad_w', 'grad_ffn_conv_b': 'grad_w', 'grad_w_down': 'grad_w', 'grad_norm_f': 'grad_w', 'delta_w_ada': 'delta_w', 'delta_b_ada': 'delta_w', 'delta_norm1': 'delta_w', 'delta_w_in': 'delta_w', 'delta_rnn_conv_w': 'delta_w', 'delta_rnn_conv_b': 'delta_w', 'delta_w_rg_a': 'delta_w', 'delta_b_rg_a': 'delta_w', 'delta_w_rg_i': 'delta_w', 'delta_b_rg_i': 'delta_w', 'delta_rg_lambda': 'delta_w', 'delta_w_o_rnn': 'delta_w', 'delta_w_o_attn': 'delta_w', 'delta_attn_sinks': 'delta_w', 'delta_rel_bias': 'delta_w', 'delta_w_out': 'delta_w', 'delta_norm2': 'delta_w', 'delta_w_up': 'delta_w', 'delta_ffn_conv_w': 'delta_w', 'delta_ffn_conv_b': 'delta_w', 'delta_w_down': 'delta_w', 'delta_norm_f': 'delta_w', 'new_m_w_ada': 'new_m', 'new_m_b_ada': 'new_m', 'new_m_norm1': 'new_m', 'new_m_w_in': 'new_m', 'new_m_rnn_conv_w': 'new_m', 'new_m_rnn_conv_b': 'new_m', 'new_m_w_rg_a': 'new_m', 'new_m_b_rg_a': 'new_m', 'new_m_w_rg_i': 'new_m', 'new_m_b_rg_i': 'new_m', 'new_m_rg_lambda': 'new_m', 'new_m_w_o_rnn': 'new_m', 'new_m_w_o_attn': 'new_m', 'new_m_attn_sinks': 'new_m', 'new_m_rel_bias': 'new_m', 'new_m_w_out': 'new_m', 'new_m_norm2': 'new_m', 'new_m_w_up': 'new_m', 'new_m_ffn_conv_w': 'new_m', 'new_m_ffn_conv_b': 'new_m', 'new_m_w_down': 'new_m', 'new_m_norm_f': 'new_m', 'new_v_w_ada': 'new_v', 'new_v_b_ada': 'new_v', 'new_v_norm1': 'new_v', 'new_v_w_in': 'new_v', 'new_v_rnn_conv_w': 'new_v', 'new_v_rnn_conv_b': 'new_v', 'new_v_w_rg_a': 'new_v', 'new_v_b_rg_a': 'new_v', 'new_v_w_rg_i': 'new_v', 'new_v_b_rg_i': 'new_v', 'new_v_rg_lambda': 'new_v', 'new_v_w_o_rnn': 'new_v', 'new_v_w_o_attn': 'new_v', 'new_v_attn_sinks': 'new_v', 'new_v_rel_bias': 'new_v', 'new_v_w_out': 'new_v', 'new_v_norm2': 'new_v', 'new_v_w_up': 'new_v', 'new_v_ffn_conv_w': 'new_v', 'new_v_ffn_conv_b': 'new_v', 'new_v_w_down': 'new_v', 'new_v_norm_f': 'new_v'}


def _forward(args):
    return _fwd_reference(*[args[k] for k in FWD_PARAMS])


def _output_shape():
    def fwd():
        inp = _fwd_setup_inputs(0)
        return _fwd_reference(*[inp[k] for k in FWD_PARAMS])
    out = _jax.eval_shape(fwd)
    return out.shape, out.dtype

N_MICROBATCH = 1
ADAM_LR = 0.001
ADAM_B1 = 0.9
ADAM_B2 = 0.999
ADAM_EPS = 1e-08
ADAM_WD = 0.01
ADAM_STEP = 10
PER_EXAMPLE_BATCH_AXIS = {'x': 0, 'c': 0, 'loss_target': 0}
SHARED_INPUTS = []
_WEIGHT_DTYPES = {'w_ada': _jnp.float32, 'b_ada': _jnp.float32, 'norm1': _jnp.float32, 'w_in': _jnp.float32, 'rnn_conv_w': _jnp.float32, 'rnn_conv_b': _jnp.float32, 'w_rg_a': _jnp.float32, 'b_rg_a': _jnp.float32, 'w_rg_i': _jnp.float32, 'b_rg_i': _jnp.float32, 'rg_lambda': _jnp.float32, 'w_o_rnn': _jnp.float32, 'w_o_attn': _jnp.float32, 'attn_sinks': _jnp.float32, 'rel_bias': _jnp.float32, 'w_out': _jnp.float32, 'norm2': _jnp.float32, 'w_up': _jnp.float32, 'ffn_conv_w': _jnp.float32, 'ffn_conv_b': _jnp.float32, 'w_down': _jnp.float32, 'norm_f': _jnp.float32}
MOMENT_SCALE = {'w_ada': 1.018425e-01, 'b_ada': 1.901180e-01, 'norm1': 1.224351e-01, 'w_in': 7.982301e-02, 'rnn_conv_w': 1.243123e-01, 'rnn_conv_b': 2.155315e-01, 'w_rg_a': 1.452287e-02, 'b_rg_a': 2.420166e-02, 'w_rg_i': 3.097340e-02, 'b_rg_i': 4.991996e-02, 'rg_lambda': 6.912671e-02, 'w_o_rnn': 1.332285e-01, 'w_o_attn': 2.520058e-02, 'attn_sinks': 8.536119e-03, 'rel_bias': 1.407208e-02, 'w_out': 1.325390e-01, 'norm2': 8.013552e-02, 'w_up': 3.664751e-02, 'ffn_conv_w': 3.687129e-02, 'ffn_conv_b': 2.734557e-02, 'w_down': 6.252174e-02, 'norm_f': 3.270947e+01}


def _to_microbatches(a, axis):
    t = _jnp.moveaxis(a, axis, 0)
    t = t.reshape((N_MICROBATCH, t.shape[0] // N_MICROBATCH) + t.shape[1:])
    return _jnp.moveaxis(t, 1, axis + 1)


def setup_inputs(seed: int = 0) -> dict:
    inp = _fwd_setup_inputs(seed)
    key = _jax.random.fold_in(_jax.random.key(seed), 7919)
    shape, _ = _output_shape()
    out = dict(inp)
    out["loss_target"] = _jax.random.normal(_jax.random.fold_in(key, 0), shape, _jnp.float32)
    for i, name in enumerate(TWIN_WEIGHTS):
        w = inp[name].astype(_jnp.float32)
        if MOMENT_SCALE is None:
            s = _jnp.sqrt(_jnp.mean(_jnp.square(w)) + 1e-30)
        else:
            s = MOMENT_SCALE[name]
        km, kv = _jax.random.split(_jax.random.fold_in(key, i + 1))
        out[name] = w
        out["m_" + name] = s * _jax.random.normal(km, w.shape, _jnp.float32)
        out["v_" + name] = (s * s) * _jax.random.uniform(kv, w.shape, _jnp.float32, 0.5, 1.5)
    if N_MICROBATCH > 1:
        for name, axis in PER_EXAMPLE_BATCH_AXIS.items():
            out[name] = _to_microbatches(out[name], axis)
    return {'x': out['x'], 'c': out['c'], 'w_ada': out['w_ada'], 'b_ada': out['b_ada'], 'norm1': out['norm1'], 'w_in': out['w_in'], 'rnn_conv_w': out['rnn_conv_w'], 'rnn_conv_b': out['rnn_conv_b'], 'w_rg_a': out['w_rg_a'], 'b_rg_a': out['b_rg_a'], 'w_rg_i': out['w_rg_i'], 'b_rg_i': out['b_rg_i'], 'rg_lambda': out['rg_lambda'], 'w_o_rnn': out['w_o_rnn'], 'w_o_attn': out['w_o_attn'], 'attn_sinks': out['attn_sinks'], 'rel_bias': out['rel_bias'], 'w_out': out['w_out'], 'norm2': out['norm2'], 'w_up': out['w_up'], 'ffn_conv_w': out['ffn_conv_w'], 'ffn_conv_b': out['ffn_conv_b'], 'w_down': out['w_down'], 'norm_f': out['norm_f'], 'loss_target': out['loss_target'], 'm_w_ada': out['m_w_ada'], 'm_b_ada': out['m_b_ada'], 'm_norm1': out['m_norm1'], 'm_w_in': out['m_w_in'], 'm_rnn_conv_w': out['m_rnn_conv_w'], 'm_rnn_conv_b': out['m_rnn_conv_b'], 'm_w_rg_a': out['m_w_rg_a'], 'm_b_rg_a': out['m_b_rg_a'], 'm_w_rg_i': out['m_w_rg_i'], 'm_b_rg_i': out['m_b_rg_i'], 'm_rg_lambda': out['m_rg_lambda'], 'm_w_o_rnn': out['m_w_o_rnn'], 'm_w_o_attn': out['m_w_o_attn'], 'm_attn_sinks': out['m_attn_sinks'], 'm_rel_bias': out['m_rel_bias'], 'm_w_out': out['m_w_out'], 'm_norm2': out['m_norm2'], 'm_w_up': out['m_w_up'], 'm_ffn_conv_w': out['m_ffn_conv_w'], 'm_ffn_conv_b': out['m_ffn_conv_b'], 'm_w_down': out['m_w_down'], 'm_norm_f': out['m_norm_f'], 'v_w_ada': out['v_w_ada'], 'v_b_ada': out['v_b_ada'], 'v_norm1': out['v_norm1'], 'v_w_in': out['v_w_in'], 'v_rnn_conv_w': out['v_rnn_conv_w'], 'v_rnn_conv_b': out['v_rnn_conv_b'], 'v_w_rg_a': out['v_w_rg_a'], 'v_b_rg_a': out['v_b_rg_a'], 'v_w_rg_i': out['v_w_rg_i'], 'v_b_rg_i': out['v_b_rg_i'], 'v_rg_lambda': out['v_rg_lambda'], 'v_w_o_rnn': out['v_w_o_rnn'], 'v_w_o_attn': out['v_w_o_attn'], 'v_attn_sinks': out['v_attn_sinks'], 'v_rel_bias': out['v_rel_bias'], 'v_w_out': out['v_w_out'], 'v_norm2': out['v_norm2'], 'v_w_up': out['v_w_up'], 'v_ffn_conv_w': out['v_ffn_conv_w'], 'v_ffn_conv_b': out['v_ffn_conv_b'], 'v_w_down': out['v_w_down'], 'v_norm_f': out['v_norm_f']}


def _loss(weights, diff, rest, loss_target):
    with _jax.named_scope("forward"):
        args = {**rest, TWIN_DIFF_INPUT: diff, **{k: w.astype(_WEIGHT_DTYPES[k]) for k, w in weights.items()}}
        y = _forward(args)
    with _jax.named_scope("loss_head"):
        err = _jnp.square(y.astype(_jnp.float32) - loss_target)
        return 0.5 * _jnp.sum(_jnp.mean(err, axis=-1)) if err.ndim else 0.5 * err


def _adamw(w, g, m, v):
    m = ADAM_B1 * m + (1.0 - ADAM_B1) * g
    v = ADAM_B2 * v + (1.0 - ADAM_B2) * _jnp.square(g)
    m_hat = m / (1.0 - ADAM_B1 ** ADAM_STEP)
    v_hat = v / (1.0 - ADAM_B2 ** ADAM_STEP)
    delta = -ADAM_LR * (m_hat / (_jnp.sqrt(v_hat) + ADAM_EPS) + ADAM_WD * w)
    return delta, m, v


def reference(x, c, w_ada, b_ada, norm1, w_in, rnn_conv_w, rnn_conv_b, w_rg_a, b_rg_a, w_rg_i, b_rg_i, rg_lambda, w_o_rnn, w_o_attn, attn_sinks, rel_bias, w_out, norm2, w_up, ffn_conv_w, ffn_conv_b, w_down, norm_f, loss_target, m_w_ada, m_b_ada, m_norm1, m_w_in, m_rnn_conv_w, m_rnn_conv_b, m_w_rg_a, m_b_rg_a, m_w_rg_i, m_b_rg_i, m_rg_lambda, m_w_o_rnn, m_w_o_attn, m_attn_sinks, m_rel_bias, m_w_out, m_norm2, m_w_up, m_ffn_conv_w, m_ffn_conv_b, m_w_down, m_norm_f, v_w_ada, v_b_ada, v_norm1, v_w_in, v_rnn_conv_w, v_rnn_conv_b, v_w_rg_a, v_b_rg_a, v_w_rg_i, v_b_rg_i, v_rg_lambda, v_w_o_rnn, v_w_o_attn, v_attn_sinks, v_rel_bias, v_w_out, v_norm2, v_w_up, v_ffn_conv_w, v_ffn_conv_b, v_w_down, v_norm_f):
    given = dict(x=x, c=c, w_ada=w_ada, b_ada=b_ada, norm1=norm1, w_in=w_in, rnn_conv_w=rnn_conv_w, rnn_conv_b=rnn_conv_b, w_rg_a=w_rg_a, b_rg_a=b_rg_a, w_rg_i=w_rg_i, b_rg_i=b_rg_i, rg_lambda=rg_lambda, w_o_rnn=w_o_rnn, w_o_attn=w_o_attn, attn_sinks=attn_sinks, rel_bias=rel_bias, w_out=w_out, norm2=norm2, w_up=w_up, ffn_conv_w=ffn_conv_w, ffn_conv_b=ffn_conv_b, w_down=w_down, norm_f=norm_f, loss_target=loss_target, m_w_ada=m_w_ada, m_b_ada=m_b_ada, m_norm1=m_norm1, m_w_in=m_w_in, m_rnn_conv_w=m_rnn_conv_w, m_rnn_conv_b=m_rnn_conv_b, m_w_rg_a=m_w_rg_a, m_b_rg_a=m_b_rg_a, m_w_rg_i=m_w_rg_i, m_b_rg_i=m_b_rg_i, m_rg_lambda=m_rg_lambda, m_w_o_rnn=m_w_o_rnn, m_w_o_attn=m_w_o_attn, m_attn_sinks=m_attn_sinks, m_rel_bias=m_rel_bias, m_w_out=m_w_out, m_norm2=m_norm2, m_w_up=m_w_up, m_ffn_conv_w=m_ffn_conv_w, m_ffn_conv_b=m_ffn_conv_b, m_w_down=m_w_down, m_norm_f=m_norm_f, v_w_ada=v_w_ada, v_b_ada=v_b_ada, v_norm1=v_norm1, v_w_in=v_w_in, v_rnn_conv_w=v_rnn_conv_w, v_rnn_conv_b=v_rnn_conv_b, v_w_rg_a=v_w_rg_a, v_b_rg_a=v_b_rg_a, v_w_rg_i=v_w_rg_i, v_b_rg_i=v_b_rg_i, v_rg_lambda=v_rg_lambda, v_w_o_rnn=v_w_o_rnn, v_w_o_attn=v_w_o_attn, v_attn_sinks=v_attn_sinks, v_rel_bias=v_rel_bias, v_w_out=v_w_out, v_norm2=v_norm2, v_w_up=v_w_up, v_ffn_conv_w=v_ffn_conv_w, v_ffn_conv_b=v_ffn_conv_b, v_w_down=v_w_down, v_norm_f=v_norm_f)
    weights = {n: given[n] for n in TWIN_WEIGHTS}
    shared = {n: given[n] for n in SHARED_INPUTS}
    per_example = {n: given[n] for n in ['x', 'c']}
    grad_fn = _jax.value_and_grad(_loss, argnums=(0, 1))

    def one_microbatch(ex, loss_target):
        ex = dict(ex)
        diff = ex.pop(TWIN_DIFF_INPUT)
        return grad_fn(weights, diff, {**shared, **ex}, loss_target)

    if N_MICROBATCH == 1:
        loss, (grad_w, grad_x) = one_microbatch(per_example, given["loss_target"])
    else:
        def body(carry, xs):
            loss_sum, grad_sum = carry
            l_k, (gw_k, gx_k) = one_microbatch(xs[0], xs[1])
            with _jax.named_scope("update"):
                return (loss_sum + l_k, _jax.tree.map(_jnp.add, grad_sum, gw_k)), gx_k

        init = (_jnp.zeros((), _jnp.float32), _jax.tree.map(_jnp.zeros_like, weights))
        (loss, grad_w), grad_x = _jax.lax.scan(body, init, (per_example, given["loss_target"]))
    with _jax.named_scope("update"):
        delta_w, new_m, new_v = {}, {}, {}
        for n in TWIN_WEIGHTS:
            delta_w[n], new_m[n], new_v[n] = _adamw(weights[n], grad_w[n], given["m_" + n], given["v_" + n])
    return (loss, grad_x, *[grad_w[n] for n in TWIN_WEIGHTS], *[delta_w[n] for n in TWIN_WEIGHTS],
            *[new_m[n] for n in TWIN_WEIGHTS], *[new_v[n] for n in TWIN_WEIGHTS])
```

```python
import math

import numpy as np
import jax
import jax.numpy as jnp
from jax import lax
from jax.experimental import pallas as pl
from jax.experimental.pallas import tpu as pltpu

F32 = jnp.float32
BF16 = jnp.bfloat16
MESH = pl.DeviceIdType.MESH
ANY = pl.BlockSpec(memory_space=pl.ANY)

EPS = 1e-6
NEG_INF = -1e30
HEAD_DIM = 64
BLOCK = 128
NUM_EXACT = 16
MAX_DISTANCE = 128
RG_C = 8.0
ADAM_LR, ADAM_B1, ADAM_B2, ADAM_EPS, ADAM_WD, ADAM_STEP = 0.001, 0.9, 0.999, 1e-08, 0.01, 10
N_CHIPS = 4
N_DEV = 8
SUBLANES = 8
LANES = 128
VMEM_LIMIT_BYTES = 48 * 1024 * 1024
GELU_C0 = math.sqrt(2.0 / math.pi)
GELU_C1 = 0.044715


def _pick(dim, pref, align):
    if dim <= pref:
        return dim
    t = (pref // align) * align
    while t >= align:
        if dim % t == 0:
            return t
        t -= align
    return dim


def _params(sem):
    return pltpu.CompilerParams(dimension_semantics=sem, vmem_limit_bytes=VMEM_LIMIT_BYTES)


def _call(name, body, grid, in_specs, out_specs, out_shape, scratch=(), nsp=0, sem=None):
    sem = sem or ("parallel",) * (len(grid) - 1) + ("arbitrary",)
    if nsp:
        gs = pltpu.PrefetchScalarGridSpec(num_scalar_prefetch=nsp, grid=grid, in_specs=in_specs,
                                          out_specs=out_specs, scratch_shapes=list(scratch))
        return pl.pallas_call(body, name=name, grid_spec=gs, out_shape=out_shape, compiler_params=_params(sem))
    return pl.pallas_call(body, name=name, grid=grid, in_specs=in_specs, out_specs=out_specs,
                          out_shape=out_shape, scratch_shapes=list(scratch), compiler_params=_params(sem))


def _sds(shape, dtype):
    return jax.ShapeDtypeStruct(shape, dtype)


def _T(tr, tc, off=0):
    return pl.BlockSpec((tr, tc), lambda j, i: (i, j + off))


def _P(rows, tc, off=0):
    return pl.BlockSpec((rows, tc), lambda j, i: (0, j + off))


def _gelu(x):
    t = jnp.tanh(GELU_C0 * (x + GELU_C1 * x * x * x))
    return 0.5 * x * (1.0 + t)


def _gelu_and_grad(x):
    t = jnp.tanh(GELU_C0 * (x + GELU_C1 * x * x * x))
    g = 0.5 * x * (1.0 + t)
    dg = 0.5 * (1.0 + t) + 0.5 * x * (1.0 - t * t) * GELU_C0 * (1.0 + 3.0 * GELU_C1 * x * x)
    return g, dg


def _sigmoid(x):
    return 1.0 / (1.0 + jnp.exp(-x))


def _accum(ref, val, first):
    @pl.when(first)
    def _():
        ref[...] = val

    @pl.when(jnp.logical_not(first))
    def _():
        ref[...] += val


def _colsum(v):
    return jnp.sum(v, axis=0, keepdims=True)


def _mm(a, b, *, name, ta=False, tb=False, out_dtype=F32, tm=512, tn=1024, tk=512, exact=False, bias=None):
    if ta:
        K, M = a.shape
    else:
        M, K = a.shape
    if tb:
        N, K2 = b.shape
    else:
        K2, N = b.shape
    assert K == K2, (a.shape, b.shape, ta, tb)
    tm, tn, tk = _pick(M, tm, LANES), _pick(N, tn, LANES), _pick(K, tk, LANES)
    nk = K // tk
    cdt = F32 if exact else BF16
    prec = lax.Precision.HIGHEST if exact else None
    dims = (((0 if ta else 1,), (1 if tb else 0,)), ((), ()))

    def body(*refs):
        if bias is None:
            a_ref, b_ref, o_ref, acc_ref = refs
        else:
            a_ref, b_ref, bias_ref, o_ref, acc_ref = refs
        k = pl.program_id(2)

        @pl.when(k == 0)
        def _():
            acc_ref[...] = jnp.zeros_like(acc_ref)

        acc_ref[...] += lax.dot_general(a_ref[...].astype(cdt), b_ref[...].astype(cdt), dims,
                                        preferred_element_type=F32, precision=prec)

        @pl.when(k == nk - 1)
        def _():
            r = acc_ref[...]
            if bias is not None:
                r = r + bias_ref[...]
            o_ref[...] = r.astype(out_dtype)

    a_spec = pl.BlockSpec((tk, tm), lambda i, j, k: (k, i)) if ta else pl.BlockSpec((tm, tk), lambda i, j, k: (i, k))
    b_spec = pl.BlockSpec((tn, tk), lambda i, j, k: (j, k)) if tb else pl.BlockSpec((tk, tn), lambda i, j, k: (k, j))
    in_specs, args = [a_spec, b_spec], [a, b]
    if bias is not None:
        in_specs.append(pl.BlockSpec((1, tn), lambda i, j, k: (0, j)))
        args.append(bias)
    return _call(name, body, (M // tm, N // tn, nk), in_specs, pl.BlockSpec((tm, tn), lambda i, j, k: (i, j)),
                 _sds((M, N), out_dtype), scratch=[pltpu.VMEM((tm, tn), F32)],
                 sem=("parallel", "parallel", "arbitrary"))(*args)


def _ew(name, fn, ins, out_dtypes, S, C, rows=(), tr=256, tc=512):
    g = C
    for _, off in ins:
        g = math.gcd(g, off) if off else g
    tc = _pick(g, tc, LANES)
    tr = _pick(S, tr, 16)
    n_in, n_row, n_out = len(ins), len(rows), len(out_dtypes)

    def body(*refs):
        vals = [r[...] for r in refs[:n_in + n_row]]
        outs = fn(*vals)
        for o_ref, o in zip(refs[n_in + n_row:], outs):
            o_ref[...] = o.astype(o_ref.dtype)

    in_specs = [_T(tr, tc, off // tc) for _, off in ins] + [_P(1, tc) for _ in rows]
    res = _call(name, body, (C // tc, S // tr), in_specs, [_T(tr, tc) for _ in out_dtypes],
                [_sds((S, C), d) for d in out_dtypes], sem=("parallel", "parallel"))(*[a for a, _ in ins], *rows)
    return res


def _adaln_fwd(x, norm, scale, shift, *, name, t=None, gate=None):
    S, D = x.shape
    tr = _pick(S, 256, 16)
    resid = t is not None

    def body(*refs):
        if resid:
            x_ref, t_ref, g_ref, n_ref, sc_ref, sh_ref, u_ref, h_ref = refs
            h = x_ref[...] + g_ref[...] * t_ref[...]
            h_ref[...] = h
        else:
            x_ref, n_ref, sc_ref, sh_ref, u_ref = refs
            h = x_ref[...]
        r = lax.rsqrt(jnp.mean(h * h, axis=-1, keepdims=True) + EPS)
        u_ref[...] = (h * r * (n_ref[...] * (1.0 + sc_ref[...])) + sh_ref[...]).astype(BF16)

    full, row = _T(tr, D), _P(1, D)
    if resid:
        return _call(name, body, (1, S // tr), [full, full, row, row, row, row], [full, full],
                     [_sds((S, D), BF16), _sds((S, D), F32)])(x, t, gate, norm, scale, shift)
    return _call(name, body, (1, S // tr), [full, row, row, row], full, _sds((S, D), BF16))(x, norm, scale, shift)


def _adaln_bwd(h, du, dres, norm, scale, *, name, t=None, gate=None):
    S, D = h.shape
    tr = _pick(S, 256, 16)
    gated = t is not None

    def body(*refs):
        if gated:
            h_ref, du_ref, dr_ref, n_ref, sc_ref, t_ref, g_ref, dh_ref, dsh_ref, dsc_ref, dn_ref, dt_ref, dg_ref = refs
        else:
            h_ref, du_ref, dr_ref, n_ref, sc_ref, dh_ref, dsh_ref, dsc_ref, dn_ref = refs
        first = pl.program_id(1) == 0
        hv, duv = h_ref[...], du_ref[...]
        r = lax.rsqrt(jnp.mean(hv * hv, axis=-1, keepdims=True) + EPS)
        xn = hv * r
        one_sc = 1.0 + sc_ref[...]
        dxn = duv * (n_ref[...] * one_sc)
        dh = dr_ref[...] + r * (dxn - xn * jnp.mean(dxn * xn, axis=-1, keepdims=True))
        dh_ref[...] = dh
        dux = duv * xn
        _accum(dsh_ref, _colsum(duv), first)
        _accum(dsc_ref, _colsum(dux * n_ref[...]), first)
        _accum(dn_ref, _colsum(dux * one_sc), first)
        if gated:
            dt_ref[...] = (dh * g_ref[...]).astype(BF16)
            _accum(dg_ref, _colsum(dh * t_ref[...]), first)

    full, row = _T(tr, D), _P(1, D)
    rowo = _sds((1, D), F32)
    if gated:
        return _call(name, body, (1, S // tr), [full, full, full, row, row, full, row],
                     [full, row, row, row, full, row],
                     [_sds((S, D), F32), rowo, rowo, rowo, _sds((S, D), BF16), rowo])(h, du, dres, norm, scale, t, gate)
    return _call(name, body, (1, S // tr), [full, full, full, row, row], [full, row, row, row],
                 [_sds((S, D), F32), rowo, rowo, rowo])(h, du, dres, norm, scale)


def _final(h1, t2, gate2, norm_f, tgt, *, name):
    S, D = h1.shape
    tr = _pick(S, 256, 16)

    def body(h_ref, t_ref, g_ref, n_ref, y_ref, dh_ref, dt_ref, loss_ref, dn_ref, dg_ref):
        first = pl.program_id(1) == 0
        tv = t_ref[...]
        h2 = h_ref[...] + g_ref[...] * tv
        r = lax.rsqrt(jnp.mean(h2 * h2, axis=-1, keepdims=True) + EPS)
        xn = h2 * r
        e = xn * n_ref[...] - y_ref[...]
        part = 0.5 * jnp.sum(jnp.mean(e * e, axis=-1, keepdims=True), axis=0, keepdims=True)
        _accum(loss_ref, jnp.broadcast_to(part, (SUBLANES, LANES)), first)
        dy = e * (1.0 / D)
        _accum(dn_ref, _colsum(dy * xn), first)
        dxn = dy * n_ref[...]
        dh2 = r * (dxn - xn * jnp.mean(dxn * xn, axis=-1, keepdims=True))
        dh_ref[...] = dh2
        dt_ref[...] = (dh2 * g_ref[...]).astype(BF16)
        _accum(dg_ref, _colsum(dh2 * tv), first)

    full, row = _T(tr, D), _P(1, D)
    rowo = _sds((1, D), F32)
    return _call(name, body, (1, S // tr), [full, full, row, row, full],
                 [full, full, _P(SUBLANES, LANES), row, row],
                 [_sds((S, D), F32), _sds((S, D), BF16), _sds((SUBLANES, LANES), F32), rowo, rowo])(h1, t2, gate2, norm_f, tgt)


def _halo_prev(tr, tc, off=0):
    return pl.BlockSpec((SUBLANES, tc), lambda j, i: (jnp.maximum(i * (tr // SUBLANES) - 1, 0), j + off))


def _halo_next(tr, tc, n_slabs, off=0):
    return pl.BlockSpec((SUBLANES, tc), lambda j, i: (jnp.minimum((i + 1) * (tr // SUBLANES), n_slabs - 1), j + off))


def _past(buf, prev_ref, cur, first, tr, taps):
    buf[0:SUBLANES, :] = jnp.where(first, 0.0, prev_ref[...])
    buf[SUBLANES:SUBLANES + tr, :] = cur
    return [cur] + [buf[SUBLANES - d:SUBLANES - d + tr, :] for d in range(1, taps)]


def _future(buf, next_ref, cur, last, tr, taps):
    buf[0:tr, :] = cur
    buf[tr:tr + SUBLANES, :] = jnp.where(last, 0.0, next_ref[...])
    return [cur] + [buf[d:d + tr, :] for d in range(1, taps)]


def _conv_apply(xs, w_ref, b_ref):
    taps = len(xs)
    y = b_ref[...] + w_ref[taps - 1:taps, :] * xs[0]
    for d in range(1, taps):
        y = y + w_ref[taps - 1 - d:taps - d, :] * xs[d]
    return y


def _conv_fwd(x, off, C, w, b, *, name):
    S = x.shape[0]
    taps = w.shape[0]
    tc = _pick(math.gcd(C, off) if off else C, 512, LANES)
    tr = _pick(S, 256, 16)

    def body(x_ref, p_ref, w_ref, b_ref, y_ref, buf):
        xs = _past(buf, p_ref, x_ref[...], pl.program_id(1) == 0, tr, taps)
        y_ref[...] = _conv_apply(xs, w_ref, b_ref)

    return _call(name, body, (C // tc, S // tr),
                 [_T(tr, tc, off // tc), _halo_prev(tr, tc, off // tc), _P(taps, tc), _P(1, tc)],
                 _T(tr, tc), _sds((S, C), F32), scratch=[pltpu.VMEM((tr + SUBLANES, tc), F32)],
                 sem=("parallel", "parallel"))(x, x, w, b)


def _conv_bwd(dy, x, off, C, w, *, name):
    S = dy.shape[0]
    taps = w.shape[0]
    tc = _pick(math.gcd(C, off) if off else C, 512, LANES)
    tr = _pick(S, 256, 16)
    n_slabs = S // SUBLANES

    def body(dy_ref, nx_ref, x_ref, p_ref, w_ref, dx_ref, dw_ref, db_ref, fbuf, pbuf):
        i = pl.program_id(1)
        first = i == 0
        dyv = dy_ref[...]
        dys = _future(fbuf, nx_ref, dyv, i == S // tr - 1, tr, taps)
        dx = w_ref[taps - 1:taps, :] * dys[0]
        for d in range(1, taps):
            dx = dx + w_ref[taps - 1 - d:taps - d, :] * dys[d]
        dx_ref[...] = dx.astype(BF16)
        xs = _past(pbuf, p_ref, x_ref[...], first, tr, taps)

        @pl.when(first)
        def _():
            dw_ref[...] = jnp.zeros_like(dw_ref)

        for d in range(taps):
            dw_ref[taps - 1 - d:taps - d, :] += _colsum(dyv * xs[d])
        _accum(db_ref, _colsum(dyv), first)

    return _call(name, body, (C // tc, S // tr),
                 [_T(tr, tc), _halo_next(tr, tc, n_slabs), _T(tr, tc, off // tc), _halo_prev(tr, tc, off // tc),
                  _P(taps, tc)],
                 [_T(tr, tc), _P(taps, tc), _P(1, tc)],
                 [_sds((S, C), BF16), _sds((taps, C), F32), _sds((1, C), F32)],
                 scratch=[pltpu.VMEM((tr + SUBLANES, tc), F32), pltpu.VMEM((tr + SUBLANES, tc), F32)])(dy, dy, x, x, w)


def _geglu_fwd(up, w, b, F, *, name):
    S = up.shape[0]
    taps = w.shape[0]
    tc = _pick(F, 512, LANES)
    tr = _pick(S, 256, 16)
    nf = F // tc

    def body(g_ref, gp_ref, v_ref, vp_ref, wg_ref, wv_ref, bg_ref, bv_ref, a_ref, gbuf, vbuf):
        first = pl.program_id(1) == 0
        g = _conv_apply(_past(gbuf, gp_ref, g_ref[...], first, tr, taps), wg_ref, bg_ref)
        v = _conv_apply(_past(vbuf, vp_ref, v_ref[...], first, tr, taps), wv_ref, bv_ref)
        a_ref[...] = (_gelu(g) * v).astype(BF16)

    buf = pltpu.VMEM((tr + SUBLANES, tc), F32)
    return _call(name, body, (nf, S // tr),
                 [_T(tr, tc), _halo_prev(tr, tc), _T(tr, tc, nf), _halo_prev(tr, tc, nf),
                  _P(taps, tc), _P(taps, tc, nf), _P(1, tc), _P(1, tc, nf)],
                 _T(tr, tc), _sds((S, F), BF16), scratch=[buf, buf], sem=("parallel", "parallel"))(up, up, up, up, w, w, b, b)


def _geglu_bwd(up, w, b, da, F, *, name):
    S = up.shape[0]
    taps = w.shape[0]
    tc = _pick(F, 512, LANES)
    tr = _pick(S, 256, 16)
    nf = F // tc

    def body(g_ref, gp_ref, v_ref, vp_ref, wg_ref, wv_ref, bg_ref, bv_ref, da_ref, dg_ref, dv_ref, gbuf, vbuf):
        first = pl.program_id(1) == 0
        g = _conv_apply(_past(gbuf, gp_ref, g_ref[...], first, tr, taps), wg_ref, bg_ref)
        v = _conv_apply(_past(vbuf, vp_ref, v_ref[...], first, tr, taps), wv_ref, bv_ref)
        ge, dge = _gelu_and_grad(g)
        dav = da_ref[...]
        dg_ref[...] = dav * v * dge
        dv_ref[...] = dav * ge

    buf = pltpu.VMEM((tr + SUBLANES, tc), F32)
    return _call(name, body, (nf, S // tr),
                 [_T(tr, tc), _halo_prev(tr, tc), _T(tr, tc, nf), _halo_prev(tr, tc, nf),
                  _P(taps, tc), _P(taps, tc, nf), _P(1, tc), _P(1, tc, nf), _T(tr, tc)],
                 [_T(tr, tc), _T(tr, tc)], [_sds((S, F), F32), _sds((S, F), F32)],
                 scratch=[buf, buf], sem=("parallel", "parallel"))(up, up, up, up, w, w, b, b, da)


def _softplus_neg(lam):
    z = -lam
    e = jnp.exp(-jnp.abs(z))
    u = 1.0 + e
    log1p_e = jnp.where(u == 1.0, e, jnp.log(u) * e / jnp.where(u == 1.0, 1.0, u - 1.0))
    sp = jnp.maximum(z, 0.0) + log1p_e
    sg = jnp.where(z >= 0, 1.0 / u, e / u)
    return sp, sg


def _one_minus_exp(x):
    series = -x * (1.0 + x * (0.5 + x * (1.0 / 6.0 + x * (1.0 / 24.0))))
    return jnp.where(x > -0.01, series, 1.0 - jnp.exp(x))


def _gate_values(xc, wa_ref, wi_ref, ba_ref, bi_ref, lam_ref, is_t0):
    xb = xc.astype(BF16)
    ra = _sigmoid(jnp.dot(xb, wa_ref[0].astype(BF16), preferred_element_type=F32) + ba_ref[...])
    ia = _sigmoid(jnp.dot(xb, wi_ref[0].astype(BF16), preferred_element_type=F32) + bi_ref[...])
    sp, sg = _softplus_neg(lam_ref[...])
    log_a = -RG_C * ra * sp
    a = jnp.exp(log_a)
    mult = jnp.where(is_t0, 1.0, jnp.sqrt(_one_minus_exp(2.0 * log_a)))
    return ra, ia, sp, sg, a, mult


def _rnn_blockspecs(tr):
    x = pl.BlockSpec((tr, LANES), lambda n, i: (i, n))
    w = pl.BlockSpec((1, LANES, LANES), lambda n, i: (n, 0, 0))
    p = pl.BlockSpec((1, LANES), lambda n, i: (0, n))
    return x, w, p


def _is_t0(tr):
    rows = lax.broadcasted_iota(jnp.int32, (tr, LANES), 0)
    return jnp.logical_and(pl.program_id(1) == 0, rows == 0)


def _gates_fwd(xc, w_a, b_a, w_i, b_i, lam, *, name):
    S, C = xc.shape
    tr = _pick(S, 1024, 16)

    def body(x_ref, wa_ref, wi_ref, ba_ref, bi_ref, lam_ref, a_ref, inp_ref):
        xv = x_ref[...]
        _, ia, _, _, a, mult = _gate_values(xv, wa_ref, wi_ref, ba_ref, bi_ref, lam_ref, _is_t0(tr))
        a_ref[...] = a
        inp_ref[...] = mult * ia * xv

    x, w, p = _rnn_blockspecs(tr)
    return _call(name, body, (C // LANES, S // tr), [x, w, w, p, p, p], [x, x],
                 [_sds((S, C), F32), _sds((S, C), F32)], sem=("parallel", "parallel"))(xc, w_a, w_i, b_a, b_i, lam)


def _gates_bwd(xc, dacc, hprev, w_a, b_a, w_i, b_i, lam, *, name):
    S, C = xc.shape
    nb = C // LANES
    tr = _pick(S, 1024, 16)

    def body(x_ref, d_ref, hp_ref, wa_ref, wi_ref, ba_ref, bi_ref, lam_ref,
             dx_ref, dwa_ref, dwi_ref, dba_ref, dbi_ref, dlam_ref):
        first = pl.program_id(1) == 0
        t0 = _is_t0(tr)
        xv, dv = x_ref[...], d_ref[...]
        ra, ia, sp, sg, a, mult = _gate_values(xv, wa_ref, wi_ref, ba_ref, bi_ref, lam_ref, t0)
        d_a = dv * hp_ref[...]
        d_mult = dv * ia * xv
        d_ia = dv * mult * xv
        d_log = d_a * a + jnp.where(t0, 0.0, -d_mult * (a * a) / mult)
        d_pa = d_log * (-RG_C * sp) * ra * (1.0 - ra)
        d_pi = d_ia * ia * (1.0 - ia)
        xb, dab, dib = xv.astype(BF16), d_pa.astype(BF16), d_pi.astype(BF16)
        nt = (((1,), (1,)), ((), ()))
        tn = (((0,), (0,)), ((), ()))
        dx_ref[...] = (dv * mult * ia
                       + lax.dot_general(dab, wa_ref[0].astype(BF16), nt, preferred_element_type=F32)
                       + lax.dot_general(dib, wi_ref[0].astype(BF16), nt, preferred_element_type=F32))
        _accum(dwa_ref, lax.dot_general(xb, dab, tn, preferred_element_type=F32)[None], first)
        _accum(dwi_ref, lax.dot_general(xb, dib, tn, preferred_element_type=F32)[None], first)
        _accum(dba_ref, _colsum(d_pa), first)
        _accum(dbi_ref, _colsum(d_pi), first)
        _accum(dlam_ref, _colsum(d_log * (-RG_C) * ra) * (-sg), first)

    x, w, p = _rnn_blockspecs(tr)
    row = _sds((1, C), F32)
    return _call(name, body, (nb, S // tr), [x, x, x, w, w, p, p, p], [x, w, w, p, p, p],
                 [_sds((S, C), F32), _sds((nb, LANES, LANES), F32), _sds((nb, LANES, LANES), F32), row, row, row])(
                     xc, dacc, hprev, w_a, w_i, b_a, b_i, lam)


def _scan_fwd(a, inp, *, name):
    S, C = a.shape
    tc = _pick(C, 1280, LANES)
    tr = _pick(S, 256, SUBLANES)

    def body(a_ref, b_ref, h_ref, hp_ref, carry):
        @pl.when(pl.program_id(1) == 0)
        def _():
            carry[...] = jnp.zeros_like(carry)

        rows = lax.broadcasted_iota(jnp.int32, (SUBLANES, tc), 0)

        def slab(s, h):
            base = pl.multiple_of(s * SUBLANES, SUBLANES)
            av, bv = a_ref[pl.ds(base, SUBLANES), :], b_ref[pl.ds(base, SUBLANES), :]
            ho = jnp.zeros((SUBLANES, tc), F32)
            po = jnp.zeros((SUBLANES, tc), F32)
            for r in range(SUBLANES):
                po = jnp.where(rows == r, h, po)
                h = av[r:r + 1, :] * h + bv[r:r + 1, :]
                ho = jnp.where(rows == r, h, ho)
            h_ref[pl.ds(base, SUBLANES), :] = ho
            hp_ref[pl.ds(base, SUBLANES), :] = po
            return h

        carry[...] = lax.fori_loop(0, tr // SUBLANES, slab, carry[...])

    t = _T(tr, tc)
    return _call(name, body, (C // tc, S // tr), [t, t], [t, t], [_sds((S, C), F32), _sds((S, C), F32)],
                 scratch=[pltpu.VMEM((1, tc), F32)])(a, inp)


def _scan_bwd(a, dh, *, name):
    S, C = a.shape
    tc = _pick(C, 1280, LANES)
    tr = _pick(S, 256, SUBLANES)
    nr = S // tr

    def body(a_ref, d_ref, o_ref, carry):
        @pl.when(pl.program_id(1) == 0)
        def _():
            carry[...] = jnp.zeros_like(carry)

        rows = lax.broadcasted_iota(jnp.int32, (SUBLANES, tc), 0)
        n_slabs = tr // SUBLANES

        def slab(s, g):
            base = pl.multiple_of((n_slabs - 1 - s) * SUBLANES, SUBLANES)
            av, dv = a_ref[pl.ds(base, SUBLANES), :], d_ref[pl.ds(base, SUBLANES), :]
            out = jnp.zeros((SUBLANES, tc), F32)
            for r in range(SUBLANES - 1, -1, -1):
                acc = dv[r:r + 1, :] + g
                out = jnp.where(rows == r, acc, out)
                g = av[r:r + 1, :] * acc
            o_ref[pl.ds(base, SUBLANES), :] = out
            return g

        carry[...] = lax.fori_loop(0, n_slabs, slab, carry[...])

    t = pl.BlockSpec((tr, tc), lambda j, i: (nr - 1 - i, j))
    return _call(name, body, (C // tc, nr), [t, t], t, _sds((S, C), F32), scratch=[pltpu.VMEM((1, tc), F32)])(a, dh)


def _bucket_table(num_buckets):
    qi = np.arange(BLOCK)[:, None]
    kj = np.arange(2 * BLOCK)[None, :]
    dist = np.maximum(qi + BLOCK - kj, 0)
    d = np.maximum(dist, 1).astype(np.float64)
    large = NUM_EXACT + (np.log(d / NUM_EXACT) / math.log(MAX_DISTANCE / NUM_EXACT) * (num_buckets - NUM_EXACT)).astype(np.int32)
    large = np.minimum(large, num_buckets - 1)
    return np.where(dist < NUM_EXACT, dist, large).astype(np.int32)


def _bias_table(rel_bias, n_heads, *, name):
    nbk = rel_bias.shape[0]
    bucket = jnp.asarray(_bucket_table(nbk))

    def body(rb_ref, bk_ref, o_ref):
        h = pl.program_id(0)
        bk = bk_ref[...]
        acc = jnp.zeros((BLOCK, 2 * BLOCK), F32)
        for b in range(nbk):
            acc = jnp.where(bk == b, rb_ref[b, h], acc)
        o_ref[0] = acc

    return _call(name, body, (n_heads,),
                 [pl.BlockSpec(memory_space=pltpu.SMEM), pl.BlockSpec((BLOCK, 2 * BLOCK), lambda h: (0, 0))],
                 pl.BlockSpec((1, BLOCK, 2 * BLOCK), lambda h: (h, 0, 0)),
                 _sds((n_heads, BLOCK, 2 * BLOCK), F32), sem=("arbitrary",))(rel_bias, bucket)


def _bias_table_bwd(dbias, nbk, *, name):
    n_heads = dbias.shape[0]
    bucket = jnp.asarray(_bucket_table(nbk))

    def body(db_ref, bk_ref, o_ref):
        h = pl.program_id(0)
        bk = bk_ref[...]
        dv = db_ref[0]
        rows = lax.broadcasted_iota(jnp.int32, (nbk, LANES), 0)
        lanes = lax.broadcasted_iota(jnp.int32, (nbk, LANES), 1)
        acc = jnp.zeros((nbk, LANES), F32)
        for b in range(nbk):
            s = jnp.sum(jnp.sum(jnp.where(bk == b, dv, 0.0), axis=0, keepdims=True), axis=1, keepdims=True)
            acc = jnp.where(jnp.logical_and(rows == b, lanes == h), s, acc)
        _accum(o_ref, acc, h == 0)

    return _call(name, body, (n_heads,),
                 [pl.BlockSpec((1, BLOCK, 2 * BLOCK), lambda h: (h, 0, 0)), pl.BlockSpec((BLOCK, 2 * BLOCK), lambda h: (0, 0))],
                 pl.BlockSpec((nbk, LANES), lambda h: (0, 0)), _sds((nbk, LANES), F32), sem=("arbitrary",))(dbias, bucket)


def _band_mask(n):
    qi = lax.broadcasted_iota(jnp.int32, (BLOCK, 2 * BLOCK), 0)
    kj = lax.broadcasted_iota(jnp.int32, (BLOCK, 2 * BLOCK), 1)
    dist = qi + BLOCK - kj
    return (dist >= 0) & (dist < BLOCK) & ((n > 0) | (kj >= BLOCK))


def _head_probs(qm, kb, bias, sink, valid):
    s = lax.dot_general(qm, kb, (((1,), (1,)), ((), ())), preferred_element_type=F32) * (HEAD_DIM ** -0.5)
    s = jnp.where(valid, s + bias, NEG_INF)
    m = jnp.maximum(jnp.max(s, axis=-1, keepdims=True), sink)
    p = jnp.exp(s - m)
    es = jnp.exp(sink - m)
    inv = 1.0 / (jnp.sum(p, axis=-1, keepdims=True) + es)
    return p * inv, es * inv


def _attn_fwd(proj, bias, sinks_b, n_kv, group, q_off, k_off, v_off, *, name):
    S = proj.shape[0]
    nblk = S // BLOCK
    gw = group * HEAD_DIM
    pairs = group // 2

    def body(q_ref, kc_ref, kp_ref, vc_ref, vp_ref, b_ref, s_ref, o_ref):
        n = pl.program_id(1)
        kb = jnp.concatenate([kp_ref[...], kc_ref[...]], axis=0).astype(BF16)
        vb = jnp.concatenate([vp_ref[...], vc_ref[...]], axis=0).astype(BF16)
        valid = _band_mask(n)
        lo = lax.broadcasted_iota(jnp.int32, (BLOCK, LANES), 1) < HEAD_DIM
        for p in range(pairs):
            qp = q_ref[:, p * LANES:(p + 1) * LANES]
            outs = []
            for e in range(2):
                g = 2 * p + e
                qm = jnp.where(lo if e == 0 else jnp.logical_not(lo), qp, 0.0).astype(BF16)
                pr, _ = _head_probs(qm, kb, b_ref[g], s_ref[g][:, 0:1], valid)
                outs.append(jnp.dot(pr.astype(BF16), vb, preferred_element_type=F32))
            o_ref[:, p * LANES:(p + 1) * LANES] = jnp.where(lo, outs[0], outs[1]).astype(BF16)

    qb, kb0, vb0 = q_off // gw, k_off // LANES, v_off // LANES
    cur = lambda b0: pl.BlockSpec((BLOCK, LANES), lambda j, n: (n, b0 + j))
    prev = lambda b0: pl.BlockSpec((BLOCK, LANES), lambda j, n: (jnp.maximum(n - 1, 0), b0 + j))
    return _call(name, body, (n_kv, nblk),
                 [pl.BlockSpec((BLOCK, gw), lambda j, n: (n, qb + j)), cur(kb0), prev(kb0), cur(vb0), prev(vb0),
                  pl.BlockSpec((group, BLOCK, 2 * BLOCK), lambda j, n: (j, 0, 0)),
                  pl.BlockSpec((group, 1, LANES), lambda j, n: (j, 0, 0))],
                 pl.BlockSpec((BLOCK, gw), lambda j, n: (n, j)), _sds((S, n_kv * gw), BF16),
                 sem=("parallel", "parallel"))(proj, proj, proj, proj, proj, bias, sinks_b)


def _attn_bwd(proj, o, do, bias, sinks_b, n_kv, group, q_off, k_off, v_off, *, name):
    S = proj.shape[0]
    nblk = S // BLOCK
    gw = group * HEAD_DIM
    pairs = group // 2
    nt = (((1,), (1,)), ((), ()))
    tn = (((0,), (0,)), ((), ()))

    def body(q_ref, kc_ref, kp_ref, vc_ref, vp_ref, b_ref, s_ref, o_ref, do_ref,
             dq_ref, dk_ref, dv_ref, db_ref, ds_ref, ck, cv, sacc):
        n = pl.program_id(1)
        lo = lax.broadcasted_iota(jnp.int32, (BLOCK, LANES), 1) < HEAD_DIM
        lo2 = lax.broadcasted_iota(jnp.int32, (2 * BLOCK, LANES), 1) < HEAD_DIM
        lane = lax.broadcasted_iota(jnp.int32, (BLOCK, LANES), 1)

        @pl.when(n == 0)
        def _():
            ck[...] = jnp.zeros_like(ck)
            cv[...] = jnp.zeros_like(cv)
            sacc[...] = jnp.zeros_like(sacc)
            db_ref[...] = jnp.zeros_like(db_ref)

        @pl.when(n < nblk)
        def _():
            kb = jnp.concatenate([kp_ref[...], kc_ref[...]], axis=0).astype(BF16)
            vb = jnp.concatenate([vp_ref[...], vc_ref[...]], axis=0).astype(BF16)
            valid = _band_mask(n)
            dk_acc = jnp.zeros((2 * BLOCK, LANES), F32)
            dv_acc = jnp.zeros((2 * BLOCK, LANES), F32)
            sink_cols = jnp.zeros((BLOCK, LANES), F32)
            for p in range(pairs):
                sl = slice(p * LANES, (p + 1) * LANES)
                qp = q_ref[:, sl]
                dop = do_ref[:, sl].astype(F32)
                opair = o_ref[:, sl].astype(F32)
                dqs = []
                for e in range(2):
                    g = 2 * p + e
                    sel = lo if e == 0 else jnp.logical_not(lo)
                    qm = jnp.where(sel, qp, 0.0).astype(BF16)
                    dom = jnp.where(sel, dop, 0.0)
                    pr, ps = _head_probs(qm, kb, b_ref[g], s_ref[g][:, 0:1], valid)
                    delta = jnp.sum(dom * opair, axis=-1, keepdims=True)
                    domb = dom.astype(BF16)
                    dp = lax.dot_general(domb, vb, nt, preferred_element_type=F32)
                    ds = pr * (dp - delta)
                    db_ref[g] += ds
                    sink_cols = jnp.where(lane == g, -ps * delta, sink_cols)
                    dsb = (ds * (HEAD_DIM ** -0.5)).astype(BF16)
                    dqs.append(jnp.dot(dsb, kb, preferred_element_type=F32))
                    dk_acc = dk_acc + lax.dot_general(dsb, qm, tn, preferred_element_type=F32)
                    dv_acc = dv_acc + lax.dot_general(pr.astype(BF16), domb, tn, preferred_element_type=F32)
                dq_ref[:, sl] = jnp.where(lo, dqs[0], dqs[1]).astype(BF16)
            sacc[...] += sink_cols
            dkf = jnp.where(lo2, dk_acc + pltpu.roll(dk_acc, HEAD_DIM, 1), 0.0)
            dvf = jnp.where(lo2, dv_acc + pltpu.roll(dv_acc, HEAD_DIM, 1), 0.0)

            @pl.when(n > 0)
            def _():
                dk_ref[...] = (ck[...] + dkf[0:BLOCK, :]).astype(BF16)
                dv_ref[...] = (cv[...] + dvf[0:BLOCK, :]).astype(BF16)

            ck[...] = dkf[BLOCK:2 * BLOCK, :]
            cv[...] = dvf[BLOCK:2 * BLOCK, :]

        @pl.when(n == nblk)
        def _():
            dk_ref[...] = ck[...].astype(BF16)
            dv_ref[...] = cv[...].astype(BF16)
            ds_ref[0] = _colsum(sacc[...])

    qb, kb0, vb0 = q_off // gw, k_off // LANES, v_off // LANES
    last = nblk - 1
    cur = lambda b0: pl.BlockSpec((BLOCK, LANES), lambda j, n: (jnp.minimum(n, last), b0 + j))
    prev = lambda b0: pl.BlockSpec((BLOCK, LANES), lambda j, n: (jnp.clip(n - 1, 0, last), b0 + j))
    qspec = lambda b0: pl.BlockSpec((BLOCK, gw), lambda j, n: (jnp.minimum(n, last), b0 + j))
    kvout = pl.BlockSpec((BLOCK, LANES), lambda j, n: (jnp.maximum(n - 1, 0), j))
    return _call(name, body, (n_kv, nblk + 1),
                 [qspec(qb), cur(kb0), prev(kb0), cur(vb0), prev(vb0),
                  pl.BlockSpec((group, BLOCK, 2 * BLOCK), lambda j, n: (j, 0, 0)),
                  pl.BlockSpec((group, 1, LANES), lambda j, n: (j, 0, 0)), qspec(0), qspec(0)],
                 [qspec(0), kvout, kvout, pl.BlockSpec((group, BLOCK, 2 * BLOCK), lambda j, n: (j, 0, 0)),
                  pl.BlockSpec((1, 1, LANES), lambda j, n: (j, 0, 0))],
                 [_sds((S, n_kv * gw), BF16), _sds((S, n_kv * LANES), BF16), _sds((S, n_kv * LANES), BF16),
                  _sds((n_kv * group, BLOCK, 2 * BLOCK), F32), _sds((n_kv, 1, LANES), F32)],
                 scratch=[pltpu.VMEM((BLOCK, LANES), F32), pltpu.VMEM((BLOCK, LANES), F32), pltpu.VMEM((BLOCK, LANES), F32)])(
                     proj, proj, proj, proj, proj, bias, sinks_b, o, do)


def _adamw(w, g, m, v, *, name):
    R, C = w.shape
    tr = _pick(R, 128, SUBLANES)
    bc1 = 1.0 - ADAM_B1 ** ADAM_STEP
    bc2 = 1.0 - ADAM_B2 ** ADAM_STEP

    def body(w_ref, g_ref, m_ref, v_ref, d_ref, nm_ref, nv_ref):
        gv = g_ref[...]
        nm = ADAM_B1 * m_ref[...] + (1.0 - ADAM_B1) * gv
        nv = ADAM_B2 * v_ref[...] + (1.0 - ADAM_B2) * (gv * gv)
        d_ref[...] = -ADAM_LR * ((nm / bc1) / (jnp.sqrt(nv / bc2) + ADAM_EPS) + ADAM_WD * w_ref[...])
        nm_ref[...] = nm
        nv_ref[...] = nv

    t = pl.BlockSpec((tr, C), lambda i: (i, 0))
    o = _sds((R, C), F32)
    return _call(name, body, (R // tr,), [t, t, t, t], [t, t, t], [o, o, o], sem=("parallel",))(w, g, m, v)


def _sum_devices(packs, *, name):
    _, R, C = packs.shape
    tr = _pick(R, 512, SUBLANES)

    def body(p_ref, o_ref):
        acc = p_ref[0]
        for d in range(1, N_DEV):
            acc = acc + p_ref[d]
        o_ref[...] = acc

    return _call(name, body, (R // tr,), [pl.BlockSpec((N_DEV, tr, C), lambda i: (0, i, 0))],
                 pl.BlockSpec((tr, C), lambda i: (i, 0)), _sds((R, C), F32), sem=("parallel",))(packs)


class _Big:
    def __init__(self, kind, R, C):
        self.kind, self.R, self.C = kind, R, C
        self.hr = R // 2
        self.full = (R, N_CHIPS * C) if kind == "col" else (N_CHIPS * R, C)

    def region(self, ref, k, c):
        if self.kind == "col":
            return ref.at[pl.ds(c * self.hr, self.hr), pl.ds(k * self.C, self.C)]
        return ref.at[pl.ds(k * self.R + c * self.hr, self.hr), :]

    def shard(self, ref, k):
        if self.kind == "col":
            return ref.at[:, pl.ds(k * self.C, self.C)]
        return ref.at[pl.ds(k * self.R, self.R), :]


def _pair_sum(spec, g_full, land, c_arr, *, name):
    hr, C = spec.hr, spec.C
    tr = _pick(hr, 128, 16)
    nr = hr // tr

    def body(c_ref, g_ref, l_ref, o_ref):
        o_ref[0] = (g_ref[...] + l_ref[0]).astype(BF16)

    if spec.kind == "col":
        gspec = pl.BlockSpec((tr, C), lambda k, i, c_ref: (c_ref[0] * nr + i, k))
    else:
        gspec = pl.BlockSpec((tr, C), lambda k, i, c_ref: (k * 2 * nr + c_ref[0] * nr + i, 0))
    lspec = pl.BlockSpec((1, tr, C), lambda k, i, c_ref: (k, i, 0))
    return _call(name, body, (N_CHIPS, nr), [gspec, lspec], lspec, _sds((N_CHIPS, hr, C), BF16), nsp=1,
                 sem=("parallel", "parallel"))(c_arr, g_full, land)


def _chip_sum(spec, chipsum, land, k_arr, *, name):
    hr, C = spec.hr, spec.C
    tr = _pick(hr, 128, 16)

    def body(k_ref, s_ref, l_ref, o_ref):
        acc = s_ref[0].astype(F32)
        for j in range(N_CHIPS - 1):
            acc = acc + l_ref[j].astype(F32)
        o_ref[...] = acc

    return _call(name, body, (hr // tr,),
                 [pl.BlockSpec((1, tr, C), lambda i, k_ref: (k_ref[0], i, 0)),
                  pl.BlockSpec((N_CHIPS - 1, tr, C), lambda i, k_ref: (0, i, 0))],
                 pl.BlockSpec((tr, C), lambda i, k_ref: (i, 0)), _sds((hr, C), F32), nsp=1, sem=("parallel",))(k_arr, chipsum, land)


def _place():
    x, y, c = lax.axis_index("x"), lax.axis_index("y"), lax.axis_index("c")
    return x, y, c, [(1 - x, y), (x, 1 - y), (1 - x, 1 - y)]


def _remote(src, dst, send_sem, recv_sem, dev):
    return pltpu.make_async_remote_copy(src_ref=src, dst_ref=dst, send_sem=send_sem, recv_sem=recv_sem,
                                        device_id=dev, device_id_type=MESH)


def _comm_call(name, body, n_in, out_shapes, n_remote, n_local):
    scratch = [pltpu.SemaphoreType.DMA((n_remote,)), pltpu.SemaphoreType.DMA((n_remote,))]
    if n_local:
        scratch.append(pltpu.SemaphoreType.DMA((n_local,)))
    return pl.pallas_call(body, name=name, out_shape=out_shapes, in_specs=[ANY] * n_in,
                          out_specs=[ANY] * len(out_shapes), scratch_shapes=scratch)


def _all_gather(xs, *, name):
    n = len(xs)
    per = 7

    def body(*refs):
        ins, outs = refs[:n], refs[n:2 * n]
        send_sems, recv_sems, local_sems = refs[2 * n:]
        x, y, c, chips = _place()
        me, sibling = (x, y, c), (x, y, 1 - c)
        waits = []
        for w in range(n):
            m = ins[w].shape[0]

            def rows(px, py, pc, w=w, m=m):
                return outs[w].at[pl.ds((4 * px + 2 * py + pc) * m, m), :]

            def copy(k, block, to, src=None, w=w, rows=rows):
                return _remote(rows(*block) if src is None else src, rows(*block),
                               send_sems.at[w * per + k], recv_sems.at[w * per + k], to)

            mine = pltpu.make_async_copy(ins[w], rows(*me), local_sems.at[w])
            mine.start()
            first = [copy(0, me, sibling, src=ins[w])]
            first += [copy(1 + j, me, (*chip, c), src=ins[w]) for j, chip in enumerate(chips)]
            for cp in first:
                cp.start()
            waits.append((copy, mine, first))
        sends = []
        for w in range(n):
            copy, mine, first = waits[w]
            passed = [copy(4 + j, (*chip, c), sibling) for j, chip in enumerate(chips)]
            for j, chip in enumerate(chips):
                copy(1 + j, (*chip, c), me).wait_recv()
                passed[j].start()
            sends.append(first + passed)
        for w in range(n):
            copy, mine, first = waits[w]
            copy(0, sibling, me).wait_recv()
            for j, chip in enumerate(chips):
                copy(4 + j, (*chip, 1 - c), me).wait_recv()
            for cp in sends[w]:
                cp.wait_send()
            mine.wait()

    outs = [_sds((N_DEV * a.shape[0], a.shape[1]), a.dtype) for a in xs]
    return _comm_call(name, body, n, outs, per * n, n)(*xs)


def _gather_weights(shards, specs, *, name):
    n = len(shards)
    per = 6

    def body(*refs):
        ins, outs = refs[:n], refs[n:2 * n]
        send_sems, recv_sems, local_sems = refs[2 * n:]
        x, y, c, chips = _place()
        sibling = (x, y, 1 - c)
        k_me = 2 * x + y
        sends = []
        locals_ = []
        for w, sp in enumerate(specs):
            loc = pltpu.make_async_copy(ins[w], sp.shard(outs[w], k_me), local_sems.at[w])
            loc.start()
            locals_.append(loc)
            half = ins[w].at[pl.ds(c * sp.hr, sp.hr), :]
            for j, chip in enumerate(chips):
                cp = _remote(half, sp.region(outs[w], k_me, c), send_sems.at[w * per + j], recv_sems.at[w * per + j], (*chip, c))
                cp.start()
                sends.append(cp)
        for w, sp in enumerate(specs):
            for j, chip in enumerate(chips):
                kj = 2 * chip[0] + chip[1]
                got = sp.region(outs[w], kj, c)
                _remote(got, got, send_sems.at[w * per + j], recv_sems.at[w * per + j], (*chip, c)).wait_recv()
                cp = _remote(got, got, send_sems.at[w * per + 3 + j], recv_sems.at[w * per + 3 + j], sibling)
                cp.start()
                sends.append(cp)
        for w, sp in enumerate(specs):
            for j, chip in enumerate(chips):
                kj = 2 * chip[0] + chip[1]
                got = sp.region(outs[w], kj, 1 - c)
                _remote(got, got, send_sems.at[w * per + 3 + j], recv_sems.at[w * per + 3 + j], sibling).wait_recv()
        for cp in sends:
            cp.wait_send()
        for loc in locals_:
            loc.wait()

    outs = [_sds(sp.full, BF16) for sp in specs]
    return _comm_call(name, body, n, outs, per * n, n)(*shards)


def _pair_exchange(grads, specs, *, name):
    n = len(grads)

    def body(*refs):
        ins, outs = refs[:n], refs[n:2 * n]
        send_sems, recv_sems = refs[2 * n:]
        x, y, c, _ = _place()
        sibling = (x, y, 1 - c)
        cps = []
        for w, sp in enumerate(specs):
            for k in range(N_CHIPS):
                cp = _remote(sp.region(ins[w], k, 1 - c), outs[w].at[k], send_sems.at[w * N_CHIPS + k],
                             recv_sems.at[w * N_CHIPS + k], sibling)
                cp.start()
                cps.append(cp)
        for cp in cps:
            cp.wait()

    outs = [_sds((N_CHIPS, sp.hr, sp.C), F32) for sp in specs]
    return _comm_call(name, body, n, outs, N_CHIPS * n, 0)(*grads)


def _chip_exchange(chipsums, specs, *, name):
    n = len(chipsums)
    per = N_CHIPS - 1

    def body(*refs):
        ins, outs = refs[:n], refs[n:2 * n]
        send_sems, recv_sems = refs[2 * n:]
        x, y, c, chips = _place()
        cps = []
        for w in range(n):
            for j, chip in enumerate(chips):
                kj = 2 * chip[0] + chip[1]
                cp = _remote(ins[w].at[kj], outs[w].at[j], send_sems.at[w * per + j], recv_sems.at[w * per + j], (*chip, c))
                cp.start()
                cps.append(cp)
        for cp in cps:
            cp.wait()

    outs = [_sds((per, sp.hr, sp.C), BF16) for sp in specs]
    return _comm_call(name, body, n, outs, per * n, 0)(*chipsums)


def _pair_share(halves, specs, *, name):
    n = len(halves)

    def body(*refs):
        ins, outs = refs[:n], refs[n:2 * n]
        send_sems, recv_sems, local_sems = refs[2 * n:]
        x, y, c, _ = _place()
        sibling = (x, y, 1 - c)
        cps, locs = [], []
        for w, sp in enumerate(specs):
            mine = outs[w].at[pl.ds(c * sp.hr, sp.hr), :]
            loc = pltpu.make_async_copy(ins[w], mine, local_sems.at[w])
            loc.start()
            locs.append(loc)
            cp = _remote(ins[w], mine, send_sems.at[w], recv_sems.at[w], sibling)
            cp.start()
            cps.append(cp)
        for w, sp in enumerate(specs):
            theirs = outs[w].at[pl.ds((1 - c) * sp.hr, sp.hr), :]
            _remote(theirs, theirs, send_sems.at[w], recv_sems.at[w], sibling).wait_recv()
        for cp in cps:
            cp.wait_send()
        for loc in locs:
            loc.wait()

    outs = [_sds((sp.R, sp.C), F32) for sp in specs]
    return _comm_call(name, body, n, outs, n, n)(*halves)


PACK_ALIGN = SUBLANES * LANES


def _pack(arrs):
    flat = jnp.concatenate([a.reshape(-1) for a in arrs])
    pad = (-flat.shape[0]) % PACK_ALIGN
    return jnp.pad(flat, (0, pad)).reshape(-1, LANES)


def _unpack(packed, shapes):
    flat = packed.reshape(-1)
    out, pos = [], 0
    for s in shapes:
        n = int(np.prod(s))
        out.append(flat[pos:pos + n].reshape(s))
        pos += n
    return out


def kernel(x, c, w_ada, b_ada, norm1, w_in, rnn_conv_w, rnn_conv_b, w_rg_a, b_rg_a, w_rg_i, b_rg_i, rg_lambda, w_o_rnn, w_o_attn, attn_sinks, rel_bias, w_out, norm2, w_up, ffn_conv_w, ffn_conv_b, w_down, norm_f, loss_target, m_w_ada, m_b_ada, m_norm1, m_w_in, m_rnn_conv_w, m_rnn_conv_b, m_w_rg_a, m_b_rg_a, m_w_rg_i, m_b_rg_i, m_rg_lambda, m_w_o_rnn, m_w_o_attn, m_attn_sinks, m_rel_bias, m_w_out, m_norm2, m_w_up, m_ffn_conv_w, m_ffn_conv_b, m_w_down, m_norm_f, v_w_ada, v_b_ada, v_norm1, v_w_in, v_rnn_conv_w, v_rnn_conv_b, v_w_rg_a, v_b_rg_a, v_w_rg_i, v_b_rg_i, v_rg_lambda, v_w_o_rnn, v_w_o_attn, v_attn_sinks, v_rel_bias, v_w_out, v_norm2, v_w_up, v_ffn_conv_w, v_ffn_conv_b, v_w_down, v_norm_f):
    S, D = x.shape[1], x.shape[2]
    d_attn = N_CHIPS * w_o_attn.shape[1]
    d_rnn = N_CHIPS * w_o_rnn.shape[1]
    d_ff = N_CHIPS * w_down.shape[1]
    d_in = N_CHIPS * w_in.shape[2]
    n_heads = attn_sinks.shape[1]
    d_kv = (d_in - d_attn - 2 * d_rnn - 2 * D) // 2
    n_kv = d_kv // HEAD_DIM
    group = n_heads // n_kv
    nbk = rel_bias.shape[0]
    assert d_attn == n_heads * HEAD_DIM and group % 2 == 0 and S % BLOCK == 0

    mx, my, mc = lax.axis_index("x"), lax.axis_index("y"), lax.axis_index("c")
    k_me = 2 * mx + my
    dev = 2 * k_me + mc
    c_arr = jnp.reshape(mc, (1,)).astype(jnp.int32)
    k_arr = jnp.reshape(k_me, (1,)).astype(jnp.int32)

    xs, tgt = x[0], loss_target[0]

    big_names = ["w_in", "w_o_rnn", "w_o_attn", "w_out", "w_up", "w_down"]
    big_w = dict(w_in=w_in[0], w_o_rnn=w_o_rnn[0], w_o_attn=w_o_attn[0], w_out=w_out[0], w_up=w_up[0], w_down=w_down[0])
    big_kind = dict(w_in="col", w_o_rnn="row", w_o_attn="row", w_out="row", w_up="col", w_down="row")
    specs = {k: _Big(big_kind[k], *big_w[k].shape) for k in big_names}
    gathered = _gather_weights([big_w[k].astype(BF16) for k in big_names], [specs[k] for k in big_names], name="gather_weights")
    W = dict(zip(big_names, gathered))
    c_all, cw4, cw3 = _all_gather([jnp.broadcast_to(c, (SUBLANES, D)),
                                   jnp.pad(rnn_conv_w[0], ((0, SUBLANES - rnn_conv_w.shape[1]), (0, 0))),
                                   jnp.pad(ffn_conv_w[0], ((0, SUBLANES - ffn_conv_w.shape[1]), (0, 0)))], name="gather_cond")
    c_all = c_all.reshape(N_DEV, SUBLANES, D)[:, 0]

    def from_chips(g, taps):
        cs = g.shape[1]
        g = g.reshape(N_CHIPS, 2, SUBLANES, cs)[:, 0, :taps]
        return jnp.transpose(g, (1, 0, 2)).reshape(taps, N_CHIPS * cs)

    conv4_w = from_chips(cw4, rnn_conv_w.shape[1])
    conv3_w = from_chips(cw3, ffn_conv_w.shape[1])

    (silu_c,) = _ew("silu_c", lambda v: (v * _sigmoid(v),), [(c_all, 0)], [F32], N_DEV, D)
    mod_sh = _mm(silu_c, w_ada[0], name="mod", exact=True, bias=lax.dynamic_slice_in_dim(b_ada, k_me * w_ada.shape[2], w_ada.shape[2], 1))
    (mod_g,) = _all_gather([mod_sh], name="gather_mod")
    mod_all = jnp.transpose(mod_g.reshape(N_CHIPS, 2, N_DEV, -1)[:, 0], (1, 0, 2)).reshape(N_DEV, 6 * D)
    mod = lax.dynamic_slice_in_dim(mod_all, dev, 1, 0)
    shift1, scale1, gate1, shift2, scale2, gate2 = [mod[:, i * D:(i + 1) * D] for i in range(6)]

    o_k, o_v, o_xr = d_attn, d_attn + d_kv, d_attn + 2 * d_kv
    wi = W["w_in"]

    def doubled(cols):
        t = cols.reshape(D, n_kv, 1, HEAD_DIM)
        return jnp.concatenate([t, t], axis=2).reshape(D, n_kv * LANES)

    w_in_x = jnp.concatenate([wi[:, :o_k], doubled(wi[:, o_k:o_v]), doubled(wi[:, o_v:o_xr]), wi[:, o_xr:]], axis=1)
    e_k = d_attn
    e_v = e_k + n_kv * LANES
    e_xr = e_v + n_kv * LANES
    e_gr = e_xr + d_rnn
    e_ga = e_gr + d_rnn
    e_gl = e_ga + D
    d_ext = e_gl + D

    u = _adaln_fwd(xs, norm1, scale1, shift1, name="adaln1")
    proj = _mm(u, w_in_x, name="proj")
    bias = _bias_table(rel_bias, n_heads, name="bias_table")
    sinks_b = jnp.broadcast_to(attn_sinks.reshape(n_heads, 1, 1), (n_heads, 1, LANES))
    o_attn = _attn_fwd(proj, bias, sinks_b, n_kv, group, 0, e_k, e_v, name="attn_fwd")
    y_attn = _mm(o_attn, W["w_o_attn"], name="y_attn")
    xc = _conv_fwd(proj, e_xr, d_rnn, conv4_w, rnn_conv_b, name="conv4")
    a_t, inp = _gates_fwd(xc, w_rg_a[0], b_rg_a, w_rg_i[0], b_rg_i, rg_lambda, name="gates")
    h_rnn, h_prev = _scan_fwd(a_t, inp, name="scan")
    (z,) = _ew("rnn_gate", lambda h, g: (h * _gelu(g),), [(h_rnn, 0), (proj, e_gr)], [BF16], S, d_rnn)
    y_rnn = _mm(z, W["w_o_rnn"], name="y_rnn")
    (merged,) = _ew("merge", lambda ya, yr, ga, gl: (_sigmoid(ga) * ya + _sigmoid(gl) * yr,),
                    [(y_attn, 0), (y_rnn, 0), (proj, e_ga), (proj, e_gl)], [BF16], S, D)
    t1 = _mm(merged, W["w_out"], name="t1")
    u2, h1 = _adaln_fwd(xs, norm2, scale2, shift2, name="adaln2", t=t1, gate=gate1)
    up = _mm(u2, W["w_up"], name="up")
    a2 = _geglu_fwd(up, conv3_w, ffn_conv_b, d_ff, name="geglu")
    t2 = _mm(a2, W["w_down"], name="t2")
    dh2, dt2, loss_tile, g_norm_f, d_gate2 = _final(h1, t2, gate2, norm_f.reshape(1, D), tgt, name="final")

    da2 = _mm(dt2, W["w_down"], name="da2", tb=True)
    g_w_down = _mm(a2, dt2, name="g_w_down", ta=True)
    dg, dval = _geglu_bwd(up, conv3_w, ffn_conv_b, da2, d_ff, name="geglu_bwd")
    dupg, g_c3g, g_b3g = _conv_bwd(dg, up, 0, d_ff, conv3_w[:, :d_ff], name="conv3_bwd_g")
    dupv, g_c3v, g_b3v = _conv_bwd(dval, up, d_ff, d_ff, conv3_w[:, d_ff:], name="conv3_bwd_v")
    dup = jnp.concatenate([dupg, dupv], axis=1)
    g_conv3_w = jnp.concatenate([g_c3g, g_c3v], axis=1)
    g_conv3_b = jnp.concatenate([g_b3g, g_b3v], axis=1)
    du2 = _mm(dup, W["w_up"], name="du2", tb=True)
    g_w_up = _mm(u2, dup, name="g_w_up", ta=True)
    dh1, d_shift2, d_scale2, g_norm2, dt1, d_gate1 = _adaln_bwd(h1, du2, dh2, norm2, scale2, name="adaln2_bwd", t=t1, gate=gate1)

    dmerged = _mm(dt1, W["w_out"], name="dmerged", tb=True)
    g_w_out = _mm(merged, dt1, name="g_w_out", ta=True)

    def merge_bwd(dm, ya, yr, ga, gl):
        sa, sl = _sigmoid(ga), _sigmoid(gl)
        return dm * sa, dm * sl, dm * ya * sa * (1.0 - sa), dm * yr * sl * (1.0 - sl)

    dy_attn, dy_rnn, d_ga, d_gl = _ew("merge_bwd", merge_bwd, [(dmerged, 0), (y_attn, 0), (y_rnn, 0), (proj, e_ga), (proj, e_gl)],
                                      [BF16, BF16, BF16, BF16], S, D)
    do = _mm(dy_attn, W["w_o_attn"], name="do", tb=True, out_dtype=BF16)
    g_w_o_attn = _mm(o_attn, dy_attn, name="g_w_o_attn", ta=True)
    dz = _mm(dy_rnn, W["w_o_rnn"], name="dz", tb=True)
    g_w_o_rnn = _mm(z, dy_rnn, name="g_w_o_rnn", ta=True)

    def rnn_gate_bwd(dzv, h, g):
        ge, dge = _gelu_and_grad(g)
        return dzv * ge, dzv * h * dge

    dh_rnn, d_gr = _ew("rnn_gate_bwd", rnn_gate_bwd, [(dz, 0), (h_rnn, 0), (proj, e_gr)], [F32, BF16], S, d_rnn)
    dacc = _scan_bwd(a_t, dh_rnn, name="scan_bwd")
    dxc, g_w_rg_a, g_w_rg_i, g_b_rg_a, g_b_rg_i, g_lam = _gates_bwd(xc, dacc, h_prev, w_rg_a[0], b_rg_a, w_rg_i[0], b_rg_i,
                                                                    rg_lambda, name="gates_bwd")
    d_xr, g_conv4_w, g_conv4_b = _conv_bwd(dxc, proj, e_xr, d_rnn, conv4_w, name="conv4_bwd")
    dq, dk, dv, dbias, dsink = _attn_bwd(proj, o_attn, do, bias, sinks_b, n_kv, group, 0, e_k, e_v, name="attn_bwd")
    g_rel = _bias_table_bwd(dbias, nbk, name="bias_table_bwd")[:, :n_heads]
    g_sinks = dsink[:, 0, :group].reshape(1, n_heads)
    dproj = jnp.concatenate([dq, dk, dv, d_xr, d_gr, d_ga, d_gl], axis=1)
    du = _mm(dproj, w_in_x, name="du", tb=True)
    g_w_in_x = _mm(u, dproj, name="g_w_in", ta=True)

    def lower_half(cols):
        return cols.reshape(D, n_kv, 2, HEAD_DIM)[:, :, 0].reshape(D, d_kv)

    g_w_in = jnp.concatenate([g_w_in_x[:, :e_k], lower_half(g_w_in_x[:, e_k:e_v]), lower_half(g_w_in_x[:, e_v:e_xr]),
                              g_w_in_x[:, e_xr:]], axis=1)
    grad_x, d_shift1, d_scale1, g_norm1 = _adaln_bwd(xs, du, dh1, norm1, scale1, name="adaln1_bwd")
    dmod = jnp.concatenate([d_shift1, d_scale1, d_gate1, d_shift2, d_scale2, d_gate2], axis=1)

    small = [loss_tile[0:1, 0:1], dmod, g_norm1, g_conv4_w, g_conv4_b, g_w_rg_a, g_b_rg_a, g_w_rg_i, g_b_rg_i, g_lam,
             g_sinks, g_rel, g_norm2, g_conv3_w, g_conv3_b, g_norm_f]
    small_shapes = [a.shape for a in small]
    pack = _pack(small)
    (packs,) = _all_gather([pack], name="gather_small")
    packs = packs.reshape(N_DEV, -1, LANES)
    summed = _unpack(_sum_devices(packs, name="sum_small"), small_shapes)
    (loss_s, g_b_ada, g_norm1, g_conv4_w, g_conv4_b, g_w_rg_a, g_b_rg_a, g_w_rg_i, g_b_rg_i, g_lam,
     g_sinks, g_rel, g_norm2, g_conv3_w, g_conv3_b, g_norm_f) = summed
    loss = loss_s.reshape(())
    dmod_all = packs.reshape(N_DEV, -1)[:, 1:1 + 6 * D]
    cs_ada = w_ada.shape[2]
    g_w_ada = _mm(silu_c, lax.dynamic_slice_in_dim(dmod_all, k_me * cs_ada, cs_ada, 1), name="g_w_ada", ta=True, exact=True)
    cs4, cs3 = rnn_conv_w.shape[2], ffn_conv_w.shape[2]
    g_conv4_sh = lax.dynamic_slice_in_dim(g_conv4_w, k_me * cs4, cs4, 1)
    g_conv3_sh = lax.dynamic_slice_in_dim(g_conv3_w, k_me * cs3, cs3, 1)

    big_g = dict(w_in=g_w_in, w_o_rnn=g_w_o_rnn, w_o_attn=g_w_o_attn, w_out=g_w_out, w_up=g_w_up, w_down=g_w_down)
    sp_list = [specs[k] for k in big_names]
    landed = _pair_exchange([big_g[k] for k in big_names], sp_list, name="pair_exchange")
    chipsums = [_pair_sum(specs[k], big_g[k], l, c_arr, name="pair_sum_" + k) for k, l in zip(big_names, landed)]
    landed2 = _chip_exchange(chipsums, sp_list, name="chip_exchange")
    halves = [_chip_sum(specs[k], s, l, k_arr, name="chip_sum_" + k) for k, s, l in zip(big_names, chipsums, landed2)]
    shards = dict(zip(big_names, _pair_share(halves, sp_list, name="pair_share")))

    grads = dict(w_ada=g_w_ada[None], b_ada=g_b_ada, norm1=g_norm1, w_in=shards["w_in"][None], rnn_conv_w=g_conv4_sh[None],
                 rnn_conv_b=g_conv4_b, w_rg_a=g_w_rg_a[None], b_rg_a=g_b_rg_a, w_rg_i=g_w_rg_i[None], b_rg_i=g_b_rg_i,
                 rg_lambda=g_lam, w_o_rnn=shards["w_o_rnn"][None], w_o_attn=shards["w_o_attn"][None], attn_sinks=g_sinks,
                 rel_bias=g_rel, w_out=shards["w_out"][None], norm2=g_norm2, w_up=shards["w_up"][None],
                 ffn_conv_w=g_conv3_sh[None], ffn_conv_b=g_conv3_b, w_down=shards["w_down"][None], norm_f=g_norm_f.reshape(D))
    weights = dict(w_ada=w_ada, b_ada=b_ada, norm1=norm1, w_in=w_in, rnn_conv_w=rnn_conv_w, rnn_conv_b=rnn_conv_b, w_rg_a=w_rg_a,
                   b_rg_a=b_rg_a, w_rg_i=w_rg_i, b_rg_i=b_rg_i, rg_lambda=rg_lambda, w_o_rnn=w_o_rnn, w_o_attn=w_o_attn,
                   attn_sinks=attn_sinks, rel_bias=rel_bias, w_out=w_out, norm2=norm2, w_up=w_up, ffn_conv_w=ffn_conv_w,
                   ffn_conv_b=ffn_conv_b, w_down=w_down, norm_f=norm_f)
    moms = dict(w_ada=(m_w_ada, v_w_ada), b_ada=(m_b_ada, v_b_ada), norm1=(m_norm1, v_norm1), w_in=(m_w_in, v_w_in),
                rnn_conv_w=(m_rnn_conv_w, v_rnn_conv_w), rnn_conv_b=(m_rnn_conv_b, v_rnn_conv_b), w_rg_a=(m_w_rg_a, v_w_rg_a),
                b_rg_a=(m_b_rg_a, v_b_rg_a), w_rg_i=(m_w_rg_i, v_w_rg_i), b_rg_i=(m_b_rg_i, v_b_rg_i),
                rg_lambda=(m_rg_lambda, v_rg_lambda), w_o_rnn=(m_w_o_rnn, v_w_o_rnn), w_o_attn=(m_w_o_attn, v_w_o_attn),
                attn_sinks=(m_attn_sinks, v_attn_sinks), rel_bias=(m_rel_bias, v_rel_bias), w_out=(m_w_out, v_w_out),
                norm2=(m_norm2, v_norm2), w_up=(m_w_up, v_w_up), ffn_conv_w=(m_ffn_conv_w, v_ffn_conv_w),
                ffn_conv_b=(m_ffn_conv_b, v_ffn_conv_b), w_down=(m_w_down, v_w_down), norm_f=(m_norm_f, v_norm_f))
    names = list(weights)
    grads = {k: grads[k].reshape(weights[k].shape) for k in names}
    large = ["w_ada"] + big_names
    delta, new_m, new_v = {}, {}, {}
    for k in large:
        shp = weights[k].shape
        two = lambda a: a.reshape(shp[-2], shp[-1])
        d_, m_, v_ = _adamw(two(weights[k]), two(grads[k]), two(moms[k][0]), two(moms[k][1]), name="adamw_" + k)
        delta[k], new_m[k], new_v[k] = d_.reshape(shp), m_.reshape(shp), v_.reshape(shp)
    rest = [k for k in names if k not in large]
    rest_shapes = [weights[k].shape for k in rest]
    d_, m_, v_ = _adamw(_pack([weights[k] for k in rest]), _pack([grads[k] for k in rest]),
                        _pack([moms[k][0] for k in rest]), _pack([moms[k][1] for k in rest]), name="adamw_small")
    for k, dd, mm_, vv in zip(rest, _unpack(d_, rest_shapes), _unpack(m_, rest_shapes), _unpack(v_, rest_shapes)):
        delta[k], new_m[k], new_v[k] = dd, mm_, vv

    return (loss, grad_x[None], *[grads[k] for k in names], *[delta[k] for k in names],
            *[new_m[k] for k in names], *[new_v[k] for k in names])
```

```python
import math

import numpy as np
import jax
import jax.numpy as jnp
from jax import lax
from jax.experimental import pallas as pl
from jax.experimental.pallas import tpu as pltpu

F32 = jnp.float32
BF16 = jnp.bfloat16
MESH = pl.DeviceIdType.MESH
ANY = pl.BlockSpec(memory_space=pl.ANY)

EPS = 1e-6
NEG_INF = -1e30
HEAD_DIM = 64
BLOCK = 128
NUM_EXACT = 16
MAX_DISTANCE = 128
RG_C = 8.0
ADAM_LR, ADAM_B1, ADAM_B2, ADAM_EPS, ADAM_WD, ADAM_STEP = 0.001, 0.9, 0.999, 1e-08, 0.01, 10
N_CHIPS = 4
N_DEV = 8
SUBLANES = 8
LANES = 128
VMEM_LIMIT_BYTES = 48 * 1024 * 1024
GELU_C0 = math.sqrt(2.0 / math.pi)
GELU_C1 = 0.044715


def _pick(dim, pref, align):
    if dim <= pref:
        return dim
    t = (pref // align) * align
    while t >= align:
        if dim % t == 0:
            return t
        t -= align
    return dim


def _params(sem):
    return pltpu.CompilerParams(dimension_semantics=sem, vmem_limit_bytes=VMEM_LIMIT_BYTES)


def _call(name, body, grid, in_specs, out_specs, out_shape, scratch=(), nsp=0, sem=None):
    sem = sem or ("parallel",) * (len(grid) - 1) + ("arbitrary",)
    if nsp:
        gs = pltpu.PrefetchScalarGridSpec(num_scalar_prefetch=nsp, grid=grid, in_specs=in_specs,
                                          out_specs=out_specs, scratch_shapes=list(scratch))
        return pl.pallas_call(body, name=name, grid_spec=gs, out_shape=out_shape, compiler_params=_params(sem))
    return pl.pallas_call(body, name=name, grid=grid, in_specs=in_specs, out_specs=out_specs,
                          out_shape=out_shape, scratch_shapes=list(scratch), compiler_params=_params(sem))


def _sds(shape, dtype):
    return jax.ShapeDtypeStruct(shape, dtype)


def _T(tr, tc, off=0):
    return pl.BlockSpec((tr, tc), lambda j, i: (i, j + off))


def _P(rows, tc, off=0):
    return pl.BlockSpec((rows, tc), lambda j, i: (0, j + off))


def _gelu(x):
    t = jnp.tanh(GELU_C0 * (x + GELU_C1 * x * x * x))
    return 0.5 * x * (1.0 + t)


def _gelu_and_grad(x):
    t = jnp.tanh(GELU_C0 * (x + GELU_C1 * x * x * x))
    g = 0.5 * x * (1.0 + t)
    dg = 0.5 * (1.0 + t) + 0.5 * x * (1.0 - t * t) * GELU_C0 * (1.0 + 3.0 * GELU_C1 * x * x)
    return g, dg


def _sigmoid(x):
    return 1.0 / (1.0 + jnp.exp(-x))


def _accum(ref, val, first):
    @pl.when(first)
    def _():
        ref[...] = val

    @pl.when(jnp.logical_not(first))
    def _():
        ref[...] += val


def _colsum(v):
    return jnp.sum(v, axis=0, keepdims=True)


def _mm(a, b, *, name, ta=False, tb=False, out_dtype=F32, tm=1024, tn=1024, tk=2048, exact=False, bias=None):
    if ta:
        K, M = a.shape
    else:
        M, K = a.shape
    if tb:
        N, K2 = b.shape
    else:
        K2, N = b.shape
    assert K == K2, (a.shape, b.shape, ta, tb)
    tm, tn, tk = _pick(M, tm, LANES), _pick(N, tn, LANES), _pick(K, tk, LANES)
    nk = K // tk
    cdt = F32 if exact else BF16
    prec = lax.Precision.HIGHEST if exact else None
    dims = (((0 if ta else 1,), (1 if tb else 0,)), ((), ()))

    def body(*refs):
        a_ref, b_ref = refs[0], refs[1]
        bias_ref = refs[2] if bias is not None else None
        o_ref = refs[3] if bias is not None else refs[2]
        part = lax.dot_general(a_ref[...].astype(cdt), b_ref[...].astype(cdt), dims,
                               preferred_element_type=F32, precision=prec)

        def finish(r):
            if bias is not None:
                r = r + bias_ref[...]
            o_ref[...] = r.astype(out_dtype)

        if nk == 1:
            finish(part)
            return
        acc_ref = refs[-1]
        k = pl.program_id(2)

        @pl.when(k == 0)
        def _():
            acc_ref[...] = part

        @pl.when(jnp.logical_and(k > 0, k < nk - 1))
        def _():
            acc_ref[...] += part

        @pl.when(k == nk - 1)
        def _():
            finish(acc_ref[...] + part)

    a_spec = pl.BlockSpec((tk, tm), lambda i, j, k: (k, i)) if ta else pl.BlockSpec((tm, tk), lambda i, j, k: (i, k))
    b_spec = pl.BlockSpec((tn, tk), lambda i, j, k: (j, k)) if tb else pl.BlockSpec((tk, tn), lambda i, j, k: (k, j))
    in_specs, args = [a_spec, b_spec], [a, b]
    if bias is not None:
        in_specs.append(pl.BlockSpec((1, tn), lambda i, j, k: (0, j)))
        args.append(bias)
    scratch = [pltpu.VMEM((tm, tn), F32)] if nk > 1 else []
    return _call(name, body, (M // tm, N // tn, nk), in_specs, pl.BlockSpec((tm, tn), lambda i, j, k: (i, j)),
                 _sds((M, N), out_dtype), scratch=scratch, sem=("parallel", "parallel", "arbitrary"))(*args)


def _ew(name, fn, ins, out_dtypes, S, C, rows=(), tr=256, tc=512):
    g = C
    for _, off in ins:
        g = math.gcd(g, off) if off else g
    tc = _pick(g, tc, LANES)
    tr = _pick(S, tr, 16)
    n_in, n_row, n_out = len(ins), len(rows), len(out_dtypes)

    def body(*refs):
        vals = [r[...] for r in refs[:n_in + n_row]]
        outs = fn(*vals)
        for o_ref, o in zip(refs[n_in + n_row:], outs):
            o_ref[...] = o.astype(o_ref.dtype)

    in_specs = [_T(tr, tc, off // tc) for _, off in ins] + [_P(1, tc) for _ in rows]
    res = _call(name, body, (C // tc, S // tr), in_specs, [_T(tr, tc) for _ in out_dtypes],
                [_sds((S, C), d) for d in out_dtypes], sem=("parallel", "parallel"))(*[a for a, _ in ins], *rows)
    return res


def _adaln_fwd(x, norm, scale, shift, *, name, t=None, gate=None):
    S, D = x.shape
    tr = _pick(S, 256, 16)
    resid = t is not None

    def body(*refs):
        if resid:
            x_ref, t_ref, g_ref, n_ref, sc_ref, sh_ref, u_ref, h_ref = refs
            h = x_ref[...] + g_ref[...] * t_ref[...]
            h_ref[...] = h
        else:
            x_ref, n_ref, sc_ref, sh_ref, u_ref = refs
            h = x_ref[...]
        r = lax.rsqrt(jnp.mean(h * h, axis=-1, keepdims=True) + EPS)
        u_ref[...] = (h * r * (n_ref[...] * (1.0 + sc_ref[...])) + sh_ref[...]).astype(BF16)

    full, row = _T(tr, D), _P(1, D)
    if resid:
        return _call(name, body, (1, S // tr), [full, full, row, row, row, row], [full, full],
                     [_sds((S, D), BF16), _sds((S, D), F32)])(x, t, gate, norm, scale, shift)
    return _call(name, body, (1, S // tr), [full, row, row, row], full, _sds((S, D), BF16))(x, norm, scale, shift)


def _adaln_bwd(h, du, dres, norm, scale, *, name, t=None, gate=None):
    S, D = h.shape
    tr = _pick(S, 256, 16)
    gated = t is not None

    def body(*refs):
        if gated:
            h_ref, du_ref, dr_ref, n_ref, sc_ref, t_ref, g_ref, dh_ref, dsh_ref, dsc_ref, dn_ref, dt_ref, dg_ref = refs
        else:
            h_ref, du_ref, dr_ref, n_ref, sc_ref, dh_ref, dsh_ref, dsc_ref, dn_ref = refs
        first = pl.program_id(1) == 0
        hv, duv = h_ref[...], du_ref[...]
        r = lax.rsqrt(jnp.mean(hv * hv, axis=-1, keepdims=True) + EPS)
        xn = hv * r
        one_sc = 1.0 + sc_ref[...]
        dxn = duv * (n_ref[...] * one_sc)
        dh = dr_ref[...] + r * (dxn - xn * jnp.mean(dxn * xn, axis=-1, keepdims=True))
        dh_ref[...] = dh
        dux = duv * xn
        _accum(dsh_ref, _colsum(duv), first)
        _accum(dsc_ref, _colsum(dux * n_ref[...]), first)
        _accum(dn_ref, _colsum(dux * one_sc), first)
        if gated:
            dt_ref[...] = (dh * g_ref[...]).astype(BF16)
            _accum(dg_ref, _colsum(dh * t_ref[...]), first)

    full, row = _T(tr, D), _P(1, D)
    rowo = _sds((1, D), F32)
    if gated:
        return _call(name, body, (1, S // tr), [full, full, full, row, row, full, row],
                     [full, row, row, row, full, row],
                     [_sds((S, D), F32), rowo, rowo, rowo, _sds((S, D), BF16), rowo])(h, du, dres, norm, scale, t, gate)
    return _call(name, body, (1, S // tr), [full, full, full, row, row], [full, row, row, row],
                 [_sds((S, D), F32), rowo, rowo, rowo])(h, du, dres, norm, scale)


def _final(h1, t2, gate2, norm_f, tgt, *, name):
    S, D = h1.shape
    tr = _pick(S, 256, 16)

    def body(h_ref, t_ref, g_ref, n_ref, y_ref, dh_ref, dt_ref, loss_ref, dn_ref, dg_ref):
        first = pl.program_id(1) == 0
        tv = t_ref[...]
        h2 = h_ref[...] + g_ref[...] * tv
        r = lax.rsqrt(jnp.mean(h2 * h2, axis=-1, keepdims=True) + EPS)
        xn = h2 * r
        e = xn * n_ref[...] - y_ref[...]
        part = 0.5 * jnp.sum(jnp.mean(e * e, axis=-1, keepdims=True), axis=0, keepdims=True)
        _accum(loss_ref, jnp.broadcast_to(part, (SUBLANES, LANES)), first)
        dy = e * (1.0 / D)
        _accum(dn_ref, _colsum(dy * xn), first)
        dxn = dy * n_ref[...]
        dh2 = r * (dxn - xn * jnp.mean(dxn * xn, axis=-1, keepdims=True))
        dh_ref[...] = dh2
        dt_ref[...] = (dh2 * g_ref[...]).astype(BF16)
        _accum(dg_ref, _colsum(dh2 * tv), first)

    full, row = _T(tr, D), _P(1, D)
    rowo = _sds((1, D), F32)
    return _call(name, body, (1, S // tr), [full, full, row, row, full],
                 [full, full, _P(SUBLANES, LANES), row, row],
                 [_sds((S, D), F32), _sds((S, D), BF16), _sds((SUBLANES, LANES), F32), rowo, rowo])(h1, t2, gate2, norm_f, tgt)


def _halo_prev(tr, tc, off=0):
    return pl.BlockSpec((SUBLANES, tc), lambda j, i: (jnp.maximum(i * (tr // SUBLANES) - 1, 0), j + off))


def _halo_next(tr, tc, n_slabs, off=0):
    return pl.BlockSpec((SUBLANES, tc), lambda j, i: (jnp.minimum((i + 1) * (tr // SUBLANES), n_slabs - 1), j + off))


def _past(buf, prev_ref, cur, first, tr, taps):
    buf[0:SUBLANES, :] = jnp.where(first, 0.0, prev_ref[...])
    buf[SUBLANES:SUBLANES + tr, :] = cur
    return [cur] + [buf[SUBLANES - d:SUBLANES - d + tr, :] for d in range(1, taps)]


def _future(buf, next_ref, cur, last, tr, taps):
    buf[0:tr, :] = cur
    buf[tr:tr + SUBLANES, :] = jnp.where(last, 0.0, next_ref[...])
    return [cur] + [buf[d:d + tr, :] for d in range(1, taps)]


def _conv_apply(xs, w_ref, b_ref):
    taps = len(xs)
    y = b_ref[...] + w_ref[taps - 1:taps, :] * xs[0]
    for d in range(1, taps):
        y = y + w_ref[taps - 1 - d:taps - d, :] * xs[d]
    return y


def _conv_fwd(x, off, C, w, b, *, name):
    S = x.shape[0]
    taps = w.shape[0]
    tc = _pick(math.gcd(C, off) if off else C, 512, LANES)
    tr = _pick(S, 256, 16)

    def body(x_ref, p_ref, w_ref, b_ref, y_ref, buf):
        xs = _past(buf, p_ref, x_ref[...], pl.program_id(1) == 0, tr, taps)
        y_ref[...] = _conv_apply(xs, w_ref, b_ref)

    return _call(name, body, (C // tc, S // tr),
                 [_T(tr, tc, off // tc), _halo_prev(tr, tc, off // tc), _P(taps, tc), _P(1, tc)],
                 _T(tr, tc), _sds((S, C), F32), scratch=[pltpu.VMEM((tr + SUBLANES, tc), F32)],
                 sem=("parallel", "parallel"))(x, x, w, b)


def _conv_bwd(dy, x, off, C, w, *, name):
    S = dy.shape[0]
    taps = w.shape[0]
    tc = _pick(math.gcd(C, off) if off else C, 512, LANES)
    tr = _pick(S, 256, 16)
    n_slabs = S // SUBLANES

    def body(dy_ref, nx_ref, x_ref, p_ref, w_ref, dx_ref, dw_ref, db_ref, fbuf, pbuf):
        i = pl.program_id(1)
        first = i == 0
        dyv = dy_ref[...]
        dys = _future(fbuf, nx_ref, dyv, i == S // tr - 1, tr, taps)
        dx = w_ref[taps - 1:taps, :] * dys[0]
        for d in range(1, taps):
            dx = dx + w_ref[taps - 1 - d:taps - d, :] * dys[d]
        dx_ref[...] = dx.astype(BF16)
        xs = _past(pbuf, p_ref, x_ref[...], first, tr, taps)

        @pl.when(first)
        def _():
            dw_ref[...] = jnp.zeros_like(dw_ref)

        for d in range(taps):
            dw_ref[taps - 1 - d:taps - d, :] += _colsum(dyv * xs[d])
        _accum(db_ref, _colsum(dyv), first)

    return _call(name, body, (C // tc, S // tr),
                 [_T(tr, tc), _halo_next(tr, tc, n_slabs), _T(tr, tc, off // tc), _halo_prev(tr, tc, off // tc),
                  _P(taps, tc)],
                 [_T(tr, tc), _P(taps, tc), _P(1, tc)],
                 [_sds((S, C), BF16), _sds((taps, C), F32), _sds((1, C), F32)],
                 scratch=[pltpu.VMEM((tr + SUBLANES, tc), F32), pltpu.VMEM((tr + SUBLANES, tc), F32)])(dy, dy, x, x, w)


def _geglu_fwd(up, w, b, F, *, name):
    S = up.shape[0]
    taps = w.shape[0]
    tc = _pick(F, 512, LANES)
    tr = _pick(S, 256, 16)
    nf = F // tc

    def body(g_ref, gp_ref, v_ref, vp_ref, wg_ref, wv_ref, bg_ref, bv_ref, a_ref, gbuf, vbuf):
        first = pl.program_id(1) == 0
        g = _conv_apply(_past(gbuf, gp_ref, g_ref[...], first, tr, taps), wg_ref, bg_ref)
        v = _conv_apply(_past(vbuf, vp_ref, v_ref[...], first, tr, taps), wv_ref, bv_ref)
        a_ref[...] = (_gelu(g) * v).astype(BF16)

    buf = pltpu.VMEM((tr + SUBLANES, tc), F32)
    return _call(name, body, (nf, S // tr),
                 [_T(tr, tc), _halo_prev(tr, tc), _T(tr, tc, nf), _halo_prev(tr, tc, nf),
                  _P(taps, tc), _P(taps, tc, nf), _P(1, tc), _P(1, tc, nf)],
                 _T(tr, tc), _sds((S, F), BF16), scratch=[buf, buf], sem=("parallel", "parallel"))(up, up, up, up, w, w, b, b)


def _geglu_bwd(up, w, b, da, F, *, name):
    S = up.shape[0]
    taps = w.shape[0]
    tc = _pick(F, 512, LANES)
    tr = _pick(S, 256, 16)
    nf = F // tc

    def body(g_ref, gp_ref, v_ref, vp_ref, wg_ref, wv_ref, bg_ref, bv_ref, da_ref, dg_ref, dv_ref, gbuf, vbuf):
        first = pl.program_id(1) == 0
        g = _conv_apply(_past(gbuf, gp_ref, g_ref[...], first, tr, taps), wg_ref, bg_ref)
        v = _conv_apply(_past(vbuf, vp_ref, v_ref[...], first, tr, taps), wv_ref, bv_ref)
        ge, dge = _gelu_and_grad(g)
        dav = da_ref[...]
        dg_ref[...] = dav * v * dge
        dv_ref[...] = dav * ge

    buf = pltpu.VMEM((tr + SUBLANES, tc), F32)
    return _call(name, body, (nf, S // tr),
                 [_T(tr, tc), _halo_prev(tr, tc), _T(tr, tc, nf), _halo_prev(tr, tc, nf),
                  _P(taps, tc), _P(taps, tc, nf), _P(1, tc), _P(1, tc, nf), _T(tr, tc)],
                 [_T(tr, tc), _T(tr, tc)], [_sds((S, F), F32), _sds((S, F), F32)],
                 scratch=[buf, buf], sem=("parallel", "parallel"))(up, up, up, up, w, w, b, b, da)


def _softplus_neg(lam):
    z = -lam
    e = jnp.exp(-jnp.abs(z))
    u = 1.0 + e
    log1p_e = jnp.where(u == 1.0, e, jnp.log(u) * e / jnp.where(u == 1.0, 1.0, u - 1.0))
    sp = jnp.maximum(z, 0.0) + log1p_e
    sg = jnp.where(z >= 0, 1.0 / u, e / u)
    return sp, sg


def _one_minus_exp(x):
    series = -x * (1.0 + x * (0.5 + x * (1.0 / 6.0 + x * (1.0 / 24.0))))
    return jnp.where(x > -0.01, series, 1.0 - jnp.exp(x))


def _gate_values(xc, wa_ref, wi_ref, ba_ref, bi_ref, lam_ref, is_t0):
    xb = xc.astype(BF16)
    ra = _sigmoid(jnp.dot(xb, wa_ref[0].astype(BF16), preferred_element_type=F32) + ba_ref[...])
    ia = _sigmoid(jnp.dot(xb, wi_ref[0].astype(BF16), preferred_element_type=F32) + bi_ref[...])
    sp, sg = _softplus_neg(lam_ref[...])
    log_a = -RG_C * ra * sp
    a = jnp.exp(log_a)
    mult = jnp.where(is_t0, 1.0, jnp.sqrt(_one_minus_exp(2.0 * log_a)))
    return ra, ia, sp, sg, a, mult


def _rnn_blockspecs(tr):
    x = pl.BlockSpec((tr, LANES), lambda n, i: (i, n))
    w = pl.BlockSpec((1, LANES, LANES), lambda n, i: (n, 0, 0))
    p = pl.BlockSpec((1, LANES), lambda n, i: (0, n))
    return x, w, p


def _is_t0(tr):
    rows = lax.broadcasted_iota(jnp.int32, (tr, LANES), 0)
    return jnp.logical_and(pl.program_id(1) == 0, rows == 0)


def _gates_fwd(xc, w_a, b_a, w_i, b_i, lam, *, name):
    S, C = xc.shape
    tr = _pick(S, 1024, 16)

    def body(x_ref, wa_ref, wi_ref, ba_ref, bi_ref, lam_ref, a_ref, inp_ref):
        xv = x_ref[...]
        _, ia, _, _, a, mult = _gate_values(xv, wa_ref, wi_ref, ba_ref, bi_ref, lam_ref, _is_t0(tr))
        a_ref[...] = a
        inp_ref[...] = mult * ia * xv

    x, w, p = _rnn_blockspecs(tr)
    return _call(name, body, (C // LANES, S // tr), [x, w, w, p, p, p], [x, x],
                 [_sds((S, C), F32), _sds((S, C), F32)], sem=("parallel", "parallel"))(xc, w_a, w_i, b_a, b_i, lam)


def _gates_bwd(xc, dacc, hprev, w_a, b_a, w_i, b_i, lam, *, name):
    S, C = xc.shape
    nb = C // LANES
    tr = _pick(S, 1024, 16)

    def body(x_ref, d_ref, hp_ref, wa_ref, wi_ref, ba_ref, bi_ref, lam_ref,
             dx_ref, dwa_ref, dwi_ref, dba_ref, dbi_ref, dlam_ref):
        first = pl.program_id(1) == 0
        t0 = _is_t0(tr)
        xv, dv = x_ref[...], d_ref[...]
        ra, ia, sp, sg, a, mult = _gate_values(xv, wa_ref, wi_ref, ba_ref, bi_ref, lam_ref, t0)
        d_a = dv * hp_ref[...]
        d_mult = dv * ia * xv
        d_ia = dv * mult * xv
        d_log = d_a * a + jnp.where(t0, 0.0, -d_mult * (a * a) / mult)
        d_pa = d_log * (-RG_C * sp) * ra * (1.0 - ra)
        d_pi = d_ia * ia * (1.0 - ia)
        xb, dab, dib = xv.astype(BF16), d_pa.astype(BF16), d_pi.astype(BF16)
        nt = (((1,), (1,)), ((), ()))
        tn = (((0,), (0,)), ((), ()))
        dx_ref[...] = (dv * mult * ia
                       + lax.dot_general(dab, wa_ref[0].astype(BF16), nt, preferred_element_type=F32)
                       + lax.dot_general(dib, wi_ref[0].astype(BF16), nt, preferred_element_type=F32))
        _accum(dwa_ref, lax.dot_general(xb, dab, tn, preferred_element_type=F32)[None], first)
        _accum(dwi_ref, lax.dot_general(xb, dib, tn, preferred_element_type=F32)[None], first)
        _accum(dba_ref, _colsum(d_pa), first)
        _accum(dbi_ref, _colsum(d_pi), first)
        _accum(dlam_ref, _colsum(d_log * (-RG_C) * ra) * (-sg), first)

    x, w, p = _rnn_blockspecs(tr)
    row = _sds((1, C), F32)
    return _call(name, body, (nb, S // tr), [x, x, x, w, w, p, p, p], [x, w, w, p, p, p],
                 [_sds((S, C), F32), _sds((nb, LANES, LANES), F32), _sds((nb, LANES, LANES), F32), row, row, row])(
                     xc, dacc, hprev, w_a, w_i, b_a, b_i, lam)


def _scan_fwd(a, inp, *, name):
    S, C = a.shape
    tc = _pick(C, 1280, LANES)
    tr = _pick(S, 256, SUBLANES)

    def body(a_ref, b_ref, h_ref, hp_ref, carry):
        @pl.when(pl.program_id(1) == 0)
        def _():
            carry[...] = jnp.zeros_like(carry)

        rows = lax.broadcasted_iota(jnp.int32, (SUBLANES, tc), 0)

        def slab(s, h):
            base = pl.multiple_of(s * SUBLANES, SUBLANES)
            av, bv = a_ref[pl.ds(base, SUBLANES), :], b_ref[pl.ds(base, SUBLANES), :]
            ho = jnp.zeros((SUBLANES, tc), F32)
            po = jnp.zeros((SUBLANES, tc), F32)
            for r in range(SUBLANES):
                po = jnp.where(rows == r, h, po)
                h = av[r:r + 1, :] * h + bv[r:r + 1, :]
                ho = jnp.where(rows == r, h, ho)
            h_ref[pl.ds(base, SUBLANES), :] = ho
            hp_ref[pl.ds(base, SUBLANES), :] = po
            return h

        carry[...] = lax.fori_loop(0, tr // SUBLANES, slab, carry[...])

    t = _T(tr, tc)
    return _call(name, body, (C // tc, S // tr), [t, t], [t, t], [_sds((S, C), F32), _sds((S, C), F32)],
                 scratch=[pltpu.VMEM((1, tc), F32)])(a, inp)


def _scan_bwd(a, dh, *, name):
    S, C = a.shape
    tc = _pick(C, 1280, LANES)
    tr = _pick(S, 256, SUBLANES)
    nr = S // tr

    def body(a_ref, d_ref, o_ref, carry):
        @pl.when(pl.program_id(1) == 0)
        def _():
            carry[...] = jnp.zeros_like(carry)

        rows = lax.broadcasted_iota(jnp.int32, (SUBLANES, tc), 0)
        n_slabs = tr // SUBLANES

        def slab(s, g):
            base = pl.multiple_of((n_slabs - 1 - s) * SUBLANES, SUBLANES)
            av, dv = a_ref[pl.ds(base, SUBLANES), :], d_ref[pl.ds(base, SUBLANES), :]
            out = jnp.zeros((SUBLANES, tc), F32)
            for r in range(SUBLANES - 1, -1, -1):
                acc = dv[r:r + 1, :] + g
                out = jnp.where(rows == r, acc, out)
                g = av[r:r + 1, :] * acc
            o_ref[pl.ds(base, SUBLANES), :] = out
            return g

        carry[...] = lax.fori_loop(0, n_slabs, slab, carry[...])

    t = pl.BlockSpec((tr, tc), lambda j, i: (nr - 1 - i, j))
    return _call(name, body, (C // tc, nr), [t, t], t, _sds((S, C), F32), scratch=[pltpu.VMEM((1, tc), F32)])(a, dh)


def _bucket_table(num_buckets):
    qi = np.arange(BLOCK)[:, None]
    kj = np.arange(2 * BLOCK)[None, :]
    dist = np.maximum(qi + BLOCK - kj, 0)
    d = np.maximum(dist, 1).astype(np.float64)
    large = NUM_EXACT + (np.log(d / NUM_EXACT) / math.log(MAX_DISTANCE / NUM_EXACT) * (num_buckets - NUM_EXACT)).astype(np.int32)
    large = np.minimum(large, num_buckets - 1)
    return np.where(dist < NUM_EXACT, dist, large).astype(np.int32)


def _bias_table(rel_bias, n_heads, *, name):
    nbk = rel_bias.shape[0]
    bucket = jnp.asarray(_bucket_table(nbk))

    def body(rb_ref, bk_ref, o_ref):
        h = pl.program_id(0)
        bk = bk_ref[...]
        acc = jnp.zeros((BLOCK, 2 * BLOCK), F32)
        for b in range(nbk):
            acc = jnp.where(bk == b, rb_ref[b, h], acc)
        o_ref[0] = acc

    return _call(name, body, (n_heads,),
                 [pl.BlockSpec(memory_space=pltpu.SMEM), pl.BlockSpec((BLOCK, 2 * BLOCK), lambda h: (0, 0))],
                 pl.BlockSpec((1, BLOCK, 2 * BLOCK), lambda h: (h, 0, 0)),
                 _sds((n_heads, BLOCK, 2 * BLOCK), F32), sem=("arbitrary",))(rel_bias, bucket)


def _bias_table_bwd(dbias, nbk, *, name):
    n_heads = dbias.shape[0]
    bucket = jnp.asarray(_bucket_table(nbk))

    def body(db_ref, bk_ref, o_ref):
        h = pl.program_id(0)
        bk = bk_ref[...]
        dv = db_ref[0]
        rows = lax.broadcasted_iota(jnp.int32, (nbk, LANES), 0)
        lanes = lax.broadcasted_iota(jnp.int32, (nbk, LANES), 1)
        acc = jnp.zeros((nbk, LANES), F32)
        for b in range(nbk):
            s = jnp.sum(jnp.sum(jnp.where(bk == b, dv, 0.0), axis=0, keepdims=True), axis=1, keepdims=True)
            acc = jnp.where(jnp.logical_and(rows == b, lanes == h), s, acc)
        _accum(o_ref, acc, h == 0)

    return _call(name, body, (n_heads,),
                 [pl.BlockSpec((1, BLOCK, 2 * BLOCK), lambda h: (h, 0, 0)), pl.BlockSpec((BLOCK, 2 * BLOCK), lambda h: (0, 0))],
                 pl.BlockSpec((nbk, LANES), lambda h: (0, 0)), _sds((nbk, LANES), F32), sem=("arbitrary",))(dbias, bucket)


def _band_mask(n):
    qi = lax.broadcasted_iota(jnp.int32, (BLOCK, 2 * BLOCK), 0)
    kj = lax.broadcasted_iota(jnp.int32, (BLOCK, 2 * BLOCK), 1)
    dist = qi + BLOCK - kj
    return (dist >= 0) & (dist < BLOCK) & ((n > 0) | (kj >= BLOCK))


def _head_probs(qm, kb, bias, sink, valid):
    s = lax.dot_general(qm, kb, (((1,), (1,)), ((), ())), preferred_element_type=F32) * (HEAD_DIM ** -0.5)
    s = jnp.where(valid, s + bias, NEG_INF)
    m = jnp.maximum(jnp.max(s, axis=-1, keepdims=True), sink)
    p = jnp.exp(s - m)
    es = jnp.exp(sink - m)
    inv = 1.0 / (jnp.sum(p, axis=-1, keepdims=True) + es)
    return p * inv, es * inv


def _attn_fwd(proj, bias, sinks_b, n_kv, group, q_off, k_off, v_off, *, name):
    S = proj.shape[0]
    nblk = S // BLOCK
    gw = group * HEAD_DIM
    pairs = group // 2

    def body(q_ref, kc_ref, kp_ref, vc_ref, vp_ref, b_ref, s_ref, o_ref):
        n = pl.program_id(1)
        kb = jnp.concatenate([kp_ref[...], kc_ref[...]], axis=0).astype(BF16)
        vb = jnp.concatenate([vp_ref[...], vc_ref[...]], axis=0).astype(BF16)
        valid = _band_mask(n)
        lo = lax.broadcasted_iota(jnp.int32, (BLOCK, LANES), 1) < HEAD_DIM
        for p in range(pairs):
            qp = q_ref[:, p * LANES:(p + 1) * LANES]
            outs = []
            for e in range(2):
                g = 2 * p + e
                qm = jnp.where(lo if e == 0 else jnp.logical_not(lo), qp, 0.0).astype(BF16)
                pr, _ = _head_probs(qm, kb, b_ref[g], s_ref[g][:, 0:1], valid)
                outs.append(jnp.dot(pr.astype(BF16), vb, preferred_element_type=F32))
            o_ref[:, p * LANES:(p + 1) * LANES] = jnp.where(lo, outs[0], outs[1]).astype(BF16)

    qb, kb0, vb0 = q_off // gw, k_off // LANES, v_off // LANES
    cur = lambda b0: pl.BlockSpec((BLOCK, LANES), lambda j, n: (n, b0 + j))
    prev = lambda b0: pl.BlockSpec((BLOCK, LANES), lambda j, n: (jnp.maximum(n - 1, 0), b0 + j))
    return _call(name, body, (n_kv, nblk),
                 [pl.BlockSpec((BLOCK, gw), lambda j, n: (n, qb + j)), cur(kb0), prev(kb0), cur(vb0), prev(vb0),
                  pl.BlockSpec((group, BLOCK, 2 * BLOCK), lambda j, n: (j, 0, 0)),
                  pl.BlockSpec((group, 1, LANES), lambda j, n: (j, 0, 0))],
                 pl.BlockSpec((BLOCK, gw), lambda j, n: (n, j)), _sds((S, n_kv * gw), BF16),
                 sem=("parallel", "parallel"))(proj, proj, proj, proj, proj, bias, sinks_b)


def _attn_bwd(proj, o, do, bias, sinks_b, n_kv, group, q_off, k_off, v_off, *, name):
    S = proj.shape[0]
    nblk = S // BLOCK
    gw = group * HEAD_DIM
    pairs = group // 2
    nt = (((1,), (1,)), ((), ()))
    tn = (((0,), (0,)), ((), ()))

    def body(q_ref, kc_ref, kp_ref, vc_ref, vp_ref, b_ref, s_ref, o_ref, do_ref,
             dq_ref, dk_ref, dv_ref, db_ref, ds_ref, ck, cv, sacc):
        n = pl.program_id(1)
        lo = lax.broadcasted_iota(jnp.int32, (BLOCK, LANES), 1) < HEAD_DIM
        lo2 = lax.broadcasted_iota(jnp.int32, (2 * BLOCK, LANES), 1) < HEAD_DIM
        lane = lax.broadcasted_iota(jnp.int32, (BLOCK, LANES), 1)

        @pl.when(n == 0)
        def _():
            ck[...] = jnp.zeros_like(ck)
            cv[...] = jnp.zeros_like(cv)
            sacc[...] = jnp.zeros_like(sacc)
            db_ref[...] = jnp.zeros_like(db_ref)

        @pl.when(n < nblk)
        def _():
            kb = jnp.concatenate([kp_ref[...], kc_ref[...]], axis=0).astype(BF16)
            vb = jnp.concatenate([vp_ref[...], vc_ref[...]], axis=0).astype(BF16)
            valid = _band_mask(n)
            dk_acc = jnp.zeros((2 * BLOCK, LANES), F32)
            dv_acc = jnp.zeros((2 * BLOCK, LANES), F32)
            sink_cols = jnp.zeros((BLOCK, LANES), F32)
            for p in range(pairs):
                sl = slice(p * LANES, (p + 1) * LANES)
                qp = q_ref[:, sl]
                dop = do_ref[:, sl].astype(F32)
                opair = o_ref[:, sl].astype(F32)
                dqs = []
                for e in range(2):
                    g = 2 * p + e
                    sel = lo if e == 0 else jnp.logical_not(lo)
                    qm = jnp.where(sel, qp, 0.0).astype(BF16)
                    dom = jnp.where(sel, dop, 0.0)
                    pr, ps = _head_probs(qm, kb, b_ref[g], s_ref[g][:, 0:1], valid)
                    delta = jnp.sum(dom * opair, axis=-1, keepdims=True)
                    domb = dom.astype(BF16)
                    dp = lax.dot_general(domb, vb, nt, preferred_element_type=F32)
                    ds = pr * (dp - delta)
                    db_ref[g] += ds
                    sink_cols = jnp.where(lane == g, -ps * delta, sink_cols)
                    dsb = (ds * (HEAD_DIM ** -0.5)).astype(BF16)
                    dqs.append(jnp.dot(dsb, kb, preferred_element_type=F32))
                    dk_acc = dk_acc + lax.dot_general(dsb, qm, tn, preferred_element_type=F32)
                    dv_acc = dv_acc + lax.dot_general(pr.astype(BF16), domb, tn, preferred_element_type=F32)
                dq_ref[:, sl] = jnp.where(lo, dqs[0], dqs[1]).astype(BF16)
            sacc[...] += sink_cols
            dkf = jnp.where(lo2, dk_acc + pltpu.roll(dk_acc, HEAD_DIM, 1), 0.0)
            dvf = jnp.where(lo2, dv_acc + pltpu.roll(dv_acc, HEAD_DIM, 1), 0.0)

            @pl.when(n > 0)
            def _():
                dk_ref[...] = (ck[...] + dkf[0:BLOCK, :]).astype(BF16)
                dv_ref[...] = (cv[...] + dvf[0:BLOCK, :]).astype(BF16)

            ck[...] = dkf[BLOCK:2 * BLOCK, :]
            cv[...] = dvf[BLOCK:2 * BLOCK, :]

        @pl.when(n == nblk)
        def _():
            dk_ref[...] = ck[...].astype(BF16)
            dv_ref[...] = cv[...].astype(BF16)
            ds_ref[0] = _colsum(sacc[...])

    qb, kb0, vb0 = q_off // gw, k_off // LANES, v_off // LANES
    last = nblk - 1
    cur = lambda b0: pl.BlockSpec((BLOCK, LANES), lambda j, n: (jnp.minimum(n, last), b0 + j))
    prev = lambda b0: pl.BlockSpec((BLOCK, LANES), lambda j, n: (jnp.clip(n - 1, 0, last), b0 + j))
    qspec = lambda b0: pl.BlockSpec((BLOCK, gw), lambda j, n: (jnp.minimum(n, last), b0 + j))
    kvout = pl.BlockSpec((BLOCK, LANES), lambda j, n: (jnp.maximum(n - 1, 0), j))
    return _call(name, body, (n_kv, nblk + 1),
                 [qspec(qb), cur(kb0), prev(kb0), cur(vb0), prev(vb0),
                  pl.BlockSpec((group, BLOCK, 2 * BLOCK), lambda j, n: (j, 0, 0)),
                  pl.BlockSpec((group, 1, LANES), lambda j, n: (j, 0, 0)), qspec(0), qspec(0)],
                 [qspec(0), kvout, kvout, pl.BlockSpec((group, BLOCK, 2 * BLOCK), lambda j, n: (j, 0, 0)),
                  pl.BlockSpec((1, 1, LANES), lambda j, n: (j, 0, 0))],
                 [_sds((S, n_kv * gw), BF16), _sds((S, n_kv * LANES), BF16), _sds((S, n_kv * LANES), BF16),
                  _sds((n_kv * group, BLOCK, 2 * BLOCK), F32), _sds((n_kv, 1, LANES), F32)],
                 scratch=[pltpu.VMEM((BLOCK, LANES), F32), pltpu.VMEM((BLOCK, LANES), F32), pltpu.VMEM((BLOCK, LANES), F32)])(
                     proj, proj, proj, proj, proj, bias, sinks_b, o, do)


def _adamw(w, g, m, v, *, name):
    R, C = w.shape
    tr = _pick(R, 128, SUBLANES)
    bc1 = 1.0 - ADAM_B1 ** ADAM_STEP
    bc2 = 1.0 - ADAM_B2 ** ADAM_STEP

    def body(w_ref, g_ref, m_ref, v_ref, d_ref, nm_ref, nv_ref):
        gv = g_ref[...]
        nm = ADAM_B1 * m_ref[...] + (1.0 - ADAM_B1) * gv
        nv = ADAM_B2 * v_ref[...] + (1.0 - ADAM_B2) * (gv * gv)
        d_ref[...] = -ADAM_LR * ((nm / bc1) / (jnp.sqrt(nv / bc2) + ADAM_EPS) + ADAM_WD * w_ref[...])
        nm_ref[...] = nm
        nv_ref[...] = nv

    t = pl.BlockSpec((tr, C), lambda i: (i, 0))
    o = _sds((R, C), F32)
    return _call(name, body, (R // tr,), [t, t, t, t], [t, t, t], [o, o, o], sem=("parallel",))(w, g, m, v)


def _sum_devices(packs, *, name):
    _, R, C = packs.shape
    tr = _pick(R, 512, SUBLANES)

    def body(p_ref, o_ref):
        acc = p_ref[0]
        for d in range(1, N_DEV):
            acc = acc + p_ref[d]
        o_ref[...] = acc

    return _call(name, body, (R // tr,), [pl.BlockSpec((N_DEV, tr, C), lambda i: (0, i, 0))],
                 pl.BlockSpec((tr, C), lambda i: (i, 0)), _sds((R, C), F32), sem=("parallel",))(packs)


class _Big:
    def __init__(self, kind, R, C):
        self.kind, self.R, self.C = kind, R, C
        self.hr = R // 2
        self.full = (R, N_CHIPS * C) if kind == "col" else (N_CHIPS * R, C)

    def region(self, ref, k, c):
        if self.kind == "col":
            return ref.at[pl.ds(c * self.hr, self.hr), pl.ds(k * self.C, self.C)]
        return ref.at[pl.ds(k * self.R + c * self.hr, self.hr), :]

    def shard(self, ref, k):
        if self.kind == "col":
            return ref.at[:, pl.ds(k * self.C, self.C)]
        return ref.at[pl.ds(k * self.R, self.R), :]


def _pair_sum(spec, g_full, land, c_arr, *, name):
    hr, C = spec.hr, spec.C
    tr = _pick(hr, 128, 16)
    nr = hr // tr

    def body(c_ref, g_ref, l_ref, o_ref):
        o_ref[0] = (g_ref[...] + l_ref[0]).astype(BF16)

    if spec.kind == "col":
        gspec = pl.BlockSpec((tr, C), lambda k, i, c_ref: (c_ref[0] * nr + i, k))
    else:
        gspec = pl.BlockSpec((tr, C), lambda k, i, c_ref: (k * 2 * nr + c_ref[0] * nr + i, 0))
    lspec = pl.BlockSpec((1, tr, C), lambda k, i, c_ref: (k, i, 0))
    return _call(name, body, (N_CHIPS, nr), [gspec, lspec], lspec, _sds((N_CHIPS, hr, C), BF16), nsp=1,
                 sem=("parallel", "parallel"))(c_arr, g_full, land)


def _chip_sum(spec, chipsum, land, kc_arr, *, name):
    hr, C = spec.hr, spec.C
    tr = _pick(hr, 128, 16)
    nr = hr // tr

    def body(kc_ref, s_ref, l_ref, o_ref):
        acc = s_ref[0].astype(F32)
        for j in range(N_CHIPS - 1):
            acc = acc + l_ref[j].astype(F32)
        o_ref[...] = acc

    return _call(name, body, (nr,),
                 [pl.BlockSpec((1, tr, C), lambda i, kc_ref: (kc_ref[0], i, 0)),
                  pl.BlockSpec((N_CHIPS - 1, tr, C), lambda i, kc_ref: (0, i, 0))],
                 pl.BlockSpec((tr, C), lambda i, kc_ref: (kc_ref[1] * nr + i, 0)), _sds((spec.R, C), F32), nsp=1,
                 sem=("parallel",))(kc_arr, chipsum, land)


def _place():
    x, y, c = lax.axis_index("x"), lax.axis_index("y"), lax.axis_index("c")
    return x, y, c, [(1 - x, y), (x, 1 - y), (1 - x, 1 - y)]


def _remote(src, dst, send_sem, recv_sem, dev):
    return pltpu.make_async_remote_copy(src_ref=src, dst_ref=dst, send_sem=send_sem, recv_sem=recv_sem,
                                        device_id=dev, device_id_type=MESH)


def _comm_call(name, body, n, out_shapes, n_remote, in_place=True):
    scratch = [pltpu.SemaphoreType.DMA((n_remote,)), pltpu.SemaphoreType.DMA((n_remote,))]
    aliases = {i: i for i in range(n)} if in_place else {}
    return pl.pallas_call(body, name=name, out_shape=out_shapes, in_specs=[ANY] * n, out_specs=[ANY] * n,
                          scratch_shapes=scratch, input_output_aliases=aliases)


def _placed(block, dev):
    m, n = block.shape
    return lax.dynamic_update_slice(jnp.zeros((N_DEV * m, n), block.dtype), block, (dev * m, 0))


def _all_gather(bufs, *, name):
    n = len(bufs)
    per = 7

    def body(*refs):
        outs = refs[n:2 * n]
        send_sems, recv_sems = refs[2 * n:]
        x, y, c, chips = _place()
        me, sibling = (x, y, c), (x, y, 1 - c)
        made = []
        for w in range(n):
            m = outs[w].shape[0] // N_DEV

            def rows(px, py, pc, w=w, m=m):
                return outs[w].at[pl.ds((4 * px + 2 * py + pc) * m, m), :]

            def copy(k, block, to, w=w, rows=rows):
                return _remote(rows(*block), rows(*block), send_sems.at[w * per + k], recv_sems.at[w * per + k], to)

            first = [copy(0, me, sibling)] + [copy(1 + j, me, (*chip, c)) for j, chip in enumerate(chips)]
            for cp in first:
                cp.start()
            made.append((copy, first))
        sends = []
        for w in range(n):
            copy, first = made[w]
            passed = [copy(4 + j, (*chip, c), sibling) for j, chip in enumerate(chips)]
            for j, chip in enumerate(chips):
                copy(1 + j, (*chip, c), me).wait_recv()
                passed[j].start()
            sends.append(first + passed)
        for w in range(n):
            copy, _ = made[w]
            copy(0, sibling, me).wait_recv()
            for j, chip in enumerate(chips):
                copy(4 + j, (*chip, 1 - c), me).wait_recv()
            for cp in sends[w]:
                cp.wait_send()

    outs = [_sds(a.shape, a.dtype) for a in bufs]
    return _comm_call(name, body, n, outs, per * n)(*bufs)


def _cast_into(spec, w, k_arr, *, name):
    R, C = spec.R, spec.C
    tr = _pick(R, 256, 16)
    nr = R // tr

    def body(k_ref, w_ref, o_ref):
        o_ref[...] = w_ref[...].astype(BF16)

    if spec.kind == "col":
        ospec = pl.BlockSpec((tr, C), lambda i, k_ref: (i, k_ref[0]))
    else:
        ospec = pl.BlockSpec((tr, C), lambda i, k_ref: (k_ref[0] * nr + i, 0))
    return _call(name, body, (nr,), [pl.BlockSpec((tr, C), lambda i, k_ref: (i, 0))], ospec,
                 _sds(spec.full, BF16), nsp=1, sem=("parallel",))(k_arr, w)


def _gather_weights(bufs, specs, *, name):
    n = len(bufs)
    per = 6

    def body(*refs):
        outs = refs[n:2 * n]
        send_sems, recv_sems = refs[2 * n:]
        x, y, c, chips = _place()
        sibling = (x, y, 1 - c)
        k_me = 2 * x + y
        sends = []
        for w, sp in enumerate(specs):
            mine = sp.region(outs[w], k_me, c)
            for j, chip in enumerate(chips):
                cp = _remote(mine, mine, send_sems.at[w * per + j], recv_sems.at[w * per + j], (*chip, c))
                cp.start()
                sends.append(cp)
        for w, sp in enumerate(specs):
            for j, chip in enumerate(chips):
                kj = 2 * chip[0] + chip[1]
                got = sp.region(outs[w], kj, c)
                _remote(got, got, send_sems.at[w * per + j], recv_sems.at[w * per + j], (*chip, c)).wait_recv()
                cp = _remote(got, got, send_sems.at[w * per + 3 + j], recv_sems.at[w * per + 3 + j], sibling)
                cp.start()
                sends.append(cp)
        for w, sp in enumerate(specs):
            for j, chip in enumerate(chips):
                kj = 2 * chip[0] + chip[1]
                got = sp.region(outs[w], kj, 1 - c)
                _remote(got, got, send_sems.at[w * per + 3 + j], recv_sems.at[w * per + 3 + j], sibling).wait_recv()
        for cp in sends:
            cp.wait_send()

    outs = [_sds(sp.full, BF16) for sp in specs]
    return _comm_call(name, body, n, outs, per * n)(*bufs)


def _pair_exchange(grads, specs, *, name):
    n = len(grads)

    def body(*refs):
        ins, outs = refs[:n], refs[n:2 * n]
        send_sems, recv_sems = refs[2 * n:]
        x, y, c, _ = _place()
        sibling = (x, y, 1 - c)
        cps = []
        for w, sp in enumerate(specs):
            for k in range(N_CHIPS):
                cp = _remote(sp.region(ins[w], k, 1 - c), outs[w].at[k], send_sems.at[w * N_CHIPS + k],
                             recv_sems.at[w * N_CHIPS + k], sibling)
                cp.start()
                cps.append(cp)
        for cp in cps:
            cp.wait()

    outs = [_sds((N_CHIPS, sp.hr, sp.C), F32) for sp in specs]
    return _comm_call(name, body, n, outs, N_CHIPS * n, in_place=False)(*grads)


def _chip_exchange(chipsums, specs, *, name):
    n = len(chipsums)
    per = N_CHIPS - 1

    def body(*refs):
        ins, outs = refs[:n], refs[n:2 * n]
        send_sems, recv_sems = refs[2 * n:]
        x, y, c, chips = _place()
        cps = []
        for w in range(n):
            for j, chip in enumerate(chips):
                kj = 2 * chip[0] + chip[1]
                cp = _remote(ins[w].at[kj], outs[w].at[j], send_sems.at[w * per + j], recv_sems.at[w * per + j], (*chip, c))
                cp.start()
                cps.append(cp)
        for cp in cps:
            cp.wait()

    outs = [_sds((per, sp.hr, sp.C), BF16) for sp in specs]
    return _comm_call(name, body, n, outs, per * n, in_place=False)(*chipsums)


def _pair_share(bufs, specs, *, name):
    n = len(bufs)

    def body(*refs):
        outs = refs[n:2 * n]
        send_sems, recv_sems = refs[2 * n:]
        x, y, c, _ = _place()
        sibling = (x, y, 1 - c)
        cps = []
        for w, sp in enumerate(specs):
            mine = outs[w].at[pl.ds(c * sp.hr, sp.hr), :]
            cp = _remote(mine, mine, send_sems.at[w], recv_sems.at[w], sibling)
            cp.start()
            cps.append(cp)
        for w, sp in enumerate(specs):
            theirs = outs[w].at[pl.ds((1 - c) * sp.hr, sp.hr), :]
            _remote(theirs, theirs, send_sems.at[w], recv_sems.at[w], sibling).wait_recv()
        for cp in cps:
            cp.wait_send()

    outs = [_sds((sp.R, sp.C), F32) for sp in specs]
    return _comm_call(name, body, n, outs, n)(*bufs)


PACK_ALIGN = SUBLANES * LANES


def _pack(arrs):
    flat = jnp.concatenate([a.reshape(-1) for a in arrs])
    pad = (-flat.shape[0]) % PACK_ALIGN
    return jnp.pad(flat, (0, pad)).reshape(-1, LANES)


def _unpack(packed, shapes):
    flat = packed.reshape(-1)
    out, pos = [], 0
    for s in shapes:
        n = int(np.prod(s))
        out.append(flat[pos:pos + n].reshape(s))
        pos += n
    return out


def kernel(x, c, w_ada, b_ada, norm1, w_in, rnn_conv_w, rnn_conv_b, w_rg_a, b_rg_a, w_rg_i, b_rg_i, rg_lambda, w_o_rnn, w_o_attn, attn_sinks, rel_bias, w_out, norm2, w_up, ffn_conv_w, ffn_conv_b, w_down, norm_f, loss_target, m_w_ada, m_b_ada, m_norm1, m_w_in, m_rnn_conv_w, m_rnn_conv_b, m_w_rg_a, m_b_rg_a, m_w_rg_i, m_b_rg_i, m_rg_lambda, m_w_o_rnn, m_w_o_attn, m_attn_sinks, m_rel_bias, m_w_out, m_norm2, m_w_up, m_ffn_conv_w, m_ffn_conv_b, m_w_down, m_norm_f, v_w_ada, v_b_ada, v_norm1, v_w_in, v_rnn_conv_w, v_rnn_conv_b, v_w_rg_a, v_b_rg_a, v_w_rg_i, v_b_rg_i, v_rg_lambda, v_w_o_rnn, v_w_o_attn, v_attn_sinks, v_rel_bias, v_w_out, v_norm2, v_w_up, v_ffn_conv_w, v_ffn_conv_b, v_w_down, v_norm_f):
    S, D = x.shape[1], x.shape[2]
    d_attn = N_CHIPS * w_o_attn.shape[1]
    d_rnn = N_CHIPS * w_o_rnn.shape[1]
    d_ff = N_CHIPS * w_down.shape[1]
    d_in = N_CHIPS * w_in.shape[2]
    n_heads = attn_sinks.shape[1]
    d_kv = (d_in - d_attn - 2 * d_rnn - 2 * D) // 2
    n_kv = d_kv // HEAD_DIM
    group = n_heads // n_kv
    nbk = rel_bias.shape[0]
    assert d_attn == n_heads * HEAD_DIM and group % 2 == 0 and S % BLOCK == 0

    mx, my, mc = lax.axis_index("x"), lax.axis_index("y"), lax.axis_index("c")
    k_me = 2 * mx + my
    dev = 2 * k_me + mc
    c_arr = jnp.reshape(mc, (1,)).astype(jnp.int32)
    k_arr = jnp.reshape(k_me, (1,)).astype(jnp.int32)
    kc_arr = jnp.stack([k_me, mc]).astype(jnp.int32)

    xs, tgt = x[0], loss_target[0]

    big_names = ["w_in", "w_o_rnn", "w_o_attn", "w_out", "w_up", "w_down"]
    big_w = dict(w_in=w_in[0], w_o_rnn=w_o_rnn[0], w_o_attn=w_o_attn[0], w_out=w_out[0], w_up=w_up[0], w_down=w_down[0])
    big_kind = dict(w_in="col", w_o_rnn="row", w_o_attn="row", w_out="row", w_up="col", w_down="row")
    specs = {k: _Big(big_kind[k], *big_w[k].shape) for k in big_names}
    placed = [_cast_into(specs[k], big_w[k], k_arr, name="cast_" + k) for k in big_names]
    gathered = _gather_weights(placed, [specs[k] for k in big_names], name="gather_weights")
    W = dict(zip(big_names, gathered))
    c_all, cw4, cw3 = _all_gather([_placed(jnp.broadcast_to(c, (SUBLANES, D)), dev),
                                   _placed(jnp.pad(rnn_conv_w[0], ((0, SUBLANES - rnn_conv_w.shape[1]), (0, 0))), dev),
                                   _placed(jnp.pad(ffn_conv_w[0], ((0, SUBLANES - ffn_conv_w.shape[1]), (0, 0))), dev)],
                                  name="gather_cond")
    c_all = c_all.reshape(N_DEV, SUBLANES, D)[:, 0]

    def from_chips(g, taps):
        cs = g.shape[1]
        g = g.reshape(N_CHIPS, 2, SUBLANES, cs)[:, 0, :taps]
        return jnp.transpose(g, (1, 0, 2)).reshape(taps, N_CHIPS * cs)

    conv4_w = from_chips(cw4, rnn_conv_w.shape[1])
    conv3_w = from_chips(cw3, ffn_conv_w.shape[1])

    (silu_c,) = _ew("silu_c", lambda v: (v * _sigmoid(v),), [(c_all, 0)], [F32], N_DEV, D)
    mod_sh = _mm(silu_c, w_ada[0], name="mod", exact=True, bias=lax.dynamic_slice_in_dim(b_ada, k_me * w_ada.shape[2], w_ada.shape[2], 1))
    (mod_g,) = _all_gather([_placed(mod_sh, dev)], name="gather_mod")
    mod_all = jnp.transpose(mod_g.reshape(N_CHIPS, 2, N_DEV, -1)[:, 0], (1, 0, 2)).reshape(N_DEV, 6 * D)
    mod = lax.dynamic_slice_in_dim(mod_all, dev, 1, 0)
    shift1, scale1, gate1, shift2, scale2, gate2 = [mod[:, i * D:(i + 1) * D] for i in range(6)]

    o_k, o_v, o_xr = d_attn, d_attn + d_kv, d_attn + 2 * d_kv
    wi = W["w_in"]

    def doubled(cols):
        t = cols.reshape(D, n_kv, 1, HEAD_DIM)
        return jnp.concatenate([t, t], axis=2).reshape(D, n_kv * LANES)

    w_in_x = jnp.concatenate([wi[:, :o_k], doubled(wi[:, o_k:o_v]), doubled(wi[:, o_v:o_xr]), wi[:, o_xr:]], axis=1)
    e_k = d_attn
    e_v = e_k + n_kv * LANES
    e_xr = e_v + n_kv * LANES
    e_gr = e_xr + d_rnn
    e_ga = e_gr + d_rnn
    e_gl = e_ga + D
    d_ext = e_gl + D

    u = _adaln_fwd(xs, norm1, scale1, shift1, name="adaln1")
    proj = _mm(u, w_in_x, name="proj")
    bias = _bias_table(rel_bias, n_heads, name="bias_table")
    sinks_b = jnp.broadcast_to(attn_sinks.reshape(n_heads, 1, 1), (n_heads, 1, LANES))
    o_attn = _attn_fwd(proj, bias, sinks_b, n_kv, group, 0, e_k, e_v, name="attn_fwd")
    y_attn = _mm(o_attn, W["w_o_attn"], name="y_attn")
    xc = _conv_fwd(proj, e_xr, d_rnn, conv4_w, rnn_conv_b, name="conv4")
    a_t, inp = _gates_fwd(xc, w_rg_a[0], b_rg_a, w_rg_i[0], b_rg_i, rg_lambda, name="gates")
    h_rnn, h_prev = _scan_fwd(a_t, inp, name="scan")
    (z,) = _ew("rnn_gate", lambda h, g: (h * _gelu(g),), [(h_rnn, 0), (proj, e_gr)], [BF16], S, d_rnn)
    y_rnn = _mm(z, W["w_o_rnn"], name="y_rnn")
    (merged,) = _ew("merge", lambda ya, yr, ga, gl: (_sigmoid(ga) * ya + _sigmoid(gl) * yr,),
                    [(y_attn, 0), (y_rnn, 0), (proj, e_ga), (proj, e_gl)], [BF16], S, D)
    t1 = _mm(merged, W["w_out"], name="t1")
    u2, h1 = _adaln_fwd(xs, norm2, scale2, shift2, name="adaln2", t=t1, gate=gate1)
    up = _mm(u2, W["w_up"], name="up")
    a2 = _geglu_fwd(up, conv3_w, ffn_conv_b, d_ff, name="geglu")
    t2 = _mm(a2, W["w_down"], name="t2")
    dh2, dt2, loss_tile, g_norm_f, d_gate2 = _final(h1, t2, gate2, norm_f.reshape(1, D), tgt, name="final")

    da2 = _mm(dt2, W["w_down"], name="da2", tb=True)
    g_w_down = _mm(a2, dt2, name="g_w_down", ta=True)
    dg, dval = _geglu_bwd(up, conv3_w, ffn_conv_b, da2, d_ff, name="geglu_bwd")
    dupg, g_c3g, g_b3g = _conv_bwd(dg, up, 0, d_ff, conv3_w[:, :d_ff], name="conv3_bwd_g")
    dupv, g_c3v, g_b3v = _conv_bwd(dval, up, d_ff, d_ff, conv3_w[:, d_ff:], name="conv3_bwd_v")
    dup = jnp.concatenate([dupg, dupv], axis=1)
    g_conv3_w = jnp.concatenate([g_c3g, g_c3v], axis=1)
    g_conv3_b = jnp.concatenate([g_b3g, g_b3v], axis=1)
    du2 = _mm(dup, W["w_up"], name="du2", tb=True)
    g_w_up = _mm(u2, dup, name="g_w_up", ta=True)
    dh1, d_shift2, d_scale2, g_norm2, dt1, d_gate1 = _adaln_bwd(h1, du2, dh2, norm2, scale2, name="adaln2_bwd", t=t1, gate=gate1)

    dmerged = _mm(dt1, W["w_out"], name="dmerged", tb=True)
    g_w_out = _mm(merged, dt1, name="g_w_out", ta=True)

    def merge_bwd(dm, ya, yr, ga, gl):
        sa, sl = _sigmoid(ga), _sigmoid(gl)
        return dm * sa, dm * sl, dm * ya * sa * (1.0 - sa), dm * yr * sl * (1.0 - sl)

    dy_attn, dy_rnn, d_ga, d_gl = _ew("merge_bwd", merge_bwd, [(dmerged, 0), (y_attn, 0), (y_rnn, 0), (proj, e_ga), (proj, e_gl)],
                                      [BF16, BF16, BF16, BF16], S, D)
    do = _mm(dy_attn, W["w_o_attn"], name="do", tb=True, out_dtype=BF16)
    g_w_o_attn = _mm(o_attn, dy_attn, name="g_w_o_attn", ta=True)
    dz = _mm(dy_rnn, W["w_o_rnn"], name="dz", tb=True)
    g_w_o_rnn = _mm(z, dy_rnn, name="g_w_o_rnn", ta=True)

    def rnn_gate_bwd(dzv, h, g):
        ge, dge = _gelu_and_grad(g)
        return dzv * ge, dzv * h * dge

    dh_rnn, d_gr = _ew("rnn_gate_bwd", rnn_gate_bwd, [(dz, 0), (h_rnn, 0), (proj, e_gr)], [F32, BF16], S, d_rnn)
    dacc = _scan_bwd(a_t, dh_rnn, name="scan_bwd")
    dxc, g_w_rg_a, g_w_rg_i, g_b_rg_a, g_b_rg_i, g_lam = _gates_bwd(xc, dacc, h_prev, w_rg_a[0], b_rg_a, w_rg_i[0], b_rg_i,
                                                                    rg_lambda, name="gates_bwd")
    d_xr, g_conv4_w, g_conv4_b = _conv_bwd(dxc, proj, e_xr, d_rnn, conv4_w, name="conv4_bwd")
    dq, dk, dv, dbias, dsink = _attn_bwd(proj, o_attn, do, bias, sinks_b, n_kv, group, 0, e_k, e_v, name="attn_bwd")
    g_rel = _bias_table_bwd(dbias, nbk, name="bias_table_bwd")[:, :n_heads]
    g_sinks = dsink[:, 0, :group].reshape(1, n_heads)
    dproj = jnp.concatenate([dq, dk, dv, d_xr, d_gr, d_ga, d_gl], axis=1)
    du = _mm(dproj, w_in_x, name="du", tb=True)
    g_w_in_x = _mm(u, dproj, name="g_w_in", ta=True)

    def lower_half(cols):
        return cols.reshape(D, n_kv, 2, HEAD_DIM)[:, :, 0].reshape(D, d_kv)

    g_w_in = jnp.concatenate([g_w_in_x[:, :e_k], lower_half(g_w_in_x[:, e_k:e_v]), lower_half(g_w_in_x[:, e_v:e_xr]),
                              g_w_in_x[:, e_xr:]], axis=1)
    grad_x, d_shift1, d_scale1, g_norm1 = _adaln_bwd(xs, du, dh1, norm1, scale1, name="adaln1_bwd")
    dmod = jnp.concatenate([d_shift1, d_scale1, d_gate1, d_shift2, d_scale2, d_gate2], axis=1)

    small = [loss_tile[0:1, 0:1], dmod, g_norm1, g_conv4_w, g_conv4_b, g_w_rg_a, g_b_rg_a, g_w_rg_i, g_b_rg_i, g_lam,
             g_sinks, g_rel, g_norm2, g_conv3_w, g_conv3_b, g_norm_f]
    small_shapes = [a.shape for a in small]
    pack = _pack(small)
    (packs,) = _all_gather([_placed(pack, dev)], name="gather_small")
    packs = packs.reshape(N_DEV, -1, LANES)
    summed = _unpack(_sum_devices(packs, name="sum_small"), small_shapes)
    (loss_s, g_b_ada, g_norm1, g_conv4_w, g_conv4_b, g_w_rg_a, g_b_rg_a, g_w_rg_i, g_b_rg_i, g_lam,
     g_sinks, g_rel, g_norm2, g_conv3_w, g_conv3_b, g_norm_f) = summed
    loss = loss_s.reshape(())
    dmod_all = packs.reshape(N_DEV, -1)[:, 1:1 + 6 * D]
    cs_ada = w_ada.shape[2]
    g_w_ada = _mm(silu_c, lax.dynamic_slice_in_dim(dmod_all, k_me * cs_ada, cs_ada, 1), name="g_w_ada", ta=True, exact=True)
    cs4, cs3 = rnn_conv_w.shape[2], ffn_conv_w.shape[2]
    g_conv4_sh = lax.dynamic_slice_in_dim(g_conv4_w, k_me * cs4, cs4, 1)
    g_conv3_sh = lax.dynamic_slice_in_dim(g_conv3_w, k_me * cs3, cs3, 1)

    big_g = dict(w_in=g_w_in, w_o_rnn=g_w_o_rnn, w_o_attn=g_w_o_attn, w_out=g_w_out, w_up=g_w_up, w_down=g_w_down)
    sp_list = [specs[k] for k in big_names]
    landed = _pair_exchange([big_g[k] for k in big_names], sp_list, name="pair_exchange")
    chipsums = [_pair_sum(specs[k], big_g[k], l, c_arr, name="pair_sum_" + k) for k, l in zip(big_names, landed)]
    landed2 = _chip_exchange(chipsums, sp_list, name="chip_exchange")
    halves = [_chip_sum(specs[k], s, l, kc_arr, name="chip_sum_" + k) for k, s, l in zip(big_names, chipsums, landed2)]
    shards = dict(zip(big_names, _pair_share(halves, sp_list, name="pair_share")))

    grads = dict(w_ada=g_w_ada[None], b_ada=g_b_ada, norm1=g_norm1, w_in=shards["w_in"][None], rnn_conv_w=g_conv4_sh[None],
                 rnn_conv_b=g_conv4_b, w_rg_a=g_w_rg_a[None], b_rg_a=g_b_rg_a, w_rg_i=g_w_rg_i[None], b_rg_i=g_b_rg_i,
                 rg_lambda=g_lam, w_o_rnn=shards["w_o_rnn"][None], w_o_attn=shards["w_o_attn"][None], attn_sinks=g_sinks,
                 rel_bias=g_rel, w_out=shards["w_out"][None], norm2=g_norm2, w_up=shards["w_up"][None],
                 ffn_conv_w=g_conv3_sh[None], ffn_conv_b=g_conv3_b, w_down=shards["w_down"][None], norm_f=g_norm_f.reshape(D))
    weights = dict(w_ada=w_ada, b_ada=b_ada, norm1=norm1, w_in=w_in, rnn_conv_w=rnn_conv_w, rnn_conv_b=rnn_conv_b, w_rg_a=w_rg_a,
                   b_rg_a=b_rg_a, w_rg_i=w_rg_i, b_rg_i=b_rg_i, rg_lambda=rg_lambda, w_o_rnn=w_o_rnn, w_o_attn=w_o_attn,
                   attn_sinks=attn_sinks, rel_bias=rel_bias, w_out=w_out, norm2=norm2, w_up=w_up, ffn_conv_w=ffn_conv_w,
                   ffn_conv_b=ffn_conv_b, w_down=w_down, norm_f=norm_f)
    moms = dict(w_ada=(m_w_ada, v_w_ada), b_ada=(m_b_ada, v_b_ada), norm1=(m_norm1, v_norm1), w_in=(m_w_in, v_w_in),
                rnn_conv_w=(m_rnn_conv_w, v_rnn_conv_w), rnn_conv_b=(m_rnn_conv_b, v_rnn_conv_b), w_rg_a=(m_w_rg_a, v_w_rg_a),
                b_rg_a=(m_b_rg_a, v_b_rg_a), w_rg_i=(m_w_rg_i, v_w_rg_i), b_rg_i=(m_b_rg_i, v_b_rg_i),
                rg_lambda=(m_rg_lambda, v_rg_lambda), w_o_rnn=(m_w_o_rnn, v_w_o_rnn), w_o_attn=(m_w_o_attn, v_w_o_attn),
                attn_sinks=(m_attn_sinks, v_attn_sinks), rel_bias=(m_rel_bias, v_rel_bias), w_out=(m_w_out, v_w_out),
                norm2=(m_norm2, v_norm2), w_up=(m_w_up, v_w_up), ffn_conv_w=(m_ffn_conv_w, v_ffn_conv_w),
                ffn_conv_b=(m_ffn_conv_b, v_ffn_conv_b), w_down=(m_w_down, v_w_down), norm_f=(m_norm_f, v_norm_f))
    names = list(weights)
    grads = {k: grads[k].reshape(weights[k].shape) for k in names}
    large = ["w_ada"] + big_names
    delta, new_m, new_v = {}, {}, {}
    for k in large:
        shp = weights[k].shape
        two = lambda a: a.reshape(shp[-2], shp[-1])
        d_, m_, v_ = _adamw(two(weights[k]), two(grads[k]), two(moms[k][0]), two(moms[k][1]), name="adamw_" + k)
        delta[k], new_m[k], new_v[k] = d_.reshape(shp), m_.reshape(shp), v_.reshape(shp)
    rest = [k for k in names if k not in large]
    rest_shapes = [weights[k].shape for k in rest]
    d_, m_, v_ = _adamw(_pack([weights[k] for k in rest]), _pack([grads[k] for k in rest]),
                        _pack([moms[k][0] for k in rest]), _pack([moms[k][1] for k in rest]), name="adamw_small")
    for k, dd, mm_, vv in zip(rest, _unpack(d_, rest_shapes), _unpack(m_, rest_shapes), _unpack(v_, rest_shapes)):
        delta[k], new_m[k], new_v[k] = dd, mm_, vv

    return (loss, grad_x[None], *[grads[k] for k in names], *[delta[k] for k in names],
            *[new_m[k] for k in names], *[new_v[k] for k in names])
```

```python
import math

import numpy as np
import jax
import jax.numpy as jnp
from jax import lax
from jax.experimental import pallas as pl
from jax.experimental.pallas import tpu as pltpu

F32 = jnp.float32
BF16 = jnp.bfloat16
MESH = pl.DeviceIdType.MESH
ANY = pl.BlockSpec(memory_space=pl.ANY)

EPS = 1e-6
NEG_INF = -1e30
HEAD_DIM = 64
BLOCK = 128
NUM_EXACT = 16
MAX_DISTANCE = 128
RG_C = 8.0
ADAM_LR, ADAM_B1, ADAM_B2, ADAM_EPS, ADAM_WD, ADAM_STEP = 0.001, 0.9, 0.999, 1e-08, 0.01, 10
N_CHIPS = 4
N_DEV = 8
SUBLANES = 8
LANES = 128
VMEM_LIMIT_BYTES = 48 * 1024 * 1024
GELU_C0 = math.sqrt(2.0 / math.pi)
GELU_C1 = 0.044715


def _pick(dim, pref, align):
    if dim <= pref:
        return dim
    t = (pref // align) * align
    while t >= align:
        if dim % t == 0:
            return t
        t -= align
    return dim


def _params(sem):
    return pltpu.CompilerParams(dimension_semantics=sem, vmem_limit_bytes=VMEM_LIMIT_BYTES)


def _call(name, body, grid, in_specs, out_specs, out_shape, scratch=(), nsp=0, sem=None):
    sem = sem or ("parallel",) * (len(grid) - 1) + ("arbitrary",)
    if nsp:
        gs = pltpu.PrefetchScalarGridSpec(num_scalar_prefetch=nsp, grid=grid, in_specs=in_specs,
                                          out_specs=out_specs, scratch_shapes=list(scratch))
        return pl.pallas_call(body, name=name, grid_spec=gs, out_shape=out_shape, compiler_params=_params(sem))
    return pl.pallas_call(body, name=name, grid=grid, in_specs=in_specs, out_specs=out_specs,
                          out_shape=out_shape, scratch_shapes=list(scratch), compiler_params=_params(sem))


def _sds(shape, dtype):
    return jax.ShapeDtypeStruct(shape, dtype)


def _T(tr, tc, off=0):
    return pl.BlockSpec((tr, tc), lambda j, i: (i, j + off))


def _P(rows, tc, off=0):
    return pl.BlockSpec((rows, tc), lambda j, i: (0, j + off))


def _gelu(x):
    t = jnp.tanh(GELU_C0 * (x + GELU_C1 * x * x * x))
    return 0.5 * x * (1.0 + t)


def _gelu_and_grad(x):
    t = jnp.tanh(GELU_C0 * (x + GELU_C1 * x * x * x))
    g = 0.5 * x * (1.0 + t)
    dg = 0.5 * (1.0 + t) + 0.5 * x * (1.0 - t * t) * GELU_C0 * (1.0 + 3.0 * GELU_C1 * x * x)
    return g, dg


def _sigmoid(x):
    return 1.0 / (1.0 + jnp.exp(-x))


def _accum(ref, val, first):
    @pl.when(first)
    def _():
        ref[...] = val

    @pl.when(jnp.logical_not(first))
    def _():
        ref[...] += val


def _colsum(v):
    return jnp.sum(v, axis=0, keepdims=True)


def _mm(a, b, *, name, ta=False, tb=False, out_dtype=F32, tm=1024, tn=1024, tk=2048, exact=False, bias=None):
    if ta:
        K, M = a.shape
    else:
        M, K = a.shape
    if tb:
        N, K2 = b.shape
    else:
        K2, N = b.shape
    assert K == K2, (a.shape, b.shape, ta, tb)
    tm, tn, tk = _pick(M, tm, LANES), _pick(N, tn, LANES), _pick(K, tk, LANES)
    nk = K // tk
    cdt = F32 if exact else BF16
    prec = lax.Precision.HIGHEST if exact else None
    dims = (((0 if ta else 1,), (1 if tb else 0,)), ((), ()))

    def body(*refs):
        a_ref, b_ref = refs[0], refs[1]
        bias_ref = refs[2] if bias is not None else None
        o_ref = refs[3] if bias is not None else refs[2]
        part = lax.dot_general(a_ref[...].astype(cdt), b_ref[...].astype(cdt), dims,
                               preferred_element_type=F32, precision=prec)

        def finish(r):
            if bias is not None:
                r = r + bias_ref[...]
            o_ref[...] = r.astype(out_dtype)

        if nk == 1:
            finish(part)
            return
        acc_ref = refs[-1]
        k = pl.program_id(2)

        @pl.when(k == 0)
        def _():
            acc_ref[...] = part

        @pl.when(jnp.logical_and(k > 0, k < nk - 1))
        def _():
            acc_ref[...] += part

        @pl.when(k == nk - 1)
        def _():
            finish(acc_ref[...] + part)

    a_spec = pl.BlockSpec((tk, tm), lambda i, j, k: (k, i)) if ta else pl.BlockSpec((tm, tk), lambda i, j, k: (i, k))
    b_spec = pl.BlockSpec((tn, tk), lambda i, j, k: (j, k)) if tb else pl.BlockSpec((tk, tn), lambda i, j, k: (k, j))
    in_specs, args = [a_spec, b_spec], [a, b]
    if bias is not None:
        in_specs.append(pl.BlockSpec((1, tn), lambda i, j, k: (0, j)))
        args.append(bias)
    scratch = [pltpu.VMEM((tm, tn), F32)] if nk > 1 else []
    return _call(name, body, (M // tm, N // tn, nk), in_specs, pl.BlockSpec((tm, tn), lambda i, j, k: (i, j)),
                 _sds((M, N), out_dtype), scratch=scratch, sem=("parallel", "parallel", "arbitrary"))(*args)


def _ew(name, fn, ins, out_dtypes, S, C, rows=(), tr=256, tc=512):
    g = C
    for _, off in ins:
        g = math.gcd(g, off) if off else g
    tc = _pick(g, tc, LANES)
    tr = _pick(S, tr, 16)
    n_in, n_row, n_out = len(ins), len(rows), len(out_dtypes)

    def body(*refs):
        vals = [r[...].astype(F32) for r in refs[:n_in + n_row]]
        outs = fn(*vals)
        for o_ref, o in zip(refs[n_in + n_row:], outs):
            o_ref[...] = o.astype(o_ref.dtype)

    in_specs = [_T(tr, tc, off // tc) for _, off in ins] + [_P(1, tc) for _ in rows]
    res = _call(name, body, (C // tc, S // tr), in_specs, [_T(tr, tc) for _ in out_dtypes],
                [_sds((S, C), d) for d in out_dtypes], sem=("parallel", "parallel"))(*[a for a, _ in ins], *rows)
    return res


def _adaln_fwd(x, norm, scale, shift, *, name, t=None, gate=None):
    S, D = x.shape
    tr = _pick(S, 256, 16)
    resid = t is not None

    def body(*refs):
        if resid:
            x_ref, t_ref, g_ref, n_ref, sc_ref, sh_ref, u_ref, h_ref = refs
            h = x_ref[...] + g_ref[...] * t_ref[...]
            h_ref[...] = h
        else:
            x_ref, n_ref, sc_ref, sh_ref, u_ref = refs
            h = x_ref[...]
        r = lax.rsqrt(jnp.mean(h * h, axis=-1, keepdims=True) + EPS)
        u_ref[...] = (h * r * (n_ref[...] * (1.0 + sc_ref[...])) + sh_ref[...]).astype(BF16)

    full, row = _T(tr, D), _P(1, D)
    if resid:
        return _call(name, body, (1, S // tr), [full, full, row, row, row, row], [full, full],
                     [_sds((S, D), BF16), _sds((S, D), F32)])(x, t, gate, norm, scale, shift)
    return _call(name, body, (1, S // tr), [full, row, row, row], full, _sds((S, D), BF16))(x, norm, scale, shift)


def _adaln_bwd(h, du, dres, norm, scale, *, name, t=None, gate=None):
    S, D = h.shape
    tr = _pick(S, 256, 16)
    gated = t is not None

    def body(*refs):
        if gated:
            h_ref, du_ref, dr_ref, n_ref, sc_ref, t_ref, g_ref, dh_ref, dsh_ref, dsc_ref, dn_ref, dt_ref, dg_ref = refs
        else:
            h_ref, du_ref, dr_ref, n_ref, sc_ref, dh_ref, dsh_ref, dsc_ref, dn_ref = refs
        first = pl.program_id(1) == 0
        hv, duv = h_ref[...], du_ref[...]
        r = lax.rsqrt(jnp.mean(hv * hv, axis=-1, keepdims=True) + EPS)
        xn = hv * r
        one_sc = 1.0 + sc_ref[...]
        dxn = duv * (n_ref[...] * one_sc)
        dh = dr_ref[...] + r * (dxn - xn * jnp.mean(dxn * xn, axis=-1, keepdims=True))
        dh_ref[...] = dh
        dux = duv * xn
        _accum(dsh_ref, _colsum(duv), first)
        _accum(dsc_ref, _colsum(dux * n_ref[...]), first)
        _accum(dn_ref, _colsum(dux * one_sc), first)
        if gated:
            dt_ref[...] = (dh * g_ref[...]).astype(BF16)
            _accum(dg_ref, _colsum(dh * t_ref[...]), first)

    full, row = _T(tr, D), _P(1, D)
    rowo = _sds((1, D), F32)
    if gated:
        return _call(name, body, (1, S // tr), [full, full, full, row, row, full, row],
                     [full, row, row, row, full, row],
                     [_sds((S, D), F32), rowo, rowo, rowo, _sds((S, D), BF16), rowo])(h, du, dres, norm, scale, t, gate)
    return _call(name, body, (1, S // tr), [full, full, full, row, row], [full, row, row, row],
                 [_sds((S, D), F32), rowo, rowo, rowo])(h, du, dres, norm, scale)


def _final(h1, t2, gate2, norm_f, tgt, *, name):
    S, D = h1.shape
    tr = _pick(S, 256, 16)

    def body(h_ref, t_ref, g_ref, n_ref, y_ref, dh_ref, dt_ref, loss_ref, dn_ref, dg_ref):
        first = pl.program_id(1) == 0
        tv = t_ref[...]
        h2 = h_ref[...] + g_ref[...] * tv
        r = lax.rsqrt(jnp.mean(h2 * h2, axis=-1, keepdims=True) + EPS)
        xn = h2 * r
        e = xn * n_ref[...] - y_ref[...]
        part = 0.5 * jnp.sum(jnp.mean(e * e, axis=-1, keepdims=True), axis=0, keepdims=True)
        _accum(loss_ref, jnp.broadcast_to(part, (SUBLANES, LANES)), first)
        dy = e * (1.0 / D)
        _accum(dn_ref, _colsum(dy * xn), first)
        dxn = dy * n_ref[...]
        dh2 = r * (dxn - xn * jnp.mean(dxn * xn, axis=-1, keepdims=True))
        dh_ref[...] = dh2
        dt_ref[...] = (dh2 * g_ref[...]).astype(BF16)
        _accum(dg_ref, _colsum(dh2 * tv), first)

    full, row = _T(tr, D), _P(1, D)
    rowo = _sds((1, D), F32)
    return _call(name, body, (1, S // tr), [full, full, row, row, full],
                 [full, full, _P(SUBLANES, LANES), row, row],
                 [_sds((S, D), F32), _sds((S, D), BF16), _sds((SUBLANES, LANES), F32), rowo, rowo])(h1, t2, gate2, norm_f, tgt)


HALO = 16


def _halo_prev(tr, tc, off=0):
    return pl.BlockSpec((HALO, tc), lambda j, i: (jnp.maximum(i * (tr // HALO) - 1, 0), j + off))


def _halo_next(tr, tc, n_slabs, off=0):
    return pl.BlockSpec((HALO, tc), lambda j, i: (jnp.minimum((i + 1) * (tr // HALO), n_slabs - 1), j + off))


def _past(buf, prev_ref, cur, first, tr, taps):
    buf[0:HALO, :] = jnp.where(first, 0.0, prev_ref[...].astype(F32))
    buf[HALO:HALO + tr, :] = cur
    return [cur] + [buf[HALO - d:HALO - d + tr, :] for d in range(1, taps)]


def _future(buf, next_ref, cur, last, tr, taps):
    buf[0:tr, :] = cur
    buf[tr:tr + HALO, :] = jnp.where(last, 0.0, next_ref[...].astype(F32))
    return [cur] + [buf[d:d + tr, :] for d in range(1, taps)]


def _conv_apply(xs, w_ref, b_ref):
    taps = len(xs)
    y = b_ref[...] + w_ref[taps - 1:taps, :] * xs[0]
    for d in range(1, taps):
        y = y + w_ref[taps - 1 - d:taps - d, :] * xs[d]
    return y


def _conv_fwd(x, off, C, w, b, *, name):
    S = x.shape[0]
    taps = w.shape[0]
    tc = _pick(math.gcd(C, off) if off else C, 512, LANES)
    tr = _pick(S, 256, 16)

    def body(x_ref, p_ref, w_ref, b_ref, y_ref, buf):
        xs = _past(buf, p_ref, x_ref[...].astype(F32), pl.program_id(1) == 0, tr, taps)
        y_ref[...] = _conv_apply(xs, w_ref, b_ref)

    return _call(name, body, (C // tc, S // tr),
                 [_T(tr, tc, off // tc), _halo_prev(tr, tc, off // tc), _P(taps, tc), _P(1, tc)],
                 _T(tr, tc), _sds((S, C), F32), scratch=[pltpu.VMEM((tr + HALO, tc), F32)],
                 sem=("parallel", "parallel"))(x, x, w, b)


def _conv_bwd(dy, x, off, C, w, *, name):
    S = dy.shape[0]
    taps = w.shape[0]
    tc = _pick(math.gcd(C, off) if off else C, 512, LANES)
    tr = _pick(S, 256, 16)
    n_slabs = S // HALO

    def body(dy_ref, nx_ref, x_ref, p_ref, w_ref, dx_ref, dw_ref, db_ref, fbuf, pbuf):
        i = pl.program_id(1)
        first = i == 0
        dyv = dy_ref[...]
        dys = _future(fbuf, nx_ref, dyv, i == S // tr - 1, tr, taps)
        dx = w_ref[taps - 1:taps, :] * dys[0]
        for d in range(1, taps):
            dx = dx + w_ref[taps - 1 - d:taps - d, :] * dys[d]
        dx_ref[...] = dx.astype(BF16)
        xs = _past(pbuf, p_ref, x_ref[...].astype(F32), first, tr, taps)

        @pl.when(first)
        def _():
            dw_ref[...] = jnp.zeros_like(dw_ref)

        for d in range(taps):
            dw_ref[taps - 1 - d:taps - d, :] += _colsum(dyv * xs[d])
        _accum(db_ref, _colsum(dyv), first)

    return _call(name, body, (C // tc, S // tr),
                 [_T(tr, tc), _halo_next(tr, tc, n_slabs), _T(tr, tc, off // tc), _halo_prev(tr, tc, off // tc),
                  _P(taps, tc)],
                 [_T(tr, tc), _P(taps, tc), _P(1, tc)],
                 [_sds((S, C), BF16), _sds((taps, C), F32), _sds((1, C), F32)],
                 scratch=[pltpu.VMEM((tr + HALO, tc), F32), pltpu.VMEM((tr + HALO, tc), F32)])(dy, dy, x, x, w)


def _geglu_fwd(up, w, b, F, *, name):
    S = up.shape[0]
    taps = w.shape[0]
    tc = _pick(F, 512, LANES)
    tr = _pick(S, 256, 16)
    nf = F // tc

    def body(g_ref, gp_ref, v_ref, vp_ref, wg_ref, wv_ref, bg_ref, bv_ref, a_ref, gbuf, vbuf):
        first = pl.program_id(1) == 0
        g = _conv_apply(_past(gbuf, gp_ref, g_ref[...].astype(F32), first, tr, taps), wg_ref, bg_ref)
        v = _conv_apply(_past(vbuf, vp_ref, v_ref[...].astype(F32), first, tr, taps), wv_ref, bv_ref)
        a_ref[...] = (_gelu(g) * v).astype(BF16)

    buf = pltpu.VMEM((tr + HALO, tc), F32)
    return _call(name, body, (nf, S // tr),
                 [_T(tr, tc), _halo_prev(tr, tc), _T(tr, tc, nf), _halo_prev(tr, tc, nf),
                  _P(taps, tc), _P(taps, tc, nf), _P(1, tc), _P(1, tc, nf)],
                 _T(tr, tc), _sds((S, F), BF16), scratch=[buf, buf], sem=("parallel", "parallel"))(up, up, up, up, w, w, b, b)


def _ffn_bwd(up, w, b, da, F, *, name):
    S = up.shape[0]
    taps = w.shape[0]
    tc = _pick(F, 512, LANES)
    tr = _pick(S, 256, 16)
    nf, nr = F // tc, S // tr
    ext = tr + HALO
    n_slabs = S // HALO

    def body(g_ref, gp_ref, gn_ref, v_ref, vp_ref, vn_ref, wg_ref, wv_ref, bg_ref, bv_ref, da_ref, dan_ref,
             dxg_ref, dxv_ref, dwg_ref, dwv_ref, dbg_ref, dbv_ref, gbuf, vbuf, dgbuf, dvbuf):
        i = pl.program_id(1)
        first, last = i == 0, i == nr - 1

        def conv_ext(buf, p_ref, c_ref, n_ref, w_ref, b_ref):
            buf[0:HALO, :] = jnp.where(first, 0.0, p_ref[...].astype(F32))
            buf[HALO:HALO + tr, :] = c_ref[...].astype(F32)
            buf[HALO + tr:HALO + ext, :] = n_ref[...].astype(F32)
            y = b_ref[...] + w_ref[taps - 1:taps, :] * buf[HALO:HALO + ext, :]
            for d in range(1, taps):
                y = y + w_ref[taps - 1 - d:taps - d, :] * buf[HALO - d:HALO - d + ext, :]
            return y

        g = conv_ext(gbuf, gp_ref, g_ref, gn_ref, wg_ref, bg_ref)
        v = conv_ext(vbuf, vp_ref, v_ref, vn_ref, wv_ref, bv_ref)
        da_ext = jnp.concatenate([da_ref[...], jnp.where(last, 0.0, dan_ref[...])], axis=0)
        ge, dge = _gelu_and_grad(g)
        dgbuf[...] = da_ext * v * dge
        dvbuf[...] = da_ext * ge

        def back(dbuf, xbuf, w_ref, dx_ref, dw_ref, db_ref):
            dcur = dbuf[0:tr, :]
            dx = w_ref[taps - 1:taps, :] * dcur
            for d in range(1, taps):
                dx = dx + w_ref[taps - 1 - d:taps - d, :] * dbuf[d:d + tr, :]
            dx_ref[...] = dx.astype(BF16)

            @pl.when(first)
            def _():
                dw_ref[...] = jnp.zeros_like(dw_ref)

            for d in range(taps):
                dw_ref[taps - 1 - d:taps - d, :] += _colsum(dcur * xbuf[HALO - d:HALO - d + tr, :])
            _accum(db_ref, _colsum(dcur), first)

        back(dgbuf, gbuf, wg_ref, dxg_ref, dwg_ref, dbg_ref)
        back(dvbuf, vbuf, wv_ref, dxv_ref, dwv_ref, dbv_ref)

    xbuf = pltpu.VMEM((HALO + ext, tc), F32)
    dbuf = pltpu.VMEM((ext, tc), F32)
    t, tw, tb = _T(tr, tc), _P(taps, tc), _P(1, tc)
    return _call(name, body, (nf, nr),
                 [t, _halo_prev(tr, tc), _halo_next(tr, tc, n_slabs), _T(tr, tc, nf), _halo_prev(tr, tc, nf),
                  _halo_next(tr, tc, n_slabs, nf), tw, _P(taps, tc, nf), tb, _P(1, tc, nf), t, _halo_next(tr, tc, n_slabs)],
                 [t, t, tw, tw, tb, tb],
                 [_sds((S, F), BF16), _sds((S, F), BF16), _sds((taps, F), F32), _sds((taps, F), F32),
                  _sds((1, F), F32), _sds((1, F), F32)],
                 scratch=[xbuf, xbuf, dbuf, dbuf])(up, up, up, up, up, up, w, w, b, b, da, da)


def _softplus_neg(lam):
    z = -lam
    e = jnp.exp(-jnp.abs(z))
    u = 1.0 + e
    log1p_e = jnp.where(u == 1.0, e, jnp.log(u) * e / jnp.where(u == 1.0, 1.0, u - 1.0))
    sp = jnp.maximum(z, 0.0) + log1p_e
    sg = jnp.where(z >= 0, 1.0 / u, e / u)
    return sp, sg


def _one_minus_exp(x):
    series = -x * (1.0 + x * (0.5 + x * (1.0 / 6.0 + x * (1.0 / 24.0))))
    return jnp.where(x > -0.01, series, 1.0 - jnp.exp(x))


def _gate_values(xc, wa_ref, wi_ref, ba_ref, bi_ref, lam_ref, is_t0):
    xb = xc.astype(BF16)
    ra = _sigmoid(jnp.dot(xb, wa_ref[0].astype(BF16), preferred_element_type=F32) + ba_ref[...])
    ia = _sigmoid(jnp.dot(xb, wi_ref[0].astype(BF16), preferred_element_type=F32) + bi_ref[...])
    sp, sg = _softplus_neg(lam_ref[...])
    log_a = -RG_C * ra * sp
    a = jnp.exp(log_a)
    mult = jnp.where(is_t0, 1.0, jnp.sqrt(_one_minus_exp(2.0 * log_a)))
    return ra, ia, sp, sg, a, mult


def _rnn_blockspecs(tr):
    x = pl.BlockSpec((tr, LANES), lambda n, i: (i, n))
    w = pl.BlockSpec((1, LANES, LANES), lambda n, i: (n, 0, 0))
    p = pl.BlockSpec((1, LANES), lambda n, i: (0, n))
    return x, w, p


def _is_t0(tr):
    rows = lax.broadcasted_iota(jnp.int32, (tr, LANES), 0)
    return jnp.logical_and(pl.program_id(1) == 0, rows == 0)


def _gates_fwd(xc, w_a, b_a, w_i, b_i, lam, *, name):
    S, C = xc.shape
    tr = _pick(S, 1024, 16)

    def body(x_ref, wa_ref, wi_ref, ba_ref, bi_ref, lam_ref, a_ref, inp_ref):
        xv = x_ref[...]
        _, ia, _, _, a, mult = _gate_values(xv, wa_ref, wi_ref, ba_ref, bi_ref, lam_ref, _is_t0(tr))
        a_ref[...] = a
        inp_ref[...] = mult * ia * xv

    x, w, p = _rnn_blockspecs(tr)
    return _call(name, body, (C // LANES, S // tr), [x, w, w, p, p, p], [x, x],
                 [_sds((S, C), F32), _sds((S, C), F32)], sem=("parallel", "parallel"))(xc, w_a, w_i, b_a, b_i, lam)


def _gates_bwd(xc, dacc, hprev, w_a, b_a, w_i, b_i, lam, *, name):
    S, C = xc.shape
    nb = C // LANES
    tr = _pick(S, 1024, 16)

    def body(x_ref, d_ref, hp_ref, wa_ref, wi_ref, ba_ref, bi_ref, lam_ref,
             dx_ref, dwa_ref, dwi_ref, dba_ref, dbi_ref, dlam_ref):
        first = pl.program_id(1) == 0
        t0 = _is_t0(tr)
        xv, dv = x_ref[...], d_ref[...]
        ra, ia, sp, sg, a, mult = _gate_values(xv, wa_ref, wi_ref, ba_ref, bi_ref, lam_ref, t0)
        d_a = dv * hp_ref[...]
        d_mult = dv * ia * xv
        d_ia = dv * mult * xv
        d_log = d_a * a + jnp.where(t0, 0.0, -d_mult * (a * a) / mult)
        d_pa = d_log * (-RG_C * sp) * ra * (1.0 - ra)
        d_pi = d_ia * ia * (1.0 - ia)
        xb, dab, dib = xv.astype(BF16), d_pa.astype(BF16), d_pi.astype(BF16)
        nt = (((1,), (1,)), ((), ()))
        tn = (((0,), (0,)), ((), ()))
        dx_ref[...] = (dv * mult * ia
                       + lax.dot_general(dab, wa_ref[0].astype(BF16), nt, preferred_element_type=F32)
                       + lax.dot_general(dib, wi_ref[0].astype(BF16), nt, preferred_element_type=F32))
        _accum(dwa_ref, lax.dot_general(xb, dab, tn, preferred_element_type=F32)[None], first)
        _accum(dwi_ref, lax.dot_general(xb, dib, tn, preferred_element_type=F32)[None], first)
        _accum(dba_ref, _colsum(d_pa), first)
        _accum(dbi_ref, _colsum(d_pi), first)
        _accum(dlam_ref, _colsum(d_log * (-RG_C) * ra) * (-sg), first)

    x, w, p = _rnn_blockspecs(tr)
    row = _sds((1, C), F32)
    return _call(name, body, (nb, S // tr), [x, x, x, w, w, p, p, p], [x, w, w, p, p, p],
                 [_sds((S, C), F32), _sds((nb, LANES, LANES), F32), _sds((nb, LANES, LANES), F32), row, row, row])(
                     xc, dacc, hprev, w_a, w_i, b_a, b_i, lam)


def _scan_fwd(a, inp, *, name):
    S, C = a.shape
    tc = _pick(C, 1280, LANES)
    tr = _pick(S, 256, SUBLANES)

    def body(a_ref, b_ref, h_ref, hp_ref, carry):
        @pl.when(pl.program_id(1) == 0)
        def _():
            carry[...] = jnp.zeros_like(carry)

        rows = lax.broadcasted_iota(jnp.int32, (SUBLANES, tc), 0)

        def slab(s, h):
            base = pl.multiple_of(s * SUBLANES, SUBLANES)
            av, bv = a_ref[pl.ds(base, SUBLANES), :], b_ref[pl.ds(base, SUBLANES), :]
            ho = jnp.zeros((SUBLANES, tc), F32)
            po = jnp.zeros((SUBLANES, tc), F32)
            for r in range(SUBLANES):
                po = jnp.where(rows == r, h, po)
                h = av[r:r + 1, :] * h + bv[r:r + 1, :]
                ho = jnp.where(rows == r, h, ho)
            h_ref[pl.ds(base, SUBLANES), :] = ho
            hp_ref[pl.ds(base, SUBLANES), :] = po
            return h

        carry[...] = lax.fori_loop(0, tr // SUBLANES, slab, carry[...])

    t = _T(tr, tc)
    return _call(name, body, (C // tc, S // tr), [t, t], [t, t], [_sds((S, C), F32), _sds((S, C), F32)],
                 scratch=[pltpu.VMEM((1, tc), F32)])(a, inp)


def _scan_bwd(a, dh, *, name):
    S, C = a.shape
    tc = _pick(C, 1280, LANES)
    tr = _pick(S, 256, SUBLANES)
    nr = S // tr

    def body(a_ref, d_ref, o_ref, carry):
        @pl.when(pl.program_id(1) == 0)
        def _():
            carry[...] = jnp.zeros_like(carry)

        rows = lax.broadcasted_iota(jnp.int32, (SUBLANES, tc), 0)
        n_slabs = tr // SUBLANES

        def slab(s, g):
            base = pl.multiple_of((n_slabs - 1 - s) * SUBLANES, SUBLANES)
            av, dv = a_ref[pl.ds(base, SUBLANES), :], d_ref[pl.ds(base, SUBLANES), :]
            out = jnp.zeros((SUBLANES, tc), F32)
            for r in range(SUBLANES - 1, -1, -1):
                acc = dv[r:r + 1, :] + g
                out = jnp.where(rows == r, acc, out)
                g = av[r:r + 1, :] * acc
            o_ref[pl.ds(base, SUBLANES), :] = out
            return g

        carry[...] = lax.fori_loop(0, n_slabs, slab, carry[...])

    t = pl.BlockSpec((tr, tc), lambda j, i: (nr - 1 - i, j))
    return _call(name, body, (C // tc, nr), [t, t], t, _sds((S, C), F32), scratch=[pltpu.VMEM((1, tc), F32)])(a, dh)


SCALE = HEAD_DIM ** -0.5


def _bucket_table(num_buckets):
    qi = np.arange(BLOCK)[:, None]
    j = np.arange(BLOCK)[None, :]
    dist = np.where(j <= qi, qi - j, qi + BLOCK - j)
    d = np.maximum(dist, 1).astype(np.float64)
    large = NUM_EXACT + (np.log(d / NUM_EXACT) / math.log(MAX_DISTANCE / NUM_EXACT) * (num_buckets - NUM_EXACT)).astype(np.int32)
    large = np.minimum(large, num_buckets - 1)
    return np.where(dist < NUM_EXACT, dist, large).astype(np.int32)


def _bias_table(rel_bias, n_heads, *, name):
    nbk = rel_bias.shape[0]
    bucket = jnp.asarray(_bucket_table(nbk))

    def body(rb_ref, bk_ref, o_ref):
        t, h = pl.program_id(0), pl.program_id(1)
        bk = bk_ref[...]
        acc = jnp.zeros((BLOCK, BLOCK), F32)
        for b in range(nbk):
            acc = jnp.where(bk == b, rb_ref[b, h], acc)
        o_ref[0, 0] = jnp.where(jnp.logical_and(t == 0, jnp.logical_not(_own_block())), NEG_INF, acc)

    return _call(name, body, (2, n_heads),
                 [pl.BlockSpec(memory_space=pltpu.SMEM), pl.BlockSpec((BLOCK, BLOCK), lambda t, h: (0, 0))],
                 pl.BlockSpec((1, 1, BLOCK, BLOCK), lambda t, h: (t, h, 0, 0)),
                 _sds((2, n_heads, BLOCK, BLOCK), F32), sem=("arbitrary", "arbitrary"))(rel_bias, bucket)


def _bias_table_bwd(dbias, nbk, *, name):
    n_heads = dbias.shape[0]
    bucket = jnp.asarray(_bucket_table(nbk))

    def body(db_ref, bk_ref, o_ref):
        h = pl.program_id(0)
        bk = bk_ref[...]
        dv = db_ref[0]
        rows = lax.broadcasted_iota(jnp.int32, (nbk, LANES), 0)
        lanes = lax.broadcasted_iota(jnp.int32, (nbk, LANES), 1)
        acc = jnp.zeros((nbk, LANES), F32)
        for b in range(nbk):
            s = jnp.sum(jnp.sum(jnp.where(bk == b, dv, 0.0), axis=0, keepdims=True), axis=1, keepdims=True)
            acc = jnp.where(jnp.logical_and(rows == b, lanes == h), s, acc)
        _accum(o_ref, acc, h == 0)

    return _call(name, body, (n_heads,),
                 [pl.BlockSpec((1, BLOCK, BLOCK), lambda h: (h, 0, 0)), pl.BlockSpec((BLOCK, BLOCK), lambda h: (0, 0))],
                 pl.BlockSpec((nbk, LANES), lambda h: (0, 0)), _sds((nbk, LANES), F32), sem=("arbitrary",))(dbias, bucket)


def _own_block(heads=1):
    qi = lax.broadcasted_iota(jnp.int32, (heads * BLOCK, BLOCK), 0) % BLOCK
    j = lax.broadcasted_iota(jnp.int32, (heads * BLOCK, BLOCK), 1)
    return j <= qi


def _to_window(band, own):
    return jnp.where(own, band[:, BLOCK:], band[:, :BLOCK])


def _to_band(win, own):
    return jnp.concatenate([jnp.where(own, 0.0, win), jnp.where(own, win, 0.0)], axis=1)


def _stack_pairs(ref, group):
    return jnp.concatenate([ref[:, (g // 2) * LANES:(g // 2 + 1) * LANES] for g in range(group)], axis=0).astype(F32)


def _own_lanes(group):
    rows = lax.broadcasted_iota(jnp.int32, (group * BLOCK, LANES), 0)
    lanes = lax.broadcasted_iota(jnp.int32, (group * BLOCK, LANES), 1)
    return (lanes < HEAD_DIM) == ((rows // BLOCK) % 2 == 0)


def _unstack_pairs(stacked, o_ref, group):
    lo = lax.broadcasted_iota(jnp.int32, (BLOCK, LANES), 1) < HEAD_DIM
    for p in range(group // 2):
        even = stacked[2 * p * BLOCK:(2 * p + 1) * BLOCK, :]
        odd = stacked[(2 * p + 1) * BLOCK:(2 * p + 2) * BLOCK, :]
        o_ref[:, p * LANES:(p + 1) * LANES] = jnp.where(lo, even, odd).astype(o_ref.dtype)


def _group_probs(qm, kb, b_ref, s_ref, own, group):
    band = lax.dot_general(qm, kb, (((1,), (1,)), ((), ())), preferred_element_type=F32)
    bias = jnp.concatenate([b_ref[0, g] for g in range(group)], axis=0)
    sink = jnp.concatenate([jnp.broadcast_to(s_ref[g][:, 0:1], (BLOCK, 1)) for g in range(group)], axis=0)
    s = _to_window(band, own) + bias
    m = jnp.maximum(jnp.max(s, axis=-1, keepdims=True), sink)
    p = jnp.exp(s - m)
    es = jnp.exp(sink - m)
    inv = 1.0 / (jnp.sum(p, axis=-1, keepdims=True) + es)
    return p, inv, es


def _attn_fwd(proj, bias, sinks_b, n_kv, group, q_off, k_off, v_off, *, name):
    S = proj.shape[0]
    nblk = S // BLOCK
    gw = group * HEAD_DIM
    pairs = group // 2

    def body(q_ref, kc_ref, kp_ref, vc_ref, vp_ref, b_ref, s_ref, o_ref):
        kb = jnp.concatenate([kp_ref[...], kc_ref[...]], axis=0).astype(BF16)
        vb = jnp.concatenate([vp_ref[...], vc_ref[...]], axis=0).astype(BF16)
        own = _own_block(group)
        qm = jnp.where(_own_lanes(group), _stack_pairs(q_ref, group) * SCALE, 0.0).astype(BF16)
        pu, inv, _ = _group_probs(qm, kb, b_ref, s_ref, own, group)
        o = jnp.dot(_to_band(pu, own).astype(BF16), vb, preferred_element_type=F32) * inv
        _unstack_pairs(o, o_ref, group)

    qb, kb0, vb0 = q_off // gw, k_off // LANES, v_off // LANES
    cur = lambda b0: pl.BlockSpec((BLOCK, LANES), lambda j, n: (n, b0 + j))
    prev = lambda b0: pl.BlockSpec((BLOCK, LANES), lambda j, n: (jnp.maximum(n - 1, 0), b0 + j))
    return _call(name, body, (n_kv, nblk),
                 [pl.BlockSpec((BLOCK, gw), lambda j, n: (n, qb + j)), cur(kb0), prev(kb0), cur(vb0), prev(vb0),
                  pl.BlockSpec((1, group, BLOCK, BLOCK), lambda j, n: (jnp.minimum(n, 1), j, 0, 0)),
                  pl.BlockSpec((group, 1, LANES), lambda j, n: (j, 0, 0))],
                 pl.BlockSpec((BLOCK, gw), lambda j, n: (n, j)), _sds((S, n_kv * gw), BF16),
                 sem=("parallel", "parallel"))(proj, proj, proj, proj, proj, bias, sinks_b)


def _attn_bwd(proj, o, do, bias, sinks_b, n_kv, group, q_off, k_off, v_off, *, name):
    S = proj.shape[0]
    nblk = S // BLOCK
    gw = group * HEAD_DIM
    nt = (((1,), (1,)), ((), ()))
    tn = (((0,), (0,)), ((), ()))

    def body(q_ref, kc_ref, kp_ref, vc_ref, vp_ref, b_ref, s_ref, o_ref, do_ref,
             dq_ref, dk_ref, dv_ref, db_ref, ds_ref, ck, cv, sacc):
        n = pl.program_id(1)
        lo2 = lax.broadcasted_iota(jnp.int32, (2 * BLOCK, LANES), 1) < HEAD_DIM

        @pl.when(n == 0)
        def _():
            ck[...] = jnp.zeros_like(ck)
            cv[...] = jnp.zeros_like(cv)
            sacc[...] = jnp.zeros_like(sacc)
            db_ref[...] = jnp.zeros_like(db_ref)

        @pl.when(n < nblk)
        def _():
            kb = jnp.concatenate([kp_ref[...], kc_ref[...]], axis=0).astype(BF16)
            vb = jnp.concatenate([vp_ref[...], vc_ref[...]], axis=0).astype(BF16)
            own = _own_block(group)
            mine = _own_lanes(group)
            qm = jnp.where(mine, _stack_pairs(q_ref, group) * SCALE, 0.0).astype(BF16)
            dom = jnp.where(mine, _stack_pairs(do_ref, group), 0.0)
            pu, inv, es = _group_probs(qm, kb, b_ref, s_ref, own, group)
            pr = pu * inv
            delta = jnp.sum(dom * _stack_pairs(o_ref, group), axis=-1, keepdims=True)
            domb = dom.astype(BF16)
            dp = _to_window(lax.dot_general(domb, vb, nt, preferred_element_type=F32), own)
            ds = pr * (dp - delta)
            for g in range(group):
                db_ref[g] += ds[g * BLOCK:(g + 1) * BLOCK, :]
            sacc[...] += -(es * inv) * delta
            dsb = _to_band(ds, own).astype(BF16)
            _unstack_pairs(jnp.dot(dsb, kb, preferred_element_type=F32) * SCALE, dq_ref, group)
            dk_acc = lax.dot_general(dsb, qm, tn, preferred_element_type=F32)
            dv_acc = lax.dot_general(_to_band(pr, own).astype(BF16), domb, tn, preferred_element_type=F32)
            dkf = jnp.where(lo2, dk_acc + pltpu.roll(dk_acc, HEAD_DIM, 1), 0.0)
            dvf = jnp.where(lo2, dv_acc + pltpu.roll(dv_acc, HEAD_DIM, 1), 0.0)

            @pl.when(n > 0)
            def _():
                dk_ref[...] = (ck[...] + dkf[0:BLOCK, :]).astype(BF16)
                dv_ref[...] = (cv[...] + dvf[0:BLOCK, :]).astype(BF16)

            ck[...] = dkf[BLOCK:2 * BLOCK, :]
            cv[...] = dvf[BLOCK:2 * BLOCK, :]

        @pl.when(n == nblk)
        def _():
            dk_ref[...] = ck[...].astype(BF16)
            dv_ref[...] = cv[...].astype(BF16)
            lane = lax.broadcasted_iota(jnp.int32, (1, LANES), 1)
            tot = jnp.zeros((1, LANES), F32)
            for g in range(group):
                tot = jnp.where(lane == g, _colsum(sacc[g * BLOCK:(g + 1) * BLOCK, :]), tot)
            ds_ref[0] = tot

    qb, kb0, vb0 = q_off // gw, k_off // LANES, v_off // LANES
    last = nblk - 1
    cur = lambda b0: pl.BlockSpec((BLOCK, LANES), lambda j, n: (jnp.minimum(n, last), b0 + j))
    prev = lambda b0: pl.BlockSpec((BLOCK, LANES), lambda j, n: (jnp.clip(n - 1, 0, last), b0 + j))
    qspec = lambda b0: pl.BlockSpec((BLOCK, gw), lambda j, n: (jnp.minimum(n, last), b0 + j))
    kvout = pl.BlockSpec((BLOCK, LANES), lambda j, n: (jnp.maximum(n - 1, 0), j))
    return _call(name, body, (n_kv, nblk + 1),
                 [qspec(qb), cur(kb0), prev(kb0), cur(vb0), prev(vb0),
                  pl.BlockSpec((1, group, BLOCK, BLOCK), lambda j, n: (jnp.minimum(n, 1), j, 0, 0)),
                  pl.BlockSpec((group, 1, LANES), lambda j, n: (j, 0, 0)), qspec(0), qspec(0)],
                 [qspec(0), kvout, kvout, pl.BlockSpec((group, BLOCK, BLOCK), lambda j, n: (j, 0, 0)),
                  pl.BlockSpec((1, 1, LANES), lambda j, n: (j, 0, 0))],
                 [_sds((S, n_kv * gw), BF16), _sds((S, n_kv * LANES), BF16), _sds((S, n_kv * LANES), BF16),
                  _sds((n_kv * group, BLOCK, BLOCK), F32), _sds((n_kv, 1, LANES), F32)],
                 scratch=[pltpu.VMEM((BLOCK, LANES), F32), pltpu.VMEM((BLOCK, LANES), F32),
                          pltpu.VMEM((group * BLOCK, 1), F32)])(proj, proj, proj, proj, proj, bias, sinks_b, o, do)


def _adamw(w, g, m, v, *, name):
    R, C = w.shape
    tr = _pick(R, 128, SUBLANES)
    bc1 = 1.0 - ADAM_B1 ** ADAM_STEP
    bc2 = 1.0 - ADAM_B2 ** ADAM_STEP

    def body(w_ref, g_ref, m_ref, v_ref, d_ref, nm_ref, nv_ref):
        gv = g_ref[...]
        nm = ADAM_B1 * m_ref[...] + (1.0 - ADAM_B1) * gv
        nv = ADAM_B2 * v_ref[...] + (1.0 - ADAM_B2) * (gv * gv)
        d_ref[...] = -ADAM_LR * ((nm / bc1) / (jnp.sqrt(nv / bc2) + ADAM_EPS) + ADAM_WD * w_ref[...])
        nm_ref[...] = nm
        nv_ref[...] = nv

    t = pl.BlockSpec((tr, C), lambda i: (i, 0))
    o = _sds((R, C), F32)
    return _call(name, body, (R // tr,), [t, t, t, t], [t, t, t], [o, o, o], sem=("parallel",))(w, g, m, v)


def _sum_devices(packs, *, name):
    _, R, C = packs.shape
    tr = _pick(R, 512, SUBLANES)

    def body(p_ref, o_ref):
        acc = p_ref[0]
        for d in range(1, N_DEV):
            acc = acc + p_ref[d]
        o_ref[...] = acc

    return _call(name, body, (R // tr,), [pl.BlockSpec((N_DEV, tr, C), lambda i: (0, i, 0))],
                 pl.BlockSpec((tr, C), lambda i: (i, 0)), _sds((R, C), F32), sem=("parallel",))(packs)


class _Big:
    def __init__(self, kind, R, C):
        self.kind, self.R, self.C = kind, R, C
        self.hr = R // 2
        self.full = (R, N_CHIPS * C) if kind == "col" else (N_CHIPS * R, C)

    def region(self, ref, k, c):
        if self.kind == "col":
            return ref.at[pl.ds(c * self.hr, self.hr), pl.ds(k * self.C, self.C)]
        return ref.at[pl.ds(k * self.R + c * self.hr, self.hr), :]

    def shard(self, ref, k):
        if self.kind == "col":
            return ref.at[:, pl.ds(k * self.C, self.C)]
        return ref.at[pl.ds(k * self.R, self.R), :]


def _pair_sum(spec, g_full, land, c_arr, *, name):
    hr, C = spec.hr, spec.C
    tr = _pick(hr, 128, 16)
    nr = hr // tr

    def body(c_ref, g_ref, l_ref, o_ref):
        o_ref[0] = (g_ref[...] + l_ref[0]).astype(BF16)

    if spec.kind == "col":
        gspec = pl.BlockSpec((tr, C), lambda k, i, c_ref: (c_ref[0] * nr + i, k))
    else:
        gspec = pl.BlockSpec((tr, C), lambda k, i, c_ref: (k * 2 * nr + c_ref[0] * nr + i, 0))
    lspec = pl.BlockSpec((1, tr, C), lambda k, i, c_ref: (k, i, 0))
    return _call(name, body, (N_CHIPS, nr), [gspec, lspec], lspec, _sds((N_CHIPS, hr, C), BF16), nsp=1,
                 sem=("parallel", "parallel"))(c_arr, g_full, land)


def _chip_sum(spec, chipsum, land, kc_arr, *, name):
    hr, C = spec.hr, spec.C
    tr = _pick(hr, 128, 16)
    nr = hr // tr

    def body(kc_ref, s_ref, l_ref, o_ref):
        acc = s_ref[0].astype(F32)
        for j in range(N_CHIPS - 1):
            acc = acc + l_ref[j].astype(F32)
        o_ref[...] = acc

    return _call(name, body, (nr,),
                 [pl.BlockSpec((1, tr, C), lambda i, kc_ref: (kc_ref[0], i, 0)),
                  pl.BlockSpec((N_CHIPS - 1, tr, C), lambda i, kc_ref: (0, i, 0))],
                 pl.BlockSpec((tr, C), lambda i, kc_ref: (kc_ref[1] * nr + i, 0)), _sds((spec.R, C), F32), nsp=1,
                 sem=("parallel",))(kc_arr, chipsum, land)


def _place():
    x, y, c = lax.axis_index("x"), lax.axis_index("y"), lax.axis_index("c")
    return x, y, c, [(1 - x, y), (x, 1 - y), (1 - x, 1 - y)]


def _remote(src, dst, send_sem, recv_sem, dev):
    return pltpu.make_async_remote_copy(src_ref=src, dst_ref=dst, send_sem=send_sem, recv_sem=recv_sem,
                                        device_id=dev, device_id_type=MESH)


def _comm_call(name, body, n, out_shapes, n_remote, in_place=True):
    scratch = [pltpu.SemaphoreType.DMA((n_remote,)), pltpu.SemaphoreType.DMA((n_remote,))]
    aliases = {i: i for i in range(n)} if in_place else {}
    return pl.pallas_call(body, name=name, out_shape=out_shapes, in_specs=[ANY] * n, out_specs=[ANY] * n,
                          scratch_shapes=scratch, input_output_aliases=aliases)


def _placed(block, dev):
    m, n = block.shape
    return lax.dynamic_update_slice(jnp.zeros((N_DEV * m, n), block.dtype), block, (dev * m, 0))


def _all_gather(bufs, *, name):
    n = len(bufs)
    per = 7

    def body(*refs):
        outs = refs[n:2 * n]
        send_sems, recv_sems = refs[2 * n:]
        x, y, c, chips = _place()
        me, sibling = (x, y, c), (x, y, 1 - c)
        made = []
        for w in range(n):
            m = outs[w].shape[0] // N_DEV

            def rows(px, py, pc, w=w, m=m):
                return outs[w].at[pl.ds((4 * px + 2 * py + pc) * m, m), :]

            def copy(k, block, to, w=w, rows=rows):
                return _remote(rows(*block), rows(*block), send_sems.at[w * per + k], recv_sems.at[w * per + k], to)

            first = [copy(0, me, sibling)] + [copy(1 + j, me, (*chip, c)) for j, chip in enumerate(chips)]
            for cp in first:
                cp.start()
            made.append((copy, first))
        sends = []
        for w in range(n):
            copy, first = made[w]
            passed = [copy(4 + j, (*chip, c), sibling) for j, chip in enumerate(chips)]
            for j, chip in enumerate(chips):
                copy(1 + j, (*chip, c), me).wait_recv()
                passed[j].start()
            sends.append(first + passed)
        for w in range(n):
            copy, _ = made[w]
            copy(0, sibling, me).wait_recv()
            for j, chip in enumerate(chips):
                copy(4 + j, (*chip, 1 - c), me).wait_recv()
            for cp in sends[w]:
                cp.wait_send()

    outs = [_sds(a.shape, a.dtype) for a in bufs]
    return _comm_call(name, body, n, outs, per * n)(*bufs)


def _cast_into(spec, w, k_arr, *, name):
    R, C = spec.R, spec.C
    tr = _pick(R, 256, 16)
    nr = R // tr

    def body(k_ref, w_ref, o_ref):
        o_ref[...] = w_ref[...].astype(BF16)

    if spec.kind == "col":
        ospec = pl.BlockSpec((tr, C), lambda i, k_ref: (i, k_ref[0]))
    else:
        ospec = pl.BlockSpec((tr, C), lambda i, k_ref: (k_ref[0] * nr + i, 0))
    return _call(name, body, (nr,), [pl.BlockSpec((tr, C), lambda i, k_ref: (i, 0))], ospec,
                 _sds(spec.full, BF16), nsp=1, sem=("parallel",))(k_arr, w)


def _gather_weights(bufs, specs, *, name):
    n = len(bufs)
    per = 6

    def body(*refs):
        outs = refs[n:2 * n]
        send_sems, recv_sems = refs[2 * n:]
        x, y, c, chips = _place()
        sibling = (x, y, 1 - c)
        k_me = 2 * x + y
        sends = []
        for w, sp in enumerate(specs):
            mine = sp.region(outs[w], k_me, c)
            for j, chip in enumerate(chips):
                cp = _remote(mine, mine, send_sems.at[w * per + j], recv_sems.at[w * per + j], (*chip, c))
                cp.start()
                sends.append(cp)
        for w, sp in enumerate(specs):
            for j, chip in enumerate(chips):
                kj = 2 * chip[0] + chip[1]
                got = sp.region(outs[w], kj, c)
                _remote(got, got, send_sems.at[w * per + j], recv_sems.at[w * per + j], (*chip, c)).wait_recv()
                cp = _remote(got, got, send_sems.at[w * per + 3 + j], recv_sems.at[w * per + 3 + j], sibling)
                cp.start()
                sends.append(cp)
        for w, sp in enumerate(specs):
            for j, chip in enumerate(chips):
                kj = 2 * chip[0] + chip[1]
                got = sp.region(outs[w], kj, 1 - c)
                _remote(got, got, send_sems.at[w * per + 3 + j], recv_sems.at[w * per + 3 + j], sibling).wait_recv()
        for cp in sends:
            cp.wait_send()

    outs = [_sds(sp.full, BF16) for sp in specs]
    return _comm_call(name, body, n, outs, per * n)(*bufs)


def _pair_exchange(grads, specs, *, name):
    n = len(grads)

    def body(*refs):
        ins, outs = refs[:n], refs[n:2 * n]
        send_sems, recv_sems = refs[2 * n:]
        x, y, c, _ = _place()
        sibling = (x, y, 1 - c)
        cps = []
        for w, sp in enumerate(specs):
            for k in range(N_CHIPS):
                cp = _remote(sp.region(ins[w], k, 1 - c), outs[w].at[k], send_sems.at[w * N_CHIPS + k],
                             recv_sems.at[w * N_CHIPS + k], sibling)
                cp.start()
                cps.append(cp)
        for cp in cps:
            cp.wait()

    outs = [_sds((N_CHIPS, sp.hr, sp.C), F32) for sp in specs]
    return _comm_call(name, body, n, outs, N_CHIPS * n, in_place=False)(*grads)


def _chip_exchange(chipsums, specs, *, name):
    n = len(chipsums)
    per = N_CHIPS - 1

    def body(*refs):
        ins, outs = refs[:n], refs[n:2 * n]
        send_sems, recv_sems = refs[2 * n:]
        x, y, c, chips = _place()
        cps = []
        for w in range(n):
            for j, chip in enumerate(chips):
                kj = 2 * chip[0] + chip[1]
                cp = _remote(ins[w].at[kj], outs[w].at[j], send_sems.at[w * per + j], recv_sems.at[w * per + j], (*chip, c))
                cp.start()
                cps.append(cp)
        for cp in cps:
            cp.wait()

    outs = [_sds((per, sp.hr, sp.C), BF16) for sp in specs]
    return _comm_call(name, body, n, outs, per * n, in_place=False)(*chipsums)


def _pair_share(bufs, specs, *, name):
    n = len(bufs)

    def body(*refs):
        outs = refs[n:2 * n]
        send_sems, recv_sems = refs[2 * n:]
        x, y, c, _ = _place()
        sibling = (x, y, 1 - c)
        cps = []
        for w, sp in enumerate(specs):
            mine = outs[w].at[pl.ds(c * sp.hr, sp.hr), :]
            cp = _remote(mine, mine, send_sems.at[w], recv_sems.at[w], sibling)
            cp.start()
            cps.append(cp)
        for w, sp in enumerate(specs):
            theirs = outs[w].at[pl.ds((1 - c) * sp.hr, sp.hr), :]
            _remote(theirs, theirs, send_sems.at[w], recv_sems.at[w], sibling).wait_recv()
        for cp in cps:
            cp.wait_send()

    outs = [_sds((sp.R, sp.C), F32) for sp in specs]
    return _comm_call(name, body, n, outs, n)(*bufs)


PACK_ALIGN = SUBLANES * LANES


def _pack(arrs):
    flat = jnp.concatenate([a.reshape(-1) for a in arrs])
    pad = (-flat.shape[0]) % PACK_ALIGN
    return jnp.pad(flat, (0, pad)).reshape(-1, LANES)


def _unpack(packed, shapes):
    flat = packed.reshape(-1)
    out, pos = [], 0
    for s in shapes:
        n = int(np.prod(s))
        out.append(flat[pos:pos + n].reshape(s))
        pos += n
    return out


def kernel(x, c, w_ada, b_ada, norm1, w_in, rnn_conv_w, rnn_conv_b, w_rg_a, b_rg_a, w_rg_i, b_rg_i, rg_lambda, w_o_rnn, w_o_attn, attn_sinks, rel_bias, w_out, norm2, w_up, ffn_conv_w, ffn_conv_b, w_down, norm_f, loss_target, m_w_ada, m_b_ada, m_norm1, m_w_in, m_rnn_conv_w, m_rnn_conv_b, m_w_rg_a, m_b_rg_a, m_w_rg_i, m_b_rg_i, m_rg_lambda, m_w_o_rnn, m_w_o_attn, m_attn_sinks, m_rel_bias, m_w_out, m_norm2, m_w_up, m_ffn_conv_w, m_ffn_conv_b, m_w_down, m_norm_f, v_w_ada, v_b_ada, v_norm1, v_w_in, v_rnn_conv_w, v_rnn_conv_b, v_w_rg_a, v_b_rg_a, v_w_rg_i, v_b_rg_i, v_rg_lambda, v_w_o_rnn, v_w_o_attn, v_attn_sinks, v_rel_bias, v_w_out, v_norm2, v_w_up, v_ffn_conv_w, v_ffn_conv_b, v_w_down, v_norm_f):
    S, D = x.shape[1], x.shape[2]
    d_attn = N_CHIPS * w_o_attn.shape[1]
    d_rnn = N_CHIPS * w_o_rnn.shape[1]
    d_ff = N_CHIPS * w_down.shape[1]
    d_in = N_CHIPS * w_in.shape[2]
    n_heads = attn_sinks.shape[1]
    d_kv = (d_in - d_attn - 2 * d_rnn - 2 * D) // 2
    n_kv = d_kv // HEAD_DIM
    group = n_heads // n_kv
    nbk = rel_bias.shape[0]
    assert d_attn == n_heads * HEAD_DIM and group % 2 == 0 and S % BLOCK == 0

    mx, my, mc = lax.axis_index("x"), lax.axis_index("y"), lax.axis_index("c")
    k_me = 2 * mx + my
    dev = 2 * k_me + mc
    c_arr = jnp.reshape(mc, (1,)).astype(jnp.int32)
    k_arr = jnp.reshape(k_me, (1,)).astype(jnp.int32)
    kc_arr = jnp.stack([k_me, mc]).astype(jnp.int32)

    xs, tgt = x[0], loss_target[0]

    big_names = ["w_in", "w_o_rnn", "w_o_attn", "w_out", "w_up", "w_down"]
    big_w = dict(w_in=w_in[0], w_o_rnn=w_o_rnn[0], w_o_attn=w_o_attn[0], w_out=w_out[0], w_up=w_up[0], w_down=w_down[0])
    big_kind = dict(w_in="col", w_o_rnn="row", w_o_attn="row", w_out="row", w_up="col", w_down="row")
    specs = {k: _Big(big_kind[k], *big_w[k].shape) for k in big_names}
    placed = [_cast_into(specs[k], big_w[k], k_arr, name="cast_" + k) for k in big_names]
    gathered = _gather_weights(placed, [specs[k] for k in big_names], name="gather_weights")
    W = dict(zip(big_names, gathered))
    c_all, cw4, cw3 = _all_gather([_placed(jnp.broadcast_to(c, (SUBLANES, D)), dev),
                                   _placed(jnp.pad(rnn_conv_w[0], ((0, SUBLANES - rnn_conv_w.shape[1]), (0, 0))), dev),
                                   _placed(jnp.pad(ffn_conv_w[0], ((0, SUBLANES - ffn_conv_w.shape[1]), (0, 0))), dev)],
                                  name="gather_cond")
    c_all = c_all.reshape(N_DEV, SUBLANES, D)[:, 0]

    def from_chips(g, taps):
        cs = g.shape[1]
        g = g.reshape(N_CHIPS, 2, SUBLANES, cs)[:, 0, :taps]
        return jnp.transpose(g, (1, 0, 2)).reshape(taps, N_CHIPS * cs)

    conv4_w = from_chips(cw4, rnn_conv_w.shape[1])
    conv3_w = from_chips(cw3, ffn_conv_w.shape[1])

    (silu_c,) = _ew("silu_c", lambda v: (v * _sigmoid(v),), [(c_all, 0)], [F32], N_DEV, D)
    mod_sh = _mm(silu_c, w_ada[0], name="mod", exact=True, bias=lax.dynamic_slice_in_dim(b_ada, k_me * w_ada.shape[2], w_ada.shape[2], 1))
    (mod_g,) = _all_gather([_placed(mod_sh, dev)], name="gather_mod")
    mod_all = jnp.transpose(mod_g.reshape(N_CHIPS, 2, N_DEV, -1)[:, 0], (1, 0, 2)).reshape(N_DEV, 6 * D)
    mod = lax.dynamic_slice_in_dim(mod_all, dev, 1, 0)
    shift1, scale1, gate1, shift2, scale2, gate2 = [mod[:, i * D:(i + 1) * D] for i in range(6)]

    o_k, o_v, o_xr = d_attn, d_attn + d_kv, d_attn + 2 * d_kv
    wi = W["w_in"]

    def doubled(cols):
        t = cols.reshape(D, n_kv, 1, HEAD_DIM)
        return jnp.concatenate([t, t], axis=2).reshape(D, n_kv * LANES)

    w_in_x = jnp.concatenate([wi[:, :o_k], doubled(wi[:, o_k:o_v]), doubled(wi[:, o_v:o_xr]), wi[:, o_xr:]], axis=1)
    e_k = d_attn
    e_v = e_k + n_kv * LANES
    e_xr = e_v + n_kv * LANES
    e_gr = e_xr + d_rnn
    e_ga = e_gr + d_rnn
    e_gl = e_ga + D
    d_ext = e_gl + D

    u = _adaln_fwd(xs, norm1, scale1, shift1, name="adaln1")
    proj = _mm(u, w_in_x, name="proj", out_dtype=BF16)
    bias = _bias_table(rel_bias, n_heads, name="bias_table")
    sinks_b = jnp.broadcast_to(attn_sinks.reshape(n_heads, 1, 1), (n_heads, 1, LANES))
    o_attn = _attn_fwd(proj, bias, sinks_b, n_kv, group, 0, e_k, e_v, name="attn_fwd")
    y_attn = _mm(o_attn, W["w_o_attn"], name="y_attn")
    xc = _conv_fwd(proj, e_xr, d_rnn, conv4_w, rnn_conv_b, name="conv4")
    a_t, inp = _gates_fwd(xc, w_rg_a[0], b_rg_a, w_rg_i[0], b_rg_i, rg_lambda, name="gates")
    h_rnn, h_prev = _scan_fwd(a_t, inp, name="scan")
    (z,) = _ew("rnn_gate", lambda h, g: (h * _gelu(g),), [(h_rnn, 0), (proj, e_gr)], [BF16], S, d_rnn)
    y_rnn = _mm(z, W["w_o_rnn"], name="y_rnn")
    (merged,) = _ew("merge", lambda ya, yr, ga, gl: (_sigmoid(ga) * ya + _sigmoid(gl) * yr,),
                    [(y_attn, 0), (y_rnn, 0), (proj, e_ga), (proj, e_gl)], [BF16], S, D)
    t1 = _mm(merged, W["w_out"], name="t1")
    u2, h1 = _adaln_fwd(xs, norm2, scale2, shift2, name="adaln2", t=t1, gate=gate1)
    up = _mm(u2, W["w_up"], name="up", out_dtype=BF16)
    a2 = _geglu_fwd(up, conv3_w, ffn_conv_b, d_ff, name="geglu")
    t2 = _mm(a2, W["w_down"], name="t2")
    dh2, dt2, loss_tile, g_norm_f, d_gate2 = _final(h1, t2, gate2, norm_f.reshape(1, D), tgt, name="final")

    da2 = _mm(dt2, W["w_down"], name="da2", tb=True)
    g_w_down = _mm(a2, dt2, name="g_w_down", ta=True)
    dupg, dupv, g_c3g, g_c3v, g_b3g, g_b3v = _ffn_bwd(up, conv3_w, ffn_conv_b, da2, d_ff, name="ffn_bwd")
    dup = jnp.concatenate([dupg, dupv], axis=1)
    g_conv3_w = jnp.concatenate([g_c3g, g_c3v], axis=1)
    g_conv3_b = jnp.concatenate([g_b3g, g_b3v], axis=1)
    du2 = _mm(dup, W["w_up"], name="du2", tb=True)
    g_w_up = _mm(u2, dup, name="g_w_up", ta=True)
    dh1, d_shift2, d_scale2, g_norm2, dt1, d_gate1 = _adaln_bwd(h1, du2, dh2, norm2, scale2, name="adaln2_bwd", t=t1, gate=gate1)

    dmerged = _mm(dt1, W["w_out"], name="dmerged", tb=True)
    g_w_out = _mm(merged, dt1, name="g_w_out", ta=True)

    def merge_bwd(dm, ya, yr, ga, gl):
        sa, sl = _sigmoid(ga), _sigmoid(gl)
        return dm * sa, dm * sl, dm * ya * sa * (1.0 - sa), dm * yr * sl * (1.0 - sl)

    dy_attn, dy_rnn, d_ga, d_gl = _ew("merge_bwd", merge_bwd, [(dmerged, 0), (y_attn, 0), (y_rnn, 0), (proj, e_ga), (proj, e_gl)],
                                      [BF16, BF16, BF16, BF16], S, D)
    do = _mm(dy_attn, W["w_o_attn"], name="do", tb=True, out_dtype=BF16)
    g_w_o_attn = _mm(o_attn, dy_attn, name="g_w_o_attn", ta=True)
    dz = _mm(dy_rnn, W["w_o_rnn"], name="dz", tb=True)
    g_w_o_rnn = _mm(z, dy_rnn, name="g_w_o_rnn", ta=True)

    def rnn_gate_bwd(dzv, h, g):
        ge, dge = _gelu_and_grad(g)
        return dzv * ge, dzv * h * dge

    dh_rnn, d_gr = _ew("rnn_gate_bwd", rnn_gate_bwd, [(dz, 0), (h_rnn, 0), (proj, e_gr)], [F32, BF16], S, d_rnn)
    dacc = _scan_bwd(a_t, dh_rnn, name="scan_bwd")
    dxc, g_w_rg_a, g_w_rg_i, g_b_rg_a, g_b_rg_i, g_lam = _gates_bwd(xc, dacc, h_prev, w_rg_a[0], b_rg_a, w_rg_i[0], b_rg_i,
                                                                    rg_lambda, name="gates_bwd")
    d_xr, g_conv4_w, g_conv4_b = _conv_bwd(dxc, proj, e_xr, d_rnn, conv4_w, name="conv4_bwd")
    dq, dk, dv, dbias, dsink = _attn_bwd(proj, o_attn, do, bias, sinks_b, n_kv, group, 0, e_k, e_v, name="attn_bwd")
    g_rel = _bias_table_bwd(dbias, nbk, name="bias_table_bwd")[:, :n_heads]
    g_sinks = dsink[:, 0, :group].reshape(1, n_heads)
    dproj = jnp.concatenate([dq, dk, dv, d_xr, d_gr, d_ga, d_gl], axis=1)
    du = _mm(dproj, w_in_x, name="du", tb=True)
    g_w_in_x = _mm(u, dproj, name="g_w_in", ta=True)

    def lower_half(cols):
        return cols.reshape(D, n_kv, 2, HEAD_DIM)[:, :, 0].reshape(D, d_kv)

    g_w_in = jnp.concatenate([g_w_in_x[:, :e_k], lower_half(g_w_in_x[:, e_k:e_v]), lower_half(g_w_in_x[:, e_v:e_xr]),
                              g_w_in_x[:, e_xr:]], axis=1)
    grad_x, d_shift1, d_scale1, g_norm1 = _adaln_bwd(xs, du, dh1, norm1, scale1, name="adaln1_bwd")
    dmod = jnp.concatenate([d_shift1, d_scale1, d_gate1, d_shift2, d_scale2, d_gate2], axis=1)

    small = [loss_tile[0:1, 0:1], dmod, g_norm1, g_conv4_w, g_conv4_b, g_w_rg_a, g_b_rg_a, g_w_rg_i, g_b_rg_i, g_lam,
             g_sinks, g_rel, g_norm2, g_conv3_w, g_conv3_b, g_norm_f]
    small_shapes = [a.shape for a in small]
    pack = _pack(small)
    (packs,) = _all_gather([_placed(pack, dev)], name="gather_small")
    packs = packs.reshape(N_DEV, -1, LANES)
    summed = _unpack(_sum_devices(packs, name="sum_small"), small_shapes)
    (loss_s, g_b_ada, g_norm1, g_conv4_w, g_conv4_b, g_w_rg_a, g_b_rg_a, g_w_rg_i, g_b_rg_i, g_lam,
     g_sinks, g_rel, g_norm2, g_conv3_w, g_conv3_b, g_norm_f) = summed
    loss = loss_s.reshape(())
    dmod_all = packs.reshape(N_DEV, -1)[:, 1:1 + 6 * D]
    cs_ada = w_ada.shape[2]
    g_w_ada = _mm(silu_c, lax.dynamic_slice_in_dim(dmod_all, k_me * cs_ada, cs_ada, 1), name="g_w_ada", ta=True, exact=True)
    cs4, cs3 = rnn_conv_w.shape[2], ffn_conv_w.shape[2]
    g_conv4_sh = lax.dynamic_slice_in_dim(g_conv4_w, k_me * cs4, cs4, 1)
    g_conv3_sh = lax.dynamic_slice_in_dim(g_conv3_w, k_me * cs3, cs3, 1)

    big_g = dict(w_in=g_w_in, w_o_rnn=g_w_o_rnn, w_o_attn=g_w_o_attn, w_out=g_w_out, w_up=g_w_up, w_down=g_w_down)
    sp_list = [specs[k] for k in big_names]
    landed = _pair_exchange([big_g[k] for k in big_names], sp_list, name="pair_exchange")
    chipsums = [_pair_sum(specs[k], big_g[k], l, c_arr, name="pair_sum_" + k) for k, l in zip(big_names, landed)]
    landed2 = _chip_exchange(chipsums, sp_list, name="chip_exchange")
    halves = [_chip_sum(specs[k], s, l, kc_arr, name="chip_sum_" + k) for k, s, l in zip(big_names, chipsums, landed2)]
    shards = dict(zip(big_names, _pair_share(halves, sp_list, name="pair_share")))

    grads = dict(w_ada=g_w_ada[None], b_ada=g_b_ada, norm1=g_norm1, w_in=shards["w_in"][None], rnn_conv_w=g_conv4_sh[None],
                 rnn_conv_b=g_conv4_b, w_rg_a=g_w_rg_a[None], b_rg_a=g_b_rg_a, w_rg_i=g_w_rg_i[None], b_rg_i=g_b_rg_i,
                 rg_lambda=g_lam, w_o_rnn=shards["w_o_rnn"][None], w_o_attn=shards["w_o_attn"][None], attn_sinks=g_sinks,
                 rel_bias=g_rel, w_out=shards["w_out"][None], norm2=g_norm2, w_up=shards["w_up"][None],
                 ffn_conv_w=g_conv3_sh[None], ffn_conv_b=g_conv3_b, w_down=shards["w_down"][None], norm_f=g_norm_f.reshape(D))
    weights = dict(w_ada=w_ada, b_ada=b_ada, norm1=norm1, w_in=w_in, rnn_conv_w=rnn_conv_w, rnn_conv_b=rnn_conv_b, w_rg_a=w_rg_a,
                   b_rg_a=b_rg_a, w_rg_i=w_rg_i, b_rg_i=b_rg_i, rg_lambda=rg_lambda, w_o_rnn=w_o_rnn, w_o_attn=w_o_attn,
                   attn_sinks=attn_sinks, rel_bias=rel_bias, w_out=w_out, norm2=norm2, w_up=w_up, ffn_conv_w=ffn_conv_w,
                   ffn_conv_b=ffn_conv_b, w_down=w_down, norm_f=norm_f)
    moms = dict(w_ada=(m_w_ada, v_w_ada), b_ada=(m_b_ada, v_b_ada), norm1=(m_norm1, v_norm1), w_in=(m_w_in, v_w_in),
                rnn_conv_w=(m_rnn_conv_w, v_rnn_conv_w), rnn_conv_b=(m_rnn_conv_b, v_rnn_conv_b), w_rg_a=(m_w_rg_a, v_w_rg_a),
                b_rg_a=(m_b_rg_a, v_b_rg_a), w_rg_i=(m_w_rg_i, v_w_rg_i), b_rg_i=(m_b_rg_i, v_b_rg_i),
                rg_lambda=(m_rg_lambda, v_rg_lambda), w_o_rnn=(m_w_o_rnn, v_w_o_rnn), w_o_attn=(m_w_o_attn, v_w_o_attn),
                attn_sinks=(m_attn_sinks, v_attn_sinks), rel_bias=(m_rel_bias, v_rel_bias), w_out=(m_w_out, v_w_out),
                norm2=(m_norm2, v_norm2), w_up=(m_w_up, v_w_up), ffn_conv_w=(m_ffn_conv_w, v_ffn_conv_w),
                ffn_conv_b=(m_ffn_conv_b, v_ffn_conv_b), w_down=(m_w_down, v_w_down), norm_f=(m_norm_f, v_norm_f))
    names = list(weights)
    grads = {k: grads[k].reshape(weights[k].shape) for k in names}
    large = ["w_ada"] + big_names
    delta, new_m, new_v = {}, {}, {}
    for k in large:
        shp = weights[k].shape
        two = lambda a: a.reshape(shp[-2], shp[-1])
        d_, m_, v_ = _adamw(two(weights[k]), two(grads[k]), two(moms[k][0]), two(moms[k][1]), name="adamw_" + k)
        delta[k], new_m[k], new_v[k] = d_.reshape(shp), m_.reshape(shp), v_.reshape(shp)
    rest = [k for k in names if k not in large]
    rest_shapes = [weights[k].shape for k in rest]
    d_, m_, v_ = _adamw(_pack([weights[k] for k in rest]), _pack([grads[k] for k in rest]),
                        _pack([moms[k][0] for k in rest]), _pack([moms[k][1] for k in rest]), name="adamw_small")
    for k, dd, mm_, vv in zip(rest, _unpack(d_, rest_shapes), _unpack(m_, rest_shapes), _unpack(v_, rest_shapes)):
        delta[k], new_m[k], new_v[k] = dd, mm_, vv

    return (loss, grad_x[None], *[grads[k] for k in names], *[delta[k] for k in names],
            *[new_m[k] for k in names], *[new_v[k] for k in names])
```

```python
import math

import numpy as np
import jax
import jax.numpy as jnp
from jax import lax
from jax.experimental import pallas as pl
from jax.experimental.pallas import tpu as pltpu

F32 = jnp.float32
BF16 = jnp.bfloat16
MESH = pl.DeviceIdType.MESH
ANY = pl.BlockSpec(memory_space=pl.ANY)

EPS = 1e-6
NEG_INF = -1e30
HEAD_DIM = 64
BLOCK = 128
NUM_EXACT = 16
MAX_DISTANCE = 128
RG_C = 8.0
ADAM_LR, ADAM_B1, ADAM_B2, ADAM_EPS, ADAM_WD, ADAM_STEP = 0.001, 0.9, 0.999, 1e-08, 0.01, 10
N_CHIPS = 4
N_DEV = 8
SUBLANES = 8
LANES = 128
VMEM_LIMIT_BYTES = 48 * 1024 * 1024
GELU_C0 = math.sqrt(2.0 / math.pi)
GELU_C1 = 0.044715


def _pick(dim, pref, align):
    if dim <= pref:
        return dim
    t = (pref // align) * align
    while t >= align:
        if dim % t == 0:
            return t
        t -= align
    return dim


def _params(sem):
    return pltpu.CompilerParams(dimension_semantics=sem, vmem_limit_bytes=VMEM_LIMIT_BYTES)


def _call(name, body, grid, in_specs, out_specs, out_shape, scratch=(), nsp=0, sem=None):
    sem = sem or ("parallel",) * (len(grid) - 1) + ("arbitrary",)
    if nsp:
        gs = pltpu.PrefetchScalarGridSpec(num_scalar_prefetch=nsp, grid=grid, in_specs=in_specs,
                                          out_specs=out_specs, scratch_shapes=list(scratch))
        return pl.pallas_call(body, name=name, grid_spec=gs, out_shape=out_shape, compiler_params=_params(sem))
    return pl.pallas_call(body, name=name, grid=grid, in_specs=in_specs, out_specs=out_specs,
                          out_shape=out_shape, scratch_shapes=list(scratch), compiler_params=_params(sem))


def _sds(shape, dtype):
    return jax.ShapeDtypeStruct(shape, dtype)


def _T(tr, tc, off=0):
    return pl.BlockSpec((tr, tc), lambda j, i: (i, j + off))


def _P(rows, tc, off=0):
    return pl.BlockSpec((rows, tc), lambda j, i: (0, j + off))


def _gelu(x):
    t = jnp.tanh(GELU_C0 * (x + GELU_C1 * x * x * x))
    return 0.5 * x * (1.0 + t)


def _gelu_and_grad(x):
    t = jnp.tanh(GELU_C0 * (x + GELU_C1 * x * x * x))
    g = 0.5 * x * (1.0 + t)
    dg = 0.5 * (1.0 + t) + 0.5 * x * (1.0 - t * t) * GELU_C0 * (1.0 + 3.0 * GELU_C1 * x * x)
    return g, dg


def _sigmoid(x):
    return 1.0 / (1.0 + jnp.exp(-x))


def _accum(ref, val, first):
    @pl.when(first)
    def _():
        ref[...] = val

    @pl.when(jnp.logical_not(first))
    def _():
        ref[...] += val


def _colsum(v):
    return jnp.sum(v, axis=0, keepdims=True)


CHUNK = 16


def _chunks(n_rows, ch, step):
    def it(i, carry):
        step(pl.multiple_of(i * ch, ch))
        return carry

    lax.fori_loop(0, n_rows // ch, it, 0)


def _fold(v):
    return v[0:SUBLANES, :] + v[SUBLANES:2 * SUBLANES, :]


def _shift_down(xe, d, ch):
    return xe[SUBLANES:SUBLANES + ch, :] if d == 0 else pltpu.roll(xe, d, 0)[SUBLANES:SUBLANES + ch, :]


def _shift_up(xe, d, ch):
    return xe[0:ch, :] if d == 0 else pltpu.roll(xe, ch + SUBLANES - d, 0)[0:ch, :]


def _mm(a, b, *, name, ta=False, tb=False, out_dtype=F32, tm=1024, tn=1024, tk=2048, exact=False, bias=None):
    if ta:
        K, M = a.shape
    else:
        M, K = a.shape
    if tb:
        N, K2 = b.shape
    else:
        K2, N = b.shape
    assert K == K2, (a.shape, b.shape, ta, tb)
    tm, tn, tk = _pick(M, tm, LANES), _pick(N, tn, LANES), _pick(K, tk, LANES)
    nk = K // tk
    cdt = F32 if exact else BF16
    prec = lax.Precision.HIGHEST if exact else None
    dims = (((0 if ta else 1,), (1 if tb else 0,)), ((), ()))

    def body(*refs):
        a_ref, b_ref = refs[0], refs[1]
        bias_ref = refs[2] if bias is not None else None
        o_ref = refs[3] if bias is not None else refs[2]
        part = lax.dot_general(a_ref[...].astype(cdt), b_ref[...].astype(cdt), dims,
                               preferred_element_type=F32, precision=prec)

        def finish(r):
            if bias is not None:
                r = r + bias_ref[...]
            o_ref[...] = r.astype(out_dtype)

        if nk == 1:
            finish(part)
            return
        acc_ref = refs[-1]
        k = pl.program_id(2)

        @pl.when(k == 0)
        def _():
            acc_ref[...] = part

        @pl.when(jnp.logical_and(k > 0, k < nk - 1))
        def _():
            acc_ref[...] += part

        @pl.when(k == nk - 1)
        def _():
            finish(acc_ref[...] + part)

    a_spec = pl.BlockSpec((tk, tm), lambda i, j, k: (k, i)) if ta else pl.BlockSpec((tm, tk), lambda i, j, k: (i, k))
    b_spec = pl.BlockSpec((tn, tk), lambda i, j, k: (j, k)) if tb else pl.BlockSpec((tk, tn), lambda i, j, k: (k, j))
    in_specs, args = [a_spec, b_spec], [a, b]
    if bias is not None:
        in_specs.append(pl.BlockSpec((1, tn), lambda i, j, k: (0, j)))
        args.append(bias)
    scratch = [pltpu.VMEM((tm, tn), F32)] if nk > 1 else []
    return _call(name, body, (M // tm, N // tn, nk), in_specs, pl.BlockSpec((tm, tn), lambda i, j, k: (i, j)),
                 _sds((M, N), out_dtype), scratch=scratch, sem=("parallel", "parallel", "arbitrary"))(*args)


def _ew(name, fn, ins, out_dtypes, S, C, rows=(), tr=256, tc=512):
    g = C
    for _, off in ins:
        g = math.gcd(g, off) if off else g
    tc = _pick(g, tc, LANES)
    tr = _pick(S, tr, 16)
    n_in, n_row, n_out = len(ins), len(rows), len(out_dtypes)

    ch = min(CHUNK, tr)

    def body(*refs):
        row_vals = [r[...] for r in refs[n_in:n_in + n_row]]

        def step(r0):
            vals = [r[pl.ds(r0, ch), :].astype(F32) for r in refs[:n_in]]
            for o_ref, o in zip(refs[n_in + n_row:], fn(*vals, *row_vals)):
                o_ref[pl.ds(r0, ch), :] = o.astype(o_ref.dtype)

        _chunks(tr, ch, step)

    in_specs = [_T(tr, tc, off // tc) for _, off in ins] + [_P(1, tc) for _ in rows]
    res = _call(name, body, (C // tc, S // tr), in_specs, [_T(tr, tc) for _ in out_dtypes],
                [_sds((S, C), d) for d in out_dtypes], sem=("parallel", "parallel"))(*[a for a, _ in ins], *rows)
    return res


def _adaln_fwd(x, norm, scale, shift, *, name, t=None, gate=None):
    S, D = x.shape
    tr = _pick(S, 256, 16)
    resid = t is not None

    def body(*refs):
        if resid:
            x_ref, t_ref, g_ref, n_ref, sc_ref, sh_ref, u_ref, h_ref = refs
            h = x_ref[...] + g_ref[...] * t_ref[...]
            h_ref[...] = h
        else:
            x_ref, n_ref, sc_ref, sh_ref, u_ref = refs
            h = x_ref[...]
        r = lax.rsqrt(jnp.mean(h * h, axis=-1, keepdims=True) + EPS)
        u_ref[...] = (h * r * (n_ref[...] * (1.0 + sc_ref[...])) + sh_ref[...]).astype(BF16)

    full, row = _T(tr, D), _P(1, D)
    if resid:
        return _call(name, body, (1, S // tr), [full, full, row, row, row, row], [full, full],
                     [_sds((S, D), BF16), _sds((S, D), F32)])(x, t, gate, norm, scale, shift)
    return _call(name, body, (1, S // tr), [full, row, row, row], full, _sds((S, D), BF16))(x, norm, scale, shift)


def _adaln_bwd(h, du, dres, norm, scale, *, name, t=None, gate=None):
    S, D = h.shape
    tr = _pick(S, 256, 16)
    gated = t is not None

    def body(*refs):
        if gated:
            h_ref, du_ref, dr_ref, n_ref, sc_ref, t_ref, g_ref, dh_ref, dsh_ref, dsc_ref, dn_ref, dt_ref, dg_ref = refs
        else:
            h_ref, du_ref, dr_ref, n_ref, sc_ref, dh_ref, dsh_ref, dsc_ref, dn_ref = refs
        first = pl.program_id(1) == 0
        hv, duv = h_ref[...], du_ref[...]
        r = lax.rsqrt(jnp.mean(hv * hv, axis=-1, keepdims=True) + EPS)
        xn = hv * r
        one_sc = 1.0 + sc_ref[...]
        dxn = duv * (n_ref[...] * one_sc)
        dh = dr_ref[...] + r * (dxn - xn * jnp.mean(dxn * xn, axis=-1, keepdims=True))
        dh_ref[...] = dh
        dux = duv * xn
        _accum(dsh_ref, _colsum(duv), first)
        _accum(dsc_ref, _colsum(dux * n_ref[...]), first)
        _accum(dn_ref, _colsum(dux * one_sc), first)
        if gated:
            dt_ref[...] = (dh * g_ref[...]).astype(BF16)
            _accum(dg_ref, _colsum(dh * t_ref[...]), first)

    full, row = _T(tr, D), _P(1, D)
    rowo = _sds((1, D), F32)
    if gated:
        return _call(name, body, (1, S // tr), [full, full, full, row, row, full, row],
                     [full, row, row, row, full, row],
                     [_sds((S, D), F32), rowo, rowo, rowo, _sds((S, D), BF16), rowo])(h, du, dres, norm, scale, t, gate)
    return _call(name, body, (1, S // tr), [full, full, full, row, row], [full, row, row, row],
                 [_sds((S, D), F32), rowo, rowo, rowo])(h, du, dres, norm, scale)


def _final(h1, t2, gate2, norm_f, tgt, *, name):
    S, D = h1.shape
    tr = _pick(S, 256, 16)

    def body(h_ref, t_ref, g_ref, n_ref, y_ref, dh_ref, dt_ref, loss_ref, dn_ref, dg_ref):
        first = pl.program_id(1) == 0
        tv = t_ref[...]
        h2 = h_ref[...] + g_ref[...] * tv
        r = lax.rsqrt(jnp.mean(h2 * h2, axis=-1, keepdims=True) + EPS)
        xn = h2 * r
        e = xn * n_ref[...] - y_ref[...]
        part = 0.5 * jnp.sum(jnp.mean(e * e, axis=-1, keepdims=True), axis=0, keepdims=True)
        _accum(loss_ref, jnp.broadcast_to(part, (SUBLANES, LANES)), first)
        dy = e * (1.0 / D)
        _accum(dn_ref, _colsum(dy * xn), first)
        dxn = dy * n_ref[...]
        dh2 = r * (dxn - xn * jnp.mean(dxn * xn, axis=-1, keepdims=True))
        dh_ref[...] = dh2
        dt_ref[...] = (dh2 * g_ref[...]).astype(BF16)
        _accum(dg_ref, _colsum(dh2 * tv), first)

    full, row = _T(tr, D), _P(1, D)
    rowo = _sds((1, D), F32)
    return _call(name, body, (1, S // tr), [full, full, row, row, full],
                 [full, full, _P(SUBLANES, LANES), row, row],
                 [_sds((S, D), F32), _sds((S, D), BF16), _sds((SUBLANES, LANES), F32), rowo, rowo])(h1, t2, gate2, norm_f, tgt)


HALO = 16


def _halo_prev(tr, tc, off=0):
    return pl.BlockSpec((HALO, tc), lambda j, i: (jnp.maximum(i * (tr // HALO) - 1, 0), j + off))


def _halo_next(tr, tc, n_slabs, off=0):
    return pl.BlockSpec((HALO, tc), lambda j, i: (jnp.minimum((i + 1) * (tr // HALO), n_slabs - 1), j + off))


def _fill_past(buf, prev_ref, cur_ref, first, tr):
    buf[0:HALO, :] = jnp.where(first, 0.0, prev_ref[...].astype(F32))
    buf[HALO:HALO + tr, :] = cur_ref[...].astype(F32)


def _fill_future(buf, cur_ref, next_ref, last, tr):
    buf[0:tr, :] = cur_ref[...].astype(F32)
    buf[tr:tr + HALO, :] = jnp.where(last, 0.0, next_ref[...].astype(F32))


def _al8(r):
    return r if isinstance(r, int) else pl.multiple_of(r, SUBLANES)


def _past_rows(buf, r0, taps):
    xe = buf[pl.ds(_al8(r0 + HALO - SUBLANES), CHUNK + SUBLANES), :]
    return [_shift_down(xe, d, CHUNK) for d in range(taps)]


def _future_rows(buf, r0, taps):
    xe = buf[pl.ds(_al8(r0), CHUNK + SUBLANES), :]
    return [_shift_up(xe, d, CHUNK) for d in range(taps)]


def _taps(w_ref):
    return [w_ref[k:k + 1, :] for k in range(w_ref.shape[0])]


def _conv(xs, w, b):
    taps = len(xs)
    y = w[taps - 1] * xs[0] if b is None else b + w[taps - 1] * xs[0]
    for d in range(1, taps):
        y = y + w[taps - 1 - d] * xs[d]
    return y


def _conv_fwd(x, off, C, w, b, *, name):
    S = x.shape[0]
    taps = w.shape[0]
    tc = _pick(math.gcd(C, off) if off else C, 512, LANES)
    tr = _pick(S, 256, 16)

    def body(x_ref, p_ref, w_ref, b_ref, y_ref, buf):
        _fill_past(buf, p_ref, x_ref, pl.program_id(1) == 0, tr)
        wt, bv = _taps(w_ref), b_ref[...]

        def step(r0):
            y_ref[pl.ds(r0, CHUNK), :] = _conv(_past_rows(buf, r0, taps), wt, bv)

        _chunks(tr, CHUNK, step)

    return _call(name, body, (C // tc, S // tr),
                 [_T(tr, tc, off // tc), _halo_prev(tr, tc, off // tc), _P(taps, tc), _P(1, tc)],
                 _T(tr, tc), _sds((S, C), F32), scratch=[pltpu.VMEM((tr + HALO, tc), F32)],
                 sem=("parallel", "parallel"))(x, x, w, b)


def _finish_conv_grads(acc, base, taps, dw_ref, db_ref, first):
    @pl.when(first)
    def _():
        dw_ref[...] = jnp.zeros_like(dw_ref)

    for d in range(taps):
        dw_ref[taps - 1 - d:taps - d, :] += _colsum(acc[base + SUBLANES * d:base + SUBLANES * (d + 1), :])
    _accum(db_ref, _colsum(acc[base + SUBLANES * taps:base + SUBLANES * (taps + 1), :]), first)


def _add_conv_grads(acc, base, dy, xs):
    taps = len(xs)
    for d in range(taps):
        acc[base + SUBLANES * d:base + SUBLANES * (d + 1), :] += _fold(dy * xs[d])
    acc[base + SUBLANES * taps:base + SUBLANES * (taps + 1), :] += _fold(dy)


def _conv_bwd(dy, x, off, C, w, *, name):
    S = dy.shape[0]
    taps = w.shape[0]
    tc = _pick(math.gcd(C, off) if off else C, 512, LANES)
    tr = _pick(S, 256, 16)
    n_slabs = S // HALO

    def body(dy_ref, nx_ref, x_ref, p_ref, w_ref, dx_ref, dw_ref, db_ref, fbuf, pbuf, acc):
        i = pl.program_id(1)
        first = i == 0
        _fill_future(fbuf, dy_ref, nx_ref, i == S // tr - 1, tr)
        _fill_past(pbuf, p_ref, x_ref, first, tr)
        acc[...] = jnp.zeros_like(acc)
        wt = _taps(w_ref)

        def step(r0):
            dys = _future_rows(fbuf, r0, taps)
            dx_ref[pl.ds(r0, CHUNK), :] = _conv(dys, wt, None).astype(BF16)
            _add_conv_grads(acc, 0, dys[0], _past_rows(pbuf, r0, taps))

        _chunks(tr, CHUNK, step)
        _finish_conv_grads(acc, 0, taps, dw_ref, db_ref, first)

    return _call(name, body, (C // tc, S // tr),
                 [_T(tr, tc), _halo_next(tr, tc, n_slabs), _T(tr, tc, off // tc), _halo_prev(tr, tc, off // tc),
                  _P(taps, tc)],
                 [_T(tr, tc), _P(taps, tc), _P(1, tc)],
                 [_sds((S, C), BF16), _sds((taps, C), F32), _sds((1, C), F32)],
                 scratch=[pltpu.VMEM((tr + HALO, tc), F32), pltpu.VMEM((tr + HALO, tc), F32),
                          pltpu.VMEM((SUBLANES * (taps + 1), tc), F32)])(dy, dy, x, x, w)


def _geglu_fwd(up, w, b, F, *, name):
    S = up.shape[0]
    taps = w.shape[0]
    tc = _pick(F, 512, LANES)
    tr = _pick(S, 256, 16)
    nf = F // tc

    def body(g_ref, gp_ref, v_ref, vp_ref, wg_ref, wv_ref, bg_ref, bv_ref, a_ref, gbuf, vbuf):
        first = pl.program_id(1) == 0
        _fill_past(gbuf, gp_ref, g_ref, first, tr)
        _fill_past(vbuf, vp_ref, v_ref, first, tr)
        wg, wv, bg, bv = _taps(wg_ref), _taps(wv_ref), bg_ref[...], bv_ref[...]

        def step(r0):
            g = _conv(_past_rows(gbuf, r0, taps), wg, bg)
            v = _conv(_past_rows(vbuf, r0, taps), wv, bv)
            a_ref[pl.ds(r0, CHUNK), :] = (_gelu(g) * v).astype(BF16)

        _chunks(tr, CHUNK, step)

    buf = pltpu.VMEM((tr + HALO, tc), F32)
    return _call(name, body, (nf, S // tr),
                 [_T(tr, tc), _halo_prev(tr, tc), _T(tr, tc, nf), _halo_prev(tr, tc, nf),
                  _P(taps, tc), _P(taps, tc, nf), _P(1, tc), _P(1, tc, nf)],
                 _T(tr, tc), _sds((S, F), BF16), scratch=[buf, buf], sem=("parallel", "parallel"))(up, up, up, up, w, w, b, b)


def _ffn_bwd(up, w, b, da, F, *, name):
    S = up.shape[0]
    taps = w.shape[0]
    tc = _pick(F, 512, LANES)
    tr = _pick(S, 256, 16)
    nf, nr = F // tc, S // tr
    ext = tr + HALO
    n_slabs = S // HALO

    assert HALO == CHUNK
    v_base = SUBLANES * (taps + 1)

    def body(g_ref, gp_ref, gn_ref, v_ref, vp_ref, vn_ref, wg_ref, wv_ref, bg_ref, bv_ref, da_ref, dan_ref,
             dxg_ref, dxv_ref, dwg_ref, dwv_ref, dbg_ref, dbv_ref, gbuf, vbuf, dgbuf, dvbuf, acc):
        i = pl.program_id(1)
        first, last = i == 0, i == nr - 1
        _fill_past(gbuf, gp_ref, g_ref, first, tr)
        _fill_past(vbuf, vp_ref, v_ref, first, tr)
        gbuf[HALO + tr:HALO + ext, :] = gn_ref[...].astype(F32)
        vbuf[HALO + tr:HALO + ext, :] = vn_ref[...].astype(F32)
        acc[...] = jnp.zeros_like(acc)
        wg, wv, bg, bv = _taps(wg_ref), _taps(wv_ref), bg_ref[...], bv_ref[...]

        def grads(r0, da, in_tile):
            gs, vs = _past_rows(gbuf, r0, taps), _past_rows(vbuf, r0, taps)
            ge, dge = _gelu_and_grad(_conv(gs, wg, bg))
            dg = da * _conv(vs, wv, bv) * dge
            dv = da * ge
            dgbuf[pl.ds(_al8(r0), CHUNK), :] = dg
            dvbuf[pl.ds(_al8(r0), CHUNK), :] = dv
            if in_tile:
                _add_conv_grads(acc, 0, dg, gs)
                _add_conv_grads(acc, v_base, dv, vs)

        _chunks(tr, CHUNK, lambda r0: grads(r0, da_ref[pl.ds(r0, CHUNK), :], True))
        grads(tr, jnp.where(last, 0.0, dan_ref[...]), False)

        def back(r0):
            dxg_ref[pl.ds(r0, CHUNK), :] = _conv(_future_rows(dgbuf, r0, taps), wg, None).astype(BF16)
            dxv_ref[pl.ds(r0, CHUNK), :] = _conv(_future_rows(dvbuf, r0, taps), wv, None).astype(BF16)

        _chunks(tr, CHUNK, back)
        _finish_conv_grads(acc, 0, taps, dwg_ref, dbg_ref, first)
        _finish_conv_grads(acc, v_base, taps, dwv_ref, dbv_ref, first)

    xbuf = pltpu.VMEM((HALO + ext, tc), F32)
    dbuf = pltpu.VMEM((ext, tc), F32)
    t, tw, tb = _T(tr, tc), _P(taps, tc), _P(1, tc)
    return _call(name, body, (nf, nr),
                 [t, _halo_prev(tr, tc), _halo_next(tr, tc, n_slabs), _T(tr, tc, nf), _halo_prev(tr, tc, nf),
                  _halo_next(tr, tc, n_slabs, nf), tw, _P(taps, tc, nf), tb, _P(1, tc, nf), t, _halo_next(tr, tc, n_slabs)],
                 [t, t, tw, tw, tb, tb],
                 [_sds((S, F), BF16), _sds((S, F), BF16), _sds((taps, F), F32), _sds((taps, F), F32),
                  _sds((1, F), F32), _sds((1, F), F32)],
                 scratch=[xbuf, xbuf, dbuf, dbuf, pltpu.VMEM((2 * v_base, tc), F32)])(up, up, up, up, up, up, w, w, b, b, da, da)


def _softplus_neg(lam):
    z = -lam
    e = jnp.exp(-jnp.abs(z))
    u = 1.0 + e
    log1p_e = jnp.where(u == 1.0, e, jnp.log(u) * e / jnp.where(u == 1.0, 1.0, u - 1.0))
    sp = jnp.maximum(z, 0.0) + log1p_e
    sg = jnp.where(z >= 0, 1.0 / u, e / u)
    return sp, sg


def _one_minus_exp(x):
    series = -x * (1.0 + x * (0.5 + x * (1.0 / 6.0 + x * (1.0 / 24.0))))
    return jnp.where(x > -0.01, series, 1.0 - jnp.exp(x))


GATE_CHUNK = 64


def _gate_logits(x_ref, wa_ref, wi_ref, pa_buf, pi_buf):
    xb = x_ref[...].astype(BF16)
    pa_buf[...] = jnp.dot(xb, wa_ref[0].astype(BF16), preferred_element_type=F32)
    pi_buf[...] = jnp.dot(xb, wi_ref[0].astype(BF16), preferred_element_type=F32)


def _gate_values(pa, pi, ba, bi, sp, is_t0):
    ra = _sigmoid(pa + ba)
    ia = _sigmoid(pi + bi)
    log_a = -RG_C * ra * sp
    a = jnp.exp(log_a)
    mult = jnp.where(is_t0, 1.0, jnp.sqrt(_one_minus_exp(2.0 * log_a)))
    return ra, ia, a, mult


def _rnn_blockspecs(tr):
    x = pl.BlockSpec((tr, LANES), lambda n, i: (i, n))
    w = pl.BlockSpec((1, LANES, LANES), lambda n, i: (n, 0, 0))
    p = pl.BlockSpec((1, LANES), lambda n, i: (0, n))
    return x, w, p


def _is_t0(r0):
    rows = r0 + lax.broadcasted_iota(jnp.int32, (GATE_CHUNK, LANES), 0)
    return jnp.logical_and(pl.program_id(1) == 0, rows == 0)


def _fold_rows(v):
    out = v[0:SUBLANES, :]
    for k in range(1, v.shape[0] // SUBLANES):
        out = out + v[k * SUBLANES:(k + 1) * SUBLANES, :]
    return out


def _gates_fwd(xc, w_a, b_a, w_i, b_i, lam, *, name):
    S, C = xc.shape
    tr = _pick(S, 1024, GATE_CHUNK)

    def body(x_ref, wa_ref, wi_ref, ba_ref, bi_ref, lam_ref, a_ref, inp_ref, pa_buf, pi_buf):
        _gate_logits(x_ref, wa_ref, wi_ref, pa_buf, pi_buf)
        sp, _ = _softplus_neg(lam_ref[...])
        ba, bi = ba_ref[...], bi_ref[...]

        def step(r0):
            rows = pl.ds(r0, GATE_CHUNK)
            _, ia, a, mult = _gate_values(pa_buf[rows, :], pi_buf[rows, :], ba, bi, sp, _is_t0(r0))
            a_ref[rows, :] = a
            inp_ref[rows, :] = mult * ia * x_ref[rows, :]

        _chunks(tr, GATE_CHUNK, step)

    x, w, p = _rnn_blockspecs(tr)
    buf = pltpu.VMEM((tr, LANES), F32)
    return _call(name, body, (C // LANES, S // tr), [x, w, w, p, p, p], [x, x],
                 [_sds((S, C), F32), _sds((S, C), F32)], scratch=[buf, buf],
                 sem=("parallel", "parallel"))(xc, w_a, w_i, b_a, b_i, lam)


def _gates_bwd(xc, dacc, hprev, w_a, b_a, w_i, b_i, lam, *, name):
    S, C = xc.shape
    nb = C // LANES
    tr = _pick(S, 1024, GATE_CHUNK)

    def body(x_ref, d_ref, hp_ref, wa_ref, wi_ref, ba_ref, bi_ref, lam_ref,
             dx_ref, dwa_ref, dwi_ref, dba_ref, dbi_ref, dlam_ref, pa_buf, pi_buf, dpa_buf, dpi_buf, acc):
        first = pl.program_id(1) == 0
        _gate_logits(x_ref, wa_ref, wi_ref, pa_buf, pi_buf)
        sp, sg = _softplus_neg(lam_ref[...])
        ba, bi = ba_ref[...], bi_ref[...]
        acc[...] = jnp.zeros_like(acc)

        def step(r0):
            rows = pl.ds(r0, GATE_CHUNK)
            t0 = _is_t0(r0)
            xv, dv = x_ref[rows, :], d_ref[rows, :]
            ra, ia, a, mult = _gate_values(pa_buf[rows, :], pi_buf[rows, :], ba, bi, sp, t0)
            d_mult = dv * ia * xv
            d_log = dv * hp_ref[rows, :] * a + jnp.where(t0, 0.0, -d_mult * (a * a) / mult)
            d_pa = d_log * (-RG_C * sp) * ra * (1.0 - ra)
            d_pi = dv * mult * xv * ia * (1.0 - ia)
            dpa_buf[rows, :] = d_pa.astype(BF16)
            dpi_buf[rows, :] = d_pi.astype(BF16)
            dx_ref[rows, :] = dv * mult * ia
            acc[0:SUBLANES, :] += _fold_rows(d_pa)
            acc[SUBLANES:2 * SUBLANES, :] += _fold_rows(d_pi)
            acc[2 * SUBLANES:3 * SUBLANES, :] += _fold_rows(d_log * (-RG_C) * ra)

        _chunks(tr, GATE_CHUNK, step)
        xb, dab, dib = x_ref[...].astype(BF16), dpa_buf[...], dpi_buf[...]
        nt = (((1,), (1,)), ((), ()))
        tn = (((0,), (0,)), ((), ()))
        dx_ref[...] += (lax.dot_general(dab, wa_ref[0].astype(BF16), nt, preferred_element_type=F32)
                        + lax.dot_general(dib, wi_ref[0].astype(BF16), nt, preferred_element_type=F32))
        _accum(dwa_ref, lax.dot_general(xb, dab, tn, preferred_element_type=F32)[None], first)
        _accum(dwi_ref, lax.dot_general(xb, dib, tn, preferred_element_type=F32)[None], first)
        _accum(dba_ref, _colsum(acc[0:SUBLANES, :]), first)
        _accum(dbi_ref, _colsum(acc[SUBLANES:2 * SUBLANES, :]), first)
        _accum(dlam_ref, _colsum(acc[2 * SUBLANES:3 * SUBLANES, :]) * (-sg), first)

    x, w, p = _rnn_blockspecs(tr)
    row = _sds((1, C), F32)
    fbuf, bbuf = pltpu.VMEM((tr, LANES), F32), pltpu.VMEM((tr, LANES), BF16)
    return _call(name, body, (nb, S // tr), [x, x, x, w, w, p, p, p], [x, w, w, p, p, p],
                 [_sds((S, C), F32), _sds((nb, LANES, LANES), F32), _sds((nb, LANES, LANES), F32), row, row, row],
                 scratch=[fbuf, fbuf, bbuf, bbuf, pltpu.VMEM((3 * SUBLANES, LANES), F32)])(
                     xc, dacc, hprev, w_a, w_i, b_a, b_i, lam)


def _scan_fwd(a, inp, *, name):
    S, C = a.shape
    tc = _pick(C, 1280, LANES)
    tr = _pick(S, 256, SUBLANES)

    def body(a_ref, b_ref, h_ref, hp_ref, carry):
        @pl.when(pl.program_id(1) == 0)
        def _():
            carry[...] = jnp.zeros_like(carry)

        rows = lax.broadcasted_iota(jnp.int32, (SUBLANES, tc), 0)

        def slab(s, h):
            base = pl.multiple_of(s * SUBLANES, SUBLANES)
            av, bv = a_ref[pl.ds(base, SUBLANES), :], b_ref[pl.ds(base, SUBLANES), :]
            ho = jnp.zeros((SUBLANES, tc), F32)
            po = jnp.zeros((SUBLANES, tc), F32)
            for r in range(SUBLANES):
                po = jnp.where(rows == r, h, po)
                h = av[r:r + 1, :] * h + bv[r:r + 1, :]
                ho = jnp.where(rows == r, h, ho)
            h_ref[pl.ds(base, SUBLANES), :] = ho
            hp_ref[pl.ds(base, SUBLANES), :] = po
            return h

        carry[...] = lax.fori_loop(0, tr // SUBLANES, slab, carry[...])

    t = _T(tr, tc)
    return _call(name, body, (C // tc, S // tr), [t, t], [t, t], [_sds((S, C), F32), _sds((S, C), F32)],
                 scratch=[pltpu.VMEM((1, tc), F32)])(a, inp)


def _scan_bwd(a, dh, *, name):
    S, C = a.shape
    tc = _pick(C, 1280, LANES)
    tr = _pick(S, 256, SUBLANES)
    nr = S // tr

    def body(a_ref, d_ref, o_ref, carry):
        @pl.when(pl.program_id(1) == 0)
        def _():
            carry[...] = jnp.zeros_like(carry)

        rows = lax.broadcasted_iota(jnp.int32, (SUBLANES, tc), 0)
        n_slabs = tr // SUBLANES

        def slab(s, g):
            base = pl.multiple_of((n_slabs - 1 - s) * SUBLANES, SUBLANES)
            av, dv = a_ref[pl.ds(base, SUBLANES), :], d_ref[pl.ds(base, SUBLANES), :]
            out = jnp.zeros((SUBLANES, tc), F32)
            for r in range(SUBLANES - 1, -1, -1):
                acc = dv[r:r + 1, :] + g
                out = jnp.where(rows == r, acc, out)
                g = av[r:r + 1, :] * acc
            o_ref[pl.ds(base, SUBLANES), :] = out
            return g

        carry[...] = lax.fori_loop(0, n_slabs, slab, carry[...])

    t = pl.BlockSpec((tr, tc), lambda j, i: (nr - 1 - i, j))
    return _call(name, body, (C // tc, nr), [t, t], t, _sds((S, C), F32), scratch=[pltpu.VMEM((1, tc), F32)])(a, dh)


SCALE = HEAD_DIM ** -0.5


def _bucket_table(num_buckets):
    qi = np.arange(BLOCK)[:, None]
    j = np.arange(BLOCK)[None, :]
    dist = np.where(j <= qi, qi - j, qi + BLOCK - j)
    d = np.maximum(dist, 1).astype(np.float64)
    large = NUM_EXACT + (np.log(d / NUM_EXACT) / math.log(MAX_DISTANCE / NUM_EXACT) * (num_buckets - NUM_EXACT)).astype(np.int32)
    large = np.minimum(large, num_buckets - 1)
    return np.where(dist < NUM_EXACT, dist, large).astype(np.int32)


def _bias_table(rel_bias, n_heads, *, name):
    nbk = rel_bias.shape[0]
    bucket = jnp.asarray(_bucket_table(nbk))

    def body(rb_ref, bk_ref, o_ref):
        t, h = pl.program_id(0), pl.program_id(1)
        bk = bk_ref[...]
        acc = jnp.zeros((BLOCK, BLOCK), F32)
        for b in range(nbk):
            acc = jnp.where(bk == b, rb_ref[b, h], acc)
        o_ref[0, 0] = jnp.where(jnp.logical_and(t == 0, jnp.logical_not(_own_block())), NEG_INF, acc)

    return _call(name, body, (2, n_heads),
                 [pl.BlockSpec(memory_space=pltpu.SMEM), pl.BlockSpec((BLOCK, BLOCK), lambda t, h: (0, 0))],
                 pl.BlockSpec((1, 1, BLOCK, BLOCK), lambda t, h: (t, h, 0, 0)),
                 _sds((2, n_heads, BLOCK, BLOCK), F32), sem=("arbitrary", "arbitrary"))(rel_bias, bucket)


def _bias_table_bwd(dbias, nbk, *, name):
    n_heads = dbias.shape[0]
    bucket = jnp.asarray(_bucket_table(nbk))

    def body(db_ref, bk_ref, o_ref):
        h = pl.program_id(0)
        bk = bk_ref[...]
        dv = db_ref[0]
        rows = lax.broadcasted_iota(jnp.int32, (nbk, LANES), 0)
        lanes = lax.broadcasted_iota(jnp.int32, (nbk, LANES), 1)
        acc = jnp.zeros((nbk, LANES), F32)
        for b in range(nbk):
            s = jnp.sum(jnp.sum(jnp.where(bk == b, dv, 0.0), axis=0, keepdims=True), axis=1, keepdims=True)
            acc = jnp.where(jnp.logical_and(rows == b, lanes == h), s, acc)
        _accum(o_ref, acc, h == 0)

    return _call(name, body, (n_heads,),
                 [pl.BlockSpec((1, BLOCK, BLOCK), lambda h: (h, 0, 0)), pl.BlockSpec((BLOCK, BLOCK), lambda h: (0, 0))],
                 pl.BlockSpec((nbk, LANES), lambda h: (0, 0)), _sds((nbk, LANES), F32), sem=("arbitrary",))(dbias, bucket)


def _own_block(heads=1):
    qi = lax.broadcasted_iota(jnp.int32, (heads * BLOCK, BLOCK), 0) % BLOCK
    j = lax.broadcasted_iota(jnp.int32, (heads * BLOCK, BLOCK), 1)
    return j <= qi


def _to_window(band, own):
    return jnp.where(own, band[:, BLOCK:], band[:, :BLOCK])


def _to_band(win, own):
    return jnp.concatenate([jnp.where(own, 0.0, win), jnp.where(own, win, 0.0)], axis=1)


def _stack_pairs(ref, group):
    return jnp.concatenate([ref[:, (g // 2) * LANES:(g // 2 + 1) * LANES] for g in range(group)], axis=0).astype(F32)


def _own_lanes(group):
    rows = lax.broadcasted_iota(jnp.int32, (group * BLOCK, LANES), 0)
    lanes = lax.broadcasted_iota(jnp.int32, (group * BLOCK, LANES), 1)
    return (lanes < HEAD_DIM) == ((rows // BLOCK) % 2 == 0)


def _unstack_pairs(stacked, o_ref, group):
    lo = lax.broadcasted_iota(jnp.int32, (BLOCK, LANES), 1) < HEAD_DIM
    for p in range(group // 2):
        even = stacked[2 * p * BLOCK:(2 * p + 1) * BLOCK, :]
        odd = stacked[(2 * p + 1) * BLOCK:(2 * p + 2) * BLOCK, :]
        o_ref[:, p * LANES:(p + 1) * LANES] = jnp.where(lo, even, odd).astype(o_ref.dtype)


def _group_probs(qm, kb, b_ref, s_ref, own, group):
    band = lax.dot_general(qm, kb, (((1,), (1,)), ((), ())), preferred_element_type=F32)
    bias = jnp.concatenate([b_ref[0, g] for g in range(group)], axis=0)
    sink = jnp.concatenate([jnp.broadcast_to(s_ref[g][:, 0:1], (BLOCK, 1)) for g in range(group)], axis=0)
    s = _to_window(band, own) + bias
    m = jnp.maximum(jnp.max(s, axis=-1, keepdims=True), sink)
    p = jnp.exp(s - m)
    es = jnp.exp(sink - m)
    inv = 1.0 / (jnp.sum(p, axis=-1, keepdims=True) + es)
    return p, inv, es


def _attn_fwd(proj, bias, sinks_b, n_kv, group, q_off, k_off, v_off, *, name):
    S = proj.shape[0]
    nblk = S // BLOCK
    gw = group * HEAD_DIM
    pairs = group // 2

    def body(q_ref, kc_ref, kp_ref, vc_ref, vp_ref, b_ref, s_ref, o_ref):
        kb = jnp.concatenate([kp_ref[...], kc_ref[...]], axis=0).astype(BF16)
        vb = jnp.concatenate([vp_ref[...], vc_ref[...]], axis=0).astype(BF16)
        own = _own_block(group)
        qm = jnp.where(_own_lanes(group), _stack_pairs(q_ref, group) * SCALE, 0.0).astype(BF16)
        pu, inv, _ = _group_probs(qm, kb, b_ref, s_ref, own, group)
        o = jnp.dot(_to_band(pu, own).astype(BF16), vb, preferred_element_type=F32) * inv
        _unstack_pairs(o, o_ref, group)

    qb, kb0, vb0 = q_off // gw, k_off // LANES, v_off // LANES
    cur = lambda b0: pl.BlockSpec((BLOCK, LANES), lambda j, n: (n, b0 + j))
    prev = lambda b0: pl.BlockSpec((BLOCK, LANES), lambda j, n: (jnp.maximum(n - 1, 0), b0 + j))
    return _call(name, body, (n_kv, nblk),
                 [pl.BlockSpec((BLOCK, gw), lambda j, n: (n, qb + j)), cur(kb0), prev(kb0), cur(vb0), prev(vb0),
                  pl.BlockSpec((1, group, BLOCK, BLOCK), lambda j, n: (jnp.minimum(n, 1), j, 0, 0)),
                  pl.BlockSpec((group, 1, LANES), lambda j, n: (j, 0, 0))],
                 pl.BlockSpec((BLOCK, gw), lambda j, n: (n, j)), _sds((S, n_kv * gw), BF16),
                 sem=("parallel", "parallel"))(proj, proj, proj, proj, proj, bias, sinks_b)


def _attn_bwd(proj, o, do, bias, sinks_b, n_kv, group, q_off, k_off, v_off, *, name):
    S = proj.shape[0]
    nblk = S // BLOCK
    gw = group * HEAD_DIM
    rows_all = group * BLOCK
    nt = (((1,), (1,)), ((), ()))
    tn = (((0,), (0,)), ((), ()))

    def body(q_ref, kc_ref, kp_ref, vc_ref, vp_ref, b_ref, s_ref, o_ref, do_ref,
             dq_ref, dk_ref, dv_ref, db_ref, ds_ref, ck, cv, sacc):
        n = pl.program_id(1)
        lo2 = lax.broadcasted_iota(jnp.int32, (2 * BLOCK, LANES), 1) < HEAD_DIM

        @pl.when(n == 0)
        def _():
            ck[...] = jnp.zeros_like(ck)
            cv[...] = jnp.zeros_like(cv)
            sacc[...] = jnp.zeros_like(sacc)
            db_ref[...] = jnp.zeros_like(db_ref)

        @pl.when(n < nblk)
        def _():
            kb = jnp.concatenate([kp_ref[...], kc_ref[...]], axis=0).astype(BF16)
            vb = jnp.concatenate([vp_ref[...], vc_ref[...]], axis=0).astype(BF16)
            own = _own_block(group)
            mine = _own_lanes(group)
            qm = jnp.where(mine, _stack_pairs(q_ref, group) * SCALE, 0.0).astype(BF16)
            dom = jnp.where(mine, _stack_pairs(do_ref, group), 0.0)
            pu, inv, es = _group_probs(qm, kb, b_ref, s_ref, own, group)
            pr = pu * inv
            delta = jnp.sum(dom * _stack_pairs(o_ref, group), axis=-1, keepdims=True)
            domb = dom.astype(BF16)
            dp = _to_window(lax.dot_general(domb, vb, nt, preferred_element_type=F32), own)
            ds = pr * (dp - delta)
            for g in range(group):
                db_ref[g] += ds[g * BLOCK:(g + 1) * BLOCK, :]
            sacc[...] += -(es * inv) * delta
            dsb = _to_band(ds, own).astype(BF16)
            _unstack_pairs(jnp.dot(dsb, kb, preferred_element_type=F32) * SCALE, dq_ref, group)
            dk_acc = lax.dot_general(dsb, qm, tn, preferred_element_type=F32)
            dv_acc = lax.dot_general(_to_band(pr, own).astype(BF16), domb, tn, preferred_element_type=F32)
            dkf = jnp.where(lo2, dk_acc + pltpu.roll(dk_acc, HEAD_DIM, 1), 0.0)
            dvf = jnp.where(lo2, dv_acc + pltpu.roll(dv_acc, HEAD_DIM, 1), 0.0)

            @pl.when(n > 0)
            def _():
                dk_ref[...] = (ck[...] + dkf[0:BLOCK, :]).astype(BF16)
                dv_ref[...] = (cv[...] + dvf[0:BLOCK, :]).astype(BF16)

            ck[...] = dkf[BLOCK:2 * BLOCK, :]
            cv[...] = dvf[BLOCK:2 * BLOCK, :]

        @pl.when(n == nblk)
        def _():
            dk_ref[...] = ck[...].astype(BF16)
            dv_ref[...] = cv[...].astype(BF16)
            lane = lax.broadcasted_iota(jnp.int32, (1, LANES), 1)
            tot = jnp.zeros((1, LANES), F32)
            for g in range(group):
                tot = jnp.where(lane == g, _colsum(sacc[g * BLOCK:(g + 1) * BLOCK, :]), tot)
            ds_ref[0] = tot

    qb, kb0, vb0 = q_off // gw, k_off // LANES, v_off // LANES
    last = nblk - 1
    cur = lambda b0: pl.BlockSpec((BLOCK, LANES), lambda j, n: (jnp.minimum(n, last), b0 + j))
    prev = lambda b0: pl.BlockSpec((BLOCK, LANES), lambda j, n: (jnp.clip(n - 1, 0, last), b0 + j))
    qspec = lambda b0: pl.BlockSpec((BLOCK, gw), lambda j, n: (jnp.minimum(n, last), b0 + j))
    kvout = pl.BlockSpec((BLOCK, LANES), lambda j, n: (jnp.maximum(n - 1, 0), j))
    return _call(name, body, (n_kv, nblk + 1),
                 [qspec(qb), cur(kb0), prev(kb0), cur(vb0), prev(vb0),
                  pl.BlockSpec((1, group, BLOCK, BLOCK), lambda j, n: (jnp.minimum(n, 1), j, 0, 0)),
                  pl.BlockSpec((group, 1, LANES), lambda j, n: (j, 0, 0)), qspec(0), qspec(0)],
                 [qspec(0), kvout, kvout, pl.BlockSpec((group, BLOCK, BLOCK), lambda j, n: (j, 0, 0)),
                  pl.BlockSpec((1, 1, LANES), lambda j, n: (j, 0, 0))],
                 [_sds((S, n_kv * gw), BF16), _sds((S, n_kv * LANES), BF16), _sds((S, n_kv * LANES), BF16),
                  _sds((n_kv * group, BLOCK, BLOCK), F32), _sds((n_kv, 1, LANES), F32)],
                 scratch=[pltpu.VMEM((BLOCK, LANES), F32), pltpu.VMEM((BLOCK, LANES), F32),
                          pltpu.VMEM((rows_all, 1), F32)])(proj, proj, proj, proj, proj, bias, sinks_b, o, do)


def _adamw(w, g, m, v, *, name):
    R, C = w.shape
    tr = _pick(R, 128, SUBLANES)
    bc1 = 1.0 - ADAM_B1 ** ADAM_STEP
    bc2 = 1.0 - ADAM_B2 ** ADAM_STEP

    def body(w_ref, g_ref, m_ref, v_ref, d_ref, nm_ref, nv_ref):
        gv = g_ref[...]
        nm = ADAM_B1 * m_ref[...] + (1.0 - ADAM_B1) * gv
        nv = ADAM_B2 * v_ref[...] + (1.0 - ADAM_B2) * (gv * gv)
        d_ref[...] = -ADAM_LR * ((nm / bc1) / (jnp.sqrt(nv / bc2) + ADAM_EPS) + ADAM_WD * w_ref[...])
        nm_ref[...] = nm
        nv_ref[...] = nv

    t = pl.BlockSpec((tr, C), lambda i: (i, 0))
    o = _sds((R, C), F32)
    return _call(name, body, (R // tr,), [t, t, t, t], [t, t, t], [o, o, o], sem=("parallel",))(w, g, m, v)


def _sum_devices(packs, *, name):
    _, R, C = packs.shape
    tr = _pick(R, 512, SUBLANES)

    def body(p_ref, o_ref):
        acc = p_ref[0]
        for d in range(1, N_DEV):
            acc = acc + p_ref[d]
        o_ref[...] = acc

    return _call(name, body, (R // tr,), [pl.BlockSpec((N_DEV, tr, C), lambda i: (0, i, 0))],
                 pl.BlockSpec((tr, C), lambda i: (i, 0)), _sds((R, C), F32), sem=("parallel",))(packs)


class _Big:
    def __init__(self, kind, R, C):
        self.kind, self.R, self.C = kind, R, C
        self.hr = R // 2
        self.full = (R, N_CHIPS * C) if kind == "col" else (N_CHIPS * R, C)

    def region(self, ref, k, c):
        if self.kind == "col":
            return ref.at[pl.ds(c * self.hr, self.hr), pl.ds(k * self.C, self.C)]
        return ref.at[pl.ds(k * self.R + c * self.hr, self.hr), :]

    def shard(self, ref, k):
        if self.kind == "col":
            return ref.at[:, pl.ds(k * self.C, self.C)]
        return ref.at[pl.ds(k * self.R, self.R), :]


def _pair_sum(spec, g_full, land, c_arr, *, name):
    hr, C = spec.hr, spec.C
    tr = _pick(hr, 128, 16)
    nr = hr // tr

    def body(c_ref, g_ref, l_ref, o_ref):
        o_ref[0] = (g_ref[...] + l_ref[0]).astype(BF16)

    if spec.kind == "col":
        gspec = pl.BlockSpec((tr, C), lambda k, i, c_ref: (c_ref[0] * nr + i, k))
    else:
        gspec = pl.BlockSpec((tr, C), lambda k, i, c_ref: (k * 2 * nr + c_ref[0] * nr + i, 0))
    lspec = pl.BlockSpec((1, tr, C), lambda k, i, c_ref: (k, i, 0))
    return _call(name, body, (N_CHIPS, nr), [gspec, lspec], lspec, _sds((N_CHIPS, hr, C), BF16), nsp=1,
                 sem=("parallel", "parallel"))(c_arr, g_full, land)


def _chip_sum(spec, chipsum, land, kc_arr, *, name):
    hr, C = spec.hr, spec.C
    tr = _pick(hr, 128, 16)
    nr = hr // tr

    def body(kc_ref, s_ref, l_ref, o_ref):
        acc = s_ref[0].astype(F32)
        for j in range(N_CHIPS - 1):
            acc = acc + l_ref[j].astype(F32)
        o_ref[...] = acc

    return _call(name, body, (nr,),
                 [pl.BlockSpec((1, tr, C), lambda i, kc_ref: (kc_ref[0], i, 0)),
                  pl.BlockSpec((N_CHIPS - 1, tr, C), lambda i, kc_ref: (0, i, 0))],
                 pl.BlockSpec((tr, C), lambda i, kc_ref: (kc_ref[1] * nr + i, 0)), _sds((spec.R, C), F32), nsp=1,
                 sem=("parallel",))(kc_arr, chipsum, land)


def _place():
    x, y, c = lax.axis_index("x"), lax.axis_index("y"), lax.axis_index("c")
    return x, y, c, [(1 - x, y), (x, 1 - y), (1 - x, 1 - y)]


def _remote(src, dst, send_sem, recv_sem, dev):
    return pltpu.make_async_remote_copy(src_ref=src, dst_ref=dst, send_sem=send_sem, recv_sem=recv_sem,
                                        device_id=dev, device_id_type=MESH)


def _comm_call(name, body, n, out_shapes, n_remote, in_place=True):
    scratch = [pltpu.SemaphoreType.DMA((n_remote,)), pltpu.SemaphoreType.DMA((n_remote,))]
    aliases = {i: i for i in range(n)} if in_place else {}
    return pl.pallas_call(body, name=name, out_shape=out_shapes, in_specs=[ANY] * n, out_specs=[ANY] * n,
                          scratch_shapes=scratch, input_output_aliases=aliases)


def _placed(block, dev):
    m, n = block.shape
    return lax.dynamic_update_slice(jnp.zeros((N_DEV * m, n), block.dtype), block, (dev * m, 0))


def _all_gather(bufs, *, name):
    n = len(bufs)
    per = 7

    def body(*refs):
        outs = refs[n:2 * n]
        send_sems, recv_sems = refs[2 * n:]
        x, y, c, chips = _place()
        me, sibling = (x, y, c), (x, y, 1 - c)
        made = []
        for w in range(n):
            m = outs[w].shape[0] // N_DEV

            def rows(px, py, pc, w=w, m=m):
                return outs[w].at[pl.ds((4 * px + 2 * py + pc) * m, m), :]

            def copy(k, block, to, w=w, rows=rows):
                return _remote(rows(*block), rows(*block), send_sems.at[w * per + k], recv_sems.at[w * per + k], to)

            first = [copy(0, me, sibling)] + [copy(1 + j, me, (*chip, c)) for j, chip in enumerate(chips)]
            for cp in first:
                cp.start()
            made.append((copy, first))
        sends = []
        for w in range(n):
            copy, first = made[w]
            passed = [copy(4 + j, (*chip, c), sibling) for j, chip in enumerate(chips)]
            for j, chip in enumerate(chips):
                copy(1 + j, (*chip, c), me).wait_recv()
                passed[j].start()
            sends.append(first + passed)
        for w in range(n):
            copy, _ = made[w]
            copy(0, sibling, me).wait_recv()
            for j, chip in enumerate(chips):
                copy(4 + j, (*chip, 1 - c), me).wait_recv()
            for cp in sends[w]:
                cp.wait_send()

    outs = [_sds(a.shape, a.dtype) for a in bufs]
    return _comm_call(name, body, n, outs, per * n)(*bufs)


def _cast_into(spec, w, k_arr, *, name):
    R, C = spec.R, spec.C
    tr = _pick(R, 256, 16)
    nr = R // tr

    def body(k_ref, w_ref, o_ref):
        o_ref[...] = w_ref[...].astype(BF16)

    if spec.kind == "col":
        ospec = pl.BlockSpec((tr, C), lambda i, k_ref: (i, k_ref[0]))
    else:
        ospec = pl.BlockSpec((tr, C), lambda i, k_ref: (k_ref[0] * nr + i, 0))
    return _call(name, body, (nr,), [pl.BlockSpec((tr, C), lambda i, k_ref: (i, 0))], ospec,
                 _sds(spec.full, BF16), nsp=1, sem=("parallel",))(k_arr, w)


def _gather_weights(bufs, specs, *, name):
    n = len(bufs)
    per = 6

    def body(*refs):
        outs = refs[n:2 * n]
        send_sems, recv_sems = refs[2 * n:]
        x, y, c, chips = _place()
        sibling = (x, y, 1 - c)
        k_me = 2 * x + y
        sends = []
        for w, sp in enumerate(specs):
            mine = sp.region(outs[w], k_me, c)
            for j, chip in enumerate(chips):
                cp = _remote(mine, mine, send_sems.at[w * per + j], recv_sems.at[w * per + j], (*chip, c))
                cp.start()
                sends.append(cp)
        for w, sp in enumerate(specs):
            for j, chip in enumerate(chips):
                kj = 2 * chip[0] + chip[1]
                got = sp.region(outs[w], kj, c)
                _remote(got, got, send_sems.at[w * per + j], recv_sems.at[w * per + j], (*chip, c)).wait_recv()
                cp = _remote(got, got, send_sems.at[w * per + 3 + j], recv_sems.at[w * per + 3 + j], sibling)
                cp.start()
                sends.append(cp)
        for w, sp in enumerate(specs):
            for j, chip in enumerate(chips):
                kj = 2 * chip[0] + chip[1]
                got = sp.region(outs[w], kj, 1 - c)
                _remote(got, got, send_sems.at[w * per + 3 + j], recv_sems.at[w * per + 3 + j], sibling).wait_recv()
        for cp in sends:
            cp.wait_send()

    outs = [_sds(sp.full, BF16) for sp in specs]
    return _comm_call(name, body, n, outs, per * n)(*bufs)


def _pair_exchange(grads, specs, *, name):
    n = len(grads)

    def body(*refs):
        ins, outs = refs[:n], refs[n:2 * n]
        send_sems, recv_sems = refs[2 * n:]
        x, y, c, _ = _place()
        sibling = (x, y, 1 - c)
        cps = []
        for w, sp in enumerate(specs):
            for k in range(N_CHIPS):
                cp = _remote(sp.region(ins[w], k, 1 - c), outs[w].at[k], send_sems.at[w * N_CHIPS + k],
                             recv_sems.at[w * N_CHIPS + k], sibling)
                cp.start()
                cps.append(cp)
        for cp in cps:
            cp.wait()

    outs = [_sds((N_CHIPS, sp.hr, sp.C), F32) for sp in specs]
    return _comm_call(name, body, n, outs, N_CHIPS * n, in_place=False)(*grads)


def _chip_exchange(chipsums, specs, *, name):
    n = len(chipsums)
    per = N_CHIPS - 1

    def body(*refs):
        ins, outs = refs[:n], refs[n:2 * n]
        send_sems, recv_sems = refs[2 * n:]
        x, y, c, chips = _place()
        cps = []
        for w in range(n):
            for j, chip in enumerate(chips):
                kj = 2 * chip[0] + chip[1]
                cp = _remote(ins[w].at[kj], outs[w].at[j], send_sems.at[w * per + j], recv_sems.at[w * per + j], (*chip, c))
                cp.start()
                cps.append(cp)
        for cp in cps:
            cp.wait()

    outs = [_sds((per, sp.hr, sp.C), BF16) for sp in specs]
    return _comm_call(name, body, n, outs, per * n, in_place=False)(*chipsums)


def _pair_share(bufs, specs, *, name):
    n = len(bufs)

    def body(*refs):
        outs = refs[n:2 * n]
        send_sems, recv_sems = refs[2 * n:]
        x, y, c, _ = _place()
        sibling = (x, y, 1 - c)
        cps = []
        for w, sp in enumerate(specs):
            mine = outs[w].at[pl.ds(c * sp.hr, sp.hr), :]
            cp = _remote(mine, mine, send_sems.at[w], recv_sems.at[w], sibling)
            cp.start()
            cps.append(cp)
        for w, sp in enumerate(specs):
            theirs = outs[w].at[pl.ds((1 - c) * sp.hr, sp.hr), :]
            _remote(theirs, theirs, send_sems.at[w], recv_sems.at[w], sibling).wait_recv()
        for cp in cps:
            cp.wait_send()

    outs = [_sds((sp.R, sp.C), F32) for sp in specs]
    return _comm_call(name, body, n, outs, n)(*bufs)


PACK_ALIGN = SUBLANES * LANES


def _pack(arrs):
    flat = jnp.concatenate([a.reshape(-1) for a in arrs])
    pad = (-flat.shape[0]) % PACK_ALIGN
    return jnp.pad(flat, (0, pad)).reshape(-1, LANES)


def _unpack(packed, shapes):
    flat = packed.reshape(-1)
    out, pos = [], 0
    for s in shapes:
        n = int(np.prod(s))
        out.append(flat[pos:pos + n].reshape(s))
        pos += n
    return out


def kernel(x, c, w_ada, b_ada, norm1, w_in, rnn_conv_w, rnn_conv_b, w_rg_a, b_rg_a, w_rg_i, b_rg_i, rg_lambda, w_o_rnn, w_o_attn, attn_sinks, rel_bias, w_out, norm2, w_up, ffn_conv_w, ffn_conv_b, w_down, norm_f, loss_target, m_w_ada, m_b_ada, m_norm1, m_w_in, m_rnn_conv_w, m_rnn_conv_b, m_w_rg_a, m_b_rg_a, m_w_rg_i, m_b_rg_i, m_rg_lambda, m_w_o_rnn, m_w_o_attn, m_attn_sinks, m_rel_bias, m_w_out, m_norm2, m_w_up, m_ffn_conv_w, m_ffn_conv_b, m_w_down, m_norm_f, v_w_ada, v_b_ada, v_norm1, v_w_in, v_rnn_conv_w, v_rnn_conv_b, v_w_rg_a, v_b_rg_a, v_w_rg_i, v_b_rg_i, v_rg_lambda, v_w_o_rnn, v_w_o_attn, v_attn_sinks, v_rel_bias, v_w_out, v_norm2, v_w_up, v_ffn_conv_w, v_ffn_conv_b, v_w_down, v_norm_f):
    S, D = x.shape[1], x.shape[2]
    d_attn = N_CHIPS * w_o_attn.shape[1]
    d_rnn = N_CHIPS * w_o_rnn.shape[1]
    d_ff = N_CHIPS * w_down.shape[1]
    d_in = N_CHIPS * w_in.shape[2]
    n_heads = attn_sinks.shape[1]
    d_kv = (d_in - d_attn - 2 * d_rnn - 2 * D) // 2
    n_kv = d_kv // HEAD_DIM
    group = n_heads // n_kv
    nbk = rel_bias.shape[0]
    assert d_attn == n_heads * HEAD_DIM and group % 2 == 0 and S % BLOCK == 0

    mx, my, mc = lax.axis_index("x"), lax.axis_index("y"), lax.axis_index("c")
    k_me = 2 * mx + my
    dev = 2 * k_me + mc
    c_arr = jnp.reshape(mc, (1,)).astype(jnp.int32)
    k_arr = jnp.reshape(k_me, (1,)).astype(jnp.int32)
    kc_arr = jnp.stack([k_me, mc]).astype(jnp.int32)

    xs, tgt = x[0], loss_target[0]

    big_names = ["w_in", "w_o_rnn", "w_o_attn", "w_out", "w_up", "w_down"]
    big_w = dict(w_in=w_in[0], w_o_rnn=w_o_rnn[0], w_o_attn=w_o_attn[0], w_out=w_out[0], w_up=w_up[0], w_down=w_down[0])
    big_kind = dict(w_in="col", w_o_rnn="row", w_o_attn="row", w_out="row", w_up="col", w_down="row")
    specs = {k: _Big(big_kind[k], *big_w[k].shape) for k in big_names}
    placed = [_cast_into(specs[k], big_w[k], k_arr, name="cast_" + k) for k in big_names]
    gathered = _gather_weights(placed, [specs[k] for k in big_names], name="gather_weights")
    W = dict(zip(big_names, gathered))
    c_all, cw4, cw3 = _all_gather([_placed(jnp.broadcast_to(c, (SUBLANES, D)), dev),
                                   _placed(jnp.pad(rnn_conv_w[0], ((0, SUBLANES - rnn_conv_w.shape[1]), (0, 0))), dev),
                                   _placed(jnp.pad(ffn_conv_w[0], ((0, SUBLANES - ffn_conv_w.shape[1]), (0, 0))), dev)],
                                  name="gather_cond")
    c_all = c_all.reshape(N_DEV, SUBLANES, D)[:, 0]

    def from_chips(g, taps):
        cs = g.shape[1]
        g = g.reshape(N_CHIPS, 2, SUBLANES, cs)[:, 0, :taps]
        return jnp.transpose(g, (1, 0, 2)).reshape(taps, N_CHIPS * cs)

    conv4_w = from_chips(cw4, rnn_conv_w.shape[1])
    conv3_w = from_chips(cw3, ffn_conv_w.shape[1])

    (silu_c,) = _ew("silu_c", lambda v: (v * _sigmoid(v),), [(c_all, 0)], [F32], N_DEV, D)
    mod_sh = _mm(silu_c, w_ada[0], name="mod", exact=True, bias=lax.dynamic_slice_in_dim(b_ada, k_me * w_ada.shape[2], w_ada.shape[2], 1))
    (mod_g,) = _all_gather([_placed(mod_sh, dev)], name="gather_mod")
    mod_all = jnp.transpose(mod_g.reshape(N_CHIPS, 2, N_DEV, -1)[:, 0], (1, 0, 2)).reshape(N_DEV, 6 * D)
    mod = lax.dynamic_slice_in_dim(mod_all, dev, 1, 0)
    shift1, scale1, gate1, shift2, scale2, gate2 = [mod[:, i * D:(i + 1) * D] for i in range(6)]

    o_k, o_v, o_xr = d_attn, d_attn + d_kv, d_attn + 2 * d_kv
    wi = W["w_in"]

    spread = np.zeros((d_kv, n_kv * LANES), np.float32)
    for col in range(d_kv):
        spread[col, (col // HEAD_DIM) * LANES + col % HEAD_DIM] = 1.0
        spread[col, (col // HEAD_DIM) * LANES + HEAD_DIM + col % HEAD_DIM] = 1.0
    w_in_x = jnp.concatenate([wi[:, :o_k],
                              _mm(wi[:, o_k:o_v], jnp.asarray(spread, BF16), name="spread_k", out_dtype=BF16),
                              _mm(wi[:, o_v:o_xr], jnp.asarray(spread, BF16), name="spread_v", out_dtype=BF16),
                              wi[:, o_xr:]], axis=1)
    e_k = d_attn
    e_v = e_k + n_kv * LANES
    e_xr = e_v + n_kv * LANES
    e_gr = e_xr + d_rnn
    e_ga = e_gr + d_rnn
    e_gl = e_ga + D
    d_ext = e_gl + D

    u = _adaln_fwd(xs, norm1, scale1, shift1, name="adaln1")
    proj = _mm(u, w_in_x, name="proj", out_dtype=BF16)
    bias = _bias_table(rel_bias, n_heads, name="bias_table")
    sinks_b = jnp.broadcast_to(attn_sinks.reshape(n_heads, 1, 1), (n_heads, 1, LANES))
    o_attn = _attn_fwd(proj, bias, sinks_b, n_kv, group, 0, e_k, e_v, name="attn_fwd")
    y_attn = _mm(o_attn, W["w_o_attn"], name="y_attn")
    xc = _conv_fwd(proj, e_xr, d_rnn, conv4_w, rnn_conv_b, name="conv4")
    a_t, inp = _gates_fwd(xc, w_rg_a[0], b_rg_a, w_rg_i[0], b_rg_i, rg_lambda, name="gates")
    h_rnn, h_prev = _scan_fwd(a_t, inp, name="scan")
    (z,) = _ew("rnn_gate", lambda h, g: (h * _gelu(g),), [(h_rnn, 0), (proj, e_gr)], [BF16], S, d_rnn)
    y_rnn = _mm(z, W["w_o_rnn"], name="y_rnn")
    (merged,) = _ew("merge", lambda ya, yr, ga, gl: (_sigmoid(ga) * ya + _sigmoid(gl) * yr,),
                    [(y_attn, 0), (y_rnn, 0), (proj, e_ga), (proj, e_gl)], [BF16], S, D)
    t1 = _mm(merged, W["w_out"], name="t1")
    u2, h1 = _adaln_fwd(xs, norm2, scale2, shift2, name="adaln2", t=t1, gate=gate1)
    up = _mm(u2, W["w_up"], name="up", out_dtype=BF16)
    a2 = _geglu_fwd(up, conv3_w, ffn_conv_b, d_ff, name="geglu")
    t2 = _mm(a2, W["w_down"], name="t2")
    dh2, dt2, loss_tile, g_norm_f, d_gate2 = _final(h1, t2, gate2, norm_f.reshape(1, D), tgt, name="final")

    da2 = _mm(dt2, W["w_down"], name="da2", tb=True)
    g_w_down = _mm(a2, dt2, name="g_w_down", ta=True)
    dupg, dupv, g_c3g, g_c3v, g_b3g, g_b3v = _ffn_bwd(up, conv3_w, ffn_conv_b, da2, d_ff, name="ffn_bwd")
    dup = jnp.concatenate([dupg, dupv], axis=1)
    g_conv3_w = jnp.concatenate([g_c3g, g_c3v], axis=1)
    g_conv3_b = jnp.concatenate([g_b3g, g_b3v], axis=1)
    du2 = _mm(dup, W["w_up"], name="du2", tb=True)
    g_w_up = _mm(u2, dup, name="g_w_up", ta=True)
    dh1, d_shift2, d_scale2, g_norm2, dt1, d_gate1 = _adaln_bwd(h1, du2, dh2, norm2, scale2, name="adaln2_bwd", t=t1, gate=gate1)

    dmerged = _mm(dt1, W["w_out"], name="dmerged", tb=True)
    g_w_out = _mm(merged, dt1, name="g_w_out", ta=True)

    def merge_bwd(dm, ya, yr, ga, gl):
        sa, sl = _sigmoid(ga), _sigmoid(gl)
        return dm * sa, dm * sl, dm * ya * sa * (1.0 - sa), dm * yr * sl * (1.0 - sl)

    dy_attn, dy_rnn, d_ga, d_gl = _ew("merge_bwd", merge_bwd, [(dmerged, 0), (y_attn, 0), (y_rnn, 0), (proj, e_ga), (proj, e_gl)],
                                      [BF16, BF16, BF16, BF16], S, D)
    do = _mm(dy_attn, W["w_o_attn"], name="do", tb=True, out_dtype=BF16)
    g_w_o_attn = _mm(o_attn, dy_attn, name="g_w_o_attn", ta=True)
    dz = _mm(dy_rnn, W["w_o_rnn"], name="dz", tb=True)
    g_w_o_rnn = _mm(z, dy_rnn, name="g_w_o_rnn", ta=True)

    def rnn_gate_bwd(dzv, h, g):
        ge, dge = _gelu_and_grad(g)
        return dzv * ge, dzv * h * dge

    dh_rnn, d_gr = _ew("rnn_gate_bwd", rnn_gate_bwd, [(dz, 0), (h_rnn, 0), (proj, e_gr)], [F32, BF16], S, d_rnn)
    dacc = _scan_bwd(a_t, dh_rnn, name="scan_bwd")
    dxc, g_w_rg_a, g_w_rg_i, g_b_rg_a, g_b_rg_i, g_lam = _gates_bwd(xc, dacc, h_prev, w_rg_a[0], b_rg_a, w_rg_i[0], b_rg_i,
                                                                    rg_lambda, name="gates_bwd")
    d_xr, g_conv4_w, g_conv4_b = _conv_bwd(dxc, proj, e_xr, d_rnn, conv4_w, name="conv4_bwd")
    dq, dk, dv, dbias, dsink = _attn_bwd(proj, o_attn, do, bias, sinks_b, n_kv, group, 0, e_k, e_v, name="attn_bwd")
    g_rel = _bias_table_bwd(dbias, nbk, name="bias_table_bwd")[:, :n_heads]
    g_sinks = dsink[:, 0, :group].reshape(1, n_heads)
    dproj = jnp.concatenate([dq, dk, dv, d_xr, d_gr, d_ga, d_gl], axis=1)
    du = _mm(dproj, w_in_x, name="du", tb=True)
    g_w_in_x = _mm(u, dproj, name="g_w_in", ta=True)

    gather_m = jnp.asarray(spread.T, F32)
    g_w_in = jnp.concatenate([g_w_in_x[:, :e_k],
                              _mm(g_w_in_x[:, e_k:e_v], gather_m, name="gather_k", exact=True),
                              _mm(g_w_in_x[:, e_v:e_xr], gather_m, name="gather_v", exact=True),
                              g_w_in_x[:, e_xr:]], axis=1)
    grad_x, d_shift1, d_scale1, g_norm1 = _adaln_bwd(xs, du, dh1, norm1, scale1, name="adaln1_bwd")
    dmod = jnp.concatenate([d_shift1, d_scale1, d_gate1, d_shift2, d_scale2, d_gate2], axis=1)

    small = [loss_tile[0:1, 0:1], dmod, g_norm1, g_conv4_w, g_conv4_b, g_w_rg_a, g_b_rg_a, g_w_rg_i, g_b_rg_i, g_lam,
             g_sinks, g_rel, g_norm2, g_conv3_w, g_conv3_b, g_norm_f]
    small_shapes = [a.shape for a in small]
    pack = _pack(small)
    (packs,) = _all_gather([_placed(pack, dev)], name="gather_small")
    packs = packs.reshape(N_DEV, -1, LANES)
    summed = _unpack(_sum_devices(packs, name="sum_small"), small_shapes)
    (loss_s, g_b_ada, g_norm1, g_conv4_w, g_conv4_b, g_w_rg_a, g_b_rg_a, g_w_rg_i, g_b_rg_i, g_lam,
     g_sinks, g_rel, g_norm2, g_conv3_w, g_conv3_b, g_norm_f) = summed
    loss = loss_s.reshape(())
    dmod_all = packs.reshape(N_DEV, -1)[:, 1:1 + 6 * D]
    cs_ada = w_ada.shape[2]
    g_w_ada = _mm(silu_c, lax.dynamic_slice_in_dim(dmod_all, k_me * cs_ada, cs_ada, 1), name="g_w_ada", ta=True, exact=True)
    cs4, cs3 = rnn_conv_w.shape[2], ffn_conv_w.shape[2]
    g_conv4_sh = lax.dynamic_slice_in_dim(g_conv4_w, k_me * cs4, cs4, 1)
    g_conv3_sh = lax.dynamic_slice_in_dim(g_conv3_w, k_me * cs3, cs3, 1)

    big_g = dict(w_in=g_w_in, w_o_rnn=g_w_o_rnn, w_o_attn=g_w_o_attn, w_out=g_w_out, w_up=g_w_up, w_down=g_w_down)
    sp_list = [specs[k] for k in big_names]
    landed = _pair_exchange([big_g[k] for k in big_names], sp_list, name="pair_exchange")
    chipsums = [_pair_sum(specs[k], big_g[k], l, c_arr, name="pair_sum_" + k) for k, l in zip(big_names, landed)]
    landed2 = _chip_exchange(chipsums, sp_list, name="chip_exchange")
    halves = [_chip_sum(specs[k], s, l, kc_arr, name="chip_sum_" + k) for k, s, l in zip(big_names, chipsums, landed2)]
    shards = dict(zip(big_names, _pair_share(halves, sp_list, name="pair_share")))

    grads = dict(w_ada=g_w_ada[None], b_ada=g_b_ada, norm1=g_norm1, w_in=shards["w_in"][None], rnn_conv_w=g_conv4_sh[None],
                 rnn_conv_b=g_conv4_b, w_rg_a=g_w_rg_a[None], b_rg_a=g_b_rg_a, w_rg_i=g_w_rg_i[None], b_rg_i=g_b_rg_i,
                 rg_lambda=g_lam, w_o_rnn=shards["w_o_rnn"][None], w_o_attn=shards["w_o_attn"][None], attn_sinks=g_sinks,
                 rel_bias=g_rel, w_out=shards["w_out"][None], norm2=g_norm2, w_up=shards["w_up"][None],
                 ffn_conv_w=g_conv3_sh[None], ffn_conv_b=g_conv3_b, w_down=shards["w_down"][None], norm_f=g_norm_f.reshape(D))
    weights = dict(w_ada=w_ada, b_ada=b_ada, norm1=norm1, w_in=w_in, rnn_conv_w=rnn_conv_w, rnn_conv_b=rnn_conv_b, w_rg_a=w_rg_a,
                   b_rg_a=b_rg_a, w_rg_i=w_rg_i, b_rg_i=b_rg_i, rg_lambda=rg_lambda, w_o_rnn=w_o_rnn, w_o_attn=w_o_attn,
                   attn_sinks=attn_sinks, rel_bias=rel_bias, w_out=w_out, norm2=norm2, w_up=w_up, ffn_conv_w=ffn_conv_w,
                   ffn_conv_b=ffn_conv_b, w_down=w_down, norm_f=norm_f)
    moms = dict(w_ada=(m_w_ada, v_w_ada), b_ada=(m_b_ada, v_b_ada), norm1=(m_norm1, v_norm1), w_in=(m_w_in, v_w_in),
                rnn_conv_w=(m_rnn_conv_w, v_rnn_conv_w), rnn_conv_b=(m_rnn_conv_b, v_rnn_conv_b), w_rg_a=(m_w_rg_a, v_w_rg_a),
                b_rg_a=(m_b_rg_a, v_b_rg_a), w_rg_i=(m_w_rg_i, v_w_rg_i), b_rg_i=(m_b_rg_i, v_b_rg_i),
                rg_lambda=(m_rg_lambda, v_rg_lambda), w_o_rnn=(m_w_o_rnn, v_w_o_rnn), w_o_attn=(m_w_o_attn, v_w_o_attn),
                attn_sinks=(m_attn_sinks, v_attn_sinks), rel_bias=(m_rel_bias, v_rel_bias), w_out=(m_w_out, v_w_out),
                norm2=(m_norm2, v_norm2), w_up=(m_w_up, v_w_up), ffn_conv_w=(m_ffn_conv_w, v_ffn_conv_w),
                ffn_conv_b=(m_ffn_conv_b, v_ffn_conv_b), w_down=(m_w_down, v_w_down), norm_f=(m_norm_f, v_norm_f))
    names = list(weights)
    grads = {k: grads[k].reshape(weights[k].shape) for k in names}
    large = ["w_ada"] + big_names
    delta, new_m, new_v = {}, {}, {}
    for k in large:
        shp = weights[k].shape
        two = lambda a: a.reshape(shp[-2], shp[-1])
        d_, m_, v_ = _adamw(two(weights[k]), two(grads[k]), two(moms[k][0]), two(moms[k][1]), name="adamw_" + k)
        delta[k], new_m[k], new_v[k] = d_.reshape(shp), m_.reshape(shp), v_.reshape(shp)
    rest = [k for k in names if k not in large]
    rest_shapes = [weights[k].shape for k in rest]
    d_, m_, v_ = _adamw(_pack([weights[k] for k in rest]), _pack([grads[k] for k in rest]),
                        _pack([moms[k][0] for k in rest]), _pack([moms[k][1] for k in rest]), name="adamw_small")
    for k, dd, mm_, vv in zip(rest, _unpack(d_, rest_shapes), _unpack(m_, rest_shapes), _unpack(v_, rest_shapes)):
        delta[k], new_m[k], new_v[k] = dd, mm_, vv

    return (loss, grad_x[None], *[grads[k] for k in names], *[delta[k] for k in names],
            *[new_m[k] for k in names], *[new_v[k] for k in names])
```

```python
import math

import numpy as np
import jax
import jax.numpy as jnp
from jax import lax
from jax.experimental import pallas as pl
from jax.experimental.pallas import tpu as pltpu

F32 = jnp.float32
BF16 = jnp.bfloat16
MESH = pl.DeviceIdType.MESH
ANY = pl.BlockSpec(memory_space=pl.ANY)

EPS = 1e-6
NEG_INF = -1e30
HEAD_DIM = 64
BLOCK = 128
NUM_EXACT = 16
MAX_DISTANCE = 128
RG_C = 8.0
ADAM_LR, ADAM_B1, ADAM_B2, ADAM_EPS, ADAM_WD, ADAM_STEP = 0.001, 0.9, 0.999, 1e-08, 0.01, 10
N_CHIPS = 4
N_DEV = 8
SUBLANES = 8
LANES = 128
VMEM_LIMIT_BYTES = 48 * 1024 * 1024
GELU_C0 = math.sqrt(2.0 / math.pi)
GELU_C1 = 0.044715


def _pick(dim, pref, align):
    if dim <= pref:
        return dim
    t = (pref // align) * align
    while t >= align:
        if dim % t == 0:
            return t
        t -= align
    return dim


def _params(sem):
    return pltpu.CompilerParams(dimension_semantics=sem, vmem_limit_bytes=VMEM_LIMIT_BYTES)


def _call(name, body, grid, in_specs, out_specs, out_shape, scratch=(), nsp=0, sem=None, ride=None):
    if ride is not None:
        return _call_with_ride(name, body, grid, in_specs, out_specs, out_shape, scratch, ride)
    sem = sem or ("parallel",) * (len(grid) - 1) + ("arbitrary",)
    if nsp:
        gs = pltpu.PrefetchScalarGridSpec(num_scalar_prefetch=nsp, grid=grid, in_specs=in_specs,
                                          out_specs=out_specs, scratch_shapes=list(scratch))
        return pl.pallas_call(body, name=name, grid_spec=gs, out_shape=out_shape, compiler_params=_params(sem))
    return pl.pallas_call(body, name=name, grid=grid, in_specs=in_specs, out_specs=out_specs,
                          out_shape=out_shape, scratch_shapes=list(scratch), compiler_params=_params(sem))


def _sds(shape, dtype):
    return jax.ShapeDtypeStruct(shape, dtype)


class _Exchange:
    def __init__(self, ins, outs, n_sems, start, finish, in_place):
        self.ins, self.outs, self.n_sems = list(ins), list(outs), n_sems
        self.start, self.finish, self.in_place = start, finish, in_place

    def alone(self, name):
        n_in, n_out = len(self.ins), len(self.outs)

        def body(*refs):
            ins, outs = refs[:n_in], refs[n_in:n_in + n_out]
            send_sems, recv_sems = refs[n_in + n_out:]
            self.start(ins, outs, send_sems, recv_sems)
            self.finish(ins, outs, send_sems, recv_sems)

        sems = [pltpu.SemaphoreType.DMA((self.n_sems,)), pltpu.SemaphoreType.DMA((self.n_sems,))]
        aliases = {i: i for i in range(n_in)} if self.in_place else {}
        return pl.pallas_call(body, name=name, out_shape=self.outs, in_specs=[ANY] * n_in, out_specs=[ANY] * n_out,
                              scratch_shapes=sems, input_output_aliases=aliases)(*self.ins)


def _call_with_ride(name, body, grid, in_specs, out_specs, out_shape, scratch, ride):
    single = not isinstance(out_specs, (list, tuple))
    out_specs = [out_specs] if single else list(out_specs)
    out_shape = [out_shape] if single else list(out_shape)
    n_in, n_out, n_sc = len(in_specs), len(out_specs), len(scratch)
    r_in, r_out = len(ride.ins), len(ride.outs)

    def wrapped(*refs):
        pos = 0
        parts = []
        for n in (n_in, r_in, n_out, r_out, n_sc, 2):
            parts.append(refs[pos:pos + n])
            pos += n
        core_in, ride_in, core_out, ride_out, core_scratch, (send_sems, recv_sems) = parts
        first = pl.program_id(0) == 0
        last = pl.program_id(0) == grid[0] - 1
        for axis in range(1, len(grid)):
            first = jnp.logical_and(first, pl.program_id(axis) == 0)
            last = jnp.logical_and(last, pl.program_id(axis) == grid[axis] - 1)

        @pl.when(first)
        def _():
            ride.start(ride_in, ride_out, send_sems, recv_sems)

        body(*core_in, *core_out, *core_scratch)

        @pl.when(last)
        def _():
            ride.finish(ride_in, ride_out, send_sems, recv_sems)

    sems = [pltpu.SemaphoreType.DMA((ride.n_sems,)), pltpu.SemaphoreType.DMA((ride.n_sems,))]
    aliases = {n_in + k: n_out + k for k in range(r_in)} if ride.in_place else {}
    call = pl.pallas_call(wrapped, name=name, grid=grid, in_specs=list(in_specs) + [ANY] * r_in,
                          out_specs=out_specs + [ANY] * r_out, out_shape=out_shape + ride.outs,
                          scratch_shapes=list(scratch) + sems, input_output_aliases=aliases,
                          compiler_params=_params(("arbitrary",) * len(grid)))

    def run(*args):
        res = call(*args, *ride.ins)
        core = res[0] if single else list(res[:n_out])
        return core, list(res[n_out:])

    return run


def _T(tr, tc, off=0):
    return pl.BlockSpec((tr, tc), lambda j, i: (i, j + off))


def _P(rows, tc, off=0):
    return pl.BlockSpec((rows, tc), lambda j, i: (0, j + off))


def _gelu(x):
    t = jnp.tanh(GELU_C0 * (x + GELU_C1 * x * x * x))
    return 0.5 * x * (1.0 + t)


def _gelu_and_grad(x):
    t = jnp.tanh(GELU_C0 * (x + GELU_C1 * x * x * x))
    g = 0.5 * x * (1.0 + t)
    dg = 0.5 * (1.0 + t) + 0.5 * x * (1.0 - t * t) * GELU_C0 * (1.0 + 3.0 * GELU_C1 * x * x)
    return g, dg


def _sigmoid(x):
    return 1.0 / (1.0 + jnp.exp(-x))


def _accum(ref, val, first):
    @pl.when(first)
    def _():
        ref[...] = val

    @pl.when(jnp.logical_not(first))
    def _():
        ref[...] += val


def _colsum(v):
    return jnp.sum(v, axis=0, keepdims=True)


CHUNK = 16


def _chunks(n_rows, ch, step):
    def it(i, carry):
        step(pl.multiple_of(i * ch, ch))
        return carry

    lax.fori_loop(0, n_rows // ch, it, 0)


def _fold(v):
    return v[0:SUBLANES, :] + v[SUBLANES:2 * SUBLANES, :]


def _shift_down(xe, d, ch):
    return xe[SUBLANES:SUBLANES + ch, :] if d == 0 else pltpu.roll(xe, d, 0)[SUBLANES:SUBLANES + ch, :]


def _shift_up(xe, d, ch):
    return xe[0:ch, :] if d == 0 else pltpu.roll(xe, ch + SUBLANES - d, 0)[0:ch, :]


def _mm(a, b, *, name, ta=False, tb=False, out_dtype=F32, tm=1024, tn=1024, tk=2048, exact=False, bias=None, ride=None):
    if ta:
        K, M = a.shape
    else:
        M, K = a.shape
    if tb:
        N, K2 = b.shape
    else:
        K2, N = b.shape
    assert K == K2, (a.shape, b.shape, ta, tb)
    tm, tn, tk = _pick(M, tm, LANES), _pick(N, tn, LANES), _pick(K, tk, LANES)
    nk = K // tk
    cdt = F32 if exact else BF16
    prec = lax.Precision.HIGHEST if exact else None
    dims = (((0 if ta else 1,), (1 if tb else 0,)), ((), ()))

    def body(*refs):
        a_ref, b_ref = refs[0], refs[1]
        bias_ref = refs[2] if bias is not None else None
        o_ref = refs[3] if bias is not None else refs[2]
        part = lax.dot_general(a_ref[...].astype(cdt), b_ref[...].astype(cdt), dims,
                               preferred_element_type=F32, precision=prec)

        def finish(r):
            if bias is not None:
                r = r + bias_ref[...]
            o_ref[...] = r.astype(out_dtype)

        if nk == 1:
            finish(part)
            return
        acc_ref = refs[-1]
        k = pl.program_id(2)

        @pl.when(k == 0)
        def _():
            acc_ref[...] = part

        @pl.when(jnp.logical_and(k > 0, k < nk - 1))
        def _():
            acc_ref[...] += part

        @pl.when(k == nk - 1)
        def _():
            finish(acc_ref[...] + part)

    a_spec = pl.BlockSpec((tk, tm), lambda i, j, k: (k, i)) if ta else pl.BlockSpec((tm, tk), lambda i, j, k: (i, k))
    b_spec = pl.BlockSpec((tn, tk), lambda i, j, k: (j, k)) if tb else pl.BlockSpec((tk, tn), lambda i, j, k: (k, j))
    in_specs, args = [a_spec, b_spec], [a, b]
    if bias is not None:
        in_specs.append(pl.BlockSpec((1, tn), lambda i, j, k: (0, j)))
        args.append(bias)
    scratch = [pltpu.VMEM((tm, tn), F32)] if nk > 1 else []
    return _call(name, body, (M // tm, N // tn, nk), in_specs, pl.BlockSpec((tm, tn), lambda i, j, k: (i, j)),
                 _sds((M, N), out_dtype), scratch=scratch, sem=("parallel", "parallel", "arbitrary"), ride=ride)(*args)


def _ew(name, fn, ins, out_dtypes, S, C, rows=(), tr=256, tc=512):
    g = C
    for _, off in ins:
        g = math.gcd(g, off) if off else g
    tc = _pick(g, tc, LANES)
    tr = _pick(S, tr, 16)
    n_in, n_row, n_out = len(ins), len(rows), len(out_dtypes)

    def body(*refs):
        vals = [r[...].astype(F32) for r in refs[:n_in + n_row]]
        for o_ref, o in zip(refs[n_in + n_row:], fn(*vals)):
            o_ref[...] = o.astype(o_ref.dtype)

    in_specs = [_T(tr, tc, off // tc) for _, off in ins] + [_P(1, tc) for _ in rows]
    res = _call(name, body, (C // tc, S // tr), in_specs, [_T(tr, tc) for _ in out_dtypes],
                [_sds((S, C), d) for d in out_dtypes], sem=("parallel", "parallel"))(*[a for a, _ in ins], *rows)
    return res


def _adaln_fwd(x, norm, scale, shift, *, name, t=None, gate=None):
    S, D = x.shape
    tr = _pick(S, 256, 16)
    resid = t is not None

    def body(*refs):
        if resid:
            x_ref, t_ref, g_ref, n_ref, sc_ref, sh_ref, u_ref, h_ref = refs
            h = x_ref[...] + g_ref[...] * t_ref[...]
            h_ref[...] = h
        else:
            x_ref, n_ref, sc_ref, sh_ref, u_ref = refs
            h = x_ref[...]
        r = lax.rsqrt(jnp.mean(h * h, axis=-1, keepdims=True) + EPS)
        u_ref[...] = (h * r * (n_ref[...] * (1.0 + sc_ref[...])) + sh_ref[...]).astype(BF16)

    full, row = _T(tr, D), _P(1, D)
    if resid:
        return _call(name, body, (1, S // tr), [full, full, row, row, row, row], [full, full],
                     [_sds((S, D), BF16), _sds((S, D), F32)])(x, t, gate, norm, scale, shift)
    return _call(name, body, (1, S // tr), [full, row, row, row], full, _sds((S, D), BF16))(x, norm, scale, shift)


def _adaln_bwd(h, du, dres, norm, scale, *, name, t=None, gate=None):
    S, D = h.shape
    tr = _pick(S, 256, 16)
    gated = t is not None

    def body(*refs):
        if gated:
            h_ref, du_ref, dr_ref, n_ref, sc_ref, t_ref, g_ref, dh_ref, dsh_ref, dsc_ref, dn_ref, dt_ref, dg_ref = refs
        else:
            h_ref, du_ref, dr_ref, n_ref, sc_ref, dh_ref, dsh_ref, dsc_ref, dn_ref = refs
        first = pl.program_id(1) == 0
        hv, duv = h_ref[...], du_ref[...]
        r = lax.rsqrt(jnp.mean(hv * hv, axis=-1, keepdims=True) + EPS)
        xn = hv * r
        one_sc = 1.0 + sc_ref[...]
        dxn = duv * (n_ref[...] * one_sc)
        dh = dr_ref[...] + r * (dxn - xn * jnp.mean(dxn * xn, axis=-1, keepdims=True))
        dh_ref[...] = dh
        dux = duv * xn
        _accum(dsh_ref, _colsum(duv), first)
        _accum(dsc_ref, _colsum(dux * n_ref[...]), first)
        _accum(dn_ref, _colsum(dux * one_sc), first)
        if gated:
            dt_ref[...] = (dh * g_ref[...]).astype(BF16)
            _accum(dg_ref, _colsum(dh * t_ref[...]), first)

    full, row = _T(tr, D), _P(1, D)
    rowo = _sds((1, D), F32)
    if gated:
        return _call(name, body, (1, S // tr), [full, full, full, row, row, full, row],
                     [full, row, row, row, full, row],
                     [_sds((S, D), F32), rowo, rowo, rowo, _sds((S, D), BF16), rowo])(h, du, dres, norm, scale, t, gate)
    return _call(name, body, (1, S // tr), [full, full, full, row, row], [full, row, row, row],
                 [_sds((S, D), F32), rowo, rowo, rowo])(h, du, dres, norm, scale)


def _final(h1, t2, gate2, norm_f, tgt, *, name):
    S, D = h1.shape
    tr = _pick(S, 256, 16)

    def body(h_ref, t_ref, g_ref, n_ref, y_ref, dh_ref, dt_ref, loss_ref, dn_ref, dg_ref):
        first = pl.program_id(1) == 0
        tv = t_ref[...]
        h2 = h_ref[...] + g_ref[...] * tv
        r = lax.rsqrt(jnp.mean(h2 * h2, axis=-1, keepdims=True) + EPS)
        xn = h2 * r
        e = xn * n_ref[...] - y_ref[...]
        part = 0.5 * jnp.sum(jnp.mean(e * e, axis=-1, keepdims=True), axis=0, keepdims=True)
        _accum(loss_ref, jnp.broadcast_to(part, (SUBLANES, LANES)), first)
        dy = e * (1.0 / D)
        _accum(dn_ref, _colsum(dy * xn), first)
        dxn = dy * n_ref[...]
        dh2 = r * (dxn - xn * jnp.mean(dxn * xn, axis=-1, keepdims=True))
        dh_ref[...] = dh2
        dt_ref[...] = (dh2 * g_ref[...]).astype(BF16)
        _accum(dg_ref, _colsum(dh2 * tv), first)

    full, row = _T(tr, D), _P(1, D)
    rowo = _sds((1, D), F32)
    return _call(name, body, (1, S // tr), [full, full, row, row, full],
                 [full, full, _P(SUBLANES, LANES), row, row],
                 [_sds((S, D), F32), _sds((S, D), BF16), _sds((SUBLANES, LANES), F32), rowo, rowo])(h1, t2, gate2, norm_f, tgt)


HALO = 16


def _halo_prev(tr, tc, off=0):
    return pl.BlockSpec((HALO, tc), lambda j, i: (jnp.maximum(i * (tr // HALO) - 1, 0), j + off))


def _halo_next(tr, tc, n_slabs, off=0):
    return pl.BlockSpec((HALO, tc), lambda j, i: (jnp.minimum((i + 1) * (tr // HALO), n_slabs - 1), j + off))


def _fill_past(buf, prev_ref, cur_ref, first, tr):
    buf[0:HALO, :] = jnp.where(first, 0.0, prev_ref[...].astype(F32))
    buf[HALO:HALO + tr, :] = cur_ref[...].astype(F32)


def _fill_future(buf, cur_ref, next_ref, last, tr):
    buf[0:tr, :] = cur_ref[...].astype(F32)
    buf[tr:tr + HALO, :] = jnp.where(last, 0.0, next_ref[...].astype(F32))


def _al8(r):
    return r if isinstance(r, int) else pl.multiple_of(r, SUBLANES)


def _past_rows(buf, r0, taps):
    xe = buf[pl.ds(_al8(r0 + HALO - SUBLANES), CHUNK + SUBLANES), :]
    return [_shift_down(xe, d, CHUNK) for d in range(taps)]


def _future_rows(buf, r0, taps):
    xe = buf[pl.ds(_al8(r0), CHUNK + SUBLANES), :]
    return [_shift_up(xe, d, CHUNK) for d in range(taps)]


def _taps(w_ref):
    return [w_ref[k:k + 1, :] for k in range(w_ref.shape[0])]


def _conv(xs, w, b):
    taps = len(xs)
    y = w[taps - 1] * xs[0] if b is None else b + w[taps - 1] * xs[0]
    for d in range(1, taps):
        y = y + w[taps - 1 - d] * xs[d]
    return y


def _conv_fwd(x, off, C, w, b, *, name):
    S = x.shape[0]
    taps = w.shape[0]
    tc = _pick(math.gcd(C, off) if off else C, 512, LANES)
    tr = _pick(S, 256, 16)

    def body(x_ref, p_ref, w_ref, b_ref, y_ref, buf):
        _fill_past(buf, p_ref, x_ref, pl.program_id(1) == 0, tr)
        wt, bv = _taps(w_ref), b_ref[...]

        def step(r0):
            y_ref[pl.ds(r0, CHUNK), :] = _conv(_past_rows(buf, r0, taps), wt, bv)

        _chunks(tr, CHUNK, step)

    return _call(name, body, (C // tc, S // tr),
                 [_T(tr, tc, off // tc), _halo_prev(tr, tc, off // tc), _P(taps, tc), _P(1, tc)],
                 _T(tr, tc), _sds((S, C), F32), scratch=[pltpu.VMEM((tr + HALO, tc), F32)],
                 sem=("parallel", "parallel"))(x, x, w, b)


def _finish_conv_grads(acc, base, taps, dw_ref, db_ref, first):
    @pl.when(first)
    def _():
        dw_ref[...] = jnp.zeros_like(dw_ref)

    for d in range(taps):
        dw_ref[taps - 1 - d:taps - d, :] += _colsum(acc[base + SUBLANES * d:base + SUBLANES * (d + 1), :])
    _accum(db_ref, _colsum(acc[base + SUBLANES * taps:base + SUBLANES * (taps + 1), :]), first)


def _add_conv_grads(acc, base, dy, xs):
    taps = len(xs)
    for d in range(taps):
        acc[base + SUBLANES * d:base + SUBLANES * (d + 1), :] += _fold(dy * xs[d])
    acc[base + SUBLANES * taps:base + SUBLANES * (taps + 1), :] += _fold(dy)


def _conv_bwd(dy, x, off, C, w, *, name):
    S = dy.shape[0]
    taps = w.shape[0]
    tc = _pick(math.gcd(C, off) if off else C, 512, LANES)
    tr = _pick(S, 256, 16)
    n_slabs = S // HALO

    def body(dy_ref, nx_ref, x_ref, p_ref, w_ref, dx_ref, dw_ref, db_ref, fbuf, pbuf, acc):
        i = pl.program_id(1)
        first = i == 0
        _fill_future(fbuf, dy_ref, nx_ref, i == S // tr - 1, tr)
        _fill_past(pbuf, p_ref, x_ref, first, tr)
        acc[...] = jnp.zeros_like(acc)
        wt = _taps(w_ref)

        def step(r0):
            dys = _future_rows(fbuf, r0, taps)
            dx_ref[pl.ds(r0, CHUNK), :] = _conv(dys, wt, None).astype(BF16)
            _add_conv_grads(acc, 0, dys[0], _past_rows(pbuf, r0, taps))

        _chunks(tr, CHUNK, step)
        _finish_conv_grads(acc, 0, taps, dw_ref, db_ref, first)

    return _call(name, body, (C // tc, S // tr),
                 [_T(tr, tc), _halo_next(tr, tc, n_slabs), _T(tr, tc, off // tc), _halo_prev(tr, tc, off // tc),
                  _P(taps, tc)],
                 [_T(tr, tc), _P(taps, tc), _P(1, tc)],
                 [_sds((S, C), BF16), _sds((taps, C), F32), _sds((1, C), F32)],
                 scratch=[pltpu.VMEM((tr + HALO, tc), F32), pltpu.VMEM((tr + HALO, tc), F32),
                          pltpu.VMEM((SUBLANES * (taps + 1), tc), F32)])(dy, dy, x, x, w)


def _geglu_fwd(up, w, b, F, *, name):
    S = up.shape[0]
    taps = w.shape[0]
    tc = _pick(F, 512, LANES)
    tr = _pick(S, 256, 16)
    nf = F // tc

    def body(g_ref, gp_ref, v_ref, vp_ref, wg_ref, wv_ref, bg_ref, bv_ref, a_ref, gbuf, vbuf):
        first = pl.program_id(1) == 0
        _fill_past(gbuf, gp_ref, g_ref, first, tr)
        _fill_past(vbuf, vp_ref, v_ref, first, tr)
        wg, wv, bg, bv = _taps(wg_ref), _taps(wv_ref), bg_ref[...], bv_ref[...]

        def step(r0):
            g = _conv(_past_rows(gbuf, r0, taps), wg, bg)
            v = _conv(_past_rows(vbuf, r0, taps), wv, bv)
            a_ref[pl.ds(r0, CHUNK), :] = (_gelu(g) * v).astype(BF16)

        _chunks(tr, CHUNK, step)

    buf = pltpu.VMEM((tr + HALO, tc), F32)
    return _call(name, body, (nf, S // tr),
                 [_T(tr, tc), _halo_prev(tr, tc), _T(tr, tc, nf), _halo_prev(tr, tc, nf),
                  _P(taps, tc), _P(taps, tc, nf), _P(1, tc), _P(1, tc, nf)],
                 _T(tr, tc), _sds((S, F), BF16), scratch=[buf, buf], sem=("parallel", "parallel"))(up, up, up, up, w, w, b, b)


def _ffn_bwd(up, w, b, da, F, *, name):
    S = up.shape[0]
    taps = w.shape[0]
    tc = _pick(F, 512, LANES)
    tr = _pick(S, 256, 16)
    nf, nr = F // tc, S // tr
    ext = tr + HALO
    n_slabs = S // HALO

    assert HALO == CHUNK
    v_base = SUBLANES * (taps + 1)

    def body(g_ref, gp_ref, gn_ref, v_ref, vp_ref, vn_ref, wg_ref, wv_ref, bg_ref, bv_ref, da_ref, dan_ref,
             dxg_ref, dxv_ref, dwg_ref, dwv_ref, dbg_ref, dbv_ref, gbuf, vbuf, dgbuf, dvbuf, acc):
        i = pl.program_id(1)
        first, last = i == 0, i == nr - 1
        _fill_past(gbuf, gp_ref, g_ref, first, tr)
        _fill_past(vbuf, vp_ref, v_ref, first, tr)
        gbuf[HALO + tr:HALO + ext, :] = gn_ref[...].astype(F32)
        vbuf[HALO + tr:HALO + ext, :] = vn_ref[...].astype(F32)
        acc[...] = jnp.zeros_like(acc)
        wg, wv, bg, bv = _taps(wg_ref), _taps(wv_ref), bg_ref[...], bv_ref[...]

        def grads(r0, da, in_tile):
            gs, vs = _past_rows(gbuf, r0, taps), _past_rows(vbuf, r0, taps)
            ge, dge = _gelu_and_grad(_conv(gs, wg, bg))
            dg = da * _conv(vs, wv, bv) * dge
            dv = da * ge
            dgbuf[pl.ds(_al8(r0), CHUNK), :] = dg
            dvbuf[pl.ds(_al8(r0), CHUNK), :] = dv
            if in_tile:
                _add_conv_grads(acc, 0, dg, gs)
                _add_conv_grads(acc, v_base, dv, vs)

        _chunks(tr, CHUNK, lambda r0: grads(r0, da_ref[pl.ds(r0, CHUNK), :], True))
        grads(tr, jnp.where(last, 0.0, dan_ref[...]), False)

        def back(r0):
            dxg_ref[pl.ds(r0, CHUNK), :] = _conv(_future_rows(dgbuf, r0, taps), wg, None).astype(BF16)
            dxv_ref[pl.ds(r0, CHUNK), :] = _conv(_future_rows(dvbuf, r0, taps), wv, None).astype(BF16)

        _chunks(tr, CHUNK, back)
        _finish_conv_grads(acc, 0, taps, dwg_ref, dbg_ref, first)
        _finish_conv_grads(acc, v_base, taps, dwv_ref, dbv_ref, first)

    xbuf = pltpu.VMEM((HALO + ext, tc), F32)
    dbuf = pltpu.VMEM((ext, tc), F32)
    t, tw, tb = _T(tr, tc), _P(taps, tc), _P(1, tc)
    return _call(name, body, (nf, nr),
                 [t, _halo_prev(tr, tc), _halo_next(tr, tc, n_slabs), _T(tr, tc, nf), _halo_prev(tr, tc, nf),
                  _halo_next(tr, tc, n_slabs, nf), tw, _P(taps, tc, nf), tb, _P(1, tc, nf), t, _halo_next(tr, tc, n_slabs)],
                 [t, t, tw, tw, tb, tb],
                 [_sds((S, F), BF16), _sds((S, F), BF16), _sds((taps, F), F32), _sds((taps, F), F32),
                  _sds((1, F), F32), _sds((1, F), F32)],
                 scratch=[xbuf, xbuf, dbuf, dbuf, pltpu.VMEM((2 * v_base, tc), F32)])(up, up, up, up, up, up, w, w, b, b, da, da)


def _softplus_neg(lam):
    z = -lam
    e = jnp.exp(-jnp.abs(z))
    u = 1.0 + e
    log1p_e = jnp.where(u == 1.0, e, jnp.log(u) * e / jnp.where(u == 1.0, 1.0, u - 1.0))
    sp = jnp.maximum(z, 0.0) + log1p_e
    sg = jnp.where(z >= 0, 1.0 / u, e / u)
    return sp, sg


def _one_minus_exp(x):
    series = -x * (1.0 + x * (0.5 + x * (1.0 / 6.0 + x * (1.0 / 24.0))))
    return jnp.where(x > -0.01, series, 1.0 - jnp.exp(x))


def _gate_values(xc, wa_ref, wi_ref, ba_ref, bi_ref, lam_ref, is_t0):
    xb = xc.astype(BF16)
    ra = _sigmoid(jnp.dot(xb, wa_ref[0].astype(BF16), preferred_element_type=F32) + ba_ref[...])
    ia = _sigmoid(jnp.dot(xb, wi_ref[0].astype(BF16), preferred_element_type=F32) + bi_ref[...])
    sp, sg = _softplus_neg(lam_ref[...])
    log_a = -RG_C * ra * sp
    a = jnp.exp(log_a)
    mult = jnp.where(is_t0, 1.0, jnp.sqrt(_one_minus_exp(2.0 * log_a)))
    return ra, ia, sp, sg, a, mult


def _rnn_blockspecs(tr):
    x = pl.BlockSpec((tr, LANES), lambda n, i: (i, n))
    w = pl.BlockSpec((1, LANES, LANES), lambda n, i: (n, 0, 0))
    p = pl.BlockSpec((1, LANES), lambda n, i: (0, n))
    return x, w, p


def _is_t0(tr):
    rows = lax.broadcasted_iota(jnp.int32, (tr, LANES), 0)
    return jnp.logical_and(pl.program_id(1) == 0, rows == 0)


def _gates_fwd(xc, w_a, b_a, w_i, b_i, lam, *, name):
    S, C = xc.shape
    tr = _pick(S, 1024, 16)

    def body(x_ref, wa_ref, wi_ref, ba_ref, bi_ref, lam_ref, a_ref, inp_ref):
        xv = x_ref[...]
        _, ia, _, _, a, mult = _gate_values(xv, wa_ref, wi_ref, ba_ref, bi_ref, lam_ref, _is_t0(tr))
        a_ref[...] = a
        inp_ref[...] = mult * ia * xv

    x, w, p = _rnn_blockspecs(tr)
    return _call(name, body, (C // LANES, S // tr), [x, w, w, p, p, p], [x, x],
                 [_sds((S, C), F32), _sds((S, C), F32)], sem=("parallel", "parallel"))(xc, w_a, w_i, b_a, b_i, lam)


def _gates_bwd(xc, dacc, hprev, w_a, b_a, w_i, b_i, lam, *, name):
    S, C = xc.shape
    nb = C // LANES
    tr = _pick(S, 1024, 16)

    def body(x_ref, d_ref, hp_ref, wa_ref, wi_ref, ba_ref, bi_ref, lam_ref,
             dx_ref, dwa_ref, dwi_ref, dba_ref, dbi_ref, dlam_ref):
        first = pl.program_id(1) == 0
        t0 = _is_t0(tr)
        xv, dv = x_ref[...], d_ref[...]
        ra, ia, sp, sg, a, mult = _gate_values(xv, wa_ref, wi_ref, ba_ref, bi_ref, lam_ref, t0)
        d_a = dv * hp_ref[...]
        d_mult = dv * ia * xv
        d_ia = dv * mult * xv
        d_log = d_a * a + jnp.where(t0, 0.0, -d_mult * (a * a) / mult)
        d_pa = d_log * (-RG_C * sp) * ra * (1.0 - ra)
        d_pi = d_ia * ia * (1.0 - ia)
        xb, dab, dib = xv.astype(BF16), d_pa.astype(BF16), d_pi.astype(BF16)
        nt = (((1,), (1,)), ((), ()))
        tn = (((0,), (0,)), ((), ()))
        dx_ref[...] = (dv * mult * ia
                       + lax.dot_general(dab, wa_ref[0].astype(BF16), nt, preferred_element_type=F32)
                       + lax.dot_general(dib, wi_ref[0].astype(BF16), nt, preferred_element_type=F32))
        _accum(dwa_ref, lax.dot_general(xb, dab, tn, preferred_element_type=F32)[None], first)
        _accum(dwi_ref, lax.dot_general(xb, dib, tn, preferred_element_type=F32)[None], first)
        _accum(dba_ref, _colsum(d_pa), first)
        _accum(dbi_ref, _colsum(d_pi), first)
        _accum(dlam_ref, _colsum(d_log * (-RG_C) * ra) * (-sg), first)

    x, w, p = _rnn_blockspecs(tr)
    row = _sds((1, C), F32)
    return _call(name, body, (nb, S // tr), [x, x, x, w, w, p, p, p], [x, w, w, p, p, p],
                 [_sds((S, C), F32), _sds((nb, LANES, LANES), F32), _sds((nb, LANES, LANES), F32), row, row, row])(
                     xc, dacc, hprev, w_a, w_i, b_a, b_i, lam)


def _scan_fwd(a, inp, *, name):
    S, C = a.shape
    tc = _pick(C, 1280, LANES)
    tr = _pick(S, 256, SUBLANES)

    def body(a_ref, b_ref, h_ref, hp_ref, carry):
        @pl.when(pl.program_id(1) == 0)
        def _():
            carry[...] = jnp.zeros_like(carry)

        rows = lax.broadcasted_iota(jnp.int32, (SUBLANES, tc), 0)

        def slab(s, h):
            base = pl.multiple_of(s * SUBLANES, SUBLANES)
            av, bv = a_ref[pl.ds(base, SUBLANES), :], b_ref[pl.ds(base, SUBLANES), :]
            ho = jnp.zeros((SUBLANES, tc), F32)
            po = jnp.zeros((SUBLANES, tc), F32)
            for r in range(SUBLANES):
                po = jnp.where(rows == r, h, po)
                h = av[r:r + 1, :] * h + bv[r:r + 1, :]
                ho = jnp.where(rows == r, h, ho)
            h_ref[pl.ds(base, SUBLANES), :] = ho
            hp_ref[pl.ds(base, SUBLANES), :] = po
            return h

        carry[...] = lax.fori_loop(0, tr // SUBLANES, slab, carry[...])

    t = _T(tr, tc)
    return _call(name, body, (C // tc, S // tr), [t, t], [t, t], [_sds((S, C), F32), _sds((S, C), F32)],
                 scratch=[pltpu.VMEM((1, tc), F32)])(a, inp)


def _scan_bwd(a, dh, *, name):
    S, C = a.shape
    tc = _pick(C, 1280, LANES)
    tr = _pick(S, 256, SUBLANES)
    nr = S // tr

    def body(a_ref, d_ref, o_ref, carry):
        @pl.when(pl.program_id(1) == 0)
        def _():
            carry[...] = jnp.zeros_like(carry)

        rows = lax.broadcasted_iota(jnp.int32, (SUBLANES, tc), 0)
        n_slabs = tr // SUBLANES

        def slab(s, g):
            base = pl.multiple_of((n_slabs - 1 - s) * SUBLANES, SUBLANES)
            av, dv = a_ref[pl.ds(base, SUBLANES), :], d_ref[pl.ds(base, SUBLANES), :]
            out = jnp.zeros((SUBLANES, tc), F32)
            for r in range(SUBLANES - 1, -1, -1):
                acc = dv[r:r + 1, :] + g
                out = jnp.where(rows == r, acc, out)
                g = av[r:r + 1, :] * acc
            o_ref[pl.ds(base, SUBLANES), :] = out
            return g

        carry[...] = lax.fori_loop(0, n_slabs, slab, carry[...])

    t = pl.BlockSpec((tr, tc), lambda j, i: (nr - 1 - i, j))
    return _call(name, body, (C // tc, nr), [t, t], t, _sds((S, C), F32), scratch=[pltpu.VMEM((1, tc), F32)])(a, dh)


SCALE = HEAD_DIM ** -0.5


def _bucket_table(num_buckets):
    qi = np.arange(BLOCK)[:, None]
    j = np.arange(BLOCK)[None, :]
    dist = np.where(j <= qi, qi - j, qi + BLOCK - j)
    d = np.maximum(dist, 1).astype(np.float64)
    large = NUM_EXACT + (np.log(d / NUM_EXACT) / math.log(MAX_DISTANCE / NUM_EXACT) * (num_buckets - NUM_EXACT)).astype(np.int32)
    large = np.minimum(large, num_buckets - 1)
    return np.where(dist < NUM_EXACT, dist, large).astype(np.int32)


def _bias_table(rel_bias, n_heads, *, name):
    nbk = rel_bias.shape[0]
    bucket = jnp.asarray(_bucket_table(nbk))

    def body(rb_ref, bk_ref, o_ref):
        t, h = pl.program_id(0), pl.program_id(1)
        bk = bk_ref[...]
        acc = jnp.zeros((BLOCK, BLOCK), F32)
        for b in range(nbk):
            acc = jnp.where(bk == b, rb_ref[b, h], acc)
        o_ref[0, 0] = jnp.where(jnp.logical_and(t == 0, jnp.logical_not(_own_block())), NEG_INF, acc)

    return _call(name, body, (2, n_heads),
                 [pl.BlockSpec(memory_space=pltpu.SMEM), pl.BlockSpec((BLOCK, BLOCK), lambda t, h: (0, 0))],
                 pl.BlockSpec((1, 1, BLOCK, BLOCK), lambda t, h: (t, h, 0, 0)),
                 _sds((2, n_heads, BLOCK, BLOCK), F32), sem=("arbitrary", "arbitrary"))(rel_bias, bucket)


def _bias_table_bwd(dbias, nbk, *, name):
    n_heads = dbias.shape[0]
    bucket = jnp.asarray(_bucket_table(nbk))

    def body(db_ref, bk_ref, o_ref):
        h = pl.program_id(0)
        bk = bk_ref[...]
        dv = db_ref[0]
        rows = lax.broadcasted_iota(jnp.int32, (nbk, LANES), 0)
        lanes = lax.broadcasted_iota(jnp.int32, (nbk, LANES), 1)
        acc = jnp.zeros((nbk, LANES), F32)
        for b in range(nbk):
            s = jnp.sum(jnp.sum(jnp.where(bk == b, dv, 0.0), axis=0, keepdims=True), axis=1, keepdims=True)
            acc = jnp.where(jnp.logical_and(rows == b, lanes == h), s, acc)
        _accum(o_ref, acc, h == 0)

    return _call(name, body, (n_heads,),
                 [pl.BlockSpec((1, BLOCK, BLOCK), lambda h: (h, 0, 0)), pl.BlockSpec((BLOCK, BLOCK), lambda h: (0, 0))],
                 pl.BlockSpec((nbk, LANES), lambda h: (0, 0)), _sds((nbk, LANES), F32), sem=("arbitrary",))(dbias, bucket)


def _own_block(heads=1):
    qi = lax.broadcasted_iota(jnp.int32, (heads * BLOCK, BLOCK), 0) % BLOCK
    j = lax.broadcasted_iota(jnp.int32, (heads * BLOCK, BLOCK), 1)
    return j <= qi


def _to_window(band, own):
    return jnp.where(own, band[:, BLOCK:], band[:, :BLOCK])


def _to_band(win, own):
    return jnp.concatenate([jnp.where(own, 0.0, win), jnp.where(own, win, 0.0)], axis=1)


def _stack_pairs(ref, group):
    return jnp.concatenate([ref[:, (g // 2) * LANES:(g // 2 + 1) * LANES] for g in range(group)], axis=0).astype(F32)


def _own_lanes(group):
    rows = lax.broadcasted_iota(jnp.int32, (group * BLOCK, LANES), 0)
    lanes = lax.broadcasted_iota(jnp.int32, (group * BLOCK, LANES), 1)
    return (lanes < HEAD_DIM) == ((rows // BLOCK) % 2 == 0)


def _unstack_pairs(stacked, o_ref, group):
    lo = lax.broadcasted_iota(jnp.int32, (BLOCK, LANES), 1) < HEAD_DIM
    for p in range(group // 2):
        even = stacked[2 * p * BLOCK:(2 * p + 1) * BLOCK, :]
        odd = stacked[(2 * p + 1) * BLOCK:(2 * p + 2) * BLOCK, :]
        o_ref[:, p * LANES:(p + 1) * LANES] = jnp.where(lo, even, odd).astype(o_ref.dtype)


def _group_probs(qm, kb, b_ref, s_ref, own, group):
    band = lax.dot_general(qm, kb, (((1,), (1,)), ((), ())), preferred_element_type=F32)
    bias = jnp.concatenate([b_ref[0, g] for g in range(group)], axis=0)
    sink = jnp.concatenate([jnp.broadcast_to(s_ref[g][:, 0:1], (BLOCK, 1)) for g in range(group)], axis=0)
    s = _to_window(band, own) + bias
    m = jnp.maximum(jnp.max(s, axis=-1, keepdims=True), sink)
    p = jnp.exp(s - m)
    es = jnp.exp(sink - m)
    inv = 1.0 / (jnp.sum(p, axis=-1, keepdims=True) + es)
    return p, inv, es


def _attn_fwd(proj, bias, sinks_b, n_kv, group, q_off, k_off, v_off, *, name, ride=None):
    S = proj.shape[0]
    nblk = S // BLOCK
    gw = group * HEAD_DIM
    pairs = group // 2

    def body(q_ref, kc_ref, kp_ref, vc_ref, vp_ref, b_ref, s_ref, o_ref):
        kb = jnp.concatenate([kp_ref[...], kc_ref[...]], axis=0).astype(BF16)
        vb = jnp.concatenate([vp_ref[...], vc_ref[...]], axis=0).astype(BF16)
        own = _own_block(group)
        qm = jnp.where(_own_lanes(group), _stack_pairs(q_ref, group) * SCALE, 0.0).astype(BF16)
        pu, inv, _ = _group_probs(qm, kb, b_ref, s_ref, own, group)
        o = jnp.dot(_to_band(pu, own).astype(BF16), vb, preferred_element_type=F32) * inv
        _unstack_pairs(o, o_ref, group)

    qb, kb0, vb0 = q_off // gw, k_off // LANES, v_off // LANES
    cur = lambda b0: pl.BlockSpec((BLOCK, LANES), lambda j, n: (n, b0 + j))
    prev = lambda b0: pl.BlockSpec((BLOCK, LANES), lambda j, n: (jnp.maximum(n - 1, 0), b0 + j))
    return _call(name, body, (n_kv, nblk),
                 [pl.BlockSpec((BLOCK, gw), lambda j, n: (n, qb + j)), cur(kb0), prev(kb0), cur(vb0), prev(vb0),
                  pl.BlockSpec((1, group, BLOCK, BLOCK), lambda j, n: (jnp.minimum(n, 1), j, 0, 0)),
                  pl.BlockSpec((group, 1, LANES), lambda j, n: (j, 0, 0))],
                 pl.BlockSpec((BLOCK, gw), lambda j, n: (n, j)), _sds((S, n_kv * gw), BF16),
                 sem=("parallel", "parallel"), ride=ride)(proj, proj, proj, proj, proj, bias, sinks_b)


def _attn_bwd(proj, o, do, bias, sinks_b, n_kv, group, q_off, k_off, v_off, *, name, ride=None):
    S = proj.shape[0]
    nblk = S // BLOCK
    gw = group * HEAD_DIM
    rows_all = group * BLOCK
    nt = (((1,), (1,)), ((), ()))
    tn = (((0,), (0,)), ((), ()))

    def body(q_ref, kc_ref, kp_ref, vc_ref, vp_ref, b_ref, s_ref, o_ref, do_ref,
             dq_ref, dk_ref, dv_ref, db_ref, ds_ref, ck, cv, sacc):
        n = pl.program_id(1)
        lo2 = lax.broadcasted_iota(jnp.int32, (2 * BLOCK, LANES), 1) < HEAD_DIM

        @pl.when(n == 0)
        def _():
            ck[...] = jnp.zeros_like(ck)
            cv[...] = jnp.zeros_like(cv)
            sacc[...] = jnp.zeros_like(sacc)
            db_ref[...] = jnp.zeros_like(db_ref)

        @pl.when(n < nblk)
        def _():
            kb = jnp.concatenate([kp_ref[...], kc_ref[...]], axis=0).astype(BF16)
            vb = jnp.concatenate([vp_ref[...], vc_ref[...]], axis=0).astype(BF16)
            own = _own_block(group)
            mine = _own_lanes(group)
            qm = jnp.where(mine, _stack_pairs(q_ref, group) * SCALE, 0.0).astype(BF16)
            dom = jnp.where(mine, _stack_pairs(do_ref, group), 0.0)
            pu, inv, es = _group_probs(qm, kb, b_ref, s_ref, own, group)
            pr = pu * inv
            delta = jnp.sum(dom * _stack_pairs(o_ref, group), axis=-1, keepdims=True)
            domb = dom.astype(BF16)
            dp = _to_window(lax.dot_general(domb, vb, nt, preferred_element_type=F32), own)
            ds = pr * (dp - delta)
            for g in range(group):
                db_ref[g] += ds[g * BLOCK:(g + 1) * BLOCK, :]
            sacc[...] += -(es * inv) * delta
            dsb = _to_band(ds, own).astype(BF16)
            _unstack_pairs(jnp.dot(dsb, kb, preferred_element_type=F32) * SCALE, dq_ref, group)
            dk_acc = lax.dot_general(dsb, qm, tn, preferred_element_type=F32)
            dv_acc = lax.dot_general(_to_band(pr, own).astype(BF16), domb, tn, preferred_element_type=F32)
            dkf = jnp.where(lo2, dk_acc + pltpu.roll(dk_acc, HEAD_DIM, 1), 0.0)
            dvf = jnp.where(lo2, dv_acc + pltpu.roll(dv_acc, HEAD_DIM, 1), 0.0)

            @pl.when(n > 0)
            def _():
                dk_ref[...] = (ck[...] + dkf[0:BLOCK, :]).astype(BF16)
                dv_ref[...] = (cv[...] + dvf[0:BLOCK, :]).astype(BF16)

            ck[...] = dkf[BLOCK:2 * BLOCK, :]
            cv[...] = dvf[BLOCK:2 * BLOCK, :]

        @pl.when(n == nblk)
        def _():
            dk_ref[...] = ck[...].astype(BF16)
            dv_ref[...] = cv[...].astype(BF16)
            lane = lax.broadcasted_iota(jnp.int32, (1, LANES), 1)
            tot = jnp.zeros((1, LANES), F32)
            for g in range(group):
                tot = jnp.where(lane == g, _colsum(sacc[g * BLOCK:(g + 1) * BLOCK, :]), tot)
            ds_ref[0] = tot

    qb, kb0, vb0 = q_off // gw, k_off // LANES, v_off // LANES
    last = nblk - 1
    cur = lambda b0: pl.BlockSpec((BLOCK, LANES), lambda j, n: (jnp.minimum(n, last), b0 + j))
    prev = lambda b0: pl.BlockSpec((BLOCK, LANES), lambda j, n: (jnp.clip(n - 1, 0, last), b0 + j))
    qspec = lambda b0: pl.BlockSpec((BLOCK, gw), lambda j, n: (jnp.minimum(n, last), b0 + j))
    kvout = pl.BlockSpec((BLOCK, LANES), lambda j, n: (jnp.maximum(n - 1, 0), j))
    return _call(name, body, (n_kv, nblk + 1),
                 [qspec(qb), cur(kb0), prev(kb0), cur(vb0), prev(vb0),
                  pl.BlockSpec((1, group, BLOCK, BLOCK), lambda j, n: (jnp.minimum(n, 1), j, 0, 0)),
                  pl.BlockSpec((group, 1, LANES), lambda j, n: (j, 0, 0)), qspec(0), qspec(0)],
                 [qspec(0), kvout, kvout, pl.BlockSpec((group, BLOCK, BLOCK), lambda j, n: (j, 0, 0)),
                  pl.BlockSpec((1, 1, LANES), lambda j, n: (j, 0, 0))],
                 [_sds((S, n_kv * gw), BF16), _sds((S, n_kv * LANES), BF16), _sds((S, n_kv * LANES), BF16),
                  _sds((n_kv * group, BLOCK, BLOCK), F32), _sds((n_kv, 1, LANES), F32)],
                 scratch=[pltpu.VMEM((BLOCK, LANES), F32), pltpu.VMEM((BLOCK, LANES), F32),
                          pltpu.VMEM((rows_all, 1), F32)], ride=ride)(proj, proj, proj, proj, proj, bias, sinks_b, o, do)


def _adamw(w, g, m, v, *, name):
    R, C = w.shape
    tr = _pick(R, 128, SUBLANES)
    bc1 = 1.0 - ADAM_B1 ** ADAM_STEP
    bc2 = 1.0 - ADAM_B2 ** ADAM_STEP

    def body(w_ref, g_ref, m_ref, v_ref, d_ref, nm_ref, nv_ref):
        gv = g_ref[...]
        nm = ADAM_B1 * m_ref[...] + (1.0 - ADAM_B1) * gv
        nv = ADAM_B2 * v_ref[...] + (1.0 - ADAM_B2) * (gv * gv)
        d_ref[...] = -ADAM_LR * ((nm / bc1) / (jnp.sqrt(nv / bc2) + ADAM_EPS) + ADAM_WD * w_ref[...])
        nm_ref[...] = nm
        nv_ref[...] = nv

    t = pl.BlockSpec((tr, C), lambda i: (i, 0))
    o = _sds((R, C), F32)
    return _call(name, body, (R // tr,), [t, t, t, t], [t, t, t], [o, o, o], sem=("parallel",))(w, g, m, v)


def _sum_devices(packs, *, name):
    _, R, C = packs.shape
    tr = _pick(R, 512, SUBLANES)

    def body(p_ref, o_ref):
        acc = p_ref[0]
        for d in range(1, N_DEV):
            acc = acc + p_ref[d]
        o_ref[...] = acc

    return _call(name, body, (R // tr,), [pl.BlockSpec((N_DEV, tr, C), lambda i: (0, i, 0))],
                 pl.BlockSpec((tr, C), lambda i: (i, 0)), _sds((R, C), F32), sem=("parallel",))(packs)


class _Big:
    def __init__(self, kind, R, C):
        self.kind, self.R, self.C = kind, R, C
        self.hr = R // 2
        self.full = (R, N_CHIPS * C) if kind == "col" else (N_CHIPS * R, C)

    def region(self, ref, k, c):
        if self.kind == "col":
            return ref.at[pl.ds(c * self.hr, self.hr), pl.ds(k * self.C, self.C)]
        return ref.at[pl.ds(k * self.R + c * self.hr, self.hr), :]

    def shard(self, ref, k):
        if self.kind == "col":
            return ref.at[:, pl.ds(k * self.C, self.C)]
        return ref.at[pl.ds(k * self.R, self.R), :]


def _pair_sum(spec, g_full, land, c_arr, *, name):
    hr, C = spec.hr, spec.C
    tr = _pick(hr, 128, 16)
    nr = hr // tr

    def body(c_ref, g_ref, l_ref, o_ref):
        o_ref[0] = (g_ref[...] + l_ref[0]).astype(BF16)

    if spec.kind == "col":
        gspec = pl.BlockSpec((tr, C), lambda k, i, c_ref: (c_ref[0] * nr + i, k))
    else:
        gspec = pl.BlockSpec((tr, C), lambda k, i, c_ref: (k * 2 * nr + c_ref[0] * nr + i, 0))
    lspec = pl.BlockSpec((1, tr, C), lambda k, i, c_ref: (k, i, 0))
    return _call(name, body, (N_CHIPS, nr), [gspec, lspec], lspec, _sds((N_CHIPS, hr, C), BF16), nsp=1,
                 sem=("parallel", "parallel"))(c_arr, g_full, land)


def _chip_sum(spec, chipsum, land, kc_arr, *, name):
    hr, C = spec.hr, spec.C
    tr = _pick(hr, 128, 16)
    nr = hr // tr

    def body(kc_ref, s_ref, l_ref, o_ref):
        acc = s_ref[0].astype(F32)
        for j in range(N_CHIPS - 1):
            acc = acc + l_ref[j].astype(F32)
        o_ref[...] = acc

    return _call(name, body, (nr,),
                 [pl.BlockSpec((1, tr, C), lambda i, kc_ref: (kc_ref[0], i, 0)),
                  pl.BlockSpec((N_CHIPS - 1, tr, C), lambda i, kc_ref: (0, i, 0))],
                 pl.BlockSpec((tr, C), lambda i, kc_ref: (kc_ref[1] * nr + i, 0)), _sds((spec.R, C), F32), nsp=1,
                 sem=("parallel",))(kc_arr, chipsum, land)


def _place():
    x, y, c = lax.axis_index("x"), lax.axis_index("y"), lax.axis_index("c")
    return x, y, c, [(1 - x, y), (x, 1 - y), (1 - x, 1 - y)]


def _remote(src, dst, send_sem, recv_sem, dev):
    return pltpu.make_async_remote_copy(src_ref=src, dst_ref=dst, send_sem=send_sem, recv_sem=recv_sem,
                                        device_id=dev, device_id_type=MESH)


def _comm_call(name, body, n, out_shapes, n_remote, in_place=True):
    scratch = [pltpu.SemaphoreType.DMA((n_remote,)), pltpu.SemaphoreType.DMA((n_remote,))]
    aliases = {i: i for i in range(n)} if in_place else {}
    return pl.pallas_call(body, name=name, out_shape=out_shapes, in_specs=[ANY] * n, out_specs=[ANY] * n,
                          scratch_shapes=scratch, input_output_aliases=aliases)


def _placed(block, dev):
    m, n = block.shape
    return lax.dynamic_update_slice(jnp.zeros((N_DEV * m, n), block.dtype), block, (dev * m, 0))


def _all_gather(bufs, *, name):
    n = len(bufs)
    per = 7

    def body(*refs):
        outs = refs[n:2 * n]
        send_sems, recv_sems = refs[2 * n:]
        x, y, c, chips = _place()
        me, sibling = (x, y, c), (x, y, 1 - c)
        made = []
        for w in range(n):
            m = outs[w].shape[0] // N_DEV

            def rows(px, py, pc, w=w, m=m):
                return outs[w].at[pl.ds((4 * px + 2 * py + pc) * m, m), :]

            def copy(k, block, to, w=w, rows=rows):
                return _remote(rows(*block), rows(*block), send_sems.at[w * per + k], recv_sems.at[w * per + k], to)

            first = [copy(0, me, sibling)] + [copy(1 + j, me, (*chip, c)) for j, chip in enumerate(chips)]
            for cp in first:
                cp.start()
            made.append((copy, first))
        sends = []
        for w in range(n):
            copy, first = made[w]
            passed = [copy(4 + j, (*chip, c), sibling) for j, chip in enumerate(chips)]
            for j, chip in enumerate(chips):
                copy(1 + j, (*chip, c), me).wait_recv()
                passed[j].start()
            sends.append(first + passed)
        for w in range(n):
            copy, _ = made[w]
            copy(0, sibling, me).wait_recv()
            for j, chip in enumerate(chips):
                copy(4 + j, (*chip, 1 - c), me).wait_recv()
            for cp in sends[w]:
                cp.wait_send()

    outs = [_sds(a.shape, a.dtype) for a in bufs]
    return _comm_call(name, body, n, outs, per * n)(*bufs)


def _cast_into(spec, w, k_arr, *, name):
    R, C = spec.R, spec.C
    tr = _pick(R, 256, 16)
    nr = R // tr

    def body(k_ref, w_ref, o_ref):
        o_ref[...] = w_ref[...].astype(BF16)

    if spec.kind == "col":
        ospec = pl.BlockSpec((tr, C), lambda i, k_ref: (i, k_ref[0]))
    else:
        ospec = pl.BlockSpec((tr, C), lambda i, k_ref: (k_ref[0] * nr + i, 0))
    return _call(name, body, (nr,), [pl.BlockSpec((tr, C), lambda i, k_ref: (i, 0))], ospec,
                 _sds(spec.full, BF16), nsp=1, sem=("parallel",))(k_arr, w)


def _weight_gather(bufs, specs):
    per = 6

    def to_chips(outs, send_sems, recv_sems):
        x, y, c, chips = _place()
        k_me = 2 * x + y
        cps = []
        for w, sp in enumerate(specs):
            mine = sp.region(outs[w], k_me, c)
            cps += [_remote(mine, mine, send_sems.at[w * per + j], recv_sems.at[w * per + j], (*chip, c))
                    for j, chip in enumerate(chips)]
        return cps

    def start(ins, outs, send_sems, recv_sems):
        for cp in to_chips(outs, send_sems, recv_sems):
            cp.start()

    def finish(ins, outs, send_sems, recv_sems):
        x, y, c, chips = _place()
        sibling = (x, y, 1 - c)
        passed = []
        for w, sp in enumerate(specs):
            for j, chip in enumerate(chips):
                got = sp.region(outs[w], 2 * chip[0] + chip[1], c)
                _remote(got, got, send_sems.at[w * per + j], recv_sems.at[w * per + j], (*chip, c)).wait_recv()
                cp = _remote(got, got, send_sems.at[w * per + 3 + j], recv_sems.at[w * per + 3 + j], sibling)
                cp.start()
                passed.append(cp)
        for w, sp in enumerate(specs):
            for j, chip in enumerate(chips):
                got = sp.region(outs[w], 2 * chip[0] + chip[1], 1 - c)
                _remote(got, got, send_sems.at[w * per + 3 + j], recv_sems.at[w * per + 3 + j], sibling).wait_recv()
        for cp in passed + to_chips(outs, send_sems, recv_sems):
            cp.wait_send()

    return _Exchange(bufs, [_sds(sp.full, BF16) for sp in specs], per * len(specs), start, finish, in_place=True)


def _pair_exchange(grads, specs, *, name):
    n = len(grads)

    def body(*refs):
        ins, outs = refs[:n], refs[n:2 * n]
        send_sems, recv_sems = refs[2 * n:]
        x, y, c, _ = _place()
        sibling = (x, y, 1 - c)
        cps = []
        for w, sp in enumerate(specs):
            for k in range(N_CHIPS):
                cp = _remote(sp.region(ins[w], k, 1 - c), outs[w].at[k], send_sems.at[w * N_CHIPS + k],
                             recv_sems.at[w * N_CHIPS + k], sibling)
                cp.start()
                cps.append(cp)
        for cp in cps:
            cp.wait()

    outs = [_sds((N_CHIPS, sp.hr, sp.C), F32) for sp in specs]
    return _comm_call(name, body, n, outs, N_CHIPS * n, in_place=False)(*grads)


def _chip_exchange(chipsums, specs):
    per = N_CHIPS - 1

    def copies(ins, outs, send_sems, recv_sems):
        x, y, c, chips = _place()
        return [_remote(ins[w].at[2 * chip[0] + chip[1]], outs[w].at[j], send_sems.at[w * per + j],
                        recv_sems.at[w * per + j], (*chip, c))
                for w in range(len(specs)) for j, chip in enumerate(chips)]

    def start(ins, outs, send_sems, recv_sems):
        for cp in copies(ins, outs, send_sems, recv_sems):
            cp.start()

    def finish(ins, outs, send_sems, recv_sems):
        for cp in copies(ins, outs, send_sems, recv_sems):
            cp.wait()

    return _Exchange(chipsums, [_sds((per, sp.hr, sp.C), BF16) for sp in specs], per * len(specs), start, finish,
                     in_place=False)


def _pair_share(bufs, specs, *, name):
    n = len(bufs)

    def body(*refs):
        outs = refs[n:2 * n]
        send_sems, recv_sems = refs[2 * n:]
        x, y, c, _ = _place()
        sibling = (x, y, 1 - c)
        cps = []
        for w, sp in enumerate(specs):
            mine = outs[w].at[pl.ds(c * sp.hr, sp.hr), :]
            cp = _remote(mine, mine, send_sems.at[w], recv_sems.at[w], sibling)
            cp.start()
            cps.append(cp)
        for w, sp in enumerate(specs):
            theirs = outs[w].at[pl.ds((1 - c) * sp.hr, sp.hr), :]
            _remote(theirs, theirs, send_sems.at[w], recv_sems.at[w], sibling).wait_recv()
        for cp in cps:
            cp.wait_send()

    outs = [_sds((sp.R, sp.C), F32) for sp in specs]
    return _comm_call(name, body, n, outs, n)(*bufs)


PACK_ALIGN = SUBLANES * LANES


def _pack(arrs):
    flat = jnp.concatenate([a.reshape(-1) for a in arrs])
    pad = (-flat.shape[0]) % PACK_ALIGN
    return jnp.pad(flat, (0, pad)).reshape(-1, LANES)


def _unpack(packed, shapes):
    flat = packed.reshape(-1)
    out, pos = [], 0
    for s in shapes:
        n = int(np.prod(s))
        out.append(flat[pos:pos + n].reshape(s))
        pos += n
    return out


def kernel(x, c, w_ada, b_ada, norm1, w_in, rnn_conv_w, rnn_conv_b, w_rg_a, b_rg_a, w_rg_i, b_rg_i, rg_lambda, w_o_rnn, w_o_attn, attn_sinks, rel_bias, w_out, norm2, w_up, ffn_conv_w, ffn_conv_b, w_down, norm_f, loss_target, m_w_ada, m_b_ada, m_norm1, m_w_in, m_rnn_conv_w, m_rnn_conv_b, m_w_rg_a, m_b_rg_a, m_w_rg_i, m_b_rg_i, m_rg_lambda, m_w_o_rnn, m_w_o_attn, m_attn_sinks, m_rel_bias, m_w_out, m_norm2, m_w_up, m_ffn_conv_w, m_ffn_conv_b, m_w_down, m_norm_f, v_w_ada, v_b_ada, v_norm1, v_w_in, v_rnn_conv_w, v_rnn_conv_b, v_w_rg_a, v_b_rg_a, v_w_rg_i, v_b_rg_i, v_rg_lambda, v_w_o_rnn, v_w_o_attn, v_attn_sinks, v_rel_bias, v_w_out, v_norm2, v_w_up, v_ffn_conv_w, v_ffn_conv_b, v_w_down, v_norm_f):
    S, D = x.shape[1], x.shape[2]
    d_attn = N_CHIPS * w_o_attn.shape[1]
    d_rnn = N_CHIPS * w_o_rnn.shape[1]
    d_ff = N_CHIPS * w_down.shape[1]
    d_in = N_CHIPS * w_in.shape[2]
    n_heads = attn_sinks.shape[1]
    d_kv = (d_in - d_attn - 2 * d_rnn - 2 * D) // 2
    n_kv = d_kv // HEAD_DIM
    group = n_heads // n_kv
    nbk = rel_bias.shape[0]
    assert d_attn == n_heads * HEAD_DIM and group % 2 == 0 and S % BLOCK == 0

    mx, my, mc = lax.axis_index("x"), lax.axis_index("y"), lax.axis_index("c")
    k_me = 2 * mx + my
    dev = 2 * k_me + mc
    c_arr = jnp.reshape(mc, (1,)).astype(jnp.int32)
    k_arr = jnp.reshape(k_me, (1,)).astype(jnp.int32)
    kc_arr = jnp.stack([k_me, mc]).astype(jnp.int32)

    xs, tgt = x[0], loss_target[0]

    big_names = ["w_in", "w_o_rnn", "w_o_attn", "w_out", "w_up", "w_down"]
    big_w = dict(w_in=w_in[0], w_o_rnn=w_o_rnn[0], w_o_attn=w_o_attn[0], w_out=w_out[0], w_up=w_up[0], w_down=w_down[0])
    big_kind = dict(w_in="col", w_o_rnn="row", w_o_attn="row", w_out="row", w_up="col", w_down="row")
    specs = {k: _Big(big_kind[k], *big_w[k].shape) for k in big_names}
    placed = {k: _cast_into(specs[k], big_w[k], k_arr, name="cast_" + k) for k in big_names}
    later = big_names[1:]
    W = dict(w_in=_weight_gather([placed["w_in"]], [specs["w_in"]]).alone("gather_w_in")[0])
    c_all, cw4, cw3 = _all_gather([_placed(jnp.broadcast_to(c, (SUBLANES, D)), dev),
                                   _placed(jnp.pad(rnn_conv_w[0], ((0, SUBLANES - rnn_conv_w.shape[1]), (0, 0))), dev),
                                   _placed(jnp.pad(ffn_conv_w[0], ((0, SUBLANES - ffn_conv_w.shape[1]), (0, 0))), dev)],
                                  name="gather_cond")
    c_all = c_all.reshape(N_DEV, SUBLANES, D)[:, 0]

    def from_chips(g, taps):
        cs = g.shape[1]
        g = g.reshape(N_CHIPS, 2, SUBLANES, cs)[:, 0, :taps]
        return jnp.transpose(g, (1, 0, 2)).reshape(taps, N_CHIPS * cs)

    conv4_w = from_chips(cw4, rnn_conv_w.shape[1])
    conv3_w = from_chips(cw3, ffn_conv_w.shape[1])

    (silu_c,) = _ew("silu_c", lambda v: (v * _sigmoid(v),), [(c_all, 0)], [F32], N_DEV, D)
    mod_sh = _mm(silu_c, w_ada[0], name="mod", exact=True, bias=lax.dynamic_slice_in_dim(b_ada, k_me * w_ada.shape[2], w_ada.shape[2], 1))
    (mod_g,) = _all_gather([_placed(mod_sh, dev)], name="gather_mod")
    mod_all = jnp.transpose(mod_g.reshape(N_CHIPS, 2, N_DEV, -1)[:, 0], (1, 0, 2)).reshape(N_DEV, 6 * D)
    mod = lax.dynamic_slice_in_dim(mod_all, dev, 1, 0)
    shift1, scale1, gate1, shift2, scale2, gate2 = [mod[:, i * D:(i + 1) * D] for i in range(6)]

    o_k, o_v, o_xr = d_attn, d_attn + d_kv, d_attn + 2 * d_kv
    wi = W["w_in"]

    spread = np.zeros((d_kv, n_kv * LANES), np.float32)
    for col in range(d_kv):
        spread[col, (col // HEAD_DIM) * LANES + col % HEAD_DIM] = 1.0
        spread[col, (col // HEAD_DIM) * LANES + HEAD_DIM + col % HEAD_DIM] = 1.0
    w_in_x = jnp.concatenate([wi[:, :o_k],
                              _mm(wi[:, o_k:o_v], jnp.asarray(spread, BF16), name="spread_k", out_dtype=BF16),
                              _mm(wi[:, o_v:o_xr], jnp.asarray(spread, BF16), name="spread_v", out_dtype=BF16),
                              wi[:, o_xr:]], axis=1)
    e_k = d_attn
    e_v = e_k + n_kv * LANES
    e_xr = e_v + n_kv * LANES
    e_gr = e_xr + d_rnn
    e_ga = e_gr + d_rnn
    e_gl = e_ga + D
    d_ext = e_gl + D

    u = _adaln_fwd(xs, norm1, scale1, shift1, name="adaln1")
    proj = _mm(u, w_in_x, name="proj", out_dtype=BF16)
    bias = _bias_table(rel_bias, n_heads, name="bias_table")
    sinks_b = jnp.broadcast_to(attn_sinks.reshape(n_heads, 1, 1), (n_heads, 1, LANES))
    o_attn, gathered = _attn_fwd(proj, bias, sinks_b, n_kv, group, 0, e_k, e_v, name="attn_fwd",
                                 ride=_weight_gather([placed[k] for k in later], [specs[k] for k in later]))
    W.update(zip(later, gathered))
    y_attn = _mm(o_attn, W["w_o_attn"], name="y_attn")
    xc = _conv_fwd(proj, e_xr, d_rnn, conv4_w, rnn_conv_b, name="conv4")
    a_t, inp = _gates_fwd(xc, w_rg_a[0], b_rg_a, w_rg_i[0], b_rg_i, rg_lambda, name="gates")
    h_rnn, h_prev = _scan_fwd(a_t, inp, name="scan")
    (z,) = _ew("rnn_gate", lambda h, g: (h * _gelu(g),), [(h_rnn, 0), (proj, e_gr)], [BF16], S, d_rnn)
    y_rnn = _mm(z, W["w_o_rnn"], name="y_rnn")
    (merged,) = _ew("merge", lambda ya, yr, ga, gl: (_sigmoid(ga) * ya + _sigmoid(gl) * yr,),
                    [(y_attn, 0), (y_rnn, 0), (proj, e_ga), (proj, e_gl)], [BF16], S, D)
    t1 = _mm(merged, W["w_out"], name="t1")
    u2, h1 = _adaln_fwd(xs, norm2, scale2, shift2, name="adaln2", t=t1, gate=gate1)
    up = _mm(u2, W["w_up"], name="up", out_dtype=BF16)
    a2 = _geglu_fwd(up, conv3_w, ffn_conv_b, d_ff, name="geglu")
    t2 = _mm(a2, W["w_down"], name="t2")
    dh2, dt2, loss_tile, g_norm_f, d_gate2 = _final(h1, t2, gate2, norm_f.reshape(1, D), tgt, name="final")

    da2 = _mm(dt2, W["w_down"], name="da2", tb=True)
    g_w_down = _mm(a2, dt2, name="g_w_down", ta=True)
    dupg, dupv, g_c3g, g_c3v, g_b3g, g_b3v = _ffn_bwd(up, conv3_w, ffn_conv_b, da2, d_ff, name="ffn_bwd")
    dup = jnp.concatenate([dupg, dupv], axis=1)
    g_conv3_w = jnp.concatenate([g_c3g, g_c3v], axis=1)
    g_conv3_b = jnp.concatenate([g_b3g, g_b3v], axis=1)
    du2 = _mm(dup, W["w_up"], name="du2", tb=True)
    g_w_up = _mm(u2, dup, name="g_w_up", ta=True)
    dh1, d_shift2, d_scale2, g_norm2, dt1, d_gate1 = _adaln_bwd(h1, du2, dh2, norm2, scale2, name="adaln2_bwd", t=t1, gate=gate1)

    dmerged = _mm(dt1, W["w_out"], name="dmerged", tb=True)
    g_w_out = _mm(merged, dt1, name="g_w_out", ta=True)

    def merge_bwd(dm, ya, yr, ga, gl):
        sa, sl = _sigmoid(ga), _sigmoid(gl)
        return dm * sa, dm * sl, dm * ya * sa * (1.0 - sa), dm * yr * sl * (1.0 - sl)

    dy_attn, dy_rnn, d_ga, d_gl = _ew("merge_bwd", merge_bwd, [(dmerged, 0), (y_attn, 0), (y_rnn, 0), (proj, e_ga), (proj, e_gl)],
                                      [BF16, BF16, BF16, BF16], S, D)
    do = _mm(dy_attn, W["w_o_attn"], name="do", tb=True, out_dtype=BF16)
    g_w_o_attn = _mm(o_attn, dy_attn, name="g_w_o_attn", ta=True)
    dz = _mm(dy_rnn, W["w_o_rnn"], name="dz", tb=True)
    g_w_o_rnn = _mm(z, dy_rnn, name="g_w_o_rnn", ta=True)

    def rnn_gate_bwd(dzv, h, g):
        ge, dge = _gelu_and_grad(g)
        return dzv * ge, dzv * h * dge

    dh_rnn, d_gr = _ew("rnn_gate_bwd", rnn_gate_bwd, [(dz, 0), (h_rnn, 0), (proj, e_gr)], [F32, BF16], S, d_rnn)
    dacc = _scan_bwd(a_t, dh_rnn, name="scan_bwd")
    dxc, g_w_rg_a, g_w_rg_i, g_b_rg_a, g_b_rg_i, g_lam = _gates_bwd(xc, dacc, h_prev, w_rg_a[0], b_rg_a, w_rg_i[0], b_rg_i,
                                                                    rg_lambda, name="gates_bwd")
    d_xr, g_conv4_w, g_conv4_b = _conv_bwd(dxc, proj, e_xr, d_rnn, conv4_w, name="conv4_bwd")
    big_g = dict(w_o_rnn=g_w_o_rnn, w_o_attn=g_w_o_attn, w_out=g_w_out, w_up=g_w_up, w_down=g_w_down)
    landed = _pair_exchange([big_g[k] for k in later], [specs[k] for k in later], name="pair_exchange")
    chipsums = {k: _pair_sum(specs[k], big_g[k], l, c_arr, name="pair_sum_" + k) for k, l in zip(later, landed)}
    (dq, dk, dv, dbias, dsink), landed2 = _attn_bwd(
        proj, o_attn, do, bias, sinks_b, n_kv, group, 0, e_k, e_v, name="attn_bwd",
        ride=_chip_exchange([chipsums[k] for k in later], [specs[k] for k in later]))
    landed2 = dict(zip(later, landed2))
    g_rel = _bias_table_bwd(dbias, nbk, name="bias_table_bwd")[:, :n_heads]
    g_sinks = dsink[:, 0, :group].reshape(1, n_heads)
    dproj = jnp.concatenate([dq, dk, dv, d_xr, d_gr, d_ga, d_gl], axis=1)
    g_w_in_x = _mm(u, dproj, name="g_w_in", ta=True)

    gather_m = jnp.asarray(spread.T, F32)
    g_w_in = jnp.concatenate([g_w_in_x[:, :e_k],
                              _mm(g_w_in_x[:, e_k:e_v], gather_m, name="gather_k", exact=True),
                              _mm(g_w_in_x[:, e_v:e_xr], gather_m, name="gather_v", exact=True),
                              g_w_in_x[:, e_xr:]], axis=1)
    big_g["w_in"] = g_w_in
    (landed_in,) = _pair_exchange([g_w_in], [specs["w_in"]], name="pair_exchange_w_in")
    chipsums["w_in"] = _pair_sum(specs["w_in"], g_w_in, landed_in, c_arr, name="pair_sum_w_in")
    du, (landed2["w_in"],) = _mm(dproj, w_in_x, name="du", tb=True, ride=_chip_exchange([chipsums["w_in"]], [specs["w_in"]]))
    grad_x, d_shift1, d_scale1, g_norm1 = _adaln_bwd(xs, du, dh1, norm1, scale1, name="adaln1_bwd")
    dmod = jnp.concatenate([d_shift1, d_scale1, d_gate1, d_shift2, d_scale2, d_gate2], axis=1)

    small = [loss_tile[0:1, 0:1], dmod, g_norm1, g_conv4_w, g_conv4_b, g_w_rg_a, g_b_rg_a, g_w_rg_i, g_b_rg_i, g_lam,
             g_sinks, g_rel, g_norm2, g_conv3_w, g_conv3_b, g_norm_f]
    small_shapes = [a.shape for a in small]
    pack = _pack(small)
    (packs,) = _all_gather([_placed(pack, dev)], name="gather_small")
    packs = packs.reshape(N_DEV, -1, LANES)
    summed = _unpack(_sum_devices(packs, name="sum_small"), small_shapes)
    (loss_s, g_b_ada, g_norm1, g_conv4_w, g_conv4_b, g_w_rg_a, g_b_rg_a, g_w_rg_i, g_b_rg_i, g_lam,
     g_sinks, g_rel, g_norm2, g_conv3_w, g_conv3_b, g_norm_f) = summed
    loss = loss_s.reshape(())
    dmod_all = packs.reshape(N_DEV, -1)[:, 1:1 + 6 * D]
    cs_ada = w_ada.shape[2]
    g_w_ada = _mm(silu_c, lax.dynamic_slice_in_dim(dmod_all, k_me * cs_ada, cs_ada, 1), name="g_w_ada", ta=True, exact=True)
    cs4, cs3 = rnn_conv_w.shape[2], ffn_conv_w.shape[2]
    g_conv4_sh = lax.dynamic_slice_in_dim(g_conv4_w, k_me * cs4, cs4, 1)
    g_conv3_sh = lax.dynamic_slice_in_dim(g_conv3_w, k_me * cs3, cs3, 1)

    sp_list = [specs[k] for k in big_names]
    halves = [_chip_sum(specs[k], chipsums[k], landed2[k], kc_arr, name="chip_sum_" + k) for k in big_names]
    shards = dict(zip(big_names, _pair_share(halves, sp_list, name="pair_share")))

    grads = dict(w_ada=g_w_ada[None], b_ada=g_b_ada, norm1=g_norm1, w_in=shards["w_in"][None], rnn_conv_w=g_conv4_sh[None],
                 rnn_conv_b=g_conv4_b, w_rg_a=g_w_rg_a[None], b_rg_a=g_b_rg_a, w_rg_i=g_w_rg_i[None], b_rg_i=g_b_rg_i,
                 rg_lambda=g_lam, w_o_rnn=shards["w_o_rnn"][None], w_o_attn=shards["w_o_attn"][None], attn_sinks=g_sinks,
                 rel_bias=g_rel, w_out=shards["w_out"][None], norm2=g_norm2, w_up=shards["w_up"][None],
                 ffn_conv_w=g_conv3_sh[None], ffn_conv_b=g_conv3_b, w_down=shards["w_down"][None], norm_f=g_norm_f.reshape(D))
    weights = dict(w_ada=w_ada, b_ada=b_ada, norm1=norm1, w_in=w_in, rnn_conv_w=rnn_conv_w, rnn_conv_b=rnn_conv_b, w_rg_a=w_rg_a,
                   b_rg_a=b_rg_a, w_rg_i=w_rg_i, b_rg_i=b_rg_i, rg_lambda=rg_lambda, w_o_rnn=w_o_rnn, w_o_attn=w_o_attn,
                   attn_sinks=attn_sinks, rel_bias=rel_bias, w_out=w_out, norm2=norm2, w_up=w_up, ffn_conv_w=ffn_conv_w,
                   ffn_conv_b=ffn_conv_b, w_down=w_down, norm_f=norm_f)
    moms = dict(w_ada=(m_w_ada, v_w_ada), b_ada=(m_b_ada, v_b_ada), norm1=(m_norm1, v_norm1), w_in=(m_w_in, v_w_in),
                rnn_conv_w=(m_rnn_conv_w, v_rnn_conv_w), rnn_conv_b=(m_rnn_conv_b, v_rnn_conv_b), w_rg_a=(m_w_rg_a, v_w_rg_a),
                b_rg_a=(m_b_rg_a, v_b_rg_a), w_rg_i=(m_w_rg_i, v_w_rg_i), b_rg_i=(m_b_rg_i, v_b_rg_i),
                rg_lambda=(m_rg_lambda, v_rg_lambda), w_o_rnn=(m_w_o_rnn, v_w_o_rnn), w_o_attn=(m_w_o_attn, v_w_o_attn),
                attn_sinks=(m_attn_sinks, v_attn_sinks), rel_bias=(m_rel_bias, v_rel_bias), w_out=(m_w_out, v_w_out),
                norm2=(m_norm2, v_norm2), w_up=(m_w_up, v_w_up), ffn_conv_w=(m_ffn_conv_w, v_ffn_conv_w),
                ffn_conv_b=(m_ffn_conv_b, v_ffn_conv_b), w_down=(m_w_down, v_w_down), norm_f=(m_norm_f, v_norm_f))
    names = list(weights)
    grads = {k: grads[k].reshape(weights[k].shape) for k in names}
    large = ["w_ada"] + big_names
    delta, new_m, new_v = {}, {}, {}
    for k in large:
        shp = weights[k].shape
        two = lambda a: a.reshape(shp[-2], shp[-1])
        d_, m_, v_ = _adamw(two(weights[k]), two(grads[k]), two(moms[k][0]), two(moms[k][1]), name="adamw_" + k)
        delta[k], new_m[k], new_v[k] = d_.reshape(shp), m_.reshape(shp), v_.reshape(shp)
    rest = [k for k in names if k not in large]
    rest_shapes = [weights[k].shape for k in rest]
    d_, m_, v_ = _adamw(_pack([weights[k] for k in rest]), _pack([grads[k] for k in rest]),
                        _pack([moms[k][0] for k in rest]), _pack([moms[k][1] for k in rest]), name="adamw_small")
    for k, dd, mm_, vv in zip(rest, _unpack(d_, rest_shapes), _unpack(m_, rest_shapes), _unpack(v_, rest_shapes)):
        delta[k], new_m[k], new_v[k] = dd, mm_, vv

    return (loss, grad_x[None], *[grads[k] for k in names], *[delta[k] for k in names],
            *[new_m[k] for k in names], *[new_v[k] for k in names])
```

```python
import math

import numpy as np
import jax
import jax.numpy as jnp
from jax import lax
from jax.experimental import pallas as pl
from jax.experimental.pallas import tpu as pltpu

F32 = jnp.float32
BF16 = jnp.bfloat16
MESH = pl.DeviceIdType.MESH
ANY = pl.BlockSpec(memory_space=pl.ANY)

EPS = 1e-6
NEG_INF = -1e30
HEAD_DIM = 64
BLOCK = 128
NUM_EXACT = 16
MAX_DISTANCE = 128
RG_C = 8.0
ADAM_LR, ADAM_B1, ADAM_B2, ADAM_EPS, ADAM_WD, ADAM_STEP = 0.001, 0.9, 0.999, 1e-08, 0.01, 10
N_CHIPS = 4
N_DEV = 8
SUBLANES = 8
LANES = 128
VMEM_LIMIT_BYTES = 48 * 1024 * 1024
GELU_C0 = math.sqrt(2.0 / math.pi)
GELU_C1 = 0.044715


def _pick(dim, pref, align):
    if dim <= pref:
        return dim
    t = (pref // align) * align
    while t >= align:
        if dim % t == 0:
            return t
        t -= align
    return dim


def _params(sem):
    return pltpu.CompilerParams(dimension_semantics=sem, vmem_limit_bytes=VMEM_LIMIT_BYTES)


def _call(name, body, grid, in_specs, out_specs, out_shape, scratch=(), nsp=0, sem=None, ride=None):
    if ride is not None:
        return _call_with_ride(name, body, grid, in_specs, out_specs, out_shape, scratch, ride)
    sem = sem or ("parallel",) * (len(grid) - 1) + ("arbitrary",)
    if nsp:
        gs = pltpu.PrefetchScalarGridSpec(num_scalar_prefetch=nsp, grid=grid, in_specs=in_specs,
                                          out_specs=out_specs, scratch_shapes=list(scratch))
        return pl.pallas_call(body, name=name, grid_spec=gs, out_shape=out_shape, compiler_params=_params(sem))
    return pl.pallas_call(body, name=name, grid=grid, in_specs=in_specs, out_specs=out_specs,
                          out_shape=out_shape, scratch_shapes=list(scratch), compiler_params=_params(sem))


def _sds(shape, dtype):
    return jax.ShapeDtypeStruct(shape, dtype)


class _Exchange:
    def __init__(self, ins, outs, n_sems, start, finish, in_place):
        self.ins, self.outs, self.n_sems = list(ins), list(outs), n_sems
        self.start, self.finish, self.in_place = start, finish, in_place

    def alone(self, name):
        n_in, n_out = len(self.ins), len(self.outs)

        def body(*refs):
            ins, outs = refs[:n_in], refs[n_in:n_in + n_out]
            send_sems, recv_sems = refs[n_in + n_out:]
            self.start(ins, outs, send_sems, recv_sems)
            self.finish(ins, outs, send_sems, recv_sems)

        sems = [pltpu.SemaphoreType.DMA((self.n_sems,)), pltpu.SemaphoreType.DMA((self.n_sems,))]
        aliases = {i: i for i in range(n_in)} if self.in_place else {}
        return pl.pallas_call(body, name=name, out_shape=self.outs, in_specs=[ANY] * n_in, out_specs=[ANY] * n_out,
                              scratch_shapes=sems, input_output_aliases=aliases)(*self.ins)


def _call_with_ride(name, body, grid, in_specs, out_specs, out_shape, scratch, ride):
    single = not isinstance(out_specs, (list, tuple))
    out_specs = [out_specs] if single else list(out_specs)
    out_shape = [out_shape] if single else list(out_shape)
    n_in, n_out, n_sc = len(in_specs), len(out_specs), len(scratch)
    r_in, r_out = len(ride.ins), len(ride.outs)

    def wrapped(*refs):
        pos = 0
        parts = []
        for n in (n_in, r_in, n_out, r_out, n_sc, 2):
            parts.append(refs[pos:pos + n])
            pos += n
        core_in, ride_in, core_out, ride_out, core_scratch, (send_sems, recv_sems) = parts
        first = pl.program_id(0) == 0
        last = pl.program_id(0) == grid[0] - 1
        for axis in range(1, len(grid)):
            first = jnp.logical_and(first, pl.program_id(axis) == 0)
            last = jnp.logical_and(last, pl.program_id(axis) == grid[axis] - 1)

        @pl.when(first)
        def _():
            ride.start(ride_in, ride_out, send_sems, recv_sems)

        body(*core_in, *core_out, *core_scratch)

        @pl.when(last)
        def _():
            ride.finish(ride_in, ride_out, send_sems, recv_sems)

    sems = [pltpu.SemaphoreType.DMA((ride.n_sems,)), pltpu.SemaphoreType.DMA((ride.n_sems,))]
    aliases = {n_in + k: n_out + k for k in range(r_in)} if ride.in_place else {}
    call = pl.pallas_call(wrapped, name=name, grid=grid, in_specs=list(in_specs) + [ANY] * r_in,
                          out_specs=out_specs + [ANY] * r_out, out_shape=out_shape + ride.outs,
                          scratch_shapes=list(scratch) + sems, input_output_aliases=aliases,
                          compiler_params=_params(("arbitrary",) * len(grid)))

    def run(*args):
        res = call(*args, *ride.ins)
        core = res[0] if single else list(res[:n_out])
        return core, list(res[n_out:])

    return run


def _T(tr, tc, off=0):
    return pl.BlockSpec((tr, tc), lambda j, i: (i, j + off))


def _P(rows, tc, off=0):
    return pl.BlockSpec((rows, tc), lambda j, i: (0, j + off))


def _gelu(x):
    t = jnp.tanh(GELU_C0 * (x + GELU_C1 * x * x * x))
    return 0.5 * x * (1.0 + t)


def _gelu_and_grad(x):
    t = jnp.tanh(GELU_C0 * (x + GELU_C1 * x * x * x))
    g = 0.5 * x * (1.0 + t)
    dg = 0.5 * (1.0 + t) + 0.5 * x * (1.0 - t * t) * GELU_C0 * (1.0 + 3.0 * GELU_C1 * x * x)
    return g, dg


def _sigmoid(x):
    return 1.0 / (1.0 + jnp.exp(-x))


def _accum(ref, val, first):
    @pl.when(first)
    def _():
        ref[...] = val

    @pl.when(jnp.logical_not(first))
    def _():
        ref[...] += val


def _colsum(v):
    return jnp.sum(v, axis=0, keepdims=True)


CHUNK = 16


def _chunks(n_rows, ch, step):
    def it(i, carry):
        step(pl.multiple_of(i * ch, ch))
        return carry

    lax.fori_loop(0, n_rows // ch, it, 0)


def _fold(v):
    return v[0:SUBLANES, :] + v[SUBLANES:2 * SUBLANES, :]


def _shift_down(xe, d, ch):
    return xe[SUBLANES:SUBLANES + ch, :] if d == 0 else pltpu.roll(xe, d, 0)[SUBLANES:SUBLANES + ch, :]


def _shift_up(xe, d, ch):
    return xe[0:ch, :] if d == 0 else pltpu.roll(xe, ch + SUBLANES - d, 0)[0:ch, :]


def _mm(a, b, *, name, ta=False, tb=False, out_dtype=F32, tm=1024, tn=1024, tk=2048, exact=False, bias=None, ride=None):
    if ta:
        K, M = a.shape
    else:
        M, K = a.shape
    if tb:
        N, K2 = b.shape
    else:
        K2, N = b.shape
    assert K == K2, (a.shape, b.shape, ta, tb)
    tm, tn, tk = _pick(M, tm, LANES), _pick(N, tn, LANES), _pick(K, tk, LANES)
    nk = K // tk
    cdt = F32 if exact else BF16
    prec = lax.Precision.HIGHEST if exact else None
    dims = (((0 if ta else 1,), (1 if tb else 0,)), ((), ()))

    def body(*refs):
        a_ref, b_ref = refs[0], refs[1]
        bias_ref = refs[2] if bias is not None else None
        o_ref = refs[3] if bias is not None else refs[2]
        part = lax.dot_general(a_ref[...].astype(cdt), b_ref[...].astype(cdt), dims,
                               preferred_element_type=F32, precision=prec)

        def finish(r):
            if bias is not None:
                r = r + bias_ref[...]
            o_ref[...] = r.astype(out_dtype)

        if nk == 1:
            finish(part)
            return
        acc_ref = refs[-1]
        k = pl.program_id(2)

        @pl.when(k == 0)
        def _():
            acc_ref[...] = part

        @pl.when(jnp.logical_and(k > 0, k < nk - 1))
        def _():
            acc_ref[...] += part

        @pl.when(k == nk - 1)
        def _():
            finish(acc_ref[...] + part)

    a_spec = pl.BlockSpec((tk, tm), lambda i, j, k: (k, i)) if ta else pl.BlockSpec((tm, tk), lambda i, j, k: (i, k))
    b_spec = pl.BlockSpec((tn, tk), lambda i, j, k: (j, k)) if tb else pl.BlockSpec((tk, tn), lambda i, j, k: (k, j))
    in_specs, args = [a_spec, b_spec], [a, b]
    if bias is not None:
        in_specs.append(pl.BlockSpec((1, tn), lambda i, j, k: (0, j)))
        args.append(bias)
    scratch = [pltpu.VMEM((tm, tn), F32)] if nk > 1 else []
    return _call(name, body, (M // tm, N // tn, nk), in_specs, pl.BlockSpec((tm, tn), lambda i, j, k: (i, j)),
                 _sds((M, N), out_dtype), scratch=scratch, sem=("parallel", "parallel", "arbitrary"), ride=ride)(*args)


def _ew(name, fn, ins, out_dtypes, S, C, rows=(), tr=256, tc=512):
    g = C
    for _, off in ins:
        g = math.gcd(g, off) if off else g
    tc = _pick(g, tc, LANES)
    tr = _pick(S, tr, 16)
    n_in, n_row, n_out = len(ins), len(rows), len(out_dtypes)

    def body(*refs):
        vals = [r[...].astype(F32) for r in refs[:n_in + n_row]]
        for o_ref, o in zip(refs[n_in + n_row:], fn(*vals)):
            o_ref[...] = o.astype(o_ref.dtype)

    in_specs = [_T(tr, tc, off // tc) for _, off in ins] + [_P(1, tc) for _ in rows]
    res = _call(name, body, (C // tc, S // tr), in_specs, [_T(tr, tc) for _ in out_dtypes],
                [_sds((S, C), d) for d in out_dtypes], sem=("parallel", "parallel"))(*[a for a, _ in ins], *rows)
    return res


def _adaln_fwd(x, norm, scale, shift, *, name, t=None, gate=None):
    S, D = x.shape
    tr = _pick(S, 256, 16)
    resid = t is not None

    def body(*refs):
        if resid:
            x_ref, t_ref, g_ref, n_ref, sc_ref, sh_ref, u_ref, h_ref = refs
            h = x_ref[...] + g_ref[...] * t_ref[...]
            h_ref[...] = h
        else:
            x_ref, n_ref, sc_ref, sh_ref, u_ref = refs
            h = x_ref[...]
        r = lax.rsqrt(jnp.mean(h * h, axis=-1, keepdims=True) + EPS)
        u_ref[...] = (h * r * (n_ref[...] * (1.0 + sc_ref[...])) + sh_ref[...]).astype(BF16)

    full, row = _T(tr, D), _P(1, D)
    if resid:
        return _call(name, body, (1, S // tr), [full, full, row, row, row, row], [full, full],
                     [_sds((S, D), BF16), _sds((S, D), F32)])(x, t, gate, norm, scale, shift)
    return _call(name, body, (1, S // tr), [full, row, row, row], full, _sds((S, D), BF16))(x, norm, scale, shift)


def _adaln_bwd(h, du, dres, norm, scale, *, name, t=None, gate=None):
    S, D = h.shape
    tr = _pick(S, 256, 16)
    gated = t is not None

    def body(*refs):
        if gated:
            h_ref, du_ref, dr_ref, n_ref, sc_ref, t_ref, g_ref, dh_ref, dsh_ref, dsc_ref, dn_ref, dt_ref, dg_ref = refs
        else:
            h_ref, du_ref, dr_ref, n_ref, sc_ref, dh_ref, dsh_ref, dsc_ref, dn_ref = refs
        first = pl.program_id(1) == 0
        hv, duv = h_ref[...], du_ref[...]
        r = lax.rsqrt(jnp.mean(hv * hv, axis=-1, keepdims=True) + EPS)
        xn = hv * r
        one_sc = 1.0 + sc_ref[...]
        dxn = duv * (n_ref[...] * one_sc)
        dh = dr_ref[...] + r * (dxn - xn * jnp.mean(dxn * xn, axis=-1, keepdims=True))
        dh_ref[...] = dh
        dux = duv * xn
        _accum(dsh_ref, _colsum(duv), first)
        _accum(dsc_ref, _colsum(dux * n_ref[...]), first)
        _accum(dn_ref, _colsum(dux * one_sc), first)
        if gated:
            dt_ref[...] = (dh * g_ref[...]).astype(BF16)
            _accum(dg_ref, _colsum(dh * t_ref[...]), first)

    full, row = _T(tr, D), _P(1, D)
    rowo = _sds((1, D), F32)
    if gated:
        return _call(name, body, (1, S // tr), [full, full, full, row, row, full, row],
                     [full, row, row, row, full, row],
                     [_sds((S, D), F32), rowo, rowo, rowo, _sds((S, D), BF16), rowo])(h, du, dres, norm, scale, t, gate)
    return _call(name, body, (1, S // tr), [full, full, full, row, row], [full, row, row, row],
                 [_sds((S, D), F32), rowo, rowo, rowo])(h, du, dres, norm, scale)


def _final(h1, t2, gate2, norm_f, tgt, *, name):
    S, D = h1.shape
    tr = _pick(S, 256, 16)

    def body(h_ref, t_ref, g_ref, n_ref, y_ref, dh_ref, dt_ref, loss_ref, dn_ref, dg_ref):
        first = pl.program_id(1) == 0
        tv = t_ref[...]
        h2 = h_ref[...] + g_ref[...] * tv
        r = lax.rsqrt(jnp.mean(h2 * h2, axis=-1, keepdims=True) + EPS)
        xn = h2 * r
        e = xn * n_ref[...] - y_ref[...]
        part = 0.5 * jnp.sum(jnp.mean(e * e, axis=-1, keepdims=True), axis=0, keepdims=True)
        _accum(loss_ref, jnp.broadcast_to(part, (SUBLANES, LANES)), first)
        dy = e * (1.0 / D)
        _accum(dn_ref, _colsum(dy * xn), first)
        dxn = dy * n_ref[...]
        dh2 = r * (dxn - xn * jnp.mean(dxn * xn, axis=-1, keepdims=True))
        dh_ref[...] = dh2
        dt_ref[...] = (dh2 * g_ref[...]).astype(BF16)
        _accum(dg_ref, _colsum(dh2 * tv), first)

    full, row = _T(tr, D), _P(1, D)
    rowo = _sds((1, D), F32)
    return _call(name, body, (1, S // tr), [full, full, row, row, full],
                 [full, full, _P(SUBLANES, LANES), row, row],
                 [_sds((S, D), F32), _sds((S, D), BF16), _sds((SUBLANES, LANES), F32), rowo, rowo])(h1, t2, gate2, norm_f, tgt)


HALO = 16


def _halo_prev(tr, tc, off=0):
    return pl.BlockSpec((HALO, tc), lambda j, i: (jnp.maximum(i * (tr // HALO) - 1, 0), j + off))


def _halo_next(tr, tc, n_slabs, off=0):
    return pl.BlockSpec((HALO, tc), lambda j, i: (jnp.minimum((i + 1) * (tr // HALO), n_slabs - 1), j + off))


def _fill_past(buf, prev_ref, cur_ref, first, tr):
    buf[0:HALO, :] = jnp.where(first, 0.0, prev_ref[...].astype(F32))
    buf[HALO:HALO + tr, :] = cur_ref[...].astype(F32)


def _fill_future(buf, cur_ref, next_ref, last, tr):
    buf[0:tr, :] = cur_ref[...].astype(F32)
    buf[tr:tr + HALO, :] = jnp.where(last, 0.0, next_ref[...].astype(F32))


def _al8(r):
    return r if isinstance(r, int) else pl.multiple_of(r, SUBLANES)


def _past_rows(buf, r0, taps):
    xe = buf[pl.ds(_al8(r0 + HALO - SUBLANES), CHUNK + SUBLANES), :]
    return [_shift_down(xe, d, CHUNK) for d in range(taps)]


def _future_rows(buf, r0, taps):
    xe = buf[pl.ds(_al8(r0), CHUNK + SUBLANES), :]
    return [_shift_up(xe, d, CHUNK) for d in range(taps)]


def _taps(w_ref):
    return [w_ref[k:k + 1, :] for k in range(w_ref.shape[0])]


def _conv(xs, w, b):
    taps = len(xs)
    y = w[taps - 1] * xs[0] if b is None else b + w[taps - 1] * xs[0]
    for d in range(1, taps):
        y = y + w[taps - 1 - d] * xs[d]
    return y


def _conv_fwd(x, off, C, w, b, *, name):
    S = x.shape[0]
    taps = w.shape[0]
    tc = _pick(math.gcd(C, off) if off else C, 512, LANES)
    tr = _pick(S, 256, 16)

    def body(x_ref, p_ref, w_ref, b_ref, y_ref, buf):
        _fill_past(buf, p_ref, x_ref, pl.program_id(1) == 0, tr)
        wt, bv = _taps(w_ref), b_ref[...]

        def step(r0):
            y_ref[pl.ds(r0, CHUNK), :] = _conv(_past_rows(buf, r0, taps), wt, bv)

        _chunks(tr, CHUNK, step)

    return _call(name, body, (C // tc, S // tr),
                 [_T(tr, tc, off // tc), _halo_prev(tr, tc, off // tc), _P(taps, tc), _P(1, tc)],
                 _T(tr, tc), _sds((S, C), F32), scratch=[pltpu.VMEM((tr + HALO, tc), F32)],
                 sem=("parallel", "parallel"))(x, x, w, b)


def _finish_conv_grads(acc, base, taps, dwb_ref, first):
    @pl.when(first)
    def _():
        dwb_ref[...] = jnp.zeros_like(dwb_ref)

    for d in range(taps):
        dwb_ref[taps - 1 - d:taps - d, :] += _colsum(acc[base + SUBLANES * d:base + SUBLANES * (d + 1), :])
    dwb_ref[taps:taps + 1, :] += _colsum(acc[base + SUBLANES * taps:base + SUBLANES * (taps + 1), :])


def _add_conv_grads(acc, base, dy, xs):
    taps = len(xs)
    for d in range(taps):
        acc[base + SUBLANES * d:base + SUBLANES * (d + 1), :] += _fold(dy * xs[d])
    acc[base + SUBLANES * taps:base + SUBLANES * (taps + 1), :] += _fold(dy)


def _conv_bwd(dy, x, off, C, w, *, name):
    S = dy.shape[0]
    taps = w.shape[0]
    assert taps < SUBLANES
    tc = _pick(math.gcd(C, off) if off else C, 512, LANES)
    tr = _pick(S, 256, 16)
    n_slabs = S // HALO

    def body(dy_ref, nx_ref, x_ref, p_ref, w_ref, dx_ref, dwb_ref, fbuf, pbuf, acc):
        i = pl.program_id(1)
        first = i == 0
        _fill_future(fbuf, dy_ref, nx_ref, i == S // tr - 1, tr)
        _fill_past(pbuf, p_ref, x_ref, first, tr)
        acc[...] = jnp.zeros_like(acc)
        wt = _taps(w_ref)

        def step(r0):
            dys = _future_rows(fbuf, r0, taps)
            dx_ref[pl.ds(r0, CHUNK), :] = _conv(dys, wt, None).astype(BF16)
            _add_conv_grads(acc, 0, dys[0], _past_rows(pbuf, r0, taps))

        _chunks(tr, CHUNK, step)
        _finish_conv_grads(acc, 0, taps, dwb_ref, first)

    return _call(name, body, (C // tc, S // tr),
                 [_T(tr, tc), _halo_next(tr, tc, n_slabs), _T(tr, tc, off // tc), _halo_prev(tr, tc, off // tc),
                  _P(taps, tc)],
                 [_T(tr, tc), _P(SUBLANES, tc)],
                 [_sds((S, C), BF16), _sds((SUBLANES, C), F32)],
                 scratch=[pltpu.VMEM((tr + HALO, tc), F32), pltpu.VMEM((tr + HALO, tc), F32),
                          pltpu.VMEM((SUBLANES * (taps + 1), tc), F32)])(dy, dy, x, x, w)


def _geglu_fwd(up, w, b, F, *, name):
    S = up.shape[0]
    taps = w.shape[0]
    tc = _pick(F, 512, LANES)
    tr = _pick(S, 256, 16)
    nf = F // tc

    def body(g_ref, gp_ref, v_ref, vp_ref, wg_ref, wv_ref, bg_ref, bv_ref, a_ref, gbuf, vbuf):
        first = pl.program_id(1) == 0
        _fill_past(gbuf, gp_ref, g_ref, first, tr)
        _fill_past(vbuf, vp_ref, v_ref, first, tr)
        wg, wv, bg, bv = _taps(wg_ref), _taps(wv_ref), bg_ref[...], bv_ref[...]

        def step(r0):
            g = _conv(_past_rows(gbuf, r0, taps), wg, bg)
            v = _conv(_past_rows(vbuf, r0, taps), wv, bv)
            a_ref[pl.ds(r0, CHUNK), :] = (_gelu(g) * v).astype(BF16)

        _chunks(tr, CHUNK, step)

    buf = pltpu.VMEM((tr + HALO, tc), F32)
    return _call(name, body, (nf, S // tr),
                 [_T(tr, tc), _halo_prev(tr, tc), _T(tr, tc, nf), _halo_prev(tr, tc, nf),
                  _P(taps, tc), _P(taps, tc, nf), _P(1, tc), _P(1, tc, nf)],
                 _T(tr, tc), _sds((S, F), BF16), scratch=[buf, buf], sem=("parallel", "parallel"))(up, up, up, up, w, w, b, b)


def _ffn_bwd(up, w, b, da, F, *, name):
    S = up.shape[0]
    taps = w.shape[0]
    tc = _pick(F, 512, LANES)
    tr = _pick(S, 256, 16)
    nf, nr = F // tc, S // tr
    ext = tr + HALO
    n_slabs = S // HALO

    assert HALO == CHUNK
    v_base = SUBLANES * (taps + 1)

    def body(g_ref, gp_ref, gn_ref, v_ref, vp_ref, vn_ref, wg_ref, wv_ref, bg_ref, bv_ref, da_ref, dan_ref,
             dxg_ref, dxv_ref, dwbg_ref, dwbv_ref, gbuf, vbuf, dgbuf, dvbuf, acc):
        i = pl.program_id(1)
        first, last = i == 0, i == nr - 1
        _fill_past(gbuf, gp_ref, g_ref, first, tr)
        _fill_past(vbuf, vp_ref, v_ref, first, tr)
        gbuf[HALO + tr:HALO + ext, :] = gn_ref[...].astype(F32)
        vbuf[HALO + tr:HALO + ext, :] = vn_ref[...].astype(F32)
        acc[...] = jnp.zeros_like(acc)
        wg, wv, bg, bv = _taps(wg_ref), _taps(wv_ref), bg_ref[...], bv_ref[...]

        def grads(r0, da, in_tile):
            gs, vs = _past_rows(gbuf, r0, taps), _past_rows(vbuf, r0, taps)
            ge, dge = _gelu_and_grad(_conv(gs, wg, bg))
            dg = da * _conv(vs, wv, bv) * dge
            dv = da * ge
            dgbuf[pl.ds(_al8(r0), CHUNK), :] = dg
            dvbuf[pl.ds(_al8(r0), CHUNK), :] = dv
            if in_tile:
                _add_conv_grads(acc, 0, dg, gs)
                _add_conv_grads(acc, v_base, dv, vs)

        _chunks(tr, CHUNK, lambda r0: grads(r0, da_ref[pl.ds(r0, CHUNK), :], True))
        grads(tr, jnp.where(last, 0.0, dan_ref[...]), False)

        def back(r0):
            dxg_ref[pl.ds(r0, CHUNK), :] = _conv(_future_rows(dgbuf, r0, taps), wg, None).astype(BF16)
            dxv_ref[pl.ds(r0, CHUNK), :] = _conv(_future_rows(dvbuf, r0, taps), wv, None).astype(BF16)

        _chunks(tr, CHUNK, back)
        _finish_conv_grads(acc, 0, taps, dwbg_ref, first)
        _finish_conv_grads(acc, v_base, taps, dwbv_ref, first)

    xbuf = pltpu.VMEM((HALO + ext, tc), F32)
    dbuf = pltpu.VMEM((ext, tc), F32)
    t, tw, tb, t8 = _T(tr, tc), _P(taps, tc), _P(1, tc), _P(SUBLANES, tc)
    return _call(name, body, (nf, nr),
                 [t, _halo_prev(tr, tc), _halo_next(tr, tc, n_slabs), _T(tr, tc, nf), _halo_prev(tr, tc, nf),
                  _halo_next(tr, tc, n_slabs, nf), tw, _P(taps, tc, nf), tb, _P(1, tc, nf), t, _halo_next(tr, tc, n_slabs)],
                 [t, t, t8, t8],
                 [_sds((S, F), BF16), _sds((S, F), BF16), _sds((SUBLANES, F), F32), _sds((SUBLANES, F), F32)],
                 scratch=[xbuf, xbuf, dbuf, dbuf, pltpu.VMEM((2 * v_base, tc), F32)])(up, up, up, up, up, up, w, w, b, b, da, da)


def _softplus_neg(lam):
    z = -lam
    e = jnp.exp(-jnp.abs(z))
    u = 1.0 + e
    log1p_e = jnp.where(u == 1.0, e, jnp.log(u) * e / jnp.where(u == 1.0, 1.0, u - 1.0))
    sp = jnp.maximum(z, 0.0) + log1p_e
    sg = jnp.where(z >= 0, 1.0 / u, e / u)
    return sp, sg


def _one_minus_exp(x):
    series = -x * (1.0 + x * (0.5 + x * (1.0 / 6.0 + x * (1.0 / 24.0))))
    return jnp.where(x > -0.01, series, 1.0 - jnp.exp(x))


def _gate_values(xc, wa_ref, wi_ref, ba_ref, bi_ref, lam_ref, is_t0):
    xb = xc.astype(BF16)
    ra = _sigmoid(jnp.dot(xb, wa_ref[0].astype(BF16), preferred_element_type=F32) + ba_ref[...])
    ia = _sigmoid(jnp.dot(xb, wi_ref[0].astype(BF16), preferred_element_type=F32) + bi_ref[...])
    sp, sg = _softplus_neg(lam_ref[...])
    log_a = -RG_C * ra * sp
    a = jnp.exp(log_a)
    mult = jnp.where(is_t0, 1.0, jnp.sqrt(_one_minus_exp(2.0 * log_a)))
    return ra, ia, sp, sg, a, mult


def _rnn_blockspecs(tr):
    x = pl.BlockSpec((tr, LANES), lambda n, i: (i, n))
    w = pl.BlockSpec((1, LANES, LANES), lambda n, i: (n, 0, 0))
    p = pl.BlockSpec((1, LANES), lambda n, i: (0, n))
    return x, w, p


def _is_t0(tr):
    rows = lax.broadcasted_iota(jnp.int32, (tr, LANES), 0)
    return jnp.logical_and(pl.program_id(1) == 0, rows == 0)


def _gates_fwd(xc, w_a, b_a, w_i, b_i, lam, *, name):
    S, C = xc.shape
    tr = _pick(S, 1024, 16)

    def body(x_ref, wa_ref, wi_ref, ba_ref, bi_ref, lam_ref, a_ref, inp_ref):
        xv = x_ref[...]
        _, ia, _, _, a, mult = _gate_values(xv, wa_ref, wi_ref, ba_ref, bi_ref, lam_ref, _is_t0(tr))
        a_ref[...] = a
        inp_ref[...] = mult * ia * xv

    x, w, p = _rnn_blockspecs(tr)
    return _call(name, body, (C // LANES, S // tr), [x, w, w, p, p, p], [x, x],
                 [_sds((S, C), F32), _sds((S, C), F32)], sem=("parallel", "parallel"))(xc, w_a, w_i, b_a, b_i, lam)


def _gates_bwd(xc, dacc, hprev, w_a, b_a, w_i, b_i, lam, *, name, ride=None):
    S, C = xc.shape
    nb = C // LANES
    tr = _pick(S, 1024, 16)

    def body(x_ref, d_ref, hp_ref, wa_ref, wi_ref, ba_ref, bi_ref, lam_ref,
             dx_ref, dwa_ref, dwi_ref, dba_ref, dbi_ref, dlam_ref):
        first = pl.program_id(1) == 0
        t0 = _is_t0(tr)
        xv, dv = x_ref[...], d_ref[...]
        ra, ia, sp, sg, a, mult = _gate_values(xv, wa_ref, wi_ref, ba_ref, bi_ref, lam_ref, t0)
        d_a = dv * hp_ref[...]
        d_mult = dv * ia * xv
        d_ia = dv * mult * xv
        d_log = d_a * a + jnp.where(t0, 0.0, -d_mult * (a * a) / mult)
        d_pa = d_log * (-RG_C * sp) * ra * (1.0 - ra)
        d_pi = d_ia * ia * (1.0 - ia)
        xb, dab, dib = xv.astype(BF16), d_pa.astype(BF16), d_pi.astype(BF16)
        nt = (((1,), (1,)), ((), ()))
        tn = (((0,), (0,)), ((), ()))
        dx_ref[...] = (dv * mult * ia
                       + lax.dot_general(dab, wa_ref[0].astype(BF16), nt, preferred_element_type=F32)
                       + lax.dot_general(dib, wi_ref[0].astype(BF16), nt, preferred_element_type=F32))
        _accum(dwa_ref, lax.dot_general(xb, dab, tn, preferred_element_type=F32)[None], first)
        _accum(dwi_ref, lax.dot_general(xb, dib, tn, preferred_element_type=F32)[None], first)
        _accum(dba_ref, _colsum(d_pa), first)
        _accum(dbi_ref, _colsum(d_pi), first)
        _accum(dlam_ref, _colsum(d_log * (-RG_C) * ra) * (-sg), first)

    x, w, p = _rnn_blockspecs(tr)
    row = _sds((1, C), F32)
    return _call(name, body, (nb, S // tr), [x, x, x, w, w, p, p, p], [x, w, w, p, p, p],
                 [_sds((S, C), F32), _sds((nb, LANES, LANES), F32), _sds((nb, LANES, LANES), F32), row, row, row],
                 ride=ride)(xc, dacc, hprev, w_a, w_i, b_a, b_i, lam)


def _scan_fwd(a, inp, *, name):
    S, C = a.shape
    tc = _pick(C, 1280, LANES)
    tr = _pick(S, 256, SUBLANES)

    def body(a_ref, b_ref, h_ref, hp_ref, carry):
        @pl.when(pl.program_id(1) == 0)
        def _():
            carry[...] = jnp.zeros_like(carry)

        rows = lax.broadcasted_iota(jnp.int32, (SUBLANES, tc), 0)

        def slab(s, h):
            base = pl.multiple_of(s * SUBLANES, SUBLANES)
            av, bv = a_ref[pl.ds(base, SUBLANES), :], b_ref[pl.ds(base, SUBLANES), :]
            ho = jnp.zeros((SUBLANES, tc), F32)
            po = jnp.zeros((SUBLANES, tc), F32)
            for r in range(SUBLANES):
                po = jnp.where(rows == r, h, po)
                h = av[r:r + 1, :] * h + bv[r:r + 1, :]
                ho = jnp.where(rows == r, h, ho)
            h_ref[pl.ds(base, SUBLANES), :] = ho
            hp_ref[pl.ds(base, SUBLANES), :] = po
            return h

        carry[...] = lax.fori_loop(0, tr // SUBLANES, slab, carry[...])

    t = _T(tr, tc)
    return _call(name, body, (C // tc, S // tr), [t, t], [t, t], [_sds((S, C), F32), _sds((S, C), F32)],
                 scratch=[pltpu.VMEM((1, tc), F32)])(a, inp)


def _scan_bwd(a, dh, *, name):
    S, C = a.shape
    tc = _pick(C, 1280, LANES)
    tr = _pick(S, 256, SUBLANES)
    nr = S // tr

    def body(a_ref, d_ref, o_ref, carry):
        @pl.when(pl.program_id(1) == 0)
        def _():
            carry[...] = jnp.zeros_like(carry)

        rows = lax.broadcasted_iota(jnp.int32, (SUBLANES, tc), 0)
        n_slabs = tr // SUBLANES

        def slab(s, g):
            base = pl.multiple_of((n_slabs - 1 - s) * SUBLANES, SUBLANES)
            av, dv = a_ref[pl.ds(base, SUBLANES), :], d_ref[pl.ds(base, SUBLANES), :]
            out = jnp.zeros((SUBLANES, tc), F32)
            for r in range(SUBLANES - 1, -1, -1):
                acc = dv[r:r + 1, :] + g
                out = jnp.where(rows == r, acc, out)
                g = av[r:r + 1, :] * acc
            o_ref[pl.ds(base, SUBLANES), :] = out
            return g

        carry[...] = lax.fori_loop(0, n_slabs, slab, carry[...])

    t = pl.BlockSpec((tr, tc), lambda j, i: (nr - 1 - i, j))
    return _call(name, body, (C // tc, nr), [t, t], t, _sds((S, C), F32), scratch=[pltpu.VMEM((1, tc), F32)])(a, dh)


SCALE = HEAD_DIM ** -0.5


def _bucket_table(num_buckets):
    qi = np.arange(BLOCK)[:, None]
    j = np.arange(BLOCK)[None, :]
    dist = np.where(j <= qi, qi - j, qi + BLOCK - j)
    d = np.maximum(dist, 1).astype(np.float64)
    large = NUM_EXACT + (np.log(d / NUM_EXACT) / math.log(MAX_DISTANCE / NUM_EXACT) * (num_buckets - NUM_EXACT)).astype(np.int32)
    large = np.minimum(large, num_buckets - 1)
    return np.where(dist < NUM_EXACT, dist, large).astype(np.int32)


def _bias_table(rel_bias, n_heads, *, name):
    nbk = rel_bias.shape[0]
    bucket = jnp.asarray(_bucket_table(nbk))

    def body(rb_ref, bk_ref, o_ref):
        t, h = pl.program_id(0), pl.program_id(1)
        bk = bk_ref[...]
        acc = jnp.zeros((BLOCK, BLOCK), F32)
        for b in range(nbk):
            acc = jnp.where(bk == b, rb_ref[b, h], acc)
        o_ref[0, 0] = jnp.where(jnp.logical_and(t == 0, jnp.logical_not(_own_block())), NEG_INF, acc)

    return _call(name, body, (2, n_heads),
                 [pl.BlockSpec(memory_space=pltpu.SMEM), pl.BlockSpec((BLOCK, BLOCK), lambda t, h: (0, 0))],
                 pl.BlockSpec((1, 1, BLOCK, BLOCK), lambda t, h: (t, h, 0, 0)),
                 _sds((2, n_heads, BLOCK, BLOCK), F32), sem=("arbitrary", "arbitrary"))(rel_bias, bucket)


def _bias_table_bwd(dbias, nbk, *, name):
    n_heads = dbias.shape[0]
    bucket = jnp.asarray(_bucket_table(nbk))

    def body(db_ref, bk_ref, o_ref):
        h = pl.program_id(0)
        bk = bk_ref[...]
        dv = db_ref[0]
        rows = lax.broadcasted_iota(jnp.int32, (nbk, LANES), 0)
        lanes = lax.broadcasted_iota(jnp.int32, (nbk, LANES), 1)
        acc = jnp.zeros((nbk, LANES), F32)
        for b in range(nbk):
            s = jnp.sum(jnp.sum(jnp.where(bk == b, dv, 0.0), axis=0, keepdims=True), axis=1, keepdims=True)
            acc = jnp.where(jnp.logical_and(rows == b, lanes == h), s, acc)
        _accum(o_ref, acc, h == 0)

    return _call(name, body, (n_heads,),
                 [pl.BlockSpec((1, BLOCK, BLOCK), lambda h: (h, 0, 0)), pl.BlockSpec((BLOCK, BLOCK), lambda h: (0, 0))],
                 pl.BlockSpec((nbk, LANES), lambda h: (0, 0)), _sds((nbk, LANES), F32), sem=("arbitrary",))(dbias, bucket)


def _own_block(heads=1):
    qi = lax.broadcasted_iota(jnp.int32, (heads * BLOCK, BLOCK), 0) % BLOCK
    j = lax.broadcasted_iota(jnp.int32, (heads * BLOCK, BLOCK), 1)
    return j <= qi


def _to_window(band, own):
    return jnp.where(own, band[:, BLOCK:], band[:, :BLOCK])


def _to_band(win, own):
    return jnp.concatenate([jnp.where(own, 0.0, win), jnp.where(own, win, 0.0)], axis=1)


def _stack_pairs(ref, group):
    return jnp.concatenate([ref[:, (g // 2) * LANES:(g // 2 + 1) * LANES] for g in range(group)], axis=0).astype(F32)


def _own_lanes(group):
    rows = lax.broadcasted_iota(jnp.int32, (group * BLOCK, LANES), 0)
    lanes = lax.broadcasted_iota(jnp.int32, (group * BLOCK, LANES), 1)
    return (lanes < HEAD_DIM) == ((rows // BLOCK) % 2 == 0)


def _unstack_pairs(stacked, o_ref, group):
    lo = lax.broadcasted_iota(jnp.int32, (BLOCK, LANES), 1) < HEAD_DIM
    for p in range(group // 2):
        even = stacked[2 * p * BLOCK:(2 * p + 1) * BLOCK, :]
        odd = stacked[(2 * p + 1) * BLOCK:(2 * p + 2) * BLOCK, :]
        o_ref[:, p * LANES:(p + 1) * LANES] = jnp.where(lo, even, odd).astype(o_ref.dtype)


def _group_probs(qm, kb, b_ref, s_ref, own, group):
    band = lax.dot_general(qm, kb, (((1,), (1,)), ((), ())), preferred_element_type=F32)
    bias = jnp.concatenate([b_ref[0, g] for g in range(group)], axis=0)
    sink = jnp.concatenate([jnp.broadcast_to(s_ref[g][:, 0:1], (BLOCK, 1)) for g in range(group)], axis=0)
    s = _to_window(band, own) + bias
    m = jnp.maximum(jnp.max(s, axis=-1, keepdims=True), sink)
    p = jnp.exp(s - m)
    es = jnp.exp(sink - m)
    inv = 1.0 / (jnp.sum(p, axis=-1, keepdims=True) + es)
    return p, inv, es


def _attn_fwd(proj, bias, sinks_b, n_kv, group, q_off, k_off, v_off, *, name, ride=None):
    S = proj.shape[0]
    nblk = S // BLOCK
    gw = group * HEAD_DIM
    pairs = group // 2

    def body(q_ref, kc_ref, kp_ref, vc_ref, vp_ref, b_ref, s_ref, o_ref):
        kb = jnp.concatenate([kp_ref[...], kc_ref[...]], axis=0).astype(BF16)
        vb = jnp.concatenate([vp_ref[...], vc_ref[...]], axis=0).astype(BF16)
        own = _own_block(group)
        qm = jnp.where(_own_lanes(group), _stack_pairs(q_ref, group) * SCALE, 0.0).astype(BF16)
        pu, inv, _ = _group_probs(qm, kb, b_ref, s_ref, own, group)
        o = jnp.dot(_to_band(pu, own).astype(BF16), vb, preferred_element_type=F32) * inv
        _unstack_pairs(o, o_ref, group)

    qb, kb0, vb0 = q_off // gw, k_off // LANES, v_off // LANES
    cur = lambda b0: pl.BlockSpec((BLOCK, LANES), lambda j, n: (n, b0 + j))
    prev = lambda b0: pl.BlockSpec((BLOCK, LANES), lambda j, n: (jnp.maximum(n - 1, 0), b0 + j))
    return _call(name, body, (n_kv, nblk),
                 [pl.BlockSpec((BLOCK, gw), lambda j, n: (n, qb + j)), cur(kb0), prev(kb0), cur(vb0), prev(vb0),
                  pl.BlockSpec((1, group, BLOCK, BLOCK), lambda j, n: (jnp.minimum(n, 1), j, 0, 0)),
                  pl.BlockSpec((group, 1, LANES), lambda j, n: (j, 0, 0))],
                 pl.BlockSpec((BLOCK, gw), lambda j, n: (n, j)), _sds((S, n_kv * gw), BF16),
                 sem=("parallel", "parallel"), ride=ride)(proj, proj, proj, proj, proj, bias, sinks_b)


def _attn_bwd(proj, o, do, bias, sinks_b, n_kv, group, q_off, k_off, v_off, *, name, ride=None):
    S = proj.shape[0]
    nblk = S // BLOCK
    gw = group * HEAD_DIM
    rows_all = group * BLOCK
    nt = (((1,), (1,)), ((), ()))
    tn = (((0,), (0,)), ((), ()))

    def body(q_ref, kc_ref, kp_ref, vc_ref, vp_ref, b_ref, s_ref, o_ref, do_ref,
             dq_ref, dk_ref, dv_ref, db_ref, ds_ref, ck, cv, sacc):
        n = pl.program_id(1)
        lo2 = lax.broadcasted_iota(jnp.int32, (2 * BLOCK, LANES), 1) < HEAD_DIM

        @pl.when(n == 0)
        def _():
            ck[...] = jnp.zeros_like(ck)
            cv[...] = jnp.zeros_like(cv)
            sacc[...] = jnp.zeros_like(sacc)
            db_ref[...] = jnp.zeros_like(db_ref)

        @pl.when(n < nblk)
        def _():
            kb = jnp.concatenate([kp_ref[...], kc_ref[...]], axis=0).astype(BF16)
            vb = jnp.concatenate([vp_ref[...], vc_ref[...]], axis=0).astype(BF16)
            own = _own_block(group)
            mine = _own_lanes(group)
            qm = jnp.where(mine, _stack_pairs(q_ref, group) * SCALE, 0.0).astype(BF16)
            dom = jnp.where(mine, _stack_pairs(do_ref, group), 0.0)
            pu, inv, es = _group_probs(qm, kb, b_ref, s_ref, own, group)
            pr = pu * inv
            delta = jnp.sum(dom * _stack_pairs(o_ref, group), axis=-1, keepdims=True)
            domb = dom.astype(BF16)
            dp = _to_window(lax.dot_general(domb, vb, nt, preferred_element_type=F32), own)
            ds = pr * (dp - delta)
            for g in range(group):
                db_ref[g] += ds[g * BLOCK:(g + 1) * BLOCK, :]
            sacc[...] += -(es * inv) * delta
            dsb = _to_band(ds, own).astype(BF16)
            _unstack_pairs(jnp.dot(dsb, kb, preferred_element_type=F32) * SCALE, dq_ref, group)
            dk_acc = lax.dot_general(dsb, qm, tn, preferred_element_type=F32)
            dv_acc = lax.dot_general(_to_band(pr, own).astype(BF16), domb, tn, preferred_element_type=F32)
            dkf = jnp.where(lo2, dk_acc + pltpu.roll(dk_acc, HEAD_DIM, 1), 0.0)
            dvf = jnp.where(lo2, dv_acc + pltpu.roll(dv_acc, HEAD_DIM, 1), 0.0)

            @pl.when(n > 0)
            def _():
                dk_ref[...] = (ck[...] + dkf[0:BLOCK, :]).astype(BF16)
                dv_ref[...] = (cv[...] + dvf[0:BLOCK, :]).astype(BF16)

            ck[...] = dkf[BLOCK:2 * BLOCK, :]
            cv[...] = dvf[BLOCK:2 * BLOCK, :]

        @pl.when(n == nblk)
        def _():
            dk_ref[...] = ck[...].astype(BF16)
            dv_ref[...] = cv[...].astype(BF16)
            lane = lax.broadcasted_iota(jnp.int32, (1, LANES), 1)
            tot = jnp.zeros((1, LANES), F32)
            for g in range(group):
                tot = jnp.where(lane == g, _colsum(sacc[g * BLOCK:(g + 1) * BLOCK, :]), tot)
            ds_ref[0] = tot

    qb, kb0, vb0 = q_off // gw, k_off // LANES, v_off // LANES
    last = nblk - 1
    cur = lambda b0: pl.BlockSpec((BLOCK, LANES), lambda j, n: (jnp.minimum(n, last), b0 + j))
    prev = lambda b0: pl.BlockSpec((BLOCK, LANES), lambda j, n: (jnp.clip(n - 1, 0, last), b0 + j))
    qspec = lambda b0: pl.BlockSpec((BLOCK, gw), lambda j, n: (jnp.minimum(n, last), b0 + j))
    kvout = pl.BlockSpec((BLOCK, LANES), lambda j, n: (jnp.maximum(n - 1, 0), j))
    return _call(name, body, (n_kv, nblk + 1),
                 [qspec(qb), cur(kb0), prev(kb0), cur(vb0), prev(vb0),
                  pl.BlockSpec((1, group, BLOCK, BLOCK), lambda j, n: (jnp.minimum(n, 1), j, 0, 0)),
                  pl.BlockSpec((group, 1, LANES), lambda j, n: (j, 0, 0)), qspec(0), qspec(0)],
                 [qspec(0), kvout, kvout, pl.BlockSpec((group, BLOCK, BLOCK), lambda j, n: (j, 0, 0)),
                  pl.BlockSpec((1, 1, LANES), lambda j, n: (j, 0, 0))],
                 [_sds((S, n_kv * gw), BF16), _sds((S, n_kv * LANES), BF16), _sds((S, n_kv * LANES), BF16),
                  _sds((n_kv * group, BLOCK, BLOCK), F32), _sds((n_kv, 1, LANES), F32)],
                 scratch=[pltpu.VMEM((BLOCK, LANES), F32), pltpu.VMEM((BLOCK, LANES), F32),
                          pltpu.VMEM((rows_all, 1), F32)], ride=ride)(proj, proj, proj, proj, proj, bias, sinks_b, o, do)


def _adamw(w, g, m, v, *, name):
    R, C = w.shape
    tr = _pick(R, 128, SUBLANES)
    bc1 = 1.0 - ADAM_B1 ** ADAM_STEP
    bc2 = 1.0 - ADAM_B2 ** ADAM_STEP

    def body(w_ref, g_ref, m_ref, v_ref, d_ref, nm_ref, nv_ref):
        gv = g_ref[...]
        nm = ADAM_B1 * m_ref[...] + (1.0 - ADAM_B1) * gv
        nv = ADAM_B2 * v_ref[...] + (1.0 - ADAM_B2) * (gv * gv)
        d_ref[...] = -ADAM_LR * ((nm / bc1) / (jnp.sqrt(nv / bc2) + ADAM_EPS) + ADAM_WD * w_ref[...])
        nm_ref[...] = nm
        nv_ref[...] = nv

    t = pl.BlockSpec((tr, C), lambda i: (i, 0))
    o = _sds((R, C), F32)
    return _call(name, body, (R // tr,), [t, t, t, t], [t, t, t], [o, o, o], sem=("parallel",))(w, g, m, v)


def _sum_devices(packs, *, name):
    _, R, C = packs.shape
    tr = _pick(R, 512, SUBLANES)

    def body(p_ref, o_ref):
        acc = p_ref[0]
        for d in range(1, N_DEV):
            acc = acc + p_ref[d]
        o_ref[...] = acc

    return _call(name, body, (R // tr,), [pl.BlockSpec((N_DEV, tr, C), lambda i: (0, i, 0))],
                 pl.BlockSpec((tr, C), lambda i: (i, 0)), _sds((R, C), F32), sem=("parallel",))(packs)


class _Big:
    def __init__(self, kind, R, C):
        self.kind, self.R, self.C = kind, R, C
        self.hr = R // 2
        self.full = (R, N_CHIPS * C) if kind == "col" else (N_CHIPS * R, C)

    def region(self, ref, k, c):
        if self.kind == "col":
            return ref.at[pl.ds(c * self.hr, self.hr), pl.ds(k * self.C, self.C)]
        return ref.at[pl.ds(k * self.R + c * self.hr, self.hr), :]

    def shard(self, ref, k):
        if self.kind == "col":
            return ref.at[:, pl.ds(k * self.C, self.C)]
        return ref.at[pl.ds(k * self.R, self.R), :]


def _pair_sum(spec, g_full, land, c_arr, *, name):
    hr, C = spec.hr, spec.C
    tr = _pick(hr, 128, 16)
    nr = hr // tr

    def body(c_ref, g_ref, l_ref, o_ref):
        o_ref[0] = (g_ref[...] + l_ref[0]).astype(BF16)

    if spec.kind == "col":
        gspec = pl.BlockSpec((tr, C), lambda k, i, c_ref: (c_ref[0] * nr + i, k))
    else:
        gspec = pl.BlockSpec((tr, C), lambda k, i, c_ref: (k * 2 * nr + c_ref[0] * nr + i, 0))
    lspec = pl.BlockSpec((1, tr, C), lambda k, i, c_ref: (k, i, 0))
    return _call(name, body, (N_CHIPS, nr), [gspec, lspec], lspec, _sds((N_CHIPS, hr, C), BF16), nsp=1,
                 sem=("parallel", "parallel"))(c_arr, g_full, land)


def _chip_sum(spec, chipsum, land, kc_arr, *, name):
    hr, C = spec.hr, spec.C
    tr = _pick(hr, 128, 16)
    nr = hr // tr

    def body(kc_ref, s_ref, l_ref, o_ref):
        acc = s_ref[0].astype(F32)
        for j in range(N_CHIPS - 1):
            acc = acc + l_ref[j].astype(F32)
        o_ref[...] = acc

    return _call(name, body, (nr,),
                 [pl.BlockSpec((1, tr, C), lambda i, kc_ref: (kc_ref[0], i, 0)),
                  pl.BlockSpec((N_CHIPS - 1, tr, C), lambda i, kc_ref: (0, i, 0))],
                 pl.BlockSpec((tr, C), lambda i, kc_ref: (kc_ref[1] * nr + i, 0)), _sds((spec.R, C), F32), nsp=1,
                 sem=("parallel",))(kc_arr, chipsum, land)


def _place():
    x, y, c = lax.axis_index("x"), lax.axis_index("y"), lax.axis_index("c")
    return x, y, c, [(1 - x, y), (x, 1 - y), (1 - x, 1 - y)]


def _remote(src, dst, send_sem, recv_sem, dev):
    return pltpu.make_async_remote_copy(src_ref=src, dst_ref=dst, send_sem=send_sem, recv_sem=recv_sem,
                                        device_id=dev, device_id_type=MESH)


def _comm_call(name, body, n, out_shapes, n_remote, in_place=True):
    scratch = [pltpu.SemaphoreType.DMA((n_remote,)), pltpu.SemaphoreType.DMA((n_remote,))]
    aliases = {i: i for i in range(n)} if in_place else {}
    return pl.pallas_call(body, name=name, out_shape=out_shapes, in_specs=[ANY] * n, out_specs=[ANY] * n,
                          scratch_shapes=scratch, input_output_aliases=aliases)


def _placed(block, dev):
    m, n = block.shape
    return lax.dynamic_update_slice(jnp.zeros((N_DEV * m, n), block.dtype), block, (dev * m, 0))


def _all_gather(bufs):
    n = len(bufs)
    per = 7

    def copier(outs, send_sems, recv_sems, w):
        m = outs[w].shape[0] // N_DEV

        def rows(px, py, pc):
            return outs[w].at[pl.ds((4 * px + 2 * py + pc) * m, m), :]

        def copy(k, block, to):
            return _remote(rows(*block), rows(*block), send_sems.at[w * per + k], recv_sems.at[w * per + k], to)

        return copy

    def own(outs, send_sems, recv_sems, w):
        x, y, c, chips = _place()
        copy = copier(outs, send_sems, recv_sems, w)
        return [copy(0, (x, y, c), (x, y, 1 - c))] + [copy(1 + j, (x, y, c), (*chip, c)) for j, chip in enumerate(chips)]

    def start(ins, outs, send_sems, recv_sems):
        for w in range(n):
            for cp in own(outs, send_sems, recv_sems, w):
                cp.start()

    def finish(ins, outs, send_sems, recv_sems):
        x, y, c, chips = _place()
        me, sibling = (x, y, c), (x, y, 1 - c)
        passed = []
        for w in range(n):
            copy = copier(outs, send_sems, recv_sems, w)
            for j, chip in enumerate(chips):
                copy(1 + j, (*chip, c), me).wait_recv()
                cp = copy(4 + j, (*chip, c), sibling)
                cp.start()
                passed.append(cp)
        for w in range(n):
            copy = copier(outs, send_sems, recv_sems, w)
            copy(0, sibling, me).wait_recv()
            for j, chip in enumerate(chips):
                copy(4 + j, (*chip, 1 - c), me).wait_recv()
            for cp in own(outs, send_sems, recv_sems, w):
                cp.wait_send()
        for cp in passed:
            cp.wait_send()

    return _Exchange(bufs, [_sds(a.shape, a.dtype) for a in bufs], per * n, start, finish, in_place=True)


def _cast_into(spec, w, k_arr, *, name):
    R, C = spec.R, spec.C
    tr = _pick(R, 256, 16)
    nr = R // tr

    def body(k_ref, w_ref, o_ref):
        o_ref[...] = w_ref[...].astype(BF16)

    if spec.kind == "col":
        ospec = pl.BlockSpec((tr, C), lambda i, k_ref: (i, k_ref[0]))
    else:
        ospec = pl.BlockSpec((tr, C), lambda i, k_ref: (k_ref[0] * nr + i, 0))
    return _call(name, body, (nr,), [pl.BlockSpec((tr, C), lambda i, k_ref: (i, 0))], ospec,
                 _sds(spec.full, BF16), nsp=1, sem=("parallel",))(k_arr, w)


def _weight_gather(bufs, specs):
    per = 6

    def to_chips(outs, send_sems, recv_sems):
        x, y, c, chips = _place()
        k_me = 2 * x + y
        cps = []
        for w, sp in enumerate(specs):
            mine = sp.region(outs[w], k_me, c)
            cps += [_remote(mine, mine, send_sems.at[w * per + j], recv_sems.at[w * per + j], (*chip, c))
                    for j, chip in enumerate(chips)]
        return cps

    def start(ins, outs, send_sems, recv_sems):
        for cp in to_chips(outs, send_sems, recv_sems):
            cp.start()

    def finish(ins, outs, send_sems, recv_sems):
        x, y, c, chips = _place()
        sibling = (x, y, 1 - c)
        passed = []
        for w, sp in enumerate(specs):
            for j, chip in enumerate(chips):
                got = sp.region(outs[w], 2 * chip[0] + chip[1], c)
                _remote(got, got, send_sems.at[w * per + j], recv_sems.at[w * per + j], (*chip, c)).wait_recv()
                cp = _remote(got, got, send_sems.at[w * per + 3 + j], recv_sems.at[w * per + 3 + j], sibling)
                cp.start()
                passed.append(cp)
        for w, sp in enumerate(specs):
            for j, chip in enumerate(chips):
                got = sp.region(outs[w], 2 * chip[0] + chip[1], 1 - c)
                _remote(got, got, send_sems.at[w * per + 3 + j], recv_sems.at[w * per + 3 + j], sibling).wait_recv()
        for cp in passed + to_chips(outs, send_sems, recv_sems):
            cp.wait_send()

    return _Exchange(bufs, [_sds(sp.full, BF16) for sp in specs], per * len(specs), start, finish, in_place=True)


def _pair_exchange(grads, specs):
    def copies(ins, outs, send_sems, recv_sems):
        x, y, c, _ = _place()
        return [_remote(sp.region(ins[w], k, 1 - c), outs[w].at[k], send_sems.at[w * N_CHIPS + k],
                        recv_sems.at[w * N_CHIPS + k], (x, y, 1 - c))
                for w, sp in enumerate(specs) for k in range(N_CHIPS)]

    def start(ins, outs, send_sems, recv_sems):
        for cp in copies(ins, outs, send_sems, recv_sems):
            cp.start()

    def finish(ins, outs, send_sems, recv_sems):
        for cp in copies(ins, outs, send_sems, recv_sems):
            cp.wait()

    return _Exchange(grads, [_sds((N_CHIPS, sp.hr, sp.C), F32) for sp in specs], N_CHIPS * len(specs), start, finish,
                     in_place=False)


def _chip_exchange(chipsums, specs):
    per = N_CHIPS - 1

    def copies(ins, outs, send_sems, recv_sems):
        x, y, c, chips = _place()
        return [_remote(ins[w].at[2 * chip[0] + chip[1]], outs[w].at[j], send_sems.at[w * per + j],
                        recv_sems.at[w * per + j], (*chip, c))
                for w in range(len(specs)) for j, chip in enumerate(chips)]

    def start(ins, outs, send_sems, recv_sems):
        for cp in copies(ins, outs, send_sems, recv_sems):
            cp.start()

    def finish(ins, outs, send_sems, recv_sems):
        for cp in copies(ins, outs, send_sems, recv_sems):
            cp.wait()

    return _Exchange(chipsums, [_sds((per, sp.hr, sp.C), BF16) for sp in specs], per * len(specs), start, finish,
                     in_place=False)


def _pair_share(bufs, specs, *, name):
    n = len(bufs)

    def body(*refs):
        outs = refs[n:2 * n]
        send_sems, recv_sems = refs[2 * n:]
        x, y, c, _ = _place()
        sibling = (x, y, 1 - c)
        cps = []
        for w, sp in enumerate(specs):
            mine = outs[w].at[pl.ds(c * sp.hr, sp.hr), :]
            cp = _remote(mine, mine, send_sems.at[w], recv_sems.at[w], sibling)
            cp.start()
            cps.append(cp)
        for w, sp in enumerate(specs):
            theirs = outs[w].at[pl.ds((1 - c) * sp.hr, sp.hr), :]
            _remote(theirs, theirs, send_sems.at[w], recv_sems.at[w], sibling).wait_recv()
        for cp in cps:
            cp.wait_send()

    outs = [_sds((sp.R, sp.C), F32) for sp in specs]
    return _comm_call(name, body, n, outs, n)(*bufs)


PACK_ALIGN = SUBLANES * LANES


def _pack(arrs):
    flat = jnp.concatenate([a.reshape(-1) for a in arrs])
    pad = (-flat.shape[0]) % PACK_ALIGN
    return jnp.pad(flat, (0, pad)).reshape(-1, LANES)


def _unpack(packed, shapes):
    flat = packed.reshape(-1)
    out, pos = [], 0
    for s in shapes:
        n = int(np.prod(s))
        out.append(flat[pos:pos + n].reshape(s))
        pos += n
    return out


def kernel(x, c, w_ada, b_ada, norm1, w_in, rnn_conv_w, rnn_conv_b, w_rg_a, b_rg_a, w_rg_i, b_rg_i, rg_lambda, w_o_rnn, w_o_attn, attn_sinks, rel_bias, w_out, norm2, w_up, ffn_conv_w, ffn_conv_b, w_down, norm_f, loss_target, m_w_ada, m_b_ada, m_norm1, m_w_in, m_rnn_conv_w, m_rnn_conv_b, m_w_rg_a, m_b_rg_a, m_w_rg_i, m_b_rg_i, m_rg_lambda, m_w_o_rnn, m_w_o_attn, m_attn_sinks, m_rel_bias, m_w_out, m_norm2, m_w_up, m_ffn_conv_w, m_ffn_conv_b, m_w_down, m_norm_f, v_w_ada, v_b_ada, v_norm1, v_w_in, v_rnn_conv_w, v_rnn_conv_b, v_w_rg_a, v_b_rg_a, v_w_rg_i, v_b_rg_i, v_rg_lambda, v_w_o_rnn, v_w_o_attn, v_attn_sinks, v_rel_bias, v_w_out, v_norm2, v_w_up, v_ffn_conv_w, v_ffn_conv_b, v_w_down, v_norm_f):
    S, D = x.shape[1], x.shape[2]
    d_attn = N_CHIPS * w_o_attn.shape[1]
    d_rnn = N_CHIPS * w_o_rnn.shape[1]
    d_ff = N_CHIPS * w_down.shape[1]
    d_in = N_CHIPS * w_in.shape[2]
    n_heads = attn_sinks.shape[1]
    d_kv = (d_in - d_attn - 2 * d_rnn - 2 * D) // 2
    n_kv = d_kv // HEAD_DIM
    group = n_heads // n_kv
    nbk = rel_bias.shape[0]
    assert d_attn == n_heads * HEAD_DIM and group % 2 == 0 and S % BLOCK == 0

    mx, my, mc = lax.axis_index("x"), lax.axis_index("y"), lax.axis_index("c")
    k_me = 2 * mx + my
    dev = 2 * k_me + mc
    c_arr = jnp.reshape(mc, (1,)).astype(jnp.int32)
    k_arr = jnp.reshape(k_me, (1,)).astype(jnp.int32)
    kc_arr = jnp.stack([k_me, mc]).astype(jnp.int32)

    xs, tgt = x[0], loss_target[0]

    big_names = ["w_in", "w_o_rnn", "w_o_attn", "w_out", "w_up", "w_down"]
    big_w = dict(w_in=w_in[0], w_o_rnn=w_o_rnn[0], w_o_attn=w_o_attn[0], w_out=w_out[0], w_up=w_up[0], w_down=w_down[0])
    big_kind = dict(w_in="col", w_o_rnn="row", w_o_attn="row", w_out="row", w_up="col", w_down="row")
    specs = {k: _Big(big_kind[k], *big_w[k].shape) for k in big_names}
    placed = {k: _cast_into(specs[k], big_w[k], k_arr, name="cast_" + k) for k in big_names}
    later = big_names[1:]
    W = dict(w_in=_weight_gather([placed["w_in"]], [specs["w_in"]]).alone("gather_w_in")[0])
    c_all, cw4, cw3 = _all_gather([_placed(jnp.broadcast_to(c, (SUBLANES, D)), dev),
                                   _placed(jnp.pad(rnn_conv_w[0], ((0, SUBLANES - rnn_conv_w.shape[1]), (0, 0))), dev),
                                   _placed(jnp.pad(ffn_conv_w[0], ((0, SUBLANES - ffn_conv_w.shape[1]), (0, 0))), dev)]
                                  ).alone("gather_cond")
    c_all = c_all.reshape(N_DEV, SUBLANES, D)[:, 0]

    def from_chips(g, taps):
        cs = g.shape[1]
        g = g.reshape(N_CHIPS, 2, SUBLANES, cs)[:, 0, :taps]
        return jnp.transpose(g, (1, 0, 2)).reshape(taps, N_CHIPS * cs)

    conv4_w = from_chips(cw4, rnn_conv_w.shape[1])
    conv3_w = from_chips(cw3, ffn_conv_w.shape[1])

    (silu_c,) = _ew("silu_c", lambda v: (v * _sigmoid(v),), [(c_all, 0)], [F32], N_DEV, D)
    mod_sh = _mm(silu_c, w_ada[0], name="mod", exact=True, bias=lax.dynamic_slice_in_dim(b_ada, k_me * w_ada.shape[2], w_ada.shape[2], 1))
    (mod_g,) = _all_gather([_placed(mod_sh, dev)]).alone("gather_mod")
    mod_all = jnp.transpose(mod_g.reshape(N_CHIPS, 2, N_DEV, -1)[:, 0], (1, 0, 2)).reshape(N_DEV, 6 * D)
    mod = lax.dynamic_slice_in_dim(mod_all, dev, 1, 0)
    shift1, scale1, gate1, shift2, scale2, gate2 = [mod[:, i * D:(i + 1) * D] for i in range(6)]

    o_k, o_v, o_xr = d_attn, d_attn + d_kv, d_attn + 2 * d_kv
    wi = W["w_in"]

    spread = np.zeros((d_kv, n_kv * LANES), np.float32)
    for col in range(d_kv):
        spread[col, (col // HEAD_DIM) * LANES + col % HEAD_DIM] = 1.0
        spread[col, (col // HEAD_DIM) * LANES + HEAD_DIM + col % HEAD_DIM] = 1.0
    w_in_x = jnp.concatenate([wi[:, :o_k],
                              _mm(wi[:, o_k:o_v], jnp.asarray(spread, BF16), name="spread_k", out_dtype=BF16),
                              _mm(wi[:, o_v:o_xr], jnp.asarray(spread, BF16), name="spread_v", out_dtype=BF16),
                              wi[:, o_xr:]], axis=1)
    e_k = d_attn
    e_v = e_k + n_kv * LANES
    e_xr = e_v + n_kv * LANES
    e_gr = e_xr + d_rnn
    e_ga = e_gr + d_rnn
    e_gl = e_ga + D
    d_ext = e_gl + D

    u = _adaln_fwd(xs, norm1, scale1, shift1, name="adaln1")
    proj = _mm(u, w_in_x, name="proj", out_dtype=BF16)
    bias = _bias_table(rel_bias, n_heads, name="bias_table")
    sinks_b = jnp.broadcast_to(attn_sinks.reshape(n_heads, 1, 1), (n_heads, 1, LANES))
    o_attn, gathered = _attn_fwd(proj, bias, sinks_b, n_kv, group, 0, e_k, e_v, name="attn_fwd",
                                 ride=_weight_gather([placed[k] for k in later], [specs[k] for k in later]))
    W.update(zip(later, gathered))
    y_attn = _mm(o_attn, W["w_o_attn"], name="y_attn")
    xc = _conv_fwd(proj, e_xr, d_rnn, conv4_w, rnn_conv_b, name="conv4")
    a_t, inp = _gates_fwd(xc, w_rg_a[0], b_rg_a, w_rg_i[0], b_rg_i, rg_lambda, name="gates")
    h_rnn, h_prev = _scan_fwd(a_t, inp, name="scan")
    (z,) = _ew("rnn_gate", lambda h, g: (h * _gelu(g),), [(h_rnn, 0), (proj, e_gr)], [BF16], S, d_rnn)
    y_rnn = _mm(z, W["w_o_rnn"], name="y_rnn")
    (merged,) = _ew("merge", lambda ya, yr, ga, gl: (_sigmoid(ga) * ya + _sigmoid(gl) * yr,),
                    [(y_attn, 0), (y_rnn, 0), (proj, e_ga), (proj, e_gl)], [BF16], S, D)
    t1 = _mm(merged, W["w_out"], name="t1")
    u2, h1 = _adaln_fwd(xs, norm2, scale2, shift2, name="adaln2", t=t1, gate=gate1)
    up = _mm(u2, W["w_up"], name="up", out_dtype=BF16)
    a2 = _geglu_fwd(up, conv3_w, ffn_conv_b, d_ff, name="geglu")
    t2 = _mm(a2, W["w_down"], name="t2")
    dh2, dt2, loss_tile, g_norm_f, d_gate2 = _final(h1, t2, gate2, norm_f.reshape(1, D), tgt, name="final")

    da2 = _mm(dt2, W["w_down"], name="da2", tb=True)
    g_w_down = _mm(a2, dt2, name="g_w_down", ta=True)
    dupg, dupv, g_c3g, g_c3v = _ffn_bwd(up, conv3_w, ffn_conv_b, da2, d_ff, name="ffn_bwd")
    dup = jnp.concatenate([dupg, dupv], axis=1)
    g_conv3 = jnp.concatenate([g_c3g, g_c3v], axis=1)
    du2 = _mm(dup, W["w_up"], name="du2", tb=True)
    g_w_up = _mm(u2, dup, name="g_w_up", ta=True)
    dh1, d_shift2, d_scale2, g_norm2, dt1, d_gate1 = _adaln_bwd(h1, du2, dh2, norm2, scale2, name="adaln2_bwd", t=t1, gate=gate1)

    dmerged = _mm(dt1, W["w_out"], name="dmerged", tb=True)
    g_w_out = _mm(merged, dt1, name="g_w_out", ta=True)

    def merge_bwd(dm, ya, yr, ga, gl):
        sa, sl = _sigmoid(ga), _sigmoid(gl)
        return dm * sa, dm * sl, dm * ya * sa * (1.0 - sa), dm * yr * sl * (1.0 - sl)

    dy_attn, dy_rnn, d_ga, d_gl = _ew("merge_bwd", merge_bwd, [(dmerged, 0), (y_attn, 0), (y_rnn, 0), (proj, e_ga), (proj, e_gl)],
                                      [BF16, BF16, BF16, BF16], S, D)
    do = _mm(dy_attn, W["w_o_attn"], name="do", tb=True, out_dtype=BF16)
    g_w_o_attn = _mm(o_attn, dy_attn, name="g_w_o_attn", ta=True)
    dz = _mm(dy_rnn, W["w_o_rnn"], name="dz", tb=True)
    g_w_o_rnn = _mm(z, dy_rnn, name="g_w_o_rnn", ta=True)

    def rnn_gate_bwd(dzv, h, g):
        ge, dge = _gelu_and_grad(g)
        return dzv * ge, dzv * h * dge

    dh_rnn, d_gr = _ew("rnn_gate_bwd", rnn_gate_bwd, [(dz, 0), (h_rnn, 0), (proj, e_gr)], [F32, BF16], S, d_rnn)
    dacc = _scan_bwd(a_t, dh_rnn, name="scan_bwd")
    big_g = dict(w_o_rnn=g_w_o_rnn, w_o_attn=g_w_o_attn, w_out=g_w_out, w_up=g_w_up, w_down=g_w_down)
    (dxc, g_w_rg_a, g_w_rg_i, g_b_rg_a, g_b_rg_i, g_lam), landed = _gates_bwd(
        xc, dacc, h_prev, w_rg_a[0], b_rg_a, w_rg_i[0], b_rg_i, rg_lambda, name="gates_bwd",
        ride=_pair_exchange([big_g[k] for k in later], [specs[k] for k in later]))
    d_xr, g_conv4 = _conv_bwd(dxc, proj, e_xr, d_rnn, conv4_w, name="conv4_bwd")
    chipsums = {k: _pair_sum(specs[k], big_g[k], l, c_arr, name="pair_sum_" + k) for k, l in zip(later, landed)}
    (dq, dk, dv, dbias, dsink), landed2 = _attn_bwd(
        proj, o_attn, do, bias, sinks_b, n_kv, group, 0, e_k, e_v, name="attn_bwd",
        ride=_chip_exchange([chipsums[k] for k in later], [specs[k] for k in later]))
    landed2 = dict(zip(later, landed2))
    g_rel = _bias_table_bwd(dbias, nbk, name="bias_table_bwd")[:, :n_heads]
    g_sinks = dsink[:, 0, :group].reshape(1, n_heads)
    dproj = jnp.concatenate([dq, dk, dv, d_xr, d_gr, d_ga, d_gl], axis=1)

    early = [loss_tile[0:1, 0:1], g_conv4, g_b_rg_a, g_b_rg_i, g_lam, g_sinks, g_rel, g_norm2, g_conv3, g_norm_f,
             d_gate1, d_shift2, d_scale2, d_gate2]
    early_shapes = [a.shape for a in early]
    rg_rows = g_w_rg_a.shape[0] * LANES
    g_w_in_x, (early_all, rg_a_all, rg_i_all) = _mm(
        u, dproj, name="g_w_in", ta=True,
        ride=_all_gather([_placed(_pack(early), dev), _placed(g_w_rg_a.reshape(rg_rows, LANES), dev),
                          _placed(g_w_rg_i.reshape(rg_rows, LANES), dev)]))

    gather_m = jnp.asarray(spread.T, F32)
    g_w_in = jnp.concatenate([g_w_in_x[:, :e_k],
                              _mm(g_w_in_x[:, e_k:e_v], gather_m, name="gather_k", exact=True),
                              _mm(g_w_in_x[:, e_v:e_xr], gather_m, name="gather_v", exact=True),
                              g_w_in_x[:, e_xr:]], axis=1)
    big_g["w_in"] = g_w_in
    (landed_in,) = _pair_exchange([g_w_in], [specs["w_in"]]).alone("pair_exchange_w_in")
    chipsums["w_in"] = _pair_sum(specs["w_in"], g_w_in, landed_in, c_arr, name="pair_sum_w_in")
    du, (landed2["w_in"],) = _mm(dproj, w_in_x, name="du", tb=True, ride=_chip_exchange([chipsums["w_in"]], [specs["w_in"]]))
    grad_x, d_shift1, d_scale1, g_norm1 = _adaln_bwd(xs, du, dh1, norm1, scale1, name="adaln1_bwd")

    late = [d_shift1, d_scale1, g_norm1]
    late_shapes = [a.shape for a in late]
    (late_all,) = _all_gather([_placed(_pack(late), dev)]).alone("gather_late")
    early_all = early_all.reshape(N_DEV, -1, LANES)
    late_all = late_all.reshape(N_DEV, -1, LANES)
    (loss_s, g_conv4, g_b_rg_a, g_b_rg_i, g_lam, g_sinks, g_rel, g_norm2, g_conv3, g_norm_f,
     s_gate1, s_shift2, s_scale2, s_gate2) = _unpack(_sum_devices(early_all, name="sum_early"), early_shapes)
    s_shift1, s_scale1, g_norm1 = _unpack(_sum_devices(late_all, name="sum_late"), late_shapes)
    g_w_rg_a = _sum_devices(rg_a_all.reshape(N_DEV, rg_rows, LANES), name="sum_w_rg_a")
    g_w_rg_i = _sum_devices(rg_i_all.reshape(N_DEV, rg_rows, LANES), name="sum_w_rg_i")
    loss = loss_s.reshape(())
    g_b_ada = jnp.concatenate([s_shift1, s_scale1, s_gate1, s_shift2, s_scale2, s_gate2], axis=1)
    taps4, taps3 = rnn_conv_w.shape[1], ffn_conv_w.shape[1]
    g_conv4_w, g_conv4_b = g_conv4[:taps4], g_conv4[taps4:taps4 + 1]
    g_conv3_w, g_conv3_b = g_conv3[:taps3], g_conv3[taps3:taps3 + 1]

    def per_device(all_packs, shapes, pick):
        flat = all_packs.reshape(N_DEV, -1)
        offs = np.cumsum([0] + [int(np.prod(s)) for s in shapes])
        return [flat[:, offs[i]:offs[i + 1]] for i in pick]

    dmod_all = jnp.concatenate(per_device(late_all, late_shapes, [0, 1]) + per_device(early_all, early_shapes, [10, 11, 12, 13]),
                               axis=1)
    cs_ada = w_ada.shape[2]
    g_w_ada = _mm(silu_c, lax.dynamic_slice_in_dim(dmod_all, k_me * cs_ada, cs_ada, 1), name="g_w_ada", ta=True, exact=True)
    cs4, cs3 = rnn_conv_w.shape[2], ffn_conv_w.shape[2]
    g_conv4_sh = lax.dynamic_slice_in_dim(g_conv4_w, k_me * cs4, cs4, 1)
    g_conv3_sh = lax.dynamic_slice_in_dim(g_conv3_w, k_me * cs3, cs3, 1)

    sp_list = [specs[k] for k in big_names]
    halves = [_chip_sum(specs[k], chipsums[k], landed2[k], kc_arr, name="chip_sum_" + k) for k in big_names]
    shards = dict(zip(big_names, _pair_share(halves, sp_list, name="pair_share")))

    grads = dict(w_ada=g_w_ada[None], b_ada=g_b_ada, norm1=g_norm1, w_in=shards["w_in"][None], rnn_conv_w=g_conv4_sh[None],
                 rnn_conv_b=g_conv4_b, w_rg_a=g_w_rg_a[None], b_rg_a=g_b_rg_a, w_rg_i=g_w_rg_i[None], b_rg_i=g_b_rg_i,
                 rg_lambda=g_lam, w_o_rnn=shards["w_o_rnn"][None], w_o_attn=shards["w_o_attn"][None], attn_sinks=g_sinks,
                 rel_bias=g_rel, w_out=shards["w_out"][None], norm2=g_norm2, w_up=shards["w_up"][None],
                 ffn_conv_w=g_conv3_sh[None], ffn_conv_b=g_conv3_b, w_down=shards["w_down"][None], norm_f=g_norm_f.reshape(D))
    weights = dict(w_ada=w_ada, b_ada=b_ada, norm1=norm1, w_in=w_in, rnn_conv_w=rnn_conv_w, rnn_conv_b=rnn_conv_b, w_rg_a=w_rg_a,
                   b_rg_a=b_rg_a, w_rg_i=w_rg_i, b_rg_i=b_rg_i, rg_lambda=rg_lambda, w_o_rnn=w_o_rnn, w_o_attn=w_o_attn,
                   attn_sinks=attn_sinks, rel_bias=rel_bias, w_out=w_out, norm2=norm2, w_up=w_up, ffn_conv_w=ffn_conv_w,
                   ffn_conv_b=ffn_conv_b, w_down=w_down, norm_f=norm_f)
    moms = dict(w_ada=(m_w_ada, v_w_ada), b_ada=(m_b_ada, v_b_ada), norm1=(m_norm1, v_norm1), w_in=(m_w_in, v_w_in),
                rnn_conv_w=(m_rnn_conv_w, v_rnn_conv_w), rnn_conv_b=(m_rnn_conv_b, v_rnn_conv_b), w_rg_a=(m_w_rg_a, v_w_rg_a),
                b_rg_a=(m_b_rg_a, v_b_rg_a), w_rg_i=(m_w_rg_i, v_w_rg_i), b_rg_i=(m_b_rg_i, v_b_rg_i),
                rg_lambda=(m_rg_lambda, v_rg_lambda), w_o_rnn=(m_w_o_rnn, v_w_o_rnn), w_o_attn=(m_w_o_attn, v_w_o_attn),
                attn_sinks=(m_attn_sinks, v_attn_sinks), rel_bias=(m_rel_bias, v_rel_bias), w_out=(m_w_out, v_w_out),
                norm2=(m_norm2, v_norm2), w_up=(m_w_up, v_w_up), ffn_conv_w=(m_ffn_conv_w, v_ffn_conv_w),
                ffn_conv_b=(m_ffn_conv_b, v_ffn_conv_b), w_down=(m_w_down, v_w_down), norm_f=(m_norm_f, v_norm_f))
    names = list(weights)
    grads = {k: grads[k].reshape(weights[k].shape) for k in names}
    large = ["w_ada"] + big_names + ["w_rg_a", "w_rg_i"]
    delta, new_m, new_v = {}, {}, {}
    for k in large:
        shp = weights[k].shape
        two = lambda a: a.reshape(-1, shp[-1])
        d_, m_, v_ = _adamw(two(weights[k]), two(grads[k]), two(moms[k][0]), two(moms[k][1]), name="adamw_" + k)
        delta[k], new_m[k], new_v[k] = d_.reshape(shp), m_.reshape(shp), v_.reshape(shp)
    rest = [k for k in names if k not in large]
    rest_shapes = [weights[k].shape for k in rest]
    d_, m_, v_ = _adamw(_pack([weights[k] for k in rest]), _pack([grads[k] for k in rest]),
                        _pack([moms[k][0] for k in rest]), _pack([moms[k][1] for k in rest]), name="adamw_small")
    for k, dd, mm_, vv in zip(rest, _unpack(d_, rest_shapes), _unpack(m_, rest_shapes), _unpack(v_, rest_shapes)):
        delta[k], new_m[k], new_v[k] = dd, mm_, vv

    return (loss, grad_x[None], *[grads[k] for k in names], *[delta[k] for k in names],
            *[new_m[k] for k in names], *[new_v[k] for k in names])
```

```python
import math

import numpy as np
import jax
import jax.numpy as jnp
from jax import lax
from jax.experimental import pallas as pl
from jax.experimental.pallas import tpu as pltpu

F32 = jnp.float32
BF16 = jnp.bfloat16
MESH = pl.DeviceIdType.MESH
ANY = pl.BlockSpec(memory_space=pl.ANY)

EPS = 1e-6
NEG_INF = -1e30
HEAD_DIM = 64
BLOCK = 128
NUM_EXACT = 16
MAX_DISTANCE = 128
RG_C = 8.0
ADAM_LR, ADAM_B1, ADAM_B2, ADAM_EPS, ADAM_WD, ADAM_STEP = 0.001, 0.9, 0.999, 1e-08, 0.01, 10
N_CHIPS = 4
N_DEV = 8
SUBLANES = 8
LANES = 128
VMEM_LIMIT_BYTES = 48 * 1024 * 1024
GELU_C0 = math.sqrt(2.0 / math.pi)
GELU_C1 = 0.044715


def _pick(dim, pref, align):
    if dim <= pref:
        return dim
    t = (pref // align) * align
    while t >= align:
        if dim % t == 0:
            return t
        t -= align
    return dim


def _params(sem):
    return pltpu.CompilerParams(dimension_semantics=sem, vmem_limit_bytes=VMEM_LIMIT_BYTES)


def _call(name, body, grid, in_specs, out_specs, out_shape, scratch=(), nsp=0, sem=None, ride=None):
    if ride is not None:
        return _call_with_ride(name, body, grid, in_specs, out_specs, out_shape, scratch, ride)
    sem = sem or ("parallel",) * (len(grid) - 1) + ("arbitrary",)
    if nsp:
        gs = pltpu.PrefetchScalarGridSpec(num_scalar_prefetch=nsp, grid=grid, in_specs=in_specs,
                                          out_specs=out_specs, scratch_shapes=list(scratch))
        return pl.pallas_call(body, name=name, grid_spec=gs, out_shape=out_shape, compiler_params=_params(sem))
    return pl.pallas_call(body, name=name, grid=grid, in_specs=in_specs, out_specs=out_specs,
                          out_shape=out_shape, scratch_shapes=list(scratch), compiler_params=_params(sem))


def _sds(shape, dtype):
    return jax.ShapeDtypeStruct(shape, dtype)


class _Exchange:
    def __init__(self, ins, outs, n_sems, start, finish, in_place):
        self.ins, self.outs, self.n_sems = list(ins), list(outs), n_sems
        self.start, self.finish, self.in_place = start, finish, in_place

    def alone(self, name):
        n_in, n_out = len(self.ins), len(self.outs)

        def body(*refs):
            ins, outs = refs[:n_in], refs[n_in:n_in + n_out]
            send_sems, recv_sems = refs[n_in + n_out:]
            self.start(ins, outs, send_sems, recv_sems)
            self.finish(ins, outs, send_sems, recv_sems)

        sems = [pltpu.SemaphoreType.DMA((self.n_sems,)), pltpu.SemaphoreType.DMA((self.n_sems,))]
        aliases = {i: i for i in range(n_in)} if self.in_place else {}
        return pl.pallas_call(body, name=name, out_shape=self.outs, in_specs=[ANY] * n_in, out_specs=[ANY] * n_out,
                              scratch_shapes=sems, input_output_aliases=aliases)(*self.ins)


def _call_with_ride(name, body, grid, in_specs, out_specs, out_shape, scratch, ride):
    single = not isinstance(out_specs, (list, tuple))
    out_specs = [out_specs] if single else list(out_specs)
    out_shape = [out_shape] if single else list(out_shape)
    n_in, n_out, n_sc = len(in_specs), len(out_specs), len(scratch)
    r_in, r_out = len(ride.ins), len(ride.outs)

    def wrapped(*refs):
        pos = 0
        parts = []
        for n in (n_in, r_in, n_out, r_out, n_sc, 2):
            parts.append(refs[pos:pos + n])
            pos += n
        core_in, ride_in, core_out, ride_out, core_scratch, (send_sems, recv_sems) = parts
        first = pl.program_id(0) == 0
        last = pl.program_id(0) == grid[0] - 1
        for axis in range(1, len(grid)):
            first = jnp.logical_and(first, pl.program_id(axis) == 0)
            last = jnp.logical_and(last, pl.program_id(axis) == grid[axis] - 1)

        @pl.when(first)
        def _():
            ride.start(ride_in, ride_out, send_sems, recv_sems)

        body(*core_in, *core_out, *core_scratch)

        @pl.when(last)
        def _():
            ride.finish(ride_in, ride_out, send_sems, recv_sems)

    sems = [pltpu.SemaphoreType.DMA((ride.n_sems,)), pltpu.SemaphoreType.DMA((ride.n_sems,))]
    aliases = {n_in + k: n_out + k for k in range(r_in)} if ride.in_place else {}
    call = pl.pallas_call(wrapped, name=name, grid=grid, in_specs=list(in_specs) + [ANY] * r_in,
                          out_specs=out_specs + [ANY] * r_out, out_shape=out_shape + ride.outs,
                          scratch_shapes=list(scratch) + sems, input_output_aliases=aliases,
                          compiler_params=_params(("arbitrary",) * len(grid)))

    def run(*args):
        res = call(*args, *ride.ins)
        core = res[0] if single else list(res[:n_out])
        return core, list(res[n_out:])

    return run


def _T(tr, tc, off=0):
    return pl.BlockSpec((tr, tc), lambda j, i: (i, j + off))


def _P(rows, tc, off=0):
    return pl.BlockSpec((rows, tc), lambda j, i: (0, j + off))


def _gelu(x):
    t = jnp.tanh(GELU_C0 * (x + GELU_C1 * x * x * x))
    return 0.5 * x * (1.0 + t)


def _gelu_and_grad(x):
    t = jnp.tanh(GELU_C0 * (x + GELU_C1 * x * x * x))
    g = 0.5 * x * (1.0 + t)
    dg = 0.5 * (1.0 + t) + 0.5 * x * (1.0 - t * t) * GELU_C0 * (1.0 + 3.0 * GELU_C1 * x * x)
    return g, dg


def _sigmoid(x):
    return 1.0 / (1.0 + jnp.exp(-x))


def _accum(ref, val, first):
    @pl.when(first)
    def _():
        ref[...] = val

    @pl.when(jnp.logical_not(first))
    def _():
        ref[...] += val


def _colsum(v):
    return jnp.sum(v, axis=0, keepdims=True)


CHUNK = 16


def _chunks(n_rows, ch, step):
    def it(i, carry):
        step(pl.multiple_of(i * ch, ch))
        return carry

    lax.fori_loop(0, n_rows // ch, it, 0)


def _fold(v):
    return v[0:SUBLANES, :] + v[SUBLANES:2 * SUBLANES, :]


def _shift_down(xe, d, ch):
    return xe[SUBLANES:SUBLANES + ch, :] if d == 0 else pltpu.roll(xe, d, 0)[SUBLANES:SUBLANES + ch, :]


def _shift_up(xe, d, ch):
    return xe[0:ch, :] if d == 0 else pltpu.roll(xe, ch + SUBLANES - d, 0)[0:ch, :]


def _mm(a, b, *, name, ta=False, tb=False, out_dtype=F32, tm=1024, tn=1024, tk=2048, exact=False, bias=None, ride=None):
    if ta:
        K, M = a.shape
    else:
        M, K = a.shape
    if tb:
        N, K2 = b.shape
    else:
        K2, N = b.shape
    assert K == K2, (a.shape, b.shape, ta, tb)
    tm, tn, tk = _pick(M, tm, LANES), _pick(N, tn, LANES), _pick(K, tk, LANES)
    nk = K // tk
    cdt = F32 if exact else BF16
    prec = lax.Precision.HIGHEST if exact else None
    dims = (((0 if ta else 1,), (1 if tb else 0,)), ((), ()))

    def body(*refs):
        a_ref, b_ref = refs[0], refs[1]
        bias_ref = refs[2] if bias is not None else None
        o_ref = refs[3] if bias is not None else refs[2]
        part = lax.dot_general(a_ref[...].astype(cdt), b_ref[...].astype(cdt), dims,
                               preferred_element_type=F32, precision=prec)

        def finish(r):
            if bias is not None:
                r = r + bias_ref[...]
            o_ref[...] = r.astype(out_dtype)

        if nk == 1:
            finish(part)
            return
        acc_ref = refs[-1]
        k = pl.program_id(2)

        @pl.when(k == 0)
        def _():
            acc_ref[...] = part

        @pl.when(jnp.logical_and(k > 0, k < nk - 1))
        def _():
            acc_ref[...] += part

        @pl.when(k == nk - 1)
        def _():
            finish(acc_ref[...] + part)

    a_spec = pl.BlockSpec((tk, tm), lambda i, j, k: (k, i)) if ta else pl.BlockSpec((tm, tk), lambda i, j, k: (i, k))
    b_spec = pl.BlockSpec((tn, tk), lambda i, j, k: (j, k)) if tb else pl.BlockSpec((tk, tn), lambda i, j, k: (k, j))
    in_specs, args = [a_spec, b_spec], [a, b]
    if bias is not None:
        in_specs.append(pl.BlockSpec((1, tn), lambda i, j, k: (0, j)))
        args.append(bias)
    scratch = [pltpu.VMEM((tm, tn), F32)] if nk > 1 else []
    return _call(name, body, (M // tm, N // tn, nk), in_specs, pl.BlockSpec((tm, tn), lambda i, j, k: (i, j)),
                 _sds((M, N), out_dtype), scratch=scratch, sem=("parallel", "parallel", "arbitrary"), ride=ride)(*args)


def _ew(name, fn, ins, out_dtypes, S, C, rows=(), tr=256, tc=512):
    g = C
    for _, off in ins:
        g = math.gcd(g, off) if off else g
    tc = _pick(g, tc, LANES)
    tr = _pick(S, tr, 16)
    n_in, n_row, n_out = len(ins), len(rows), len(out_dtypes)

    def body(*refs):
        vals = [r[...].astype(F32) for r in refs[:n_in + n_row]]
        for o_ref, o in zip(refs[n_in + n_row:], fn(*vals)):
            o_ref[...] = o.astype(o_ref.dtype)

    in_specs = [_T(tr, tc, off // tc) for _, off in ins] + [_P(1, tc) for _ in rows]
    res = _call(name, body, (C // tc, S // tr), in_specs, [_T(tr, tc) for _ in out_dtypes],
                [_sds((S, C), d) for d in out_dtypes], sem=("parallel", "parallel"))(*[a for a, _ in ins], *rows)
    return res


def _adaln_fwd(x, norm, scale, shift, *, name, t=None, gate=None):
    S, D = x.shape
    tr = _pick(S, 256, 16)
    resid = t is not None

    def body(*refs):
        if resid:
            x_ref, t_ref, g_ref, n_ref, sc_ref, sh_ref, u_ref, h_ref = refs
            h = x_ref[...] + g_ref[...] * t_ref[...]
            h_ref[...] = h
        else:
            x_ref, n_ref, sc_ref, sh_ref, u_ref = refs
            h = x_ref[...]
        r = lax.rsqrt(jnp.mean(h * h, axis=-1, keepdims=True) + EPS)
        u_ref[...] = (h * r * (n_ref[...] * (1.0 + sc_ref[...])) + sh_ref[...]).astype(BF16)

    full, row = _T(tr, D), _P(1, D)
    if resid:
        return _call(name, body, (1, S // tr), [full, full, row, row, row, row], [full, full],
                     [_sds((S, D), BF16), _sds((S, D), F32)])(x, t, gate, norm, scale, shift)
    return _call(name, body, (1, S // tr), [full, row, row, row], full, _sds((S, D), BF16))(x, norm, scale, shift)


def _adaln_bwd(h, du, dres, norm, scale, *, name, t=None, gate=None):
    S, D = h.shape
    tr = _pick(S, 256, 16)
    gated = t is not None

    def body(*refs):
        if gated:
            h_ref, du_ref, dr_ref, n_ref, sc_ref, t_ref, g_ref, dh_ref, dsh_ref, dsc_ref, dn_ref, dt_ref, dg_ref = refs
        else:
            h_ref, du_ref, dr_ref, n_ref, sc_ref, dh_ref, dsh_ref, dsc_ref, dn_ref = refs
        first = pl.program_id(1) == 0
        hv, duv = h_ref[...], du_ref[...]
        r = lax.rsqrt(jnp.mean(hv * hv, axis=-1, keepdims=True) + EPS)
        xn = hv * r
        one_sc = 1.0 + sc_ref[...]
        dxn = duv * (n_ref[...] * one_sc)
        dh = dr_ref[...] + r * (dxn - xn * jnp.mean(dxn * xn, axis=-1, keepdims=True))
        dh_ref[...] = dh
        dux = duv * xn
        _accum(dsh_ref, _colsum(duv), first)
        _accum(dsc_ref, _colsum(dux * n_ref[...]), first)
        _accum(dn_ref, _colsum(dux * one_sc), first)
        if gated:
            dt_ref[...] = (dh * g_ref[...]).astype(BF16)
            _accum(dg_ref, _colsum(dh * t_ref[...]), first)

    full, row = _T(tr, D), _P(1, D)
    rowo = _sds((1, D), F32)
    if gated:
        return _call(name, body, (1, S // tr), [full, full, full, row, row, full, row],
                     [full, row, row, row, full, row],
                     [_sds((S, D), F32), rowo, rowo, rowo, _sds((S, D), BF16), rowo])(h, du, dres, norm, scale, t, gate)
    return _call(name, body, (1, S // tr), [full, full, full, row, row], [full, row, row, row],
                 [_sds((S, D), F32), rowo, rowo, rowo])(h, du, dres, norm, scale)


def _final(h1, t2, gate2, norm_f, tgt, *, name):
    S, D = h1.shape
    tr = _pick(S, 256, 16)

    def body(h_ref, t_ref, g_ref, n_ref, y_ref, dh_ref, dt_ref, loss_ref, dn_ref, dg_ref):
        first = pl.program_id(1) == 0
        tv = t_ref[...]
        h2 = h_ref[...] + g_ref[...] * tv
        r = lax.rsqrt(jnp.mean(h2 * h2, axis=-1, keepdims=True) + EPS)
        xn = h2 * r
        e = xn * n_ref[...] - y_ref[...]
        part = 0.5 * jnp.sum(jnp.mean(e * e, axis=-1, keepdims=True), axis=0, keepdims=True)
        _accum(loss_ref, jnp.broadcast_to(part, (SUBLANES, LANES)), first)
        dy = e * (1.0 / D)
        _accum(dn_ref, _colsum(dy * xn), first)
        dxn = dy * n_ref[...]
        dh2 = r * (dxn - xn * jnp.mean(dxn * xn, axis=-1, keepdims=True))
        dh_ref[...] = dh2
        dt_ref[...] = (dh2 * g_ref[...]).astype(BF16)
        _accum(dg_ref, _colsum(dh2 * tv), first)

    full, row = _T(tr, D), _P(1, D)
    rowo = _sds((1, D), F32)
    return _call(name, body, (1, S // tr), [full, full, row, row, full],
                 [full, full, _P(SUBLANES, LANES), row, row],
                 [_sds((S, D), F32), _sds((S, D), BF16), _sds((SUBLANES, LANES), F32), rowo, rowo])(h1, t2, gate2, norm_f, tgt)


HALO = 16


def _halo_prev(tr, tc, off=0):
    return pl.BlockSpec((HALO, tc), lambda j, i: (jnp.maximum(i * (tr // HALO) - 1, 0), j + off))


def _halo_next(tr, tc, n_slabs, off=0):
    return pl.BlockSpec((HALO, tc), lambda j, i: (jnp.minimum((i + 1) * (tr // HALO), n_slabs - 1), j + off))


def _fill_past(buf, prev_ref, cur_ref, first, tr):
    buf[0:HALO, :] = jnp.where(first, 0.0, prev_ref[...].astype(F32))
    buf[HALO:HALO + tr, :] = cur_ref[...].astype(F32)


def _fill_future(buf, cur_ref, next_ref, last, tr):
    buf[0:tr, :] = cur_ref[...].astype(F32)
    buf[tr:tr + HALO, :] = jnp.where(last, 0.0, next_ref[...].astype(F32))


def _al8(r):
    return r if isinstance(r, int) else pl.multiple_of(r, SUBLANES)


def _past_rows(buf, r0, taps):
    xe = buf[pl.ds(_al8(r0 + HALO - SUBLANES), CHUNK + SUBLANES), :]
    return [_shift_down(xe, d, CHUNK) for d in range(taps)]


def _future_rows(buf, r0, taps):
    xe = buf[pl.ds(_al8(r0), CHUNK + SUBLANES), :]
    return [_shift_up(xe, d, CHUNK) for d in range(taps)]


def _taps(w_ref):
    return [w_ref[k:k + 1, :] for k in range(w_ref.shape[0])]


def _conv(xs, w, b):
    taps = len(xs)
    y = w[taps - 1] * xs[0] if b is None else b + w[taps - 1] * xs[0]
    for d in range(1, taps):
        y = y + w[taps - 1 - d] * xs[d]
    return y


def _conv_fwd(x, off, C, w, b, *, name):
    S = x.shape[0]
    taps = w.shape[0]
    tc = _pick(math.gcd(C, off) if off else C, 512, LANES)
    tr = _pick(S, 256, 16)

    def body(x_ref, p_ref, w_ref, b_ref, y_ref, buf):
        _fill_past(buf, p_ref, x_ref, pl.program_id(1) == 0, tr)
        wt, bv = _taps(w_ref), b_ref[...]

        def step(r0):
            y_ref[pl.ds(r0, CHUNK), :] = _conv(_past_rows(buf, r0, taps), wt, bv)

        _chunks(tr, CHUNK, step)

    return _call(name, body, (C // tc, S // tr),
                 [_T(tr, tc, off // tc), _halo_prev(tr, tc, off // tc), _P(taps, tc), _P(1, tc)],
                 _T(tr, tc), _sds((S, C), F32), scratch=[pltpu.VMEM((tr + HALO, tc), F32)],
                 sem=("parallel", "parallel"))(x, x, w, b)


def _finish_conv_grads(acc, base, taps, dwb_ref, first):
    @pl.when(first)
    def _():
        dwb_ref[...] = jnp.zeros_like(dwb_ref)

    for d in range(taps):
        dwb_ref[taps - 1 - d:taps - d, :] += _colsum(acc[base + SUBLANES * d:base + SUBLANES * (d + 1), :])
    dwb_ref[taps:taps + 1, :] += _colsum(acc[base + SUBLANES * taps:base + SUBLANES * (taps + 1), :])


def _add_conv_grads(acc, base, dy, xs):
    taps = len(xs)
    for d in range(taps):
        acc[base + SUBLANES * d:base + SUBLANES * (d + 1), :] += _fold(dy * xs[d])
    acc[base + SUBLANES * taps:base + SUBLANES * (taps + 1), :] += _fold(dy)


def _conv_bwd(dy, x, off, C, w, *, name):
    S = dy.shape[0]
    taps = w.shape[0]
    assert taps < SUBLANES
    tc = _pick(math.gcd(C, off) if off else C, 512, LANES)
    tr = _pick(S, 256, 16)
    n_slabs = S // HALO

    def body(dy_ref, nx_ref, x_ref, p_ref, w_ref, dx_ref, dwb_ref, fbuf, pbuf, acc):
        i = pl.program_id(1)
        first = i == 0
        _fill_future(fbuf, dy_ref, nx_ref, i == S // tr - 1, tr)
        _fill_past(pbuf, p_ref, x_ref, first, tr)
        acc[...] = jnp.zeros_like(acc)
        wt = _taps(w_ref)

        def step(r0):
            dys = _future_rows(fbuf, r0, taps)
            dx_ref[pl.ds(r0, CHUNK), :] = _conv(dys, wt, None).astype(BF16)
            _add_conv_grads(acc, 0, dys[0], _past_rows(pbuf, r0, taps))

        _chunks(tr, CHUNK, step)
        _finish_conv_grads(acc, 0, taps, dwb_ref, first)

    return _call(name, body, (C // tc, S // tr),
                 [_T(tr, tc), _halo_next(tr, tc, n_slabs), _T(tr, tc, off // tc), _halo_prev(tr, tc, off // tc),
                  _P(taps, tc)],
                 [_T(tr, tc), _P(SUBLANES, tc)],
                 [_sds((S, C), BF16), _sds((SUBLANES, C), F32)],
                 scratch=[pltpu.VMEM((tr + HALO, tc), F32), pltpu.VMEM((tr + HALO, tc), F32),
                          pltpu.VMEM((SUBLANES * (taps + 1), tc), F32)])(dy, dy, x, x, w)


def _geglu_fwd(up, w, b, F, *, name):
    S = up.shape[0]
    taps = w.shape[0]
    tc = _pick(F, 512, LANES)
    tr = _pick(S, 256, 16)
    nf = F // tc

    def body(g_ref, gp_ref, v_ref, vp_ref, wg_ref, wv_ref, bg_ref, bv_ref, a_ref, ge_ref, dge_ref, vo_ref, gbuf, vbuf):
        first = pl.program_id(1) == 0
        _fill_past(gbuf, gp_ref, g_ref, first, tr)
        _fill_past(vbuf, vp_ref, v_ref, first, tr)
        wg, wv, bg, bv = _taps(wg_ref), _taps(wv_ref), bg_ref[...], bv_ref[...]

        def step(r0):
            rows = pl.ds(r0, CHUNK)
            ge, dge = _gelu_and_grad(_conv(_past_rows(gbuf, r0, taps), wg, bg))
            v = _conv(_past_rows(vbuf, r0, taps), wv, bv)
            a_ref[rows, :] = (ge * v).astype(BF16)
            ge_ref[rows, :] = ge.astype(BF16)
            dge_ref[rows, :] = dge.astype(BF16)
            vo_ref[rows, :] = v.astype(BF16)

        _chunks(tr, CHUNK, step)

    buf = pltpu.VMEM((tr + HALO, tc), F32)
    t, o = _T(tr, tc), _sds((S, F), BF16)
    return _call(name, body, (nf, S // tr),
                 [t, _halo_prev(tr, tc), _T(tr, tc, nf), _halo_prev(tr, tc, nf),
                  _P(taps, tc), _P(taps, tc, nf), _P(1, tc), _P(1, tc, nf)],
                 [t, t, t, t], [o, o, o, o], scratch=[buf, buf], sem=("parallel", "parallel"))(up, up, up, up, w, w, b, b)


def _ffn_bwd(up, w, ge, dge, vv, da, F, *, name):
    S = up.shape[0]
    taps = w.shape[0]
    tc = _pick(F, 512, LANES)
    tr = _pick(S, 256, 16)
    nf, nr = F // tc, S // tr
    ext = tr + HALO
    n_slabs = S // HALO

    assert HALO == CHUNK
    v_base = SUBLANES * (taps + 1)

    def body(g_ref, gp_ref, v_ref, vp_ref, wg_ref, wv_ref, ge_ref, gen_ref, dge_ref, dgen_ref, vo_ref, von_ref,
             da_ref, dan_ref, dxg_ref, dxv_ref, dwbg_ref, dwbv_ref, gbuf, vbuf, dgbuf, dvbuf, acc):
        i = pl.program_id(1)
        first, last = i == 0, i == nr - 1
        _fill_past(gbuf, gp_ref, g_ref, first, tr)
        _fill_past(vbuf, vp_ref, v_ref, first, tr)
        acc[...] = jnp.zeros_like(acc)
        wg, wv = _taps(wg_ref), _taps(wv_ref)

        def grads(r0, da, gev, dgev, vov, in_tile):
            dg = da * vov.astype(F32) * dgev.astype(F32)
            dv = da * gev.astype(F32)
            dgbuf[pl.ds(_al8(r0), CHUNK), :] = dg
            dvbuf[pl.ds(_al8(r0), CHUNK), :] = dv
            if in_tile:
                _add_conv_grads(acc, 0, dg, _past_rows(gbuf, r0, taps))
                _add_conv_grads(acc, v_base, dv, _past_rows(vbuf, r0, taps))

        def in_tile(r0):
            rows = pl.ds(r0, CHUNK)
            grads(r0, da_ref[rows, :], ge_ref[rows, :], dge_ref[rows, :], vo_ref[rows, :], True)

        _chunks(tr, CHUNK, in_tile)
        grads(tr, jnp.where(last, 0.0, dan_ref[...]), gen_ref[...], dgen_ref[...], von_ref[...], False)

        def back(r0):
            dxg_ref[pl.ds(r0, CHUNK), :] = _conv(_future_rows(dgbuf, r0, taps), wg, None).astype(BF16)
            dxv_ref[pl.ds(r0, CHUNK), :] = _conv(_future_rows(dvbuf, r0, taps), wv, None).astype(BF16)

        _chunks(tr, CHUNK, back)
        _finish_conv_grads(acc, 0, taps, dwbg_ref, first)
        _finish_conv_grads(acc, v_base, taps, dwbv_ref, first)

    xbuf = pltpu.VMEM((HALO + tr, tc), F32)
    dbuf = pltpu.VMEM((ext, tc), F32)
    t, tw, t8, nx = _T(tr, tc), _P(taps, tc), _P(SUBLANES, tc), _halo_next(tr, tc, n_slabs)
    return _call(name, body, (nf, nr),
                 [t, _halo_prev(tr, tc), _T(tr, tc, nf), _halo_prev(tr, tc, nf), tw, _P(taps, tc, nf),
                  t, nx, t, nx, t, nx, t, nx],
                 [t, t, t8, t8],
                 [_sds((S, F), BF16), _sds((S, F), BF16), _sds((SUBLANES, F), F32), _sds((SUBLANES, F), F32)],
                 scratch=[xbuf, xbuf, dbuf, dbuf, pltpu.VMEM((2 * v_base, tc), F32)])(
                     up, up, up, up, w, w, ge, ge, dge, dge, vv, vv, da, da)


def _softplus_neg(lam):
    z = -lam
    e = jnp.exp(-jnp.abs(z))
    u = 1.0 + e
    log1p_e = jnp.where(u == 1.0, e, jnp.log(u) * e / jnp.where(u == 1.0, 1.0, u - 1.0))
    sp = jnp.maximum(z, 0.0) + log1p_e
    sg = jnp.where(z >= 0, 1.0 / u, e / u)
    return sp, sg


def _one_minus_exp(x):
    series = -x * (1.0 + x * (0.5 + x * (1.0 / 6.0 + x * (1.0 / 24.0))))
    return jnp.where(x > -0.01, series, 1.0 - jnp.exp(x))


def _gate_values(xc, wa_ref, wi_ref, ba_ref, bi_ref, lam_ref, is_t0):
    xb = xc.astype(BF16)
    ra = _sigmoid(jnp.dot(xb, wa_ref[0].astype(BF16), preferred_element_type=F32) + ba_ref[...])
    ia = _sigmoid(jnp.dot(xb, wi_ref[0].astype(BF16), preferred_element_type=F32) + bi_ref[...])
    sp, sg = _softplus_neg(lam_ref[...])
    log_a = -RG_C * ra * sp
    a = jnp.exp(log_a)
    mult = jnp.where(is_t0, 1.0, jnp.sqrt(_one_minus_exp(2.0 * log_a)))
    return ra, ia, sp, sg, a, mult


def _rnn_blockspecs(tr):
    x = pl.BlockSpec((tr, LANES), lambda n, i: (i, n))
    w = pl.BlockSpec((1, LANES, LANES), lambda n, i: (n, 0, 0))
    p = pl.BlockSpec((1, LANES), lambda n, i: (0, n))
    return x, w, p


def _is_t0(tr):
    rows = lax.broadcasted_iota(jnp.int32, (tr, LANES), 0)
    return jnp.logical_and(pl.program_id(1) == 0, rows == 0)


def _gates_fwd(xc, w_a, b_a, w_i, b_i, lam, *, name):
    S, C = xc.shape
    tr = _pick(S, 1024, 16)

    def body(x_ref, wa_ref, wi_ref, ba_ref, bi_ref, lam_ref, a_ref, inp_ref):
        xv = x_ref[...]
        _, ia, _, _, a, mult = _gate_values(xv, wa_ref, wi_ref, ba_ref, bi_ref, lam_ref, _is_t0(tr))
        a_ref[...] = a
        inp_ref[...] = mult * ia * xv

    x, w, p = _rnn_blockspecs(tr)
    return _call(name, body, (C // LANES, S // tr), [x, w, w, p, p, p], [x, x],
                 [_sds((S, C), F32), _sds((S, C), F32)], sem=("parallel", "parallel"))(xc, w_a, w_i, b_a, b_i, lam)


def _gates_bwd(xc, dacc, hprev, w_a, b_a, w_i, b_i, lam, *, name, ride=None):
    S, C = xc.shape
    nb = C // LANES
    tr = _pick(S, 1024, 16)

    def body(x_ref, d_ref, hp_ref, wa_ref, wi_ref, ba_ref, bi_ref, lam_ref,
             dx_ref, dwa_ref, dwi_ref, dba_ref, dbi_ref, dlam_ref):
        first = pl.program_id(1) == 0
        t0 = _is_t0(tr)
        xv, dv = x_ref[...], d_ref[...]
        ra, ia, sp, sg, a, mult = _gate_values(xv, wa_ref, wi_ref, ba_ref, bi_ref, lam_ref, t0)
        d_a = dv * hp_ref[...]
        d_mult = dv * ia * xv
        d_ia = dv * mult * xv
        d_log = d_a * a + jnp.where(t0, 0.0, -d_mult * (a * a) / mult)
        d_pa = d_log * (-RG_C * sp) * ra * (1.0 - ra)
        d_pi = d_ia * ia * (1.0 - ia)
        xb, dab, dib = xv.astype(BF16), d_pa.astype(BF16), d_pi.astype(BF16)
        nt = (((1,), (1,)), ((), ()))
        tn = (((0,), (0,)), ((), ()))
        dx_ref[...] = (dv * mult * ia
                       + lax.dot_general(dab, wa_ref[0].astype(BF16), nt, preferred_element_type=F32)
                       + lax.dot_general(dib, wi_ref[0].astype(BF16), nt, preferred_element_type=F32))
        _accum(dwa_ref, lax.dot_general(xb, dab, tn, preferred_element_type=F32)[None], first)
        _accum(dwi_ref, lax.dot_general(xb, dib, tn, preferred_element_type=F32)[None], first)
        _accum(dba_ref, _colsum(d_pa), first)
        _accum(dbi_ref, _colsum(d_pi), first)
        _accum(dlam_ref, _colsum(d_log * (-RG_C) * ra) * (-sg), first)

    x, w, p = _rnn_blockspecs(tr)
    row = _sds((1, C), F32)
    return _call(name, body, (nb, S // tr), [x, x, x, w, w, p, p, p], [x, w, w, p, p, p],
                 [_sds((S, C), F32), _sds((nb, LANES, LANES), F32), _sds((nb, LANES, LANES), F32), row, row, row],
                 ride=ride)(xc, dacc, hprev, w_a, w_i, b_a, b_i, lam)


def _scan_fwd(a, inp, *, name):
    S, C = a.shape
    tc = _pick(C, 1280, LANES)
    tr = _pick(S, 256, SUBLANES)

    def body(a_ref, b_ref, h_ref, hp_ref, carry):
        @pl.when(pl.program_id(1) == 0)
        def _():
            carry[...] = jnp.zeros_like(carry)

        rows = lax.broadcasted_iota(jnp.int32, (SUBLANES, tc), 0)

        def slab(s, h):
            base = pl.multiple_of(s * SUBLANES, SUBLANES)
            av, bv = a_ref[pl.ds(base, SUBLANES), :], b_ref[pl.ds(base, SUBLANES), :]
            ho = jnp.zeros((SUBLANES, tc), F32)
            po = jnp.zeros((SUBLANES, tc), F32)
            for r in range(SUBLANES):
                po = jnp.where(rows == r, h, po)
                h = av[r:r + 1, :] * h + bv[r:r + 1, :]
                ho = jnp.where(rows == r, h, ho)
            h_ref[pl.ds(base, SUBLANES), :] = ho
            hp_ref[pl.ds(base, SUBLANES), :] = po
            return h

        carry[...] = lax.fori_loop(0, tr // SUBLANES, slab, carry[...])

    t = _T(tr, tc)
    return _call(name, body, (C // tc, S // tr), [t, t], [t, t], [_sds((S, C), F32), _sds((S, C), F32)],
                 scratch=[pltpu.VMEM((1, tc), F32)])(a, inp)


def _scan_bwd(a, dh, *, name):
    S, C = a.shape
    tc = _pick(C, 1280, LANES)
    tr = _pick(S, 256, SUBLANES)
    nr = S // tr

    def body(a_ref, d_ref, o_ref, carry):
        @pl.when(pl.program_id(1) == 0)
        def _():
            carry[...] = jnp.zeros_like(carry)

        rows = lax.broadcasted_iota(jnp.int32, (SUBLANES, tc), 0)
        n_slabs = tr // SUBLANES

        def slab(s, g):
            base = pl.multiple_of((n_slabs - 1 - s) * SUBLANES, SUBLANES)
            av, dv = a_ref[pl.ds(base, SUBLANES), :], d_ref[pl.ds(base, SUBLANES), :]
            out = jnp.zeros((SUBLANES, tc), F32)
            for r in range(SUBLANES - 1, -1, -1):
                acc = dv[r:r + 1, :] + g
                out = jnp.where(rows == r, acc, out)
                g = av[r:r + 1, :] * acc
            o_ref[pl.ds(base, SUBLANES), :] = out
            return g

        carry[...] = lax.fori_loop(0, n_slabs, slab, carry[...])

    t = pl.BlockSpec((tr, tc), lambda j, i: (nr - 1 - i, j))
    return _call(name, body, (C // tc, nr), [t, t], t, _sds((S, C), F32), scratch=[pltpu.VMEM((1, tc), F32)])(a, dh)


SCALE = HEAD_DIM ** -0.5


def _bucket_table(num_buckets):
    qi = np.arange(BLOCK)[:, None]
    j = np.arange(BLOCK)[None, :]
    dist = np.where(j <= qi, qi - j, qi + BLOCK - j)
    d = np.maximum(dist, 1).astype(np.float64)
    large = NUM_EXACT + (np.log(d / NUM_EXACT) / math.log(MAX_DISTANCE / NUM_EXACT) * (num_buckets - NUM_EXACT)).astype(np.int32)
    large = np.minimum(large, num_buckets - 1)
    return np.where(dist < NUM_EXACT, dist, large).astype(np.int32)


def _bias_table(rel_bias, n_heads, *, name):
    nbk = rel_bias.shape[0]
    bucket = jnp.asarray(_bucket_table(nbk))

    def body(rb_ref, bk_ref, o_ref):
        t, h = pl.program_id(0), pl.program_id(1)
        bk = bk_ref[...]
        acc = jnp.zeros((BLOCK, BLOCK), F32)
        for b in range(nbk):
            acc = jnp.where(bk == b, rb_ref[b, h], acc)
        o_ref[0, 0] = jnp.where(jnp.logical_and(t == 0, jnp.logical_not(_own_block())), NEG_INF, acc)

    return _call(name, body, (2, n_heads),
                 [pl.BlockSpec(memory_space=pltpu.SMEM), pl.BlockSpec((BLOCK, BLOCK), lambda t, h: (0, 0))],
                 pl.BlockSpec((1, 1, BLOCK, BLOCK), lambda t, h: (t, h, 0, 0)),
                 _sds((2, n_heads, BLOCK, BLOCK), F32), sem=("arbitrary", "arbitrary"))(rel_bias, bucket)


def _bias_table_bwd(dbias, nbk, *, name):
    n_heads = dbias.shape[0]
    bucket = jnp.asarray(_bucket_table(nbk))

    def body(db_ref, bk_ref, o_ref):
        h = pl.program_id(0)
        bk = bk_ref[...]
        dv = db_ref[0]
        rows = lax.broadcasted_iota(jnp.int32, (nbk, LANES), 0)
        lanes = lax.broadcasted_iota(jnp.int32, (nbk, LANES), 1)
        acc = jnp.zeros((nbk, LANES), F32)
        for b in range(nbk):
            s = jnp.sum(jnp.sum(jnp.where(bk == b, dv, 0.0), axis=0, keepdims=True), axis=1, keepdims=True)
            acc = jnp.where(jnp.logical_and(rows == b, lanes == h), s, acc)
        _accum(o_ref, acc, h == 0)

    return _call(name, body, (n_heads,),
                 [pl.BlockSpec((1, BLOCK, BLOCK), lambda h: (h, 0, 0)), pl.BlockSpec((BLOCK, BLOCK), lambda h: (0, 0))],
                 pl.BlockSpec((nbk, LANES), lambda h: (0, 0)), _sds((nbk, LANES), F32), sem=("arbitrary",))(dbias, bucket)


def _own_block(heads=1):
    qi = lax.broadcasted_iota(jnp.int32, (heads * BLOCK, BLOCK), 0) % BLOCK
    j = lax.broadcasted_iota(jnp.int32, (heads * BLOCK, BLOCK), 1)
    return j <= qi


def _to_window(band, own):
    return jnp.where(own, band[:, BLOCK:], band[:, :BLOCK])


def _to_band(win, own):
    return jnp.concatenate([jnp.where(own, 0.0, win), jnp.where(own, win, 0.0)], axis=1)


def _stack_pairs(ref, group):
    return jnp.concatenate([ref[:, (g // 2) * LANES:(g // 2 + 1) * LANES] for g in range(group)], axis=0).astype(F32)


def _own_lanes(group):
    rows = lax.broadcasted_iota(jnp.int32, (group * BLOCK, LANES), 0)
    lanes = lax.broadcasted_iota(jnp.int32, (group * BLOCK, LANES), 1)
    return (lanes < HEAD_DIM) == ((rows // BLOCK) % 2 == 0)


def _unstack_pairs(stacked, o_ref, group):
    lo = lax.broadcasted_iota(jnp.int32, (BLOCK, LANES), 1) < HEAD_DIM
    for p in range(group // 2):
        even = stacked[2 * p * BLOCK:(2 * p + 1) * BLOCK, :]
        odd = stacked[(2 * p + 1) * BLOCK:(2 * p + 2) * BLOCK, :]
        o_ref[:, p * LANES:(p + 1) * LANES] = jnp.where(lo, even, odd).astype(o_ref.dtype)


def _group_probs(qm, kb, b_ref, s_ref, own, group):
    band = lax.dot_general(qm, kb, (((1,), (1,)), ((), ())), preferred_element_type=F32)
    bias = jnp.concatenate([b_ref[0, g] for g in range(group)], axis=0)
    sink = jnp.concatenate([jnp.broadcast_to(s_ref[g][:, 0:1], (BLOCK, 1)) for g in range(group)], axis=0)
    s = _to_window(band, own) + bias
    m = jnp.maximum(jnp.max(s, axis=-1, keepdims=True), sink)
    p = jnp.exp(s - m)
    es = jnp.exp(sink - m)
    inv = 1.0 / (jnp.sum(p, axis=-1, keepdims=True) + es)
    return p, inv, es


def _attn_fwd(proj, bias, sinks_b, n_kv, group, q_off, k_off, v_off, *, name, ride=None):
    S = proj.shape[0]
    nblk = S // BLOCK
    gw = group * HEAD_DIM
    pairs = group // 2

    def body(q_ref, kc_ref, kp_ref, vc_ref, vp_ref, b_ref, s_ref, o_ref):
        kb = jnp.concatenate([kp_ref[...], kc_ref[...]], axis=0).astype(BF16)
        vb = jnp.concatenate([vp_ref[...], vc_ref[...]], axis=0).astype(BF16)
        own = _own_block(group)
        qm = jnp.where(_own_lanes(group), _stack_pairs(q_ref, group) * SCALE, 0.0).astype(BF16)
        pu, inv, _ = _group_probs(qm, kb, b_ref, s_ref, own, group)
        o = jnp.dot(_to_band(pu, own).astype(BF16), vb, preferred_element_type=F32) * inv
        _unstack_pairs(o, o_ref, group)

    qb, kb0, vb0 = q_off // gw, k_off // LANES, v_off // LANES
    cur = lambda b0: pl.BlockSpec((BLOCK, LANES), lambda j, n: (n, b0 + j))
    prev = lambda b0: pl.BlockSpec((BLOCK, LANES), lambda j, n: (jnp.maximum(n - 1, 0), b0 + j))
    return _call(name, body, (n_kv, nblk),
                 [pl.BlockSpec((BLOCK, gw), lambda j, n: (n, qb + j)), cur(kb0), prev(kb0), cur(vb0), prev(vb0),
                  pl.BlockSpec((1, group, BLOCK, BLOCK), lambda j, n: (jnp.minimum(n, 1), j, 0, 0)),
                  pl.BlockSpec((group, 1, LANES), lambda j, n: (j, 0, 0))],
                 pl.BlockSpec((BLOCK, gw), lambda j, n: (n, j)), _sds((S, n_kv * gw), BF16),
                 sem=("parallel", "parallel"), ride=ride)(proj, proj, proj, proj, proj, bias, sinks_b)


def _attn_bwd(proj, o, do, bias, sinks_b, n_kv, group, q_off, k_off, v_off, *, name, ride=None):
    S = proj.shape[0]
    nblk = S // BLOCK
    gw = group * HEAD_DIM
    rows_all = group * BLOCK
    nt = (((1,), (1,)), ((), ()))
    tn = (((0,), (0,)), ((), ()))

    def body(q_ref, kc_ref, kp_ref, vc_ref, vp_ref, b_ref, s_ref, o_ref, do_ref,
             dq_ref, dk_ref, dv_ref, db_ref, ds_ref, ck, cv, sacc):
        n = pl.program_id(1)
        lo2 = lax.broadcasted_iota(jnp.int32, (2 * BLOCK, LANES), 1) < HEAD_DIM

        @pl.when(n == 0)
        def _():
            ck[...] = jnp.zeros_like(ck)
            cv[...] = jnp.zeros_like(cv)
            sacc[...] = jnp.zeros_like(sacc)
            db_ref[...] = jnp.zeros_like(db_ref)

        @pl.when(n < nblk)
        def _():
            kb = jnp.concatenate([kp_ref[...], kc_ref[...]], axis=0).astype(BF16)
            vb = jnp.concatenate([vp_ref[...], vc_ref[...]], axis=0).astype(BF16)
            own = _own_block(group)
            mine = _own_lanes(group)
            qm = jnp.where(mine, _stack_pairs(q_ref, group) * SCALE, 0.0).astype(BF16)
            dom = jnp.where(mine, _stack_pairs(do_ref, group), 0.0)
            pu, inv, es = _group_probs(qm, kb, b_ref, s_ref, own, group)
            pr = pu * inv
            delta = jnp.sum(dom * _stack_pairs(o_ref, group), axis=-1, keepdims=True)
            domb = dom.astype(BF16)
            dp = _to_window(lax.dot_general(domb, vb, nt, preferred_element_type=F32), own)
            ds = pr * (dp - delta)
            for g in range(group):
                db_ref[g] += ds[g * BLOCK:(g + 1) * BLOCK, :]
            sacc[...] += -(es * inv) * delta
            dsb = _to_band(ds, own).astype(BF16)
            _unstack_pairs(jnp.dot(dsb, kb, preferred_element_type=F32) * SCALE, dq_ref, group)
            dk_acc = lax.dot_general(dsb, qm, tn, preferred_element_type=F32)
            dv_acc = lax.dot_general(_to_band(pr, own).astype(BF16), domb, tn, preferred_element_type=F32)
            dkf = jnp.where(lo2, dk_acc + pltpu.roll(dk_acc, HEAD_DIM, 1), 0.0)
            dvf = jnp.where(lo2, dv_acc + pltpu.roll(dv_acc, HEAD_DIM, 1), 0.0)

            @pl.when(n > 0)
            def _():
                dk_ref[...] = (ck[...] + dkf[0:BLOCK, :]).astype(BF16)
                dv_ref[...] = (cv[...] + dvf[0:BLOCK, :]).astype(BF16)

            ck[...] = dkf[BLOCK:2 * BLOCK, :]
            cv[...] = dvf[BLOCK:2 * BLOCK, :]

        @pl.when(n == nblk)
        def _():
            dk_ref[...] = ck[...].astype(BF16)
            dv_ref[...] = cv[...].astype(BF16)
            lane = lax.broadcasted_iota(jnp.int32, (1, LANES), 1)
            tot = jnp.zeros((1, LANES), F32)
            for g in range(group):
                tot = jnp.where(lane == g, _colsum(sacc[g * BLOCK:(g + 1) * BLOCK, :]), tot)
            ds_ref[0] = tot

    qb, kb0, vb0 = q_off // gw, k_off // LANES, v_off // LANES
    last = nblk - 1
    cur = lambda b0: pl.BlockSpec((BLOCK, LANES), lambda j, n: (jnp.minimum(n, last), b0 + j))
    prev = lambda b0: pl.BlockSpec((BLOCK, LANES), lambda j, n: (jnp.clip(n - 1, 0, last), b0 + j))
    qspec = lambda b0: pl.BlockSpec((BLOCK, gw), lambda j, n: (jnp.minimum(n, last), b0 + j))
    kvout = pl.BlockSpec((BLOCK, LANES), lambda j, n: (jnp.maximum(n - 1, 0), j))
    return _call(name, body, (n_kv, nblk + 1),
                 [qspec(qb), cur(kb0), prev(kb0), cur(vb0), prev(vb0),
                  pl.BlockSpec((1, group, BLOCK, BLOCK), lambda j, n: (jnp.minimum(n, 1), j, 0, 0)),
                  pl.BlockSpec((group, 1, LANES), lambda j, n: (j, 0, 0)), qspec(0), qspec(0)],
                 [qspec(0), kvout, kvout, pl.BlockSpec((group, BLOCK, BLOCK), lambda j, n: (j, 0, 0)),
                  pl.BlockSpec((1, 1, LANES), lambda j, n: (j, 0, 0))],
                 [_sds((S, n_kv * gw), BF16), _sds((S, n_kv * LANES), BF16), _sds((S, n_kv * LANES), BF16),
                  _sds((n_kv * group, BLOCK, BLOCK), F32), _sds((n_kv, 1, LANES), F32)],
                 scratch=[pltpu.VMEM((BLOCK, LANES), F32), pltpu.VMEM((BLOCK, LANES), F32),
                          pltpu.VMEM((rows_all, 1), F32)], ride=ride)(proj, proj, proj, proj, proj, bias, sinks_b, o, do)


def _adamw(w, g, m, v, *, name, ride=None):
    R, C = w.shape
    tr = _pick(R, 128, SUBLANES)
    bc1 = 1.0 - ADAM_B1 ** ADAM_STEP
    bc2 = 1.0 - ADAM_B2 ** ADAM_STEP

    def body(w_ref, g_ref, m_ref, v_ref, d_ref, nm_ref, nv_ref):
        gv = g_ref[...]
        nm = ADAM_B1 * m_ref[...] + (1.0 - ADAM_B1) * gv
        nv = ADAM_B2 * v_ref[...] + (1.0 - ADAM_B2) * (gv * gv)
        d_ref[...] = -ADAM_LR * ((nm / bc1) / (jnp.sqrt(nv / bc2) + ADAM_EPS) + ADAM_WD * w_ref[...])
        nm_ref[...] = nm
        nv_ref[...] = nv

    t = pl.BlockSpec((tr, C), lambda i: (i, 0))
    o = _sds((R, C), F32)
    return _call(name, body, (R // tr,), [t, t, t, t], [t, t, t], [o, o, o], sem=("parallel",), ride=ride)(w, g, m, v)


def _sum_devices(packs, *, name):
    _, R, C = packs.shape
    tr = _pick(R, 512, SUBLANES)

    def body(p_ref, o_ref):
        acc = p_ref[0]
        for d in range(1, N_DEV):
            acc = acc + p_ref[d]
        o_ref[...] = acc

    return _call(name, body, (R // tr,), [pl.BlockSpec((N_DEV, tr, C), lambda i: (0, i, 0))],
                 pl.BlockSpec((tr, C), lambda i: (i, 0)), _sds((R, C), F32), sem=("parallel",))(packs)


class _Big:
    def __init__(self, kind, R, C):
        self.kind, self.R, self.C = kind, R, C
        self.hr = R // 2
        self.full = (R, N_CHIPS * C) if kind == "col" else (N_CHIPS * R, C)

    def region(self, ref, k, c):
        if self.kind == "col":
            return ref.at[pl.ds(c * self.hr, self.hr), pl.ds(k * self.C, self.C)]
        return ref.at[pl.ds(k * self.R + c * self.hr, self.hr), :]


def _pair_sum(spec, g_full, land, c_arr, *, name):
    hr, C = spec.hr, spec.C
    tr = _pick(hr, 128, 16)
    nr = hr // tr

    def body(c_ref, g_ref, l_ref, o_ref):
        o_ref[0] = (g_ref[...] + l_ref[0]).astype(BF16)

    if spec.kind == "col":
        gspec = pl.BlockSpec((tr, C), lambda k, i, c_ref: (c_ref[0] * nr + i, k))
    else:
        gspec = pl.BlockSpec((tr, C), lambda k, i, c_ref: (k * 2 * nr + c_ref[0] * nr + i, 0))
    lspec = pl.BlockSpec((1, tr, C), lambda k, i, c_ref: (k, i, 0))
    return _call(name, body, (N_CHIPS, nr), [gspec, lspec], lspec, _sds((N_CHIPS, hr, C), BF16), nsp=1,
                 sem=("parallel", "parallel"))(c_arr, g_full, land)


def _chip_sum(spec, chipsum, land, kc_arr, *, name):
    hr, C = spec.hr, spec.C
    tr = _pick(hr, 128, 16)
    nr = hr // tr

    def body(kc_ref, s_ref, l_ref, o_ref):
        acc = s_ref[0].astype(F32)
        for j in range(N_CHIPS - 1):
            acc = acc + l_ref[j].astype(F32)
        o_ref[...] = acc

    return _call(name, body, (nr,),
                 [pl.BlockSpec((1, tr, C), lambda i, kc_ref: (kc_ref[0], i, 0)),
                  pl.BlockSpec((N_CHIPS - 1, tr, C), lambda i, kc_ref: (0, i, 0))],
                 pl.BlockSpec((tr, C), lambda i, kc_ref: (kc_ref[1] * nr + i, 0)), _sds((spec.R, C), F32), nsp=1,
                 sem=("parallel",))(kc_arr, chipsum, land)


def _place():
    x, y, c = lax.axis_index("x"), lax.axis_index("y"), lax.axis_index("c")
    return x, y, c, [(1 - x, y), (x, 1 - y), (1 - x, 1 - y)]


def _remote(src, dst, send_sem, recv_sem, dev):
    return pltpu.make_async_remote_copy(src_ref=src, dst_ref=dst, send_sem=send_sem, recv_sem=recv_sem,
                                        device_id=dev, device_id_type=MESH)


def _placed(block, dev):
    m, n = block.shape
    return lax.dynamic_update_slice(jnp.zeros((N_DEV * m, n), block.dtype), block, (dev * m, 0))


def _all_gather(bufs):
    n = len(bufs)
    per = 7

    def copier(outs, send_sems, recv_sems, w):
        m = outs[w].shape[0] // N_DEV

        def rows(px, py, pc):
            return outs[w].at[pl.ds((4 * px + 2 * py + pc) * m, m), :]

        def copy(k, block, to):
            return _remote(rows(*block), rows(*block), send_sems.at[w * per + k], recv_sems.at[w * per + k], to)

        return copy

    def own(outs, send_sems, recv_sems, w):
        x, y, c, chips = _place()
        copy = copier(outs, send_sems, recv_sems, w)
        return [copy(0, (x, y, c), (x, y, 1 - c))] + [copy(1 + j, (x, y, c), (*chip, c)) for j, chip in enumerate(chips)]

    def start(ins, outs, send_sems, recv_sems):
        for w in range(n):
            for cp in own(outs, send_sems, recv_sems, w):
                cp.start()

    def finish(ins, outs, send_sems, recv_sems):
        x, y, c, chips = _place()
        me, sibling = (x, y, c), (x, y, 1 - c)
        passed = []
        for w in range(n):
            copy = copier(outs, send_sems, recv_sems, w)
            for j, chip in enumerate(chips):
                copy(1 + j, (*chip, c), me).wait_recv()
                cp = copy(4 + j, (*chip, c), sibling)
                cp.start()
                passed.append(cp)
        for w in range(n):
            copy = copier(outs, send_sems, recv_sems, w)
            copy(0, sibling, me).wait_recv()
            for j, chip in enumerate(chips):
                copy(4 + j, (*chip, 1 - c), me).wait_recv()
            for cp in own(outs, send_sems, recv_sems, w):
                cp.wait_send()
        for cp in passed:
            cp.wait_send()

    return _Exchange(bufs, [_sds(a.shape, a.dtype) for a in bufs], per * n, start, finish, in_place=True)


def _cast_into(spec, w, k_arr, *, name):
    R, C = spec.R, spec.C
    tr = _pick(R, 256, 16)
    nr = R // tr

    def body(k_ref, w_ref, o_ref):
        o_ref[...] = w_ref[...].astype(BF16)

    if spec.kind == "col":
        ospec = pl.BlockSpec((tr, C), lambda i, k_ref: (i, k_ref[0]))
    else:
        ospec = pl.BlockSpec((tr, C), lambda i, k_ref: (k_ref[0] * nr + i, 0))
    return _call(name, body, (nr,), [pl.BlockSpec((tr, C), lambda i, k_ref: (i, 0))], ospec,
                 _sds(spec.full, BF16), nsp=1, sem=("parallel",))(k_arr, w)


def _weight_gather(bufs, specs):
    per = 6

    def to_chips(outs, send_sems, recv_sems):
        x, y, c, chips = _place()
        k_me = 2 * x + y
        cps = []
        for w, sp in enumerate(specs):
            mine = sp.region(outs[w], k_me, c)
            cps += [_remote(mine, mine, send_sems.at[w * per + j], recv_sems.at[w * per + j], (*chip, c))
                    for j, chip in enumerate(chips)]
        return cps

    def start(ins, outs, send_sems, recv_sems):
        for cp in to_chips(outs, send_sems, recv_sems):
            cp.start()

    def finish(ins, outs, send_sems, recv_sems):
        x, y, c, chips = _place()
        sibling = (x, y, 1 - c)
        passed = []
        for w, sp in enumerate(specs):
            for j, chip in enumerate(chips):
                got = sp.region(outs[w], 2 * chip[0] + chip[1], c)
                _remote(got, got, send_sems.at[w * per + j], recv_sems.at[w * per + j], (*chip, c)).wait_recv()
                cp = _remote(got, got, send_sems.at[w * per + 3 + j], recv_sems.at[w * per + 3 + j], sibling)
                cp.start()
                passed.append(cp)
        for w, sp in enumerate(specs):
            for j, chip in enumerate(chips):
                got = sp.region(outs[w], 2 * chip[0] + chip[1], 1 - c)
                _remote(got, got, send_sems.at[w * per + 3 + j], recv_sems.at[w * per + 3 + j], sibling).wait_recv()
        for cp in passed + to_chips(outs, send_sems, recv_sems):
            cp.wait_send()

    return _Exchange(bufs, [_sds(sp.full, BF16) for sp in specs], per * len(specs), start, finish, in_place=True)


def _pair_exchange(grads, specs):
    def copies(ins, outs, send_sems, recv_sems):
        x, y, c, _ = _place()
        return [_remote(sp.region(ins[w], k, 1 - c), outs[w].at[k], send_sems.at[w * N_CHIPS + k],
                        recv_sems.at[w * N_CHIPS + k], (x, y, 1 - c))
                for w, sp in enumerate(specs) for k in range(N_CHIPS)]

    def start(ins, outs, send_sems, recv_sems):
        for cp in copies(ins, outs, send_sems, recv_sems):
            cp.start()

    def finish(ins, outs, send_sems, recv_sems):
        for cp in copies(ins, outs, send_sems, recv_sems):
            cp.wait()

    return _Exchange(grads, [_sds((N_CHIPS, sp.hr, sp.C), F32) for sp in specs], N_CHIPS * len(specs), start, finish,
                     in_place=False)


def _chip_exchange(chipsums, specs):
    per = N_CHIPS - 1

    def copies(ins, outs, send_sems, recv_sems):
        x, y, c, chips = _place()
        return [_remote(ins[w].at[2 * chip[0] + chip[1]], outs[w].at[j], send_sems.at[w * per + j],
                        recv_sems.at[w * per + j], (*chip, c))
                for w in range(len(specs)) for j, chip in enumerate(chips)]

    def start(ins, outs, send_sems, recv_sems):
        for cp in copies(ins, outs, send_sems, recv_sems):
            cp.start()

    def finish(ins, outs, send_sems, recv_sems):
        for cp in copies(ins, outs, send_sems, recv_sems):
            cp.wait()

    return _Exchange(chipsums, [_sds((per, sp.hr, sp.C), BF16) for sp in specs], per * len(specs), start, finish,
                     in_place=False)


def _pair_share(bufs, specs):
    def mine(outs, send_sems, recv_sems):
        x, y, c, _ = _place()
        rows = [outs[w].at[pl.ds(c * sp.hr, sp.hr), :] for w, sp in enumerate(specs)]
        return [_remote(r, r, send_sems.at[w], recv_sems.at[w], (x, y, 1 - c)) for w, r in enumerate(rows)]

    def start(ins, outs, send_sems, recv_sems):
        for cp in mine(outs, send_sems, recv_sems):
            cp.start()

    def finish(ins, outs, send_sems, recv_sems):
        x, y, c, _ = _place()
        for w, sp in enumerate(specs):
            theirs = outs[w].at[pl.ds((1 - c) * sp.hr, sp.hr), :]
            _remote(theirs, theirs, send_sems.at[w], recv_sems.at[w], (x, y, 1 - c)).wait_recv()
        for cp in mine(outs, send_sems, recv_sems):
            cp.wait_send()

    return _Exchange(bufs, [_sds((sp.R, sp.C), F32) for sp in specs], len(specs), start, finish, in_place=True)


PACK_ALIGN = SUBLANES * LANES


def _pack(arrs):
    flat = jnp.concatenate([a.reshape(-1) for a in arrs])
    pad = (-flat.shape[0]) % PACK_ALIGN
    return jnp.pad(flat, (0, pad)).reshape(-1, LANES)


def _unpack(packed, shapes):
    flat = packed.reshape(-1)
    out, pos = [], 0
    for s in shapes:
        n = int(np.prod(s))
        out.append(flat[pos:pos + n].reshape(s))
        pos += n
    return out


def kernel(x, c, w_ada, b_ada, norm1, w_in, rnn_conv_w, rnn_conv_b, w_rg_a, b_rg_a, w_rg_i, b_rg_i, rg_lambda, w_o_rnn, w_o_attn, attn_sinks, rel_bias, w_out, norm2, w_up, ffn_conv_w, ffn_conv_b, w_down, norm_f, loss_target, m_w_ada, m_b_ada, m_norm1, m_w_in, m_rnn_conv_w, m_rnn_conv_b, m_w_rg_a, m_b_rg_a, m_w_rg_i, m_b_rg_i, m_rg_lambda, m_w_o_rnn, m_w_o_attn, m_attn_sinks, m_rel_bias, m_w_out, m_norm2, m_w_up, m_ffn_conv_w, m_ffn_conv_b, m_w_down, m_norm_f, v_w_ada, v_b_ada, v_norm1, v_w_in, v_rnn_conv_w, v_rnn_conv_b, v_w_rg_a, v_b_rg_a, v_w_rg_i, v_b_rg_i, v_rg_lambda, v_w_o_rnn, v_w_o_attn, v_attn_sinks, v_rel_bias, v_w_out, v_norm2, v_w_up, v_ffn_conv_w, v_ffn_conv_b, v_w_down, v_norm_f):
    S, D = x.shape[1], x.shape[2]
    d_attn = N_CHIPS * w_o_attn.shape[1]
    d_rnn = N_CHIPS * w_o_rnn.shape[1]
    d_ff = N_CHIPS * w_down.shape[1]
    d_in = N_CHIPS * w_in.shape[2]
    n_heads = attn_sinks.shape[1]
    d_kv = (d_in - d_attn - 2 * d_rnn - 2 * D) // 2
    n_kv = d_kv // HEAD_DIM
    group = n_heads // n_kv
    nbk = rel_bias.shape[0]
    assert d_attn == n_heads * HEAD_DIM and group % 2 == 0 and S % BLOCK == 0

    mx, my, mc = lax.axis_index("x"), lax.axis_index("y"), lax.axis_index("c")
    k_me = 2 * mx + my
    dev = 2 * k_me + mc
    c_arr = jnp.reshape(mc, (1,)).astype(jnp.int32)
    k_arr = jnp.reshape(k_me, (1,)).astype(jnp.int32)
    kc_arr = jnp.stack([k_me, mc]).astype(jnp.int32)

    xs, tgt = x[0], loss_target[0]

    big_names = ["w_in", "w_o_rnn", "w_o_attn", "w_out", "w_up", "w_down"]
    big_w = dict(w_in=w_in[0], w_o_rnn=w_o_rnn[0], w_o_attn=w_o_attn[0], w_out=w_out[0], w_up=w_up[0], w_down=w_down[0])
    big_kind = dict(w_in="col", w_o_rnn="row", w_o_attn="row", w_out="row", w_up="col", w_down="row")
    specs = {k: _Big(big_kind[k], *big_w[k].shape) for k in big_names}
    placed = {k: _cast_into(specs[k], big_w[k], k_arr, name="cast_" + k) for k in big_names}
    later = big_names[1:]
    W = dict(w_in=_weight_gather([placed["w_in"]], [specs["w_in"]]).alone("gather_w_in")[0])
    c_all, cw4, cw3 = _all_gather([_placed(jnp.broadcast_to(c, (SUBLANES, D)), dev),
                                   _placed(jnp.pad(rnn_conv_w[0], ((0, SUBLANES - rnn_conv_w.shape[1]), (0, 0))), dev),
                                   _placed(jnp.pad(ffn_conv_w[0], ((0, SUBLANES - ffn_conv_w.shape[1]), (0, 0))), dev)]
                                  ).alone("gather_cond")
    c_all = c_all.reshape(N_DEV, SUBLANES, D)[:, 0]

    def from_chips(g, taps):
        cs = g.shape[1]
        g = g.reshape(N_CHIPS, 2, SUBLANES, cs)[:, 0, :taps]
        return jnp.transpose(g, (1, 0, 2)).reshape(taps, N_CHIPS * cs)

    conv4_w = from_chips(cw4, rnn_conv_w.shape[1])
    conv3_w = from_chips(cw3, ffn_conv_w.shape[1])

    (silu_c,) = _ew("silu_c", lambda v: (v * _sigmoid(v),), [(c_all, 0)], [F32], N_DEV, D)
    mod_sh = _mm(silu_c, w_ada[0], name="mod", exact=True, bias=lax.dynamic_slice_in_dim(b_ada, k_me * w_ada.shape[2], w_ada.shape[2], 1))
    (mod_g,) = _all_gather([_placed(mod_sh, dev)]).alone("gather_mod")
    mod_all = jnp.transpose(mod_g.reshape(N_CHIPS, 2, N_DEV, -1)[:, 0], (1, 0, 2)).reshape(N_DEV, 6 * D)
    mod = lax.dynamic_slice_in_dim(mod_all, dev, 1, 0)
    shift1, scale1, gate1, shift2, scale2, gate2 = [mod[:, i * D:(i + 1) * D] for i in range(6)]

    o_k, o_v, o_xr = d_attn, d_attn + d_kv, d_attn + 2 * d_kv
    wi = W["w_in"]

    spread = np.zeros((d_kv, n_kv * LANES), np.float32)
    for col in range(d_kv):
        spread[col, (col // HEAD_DIM) * LANES + col % HEAD_DIM] = 1.0
        spread[col, (col // HEAD_DIM) * LANES + HEAD_DIM + col % HEAD_DIM] = 1.0
    w_in_x = jnp.concatenate([wi[:, :o_k],
                              _mm(wi[:, o_k:o_v], jnp.asarray(spread, BF16), name="spread_k", out_dtype=BF16),
                              _mm(wi[:, o_v:o_xr], jnp.asarray(spread, BF16), name="spread_v", out_dtype=BF16),
                              wi[:, o_xr:]], axis=1)
    e_k = d_attn
    e_v = e_k + n_kv * LANES
    e_xr = e_v + n_kv * LANES
    e_gr = e_xr + d_rnn
    e_ga = e_gr + d_rnn
    e_gl = e_ga + D
    d_ext = e_gl + D

    u = _adaln_fwd(xs, norm1, scale1, shift1, name="adaln1")
    proj = _mm(u, w_in_x, name="proj", out_dtype=BF16)
    bias = _bias_table(rel_bias, n_heads, name="bias_table")
    sinks_b = jnp.broadcast_to(attn_sinks.reshape(n_heads, 1, 1), (n_heads, 1, LANES))
    o_attn, gathered = _attn_fwd(proj, bias, sinks_b, n_kv, group, 0, e_k, e_v, name="attn_fwd",
                                 ride=_weight_gather([placed[k] for k in later], [specs[k] for k in later]))
    W.update(zip(later, gathered))
    y_attn = _mm(o_attn, W["w_o_attn"], name="y_attn")
    xc = _conv_fwd(proj, e_xr, d_rnn, conv4_w, rnn_conv_b, name="conv4")
    a_t, inp = _gates_fwd(xc, w_rg_a[0], b_rg_a, w_rg_i[0], b_rg_i, rg_lambda, name="gates")
    h_rnn, h_prev = _scan_fwd(a_t, inp, name="scan")
    (z,) = _ew("rnn_gate", lambda h, g: (h * _gelu(g),), [(h_rnn, 0), (proj, e_gr)], [BF16], S, d_rnn)
    y_rnn = _mm(z, W["w_o_rnn"], name="y_rnn")
    (merged,) = _ew("merge", lambda ya, yr, ga, gl: (_sigmoid(ga) * ya + _sigmoid(gl) * yr,),
                    [(y_attn, 0), (y_rnn, 0), (proj, e_ga), (proj, e_gl)], [BF16], S, D)
    t1 = _mm(merged, W["w_out"], name="t1")
    u2, h1 = _adaln_fwd(xs, norm2, scale2, shift2, name="adaln2", t=t1, gate=gate1)
    up = _mm(u2, W["w_up"], name="up", out_dtype=BF16)
    a2, gelu_g, gelu_dg, conv_v = _geglu_fwd(up, conv3_w, ffn_conv_b, d_ff, name="geglu")
    t2 = _mm(a2, W["w_down"], name="t2")
    dh2, dt2, loss_tile, g_norm_f, d_gate2 = _final(h1, t2, gate2, norm_f.reshape(1, D), tgt, name="final")

    da2 = _mm(dt2, W["w_down"], name="da2", tb=True)
    g_w_down = _mm(a2, dt2, name="g_w_down", ta=True)
    dupg, dupv, g_c3g, g_c3v = _ffn_bwd(up, conv3_w, gelu_g, gelu_dg, conv_v, da2, d_ff, name="ffn_bwd")
    dup = jnp.concatenate([dupg, dupv], axis=1)
    g_conv3 = jnp.concatenate([g_c3g, g_c3v], axis=1)
    du2 = _mm(dup, W["w_up"], name="du2", tb=True)
    g_w_up = _mm(u2, dup, name="g_w_up", ta=True)
    dh1, d_shift2, d_scale2, g_norm2, dt1, d_gate1 = _adaln_bwd(h1, du2, dh2, norm2, scale2, name="adaln2_bwd", t=t1, gate=gate1)

    dmerged = _mm(dt1, W["w_out"], name="dmerged", tb=True)
    g_w_out = _mm(merged, dt1, name="g_w_out", ta=True)

    def merge_bwd(dm, ya, yr, ga, gl):
        sa, sl = _sigmoid(ga), _sigmoid(gl)
        return dm * sa, dm * sl, dm * ya * sa * (1.0 - sa), dm * yr * sl * (1.0 - sl)

    dy_attn, dy_rnn, d_ga, d_gl = _ew("merge_bwd", merge_bwd, [(dmerged, 0), (y_attn, 0), (y_rnn, 0), (proj, e_ga), (proj, e_gl)],
                                      [BF16, BF16, BF16, BF16], S, D)
    do = _mm(dy_attn, W["w_o_attn"], name="do", tb=True, out_dtype=BF16)
    g_w_o_attn = _mm(o_attn, dy_attn, name="g_w_o_attn", ta=True)
    dz = _mm(dy_rnn, W["w_o_rnn"], name="dz", tb=True)
    g_w_o_rnn = _mm(z, dy_rnn, name="g_w_o_rnn", ta=True)

    def rnn_gate_bwd(dzv, h, g):
        ge, dge = _gelu_and_grad(g)
        return dzv * ge, dzv * h * dge

    dh_rnn, d_gr = _ew("rnn_gate_bwd", rnn_gate_bwd, [(dz, 0), (h_rnn, 0), (proj, e_gr)], [F32, BF16], S, d_rnn)
    dacc = _scan_bwd(a_t, dh_rnn, name="scan_bwd")
    big_g = dict(w_o_rnn=g_w_o_rnn, w_o_attn=g_w_o_attn, w_out=g_w_out, w_up=g_w_up, w_down=g_w_down)
    (dxc, g_w_rg_a, g_w_rg_i, g_b_rg_a, g_b_rg_i, g_lam), landed = _gates_bwd(
        xc, dacc, h_prev, w_rg_a[0], b_rg_a, w_rg_i[0], b_rg_i, rg_lambda, name="gates_bwd",
        ride=_pair_exchange([big_g[k] for k in later], [specs[k] for k in later]))
    d_xr, g_conv4 = _conv_bwd(dxc, proj, e_xr, d_rnn, conv4_w, name="conv4_bwd")
    chipsums = {k: _pair_sum(specs[k], big_g[k], l, c_arr, name="pair_sum_" + k) for k, l in zip(later, landed)}
    (dq, dk, dv, dbias, dsink), landed2 = _attn_bwd(
        proj, o_attn, do, bias, sinks_b, n_kv, group, 0, e_k, e_v, name="attn_bwd",
        ride=_chip_exchange([chipsums[k] for k in later], [specs[k] for k in later]))
    landed2 = dict(zip(later, landed2))
    g_rel = _bias_table_bwd(dbias, nbk, name="bias_table_bwd")[:, :n_heads]
    g_sinks = dsink[:, 0, :group].reshape(1, n_heads)
    dproj = jnp.concatenate([dq, dk, dv, d_xr, d_gr, d_ga, d_gl], axis=1)

    early = [loss_tile[0:1, 0:1], g_conv4, g_b_rg_a, g_b_rg_i, g_lam, g_sinks, g_rel, g_norm2, g_conv3, g_norm_f,
             d_gate1, d_shift2, d_scale2, d_gate2]
    early_shapes = [a.shape for a in early]
    rg_rows = g_w_rg_a.shape[0] * LANES
    g_w_in_x, (early_all, rg_a_all, rg_i_all) = _mm(
        u, dproj, name="g_w_in", ta=True,
        ride=_all_gather([_placed(_pack(early), dev), _placed(g_w_rg_a.reshape(rg_rows, LANES), dev),
                          _placed(g_w_rg_i.reshape(rg_rows, LANES), dev)]))

    gather_m = jnp.asarray(spread.T, F32)
    g_w_in = jnp.concatenate([g_w_in_x[:, :e_k],
                              _mm(g_w_in_x[:, e_k:e_v], gather_m, name="gather_k", exact=True),
                              _mm(g_w_in_x[:, e_v:e_xr], gather_m, name="gather_v", exact=True),
                              g_w_in_x[:, e_xr:]], axis=1)
    big_g["w_in"] = g_w_in
    (landed_in,) = _pair_exchange([g_w_in], [specs["w_in"]]).alone("pair_exchange_w_in")
    chipsums["w_in"] = _pair_sum(specs["w_in"], g_w_in, landed_in, c_arr, name="pair_sum_w_in")
    du, (landed2["w_in"],) = _mm(dproj, w_in_x, name="du", tb=True, ride=_chip_exchange([chipsums["w_in"]], [specs["w_in"]]))
    grad_x, d_shift1, d_scale1, g_norm1 = _adaln_bwd(xs, du, dh1, norm1, scale1, name="adaln1_bwd")

    late = [d_shift1, d_scale1, g_norm1]
    late_shapes = [a.shape for a in late]
    (late_all,) = _all_gather([_placed(_pack(late), dev)]).alone("gather_late")
    early_all = early_all.reshape(N_DEV, -1, LANES)
    late_all = late_all.reshape(N_DEV, -1, LANES)
    (loss_s, g_conv4, g_b_rg_a, g_b_rg_i, g_lam, g_sinks, g_rel, g_norm2, g_conv3, g_norm_f,
     s_gate1, s_shift2, s_scale2, s_gate2) = _unpack(_sum_devices(early_all, name="sum_early"), early_shapes)
    s_shift1, s_scale1, g_norm1 = _unpack(_sum_devices(late_all, name="sum_late"), late_shapes)
    g_w_rg_a = _sum_devices(rg_a_all.reshape(N_DEV, rg_rows, LANES), name="sum_w_rg_a")
    g_w_rg_i = _sum_devices(rg_i_all.reshape(N_DEV, rg_rows, LANES), name="sum_w_rg_i")
    loss = loss_s.reshape(())
    g_b_ada = jnp.concatenate([s_shift1, s_scale1, s_gate1, s_shift2, s_scale2, s_gate2], axis=1)
    taps4, taps3 = rnn_conv_w.shape[1], ffn_conv_w.shape[1]
    g_conv4_w, g_conv4_b = g_conv4[:taps4], g_conv4[taps4:taps4 + 1]
    g_conv3_w, g_conv3_b = g_conv3[:taps3], g_conv3[taps3:taps3 + 1]

    def per_device(all_packs, shapes, pick):
        flat = all_packs.reshape(N_DEV, -1)
        offs = np.cumsum([0] + [int(np.prod(s)) for s in shapes])
        return [flat[:, offs[i]:offs[i + 1]] for i in pick]

    dmod_all = jnp.concatenate(per_device(late_all, late_shapes, [0, 1]) + per_device(early_all, early_shapes, [10, 11, 12, 13]),
                               axis=1)
    cs_ada = w_ada.shape[2]
    g_w_ada = _mm(silu_c, lax.dynamic_slice_in_dim(dmod_all, k_me * cs_ada, cs_ada, 1), name="g_w_ada", ta=True, exact=True)
    cs4, cs3 = rnn_conv_w.shape[2], ffn_conv_w.shape[2]
    g_conv4_sh = lax.dynamic_slice_in_dim(g_conv4_w, k_me * cs4, cs4, 1)
    g_conv3_sh = lax.dynamic_slice_in_dim(g_conv3_w, k_me * cs3, cs3, 1)

    halves = [_chip_sum(specs[k], chipsums[k], landed2[k], kc_arr, name="chip_sum_" + k) for k in big_names]
    share = _pair_share(halves, [specs[k] for k in big_names])

    (d_ada, m_ada, v_ada), shared = _adamw(w_ada[0], g_w_ada, m_w_ada[0], v_w_ada[0], name="adamw_w_ada", ride=share)
    shards = dict(zip(big_names, shared))
    grads = dict(w_ada=g_w_ada[None], b_ada=g_b_ada, norm1=g_norm1, w_in=shards["w_in"][None], rnn_conv_w=g_conv4_sh[None],
                 rnn_conv_b=g_conv4_b, w_rg_a=g_w_rg_a[None], b_rg_a=g_b_rg_a, w_rg_i=g_w_rg_i[None], b_rg_i=g_b_rg_i,
                 rg_lambda=g_lam, w_o_rnn=shards["w_o_rnn"][None], w_o_attn=shards["w_o_attn"][None], attn_sinks=g_sinks,
                 rel_bias=g_rel, w_out=shards["w_out"][None], norm2=g_norm2, w_up=shards["w_up"][None],
                 ffn_conv_w=g_conv3_sh[None], ffn_conv_b=g_conv3_b, w_down=shards["w_down"][None], norm_f=g_norm_f.reshape(D))
    weights = dict(w_ada=w_ada, b_ada=b_ada, norm1=norm1, w_in=w_in, rnn_conv_w=rnn_conv_w, rnn_conv_b=rnn_conv_b, w_rg_a=w_rg_a,
                   b_rg_a=b_rg_a, w_rg_i=w_rg_i, b_rg_i=b_rg_i, rg_lambda=rg_lambda, w_o_rnn=w_o_rnn, w_o_attn=w_o_attn,
                   attn_sinks=attn_sinks, rel_bias=rel_bias, w_out=w_out, norm2=norm2, w_up=w_up, ffn_conv_w=ffn_conv_w,
                   ffn_conv_b=ffn_conv_b, w_down=w_down, norm_f=norm_f)
    moms = dict(w_ada=(m_w_ada, v_w_ada), b_ada=(m_b_ada, v_b_ada), norm1=(m_norm1, v_norm1), w_in=(m_w_in, v_w_in),
                rnn_conv_w=(m_rnn_conv_w, v_rnn_conv_w), rnn_conv_b=(m_rnn_conv_b, v_rnn_conv_b), w_rg_a=(m_w_rg_a, v_w_rg_a),
                b_rg_a=(m_b_rg_a, v_b_rg_a), w_rg_i=(m_w_rg_i, v_w_rg_i), b_rg_i=(m_b_rg_i, v_b_rg_i),
                rg_lambda=(m_rg_lambda, v_rg_lambda), w_o_rnn=(m_w_o_rnn, v_w_o_rnn), w_o_attn=(m_w_o_attn, v_w_o_attn),
                attn_sinks=(m_attn_sinks, v_attn_sinks), rel_bias=(m_rel_bias, v_rel_bias), w_out=(m_w_out, v_w_out),
                norm2=(m_norm2, v_norm2), w_up=(m_w_up, v_w_up), ffn_conv_w=(m_ffn_conv_w, v_ffn_conv_w),
                ffn_conv_b=(m_ffn_conv_b, v_ffn_conv_b), w_down=(m_w_down, v_w_down), norm_f=(m_norm_f, v_norm_f))
    names = list(weights)
    grads = {k: grads[k].reshape(weights[k].shape) for k in names}
    large = ["w_ada"] + big_names + ["w_rg_a", "w_rg_i"]
    delta, new_m, new_v = dict(w_ada=d_ada[None]), dict(w_ada=m_ada[None]), dict(w_ada=v_ada[None])
    for k in large[1:]:
        shp = weights[k].shape
        two = lambda a: a.reshape(-1, shp[-1])
        d_, m_, v_ = _adamw(two(weights[k]), two(grads[k]), two(moms[k][0]), two(moms[k][1]), name="adamw_" + k)
        delta[k], new_m[k], new_v[k] = d_.reshape(shp), m_.reshape(shp), v_.reshape(shp)
    rest = [k for k in names if k not in large]
    rest_shapes = [weights[k].shape for k in rest]
    d_, m_, v_ = _adamw(_pack([weights[k] for k in rest]), _pack([grads[k] for k in rest]),
                        _pack([moms[k][0] for k in rest]), _pack([moms[k][1] for k in rest]), name="adamw_small")
    for k, dd, mm_, vv in zip(rest, _unpack(d_, rest_shapes), _unpack(m_, rest_shapes), _unpack(v_, rest_shapes)):
        delta[k], new_m[k], new_v[k] = dd, mm_, vv

    return (loss, grad_x[None], *[grads[k] for k in names], *[delta[k] for k in names],
            *[new_m[k] for k in names], *[new_v[k] for k in names])
```

```python
import math

import numpy as np
import jax
import jax.numpy as jnp
from jax import lax
from jax.experimental import pallas as pl
from jax.experimental.pallas import tpu as pltpu

F32 = jnp.float32
BF16 = jnp.bfloat16
MESH = pl.DeviceIdType.MESH
ANY = pl.BlockSpec(memory_space=pl.ANY)

EPS = 1e-6
NEG_INF = -1e30
HEAD_DIM = 64
BLOCK = 128
NUM_EXACT = 16
MAX_DISTANCE = 128
RG_C = 8.0
ADAM_LR, ADAM_B1, ADAM_B2, ADAM_EPS, ADAM_WD, ADAM_STEP = 0.001, 0.9, 0.999, 1e-08, 0.01, 10
N_CHIPS = 4
N_DEV = 8
SUBLANES = 8
LANES = 128
VMEM_LIMIT_BYTES = 56 * 1024 * 1024
GELU_C0 = math.sqrt(2.0 / math.pi)
GELU_C1 = 0.044715


def _pick(dim, pref, align):
    if dim <= pref:
        return dim
    t = (pref // align) * align
    while t >= align:
        if dim % t == 0:
            return t
        t -= align
    return dim


def _params(sem):
    return pltpu.CompilerParams(dimension_semantics=sem, vmem_limit_bytes=VMEM_LIMIT_BYTES)


def _call(name, body, grid, in_specs, out_specs, out_shape, scratch=(), nsp=0, sem=None, ride=None):
    if ride is not None:
        return _call_with_ride(name, body, grid, in_specs, out_specs, out_shape, scratch, ride)
    sem = sem or ("parallel",) * (len(grid) - 1) + ("arbitrary",)
    if nsp:
        gs = pltpu.PrefetchScalarGridSpec(num_scalar_prefetch=nsp, grid=grid, in_specs=in_specs,
                                          out_specs=out_specs, scratch_shapes=list(scratch))
        return pl.pallas_call(body, name=name, grid_spec=gs, out_shape=out_shape, compiler_params=_params(sem))
    return pl.pallas_call(body, name=name, grid=grid, in_specs=in_specs, out_specs=out_specs,
                          out_shape=out_shape, scratch_shapes=list(scratch), compiler_params=_params(sem))


def _sds(shape, dtype):
    return jax.ShapeDtypeStruct(shape, dtype)


class _Exchange:
    def __init__(self, ins, outs, n_sems, start, finish, in_place):
        self.ins, self.outs, self.n_sems = list(ins), list(outs), n_sems
        self.start, self.finish, self.in_place = start, finish, in_place

    def alone(self, name):
        n_in, n_out = len(self.ins), len(self.outs)

        def body(*refs):
            ins, outs = refs[:n_in], refs[n_in:n_in + n_out]
            send_sems, recv_sems = refs[n_in + n_out:]
            self.start(ins, outs, send_sems, recv_sems)
            self.finish(ins, outs, send_sems, recv_sems)

        sems = [pltpu.SemaphoreType.DMA((self.n_sems,)), pltpu.SemaphoreType.DMA((self.n_sems,))]
        aliases = {i: i for i in range(n_in)} if self.in_place else {}
        return pl.pallas_call(body, name=name, out_shape=self.outs, in_specs=[ANY] * n_in, out_specs=[ANY] * n_out,
                              scratch_shapes=sems, input_output_aliases=aliases)(*self.ins)


def _call_with_ride(name, body, grid, in_specs, out_specs, out_shape, scratch, ride):
    single = not isinstance(out_specs, (list, tuple))
    out_specs = [out_specs] if single else list(out_specs)
    out_shape = [out_shape] if single else list(out_shape)
    n_in, n_out, n_sc = len(in_specs), len(out_specs), len(scratch)
    r_in, r_out = len(ride.ins), len(ride.outs)

    def wrapped(*refs):
        pos = 0
        parts = []
        for n in (n_in, r_in, n_out, r_out, n_sc, 2):
            parts.append(refs[pos:pos + n])
            pos += n
        core_in, ride_in, core_out, ride_out, core_scratch, (send_sems, recv_sems) = parts
        first = pl.program_id(0) == 0
        last = pl.program_id(0) == grid[0] - 1
        for axis in range(1, len(grid)):
            first = jnp.logical_and(first, pl.program_id(axis) == 0)
            last = jnp.logical_and(last, pl.program_id(axis) == grid[axis] - 1)

        @pl.when(first)
        def _():
            ride.start(ride_in, ride_out, send_sems, recv_sems)

        body(*core_in, *core_out, *core_scratch)

        @pl.when(last)
        def _():
            ride.finish(ride_in, ride_out, send_sems, recv_sems)

    sems = [pltpu.SemaphoreType.DMA((ride.n_sems,)), pltpu.SemaphoreType.DMA((ride.n_sems,))]
    aliases = {n_in + k: n_out + k for k in range(r_in)} if ride.in_place else {}
    call = pl.pallas_call(wrapped, name=name, grid=grid, in_specs=list(in_specs) + [ANY] * r_in,
                          out_specs=out_specs + [ANY] * r_out, out_shape=out_shape + ride.outs,
                          scratch_shapes=list(scratch) + sems, input_output_aliases=aliases,
                          compiler_params=_params(("arbitrary",) * len(grid)))

    def run(*args):
        res = call(*args, *ride.ins)
        core = res[0] if single else list(res[:n_out])
        return core, list(res[n_out:])

    return run


def _T(tr, tc, off=0):
    return pl.BlockSpec((tr, tc), lambda j, i: (i, j + off))


def _P(rows, tc, off=0):
    return pl.BlockSpec((rows, tc), lambda j, i: (0, j + off))


def _gelu(x):
    t = jnp.tanh(GELU_C0 * (x + GELU_C1 * x * x * x))
    return 0.5 * x * (1.0 + t)


def _gelu_and_grad(x):
    t = jnp.tanh(GELU_C0 * (x + GELU_C1 * x * x * x))
    g = 0.5 * x * (1.0 + t)
    dg = 0.5 * (1.0 + t) + 0.5 * x * (1.0 - t * t) * GELU_C0 * (1.0 + 3.0 * GELU_C1 * x * x)
    return g, dg


def _sigmoid(x):
    return 1.0 / (1.0 + jnp.exp(-x))


def _accum(ref, val, first):
    @pl.when(first)
    def _():
        ref[...] = val

    @pl.when(jnp.logical_not(first))
    def _():
        ref[...] += val


def _colsum(v):
    return jnp.sum(v, axis=0, keepdims=True)


CHUNK = 16


def _chunks(n_rows, ch, step):
    def it(i, carry):
        step(pl.multiple_of(i * ch, ch))
        return carry

    lax.fori_loop(0, n_rows // ch, it, 0)


def _fold(v):
    return v[0:SUBLANES, :] + v[SUBLANES:2 * SUBLANES, :]


def _shift_down(xe, d, ch):
    return xe[SUBLANES:SUBLANES + ch, :] if d == 0 else pltpu.roll(xe, d, 0)[SUBLANES:SUBLANES + ch, :]


def _shift_up(xe, d, ch):
    return xe[0:ch, :] if d == 0 else pltpu.roll(xe, ch + SUBLANES - d, 0)[0:ch, :]


def _mm(a, b, *, name, ta=False, tb=False, out_dtype=F32, tm=1024, tn=1024, tk=4096, exact=False, bias=None, ride=None):
    if ta:
        K, M = a.shape
    else:
        M, K = a.shape
    if tb:
        N, K2 = b.shape
    else:
        K2, N = b.shape
    assert K == K2, (a.shape, b.shape, ta, tb)
    tm, tn, tk = _pick(M, tm, LANES), _pick(N, tn, LANES), _pick(K, tk, LANES)
    nk = K // tk
    cdt = F32 if exact else BF16
    prec = lax.Precision.HIGHEST if exact else None
    dims = (((0 if ta else 1,), (1 if tb else 0,)), ((), ()))

    def body(*refs):
        a_ref, b_ref = refs[0], refs[1]
        bias_ref = refs[2] if bias is not None else None
        o_ref = refs[3] if bias is not None else refs[2]
        part = lax.dot_general(a_ref[...].astype(cdt), b_ref[...].astype(cdt), dims,
                               preferred_element_type=F32, precision=prec)

        def finish(r):
            if bias is not None:
                r = r + bias_ref[...]
            o_ref[...] = r.astype(out_dtype)

        if nk == 1:
            finish(part)
            return
        acc_ref = refs[-1]
        k = pl.program_id(2)

        @pl.when(k == 0)
        def _():
            acc_ref[...] = part

        @pl.when(jnp.logical_and(k > 0, k < nk - 1))
        def _():
            acc_ref[...] += part

        @pl.when(k == nk - 1)
        def _():
            finish(acc_ref[...] + part)

    a_spec = pl.BlockSpec((tk, tm), lambda i, j, k: (k, i)) if ta else pl.BlockSpec((tm, tk), lambda i, j, k: (i, k))
    b_spec = pl.BlockSpec((tn, tk), lambda i, j, k: (j, k)) if tb else pl.BlockSpec((tk, tn), lambda i, j, k: (k, j))
    in_specs, args = [a_spec, b_spec], [a, b]
    if bias is not None:
        in_specs.append(pl.BlockSpec((1, tn), lambda i, j, k: (0, j)))
        args.append(bias)
    scratch = [pltpu.VMEM((tm, tn), F32)] if nk > 1 else []
    return _call(name, body, (M // tm, N // tn, nk), in_specs, pl.BlockSpec((tm, tn), lambda i, j, k: (i, j)),
                 _sds((M, N), out_dtype), scratch=scratch, sem=("parallel", "parallel", "arbitrary"), ride=ride)(*args)


def _ew(name, fn, ins, out_dtypes, S, C, rows=(), tr=256, tc=512):
    g = C
    for _, off in ins:
        g = math.gcd(g, off) if off else g
    tc = _pick(g, tc, LANES)
    tr = _pick(S, tr, 16)
    n_in, n_row, n_out = len(ins), len(rows), len(out_dtypes)

    def body(*refs):
        vals = [r[...].astype(F32) for r in refs[:n_in + n_row]]
        for o_ref, o in zip(refs[n_in + n_row:], fn(*vals)):
            o_ref[...] = o.astype(o_ref.dtype)

    in_specs = [_T(tr, tc, off // tc) for _, off in ins] + [_P(1, tc) for _ in rows]
    res = _call(name, body, (C // tc, S // tr), in_specs, [_T(tr, tc) for _ in out_dtypes],
                [_sds((S, C), d) for d in out_dtypes], sem=("parallel", "parallel"))(*[a for a, _ in ins], *rows)
    return res


def _adaln_fwd(x, norm, scale, shift, *, name, t=None, gate=None):
    S, D = x.shape
    tr = _pick(S, 256, 16)
    resid = t is not None

    def body(*refs):
        if resid:
            x_ref, t_ref, g_ref, n_ref, sc_ref, sh_ref, u_ref, h_ref = refs
            h = x_ref[...] + g_ref[...] * t_ref[...]
            h_ref[...] = h
        else:
            x_ref, n_ref, sc_ref, sh_ref, u_ref = refs
            h = x_ref[...]
        r = lax.rsqrt(jnp.mean(h * h, axis=-1, keepdims=True) + EPS)
        u_ref[...] = (h * r * (n_ref[...] * (1.0 + sc_ref[...])) + sh_ref[...]).astype(BF16)

    full, row = _T(tr, D), _P(1, D)
    if resid:
        return _call(name, body, (1, S // tr), [full, full, row, row, row, row], [full, full],
                     [_sds((S, D), BF16), _sds((S, D), F32)])(x, t, gate, norm, scale, shift)
    return _call(name, body, (1, S // tr), [full, row, row, row], full, _sds((S, D), BF16))(x, norm, scale, shift)


def _adaln_bwd(h, du, dres, norm, scale, *, name, t=None, gate=None):
    S, D = h.shape
    tr = _pick(S, 256, 16)
    gated = t is not None

    def body(*refs):
        if gated:
            h_ref, du_ref, dr_ref, n_ref, sc_ref, t_ref, g_ref, dh_ref, dsh_ref, dsc_ref, dn_ref, dt_ref, dg_ref = refs
        else:
            h_ref, du_ref, dr_ref, n_ref, sc_ref, dh_ref, dsh_ref, dsc_ref, dn_ref = refs
        first = pl.program_id(1) == 0
        hv, duv = h_ref[...], du_ref[...]
        r = lax.rsqrt(jnp.mean(hv * hv, axis=-1, keepdims=True) + EPS)
        xn = hv * r
        one_sc = 1.0 + sc_ref[...]
        dxn = duv * (n_ref[...] * one_sc)
        dh = dr_ref[...] + r * (dxn - xn * jnp.mean(dxn * xn, axis=-1, keepdims=True))
        dh_ref[...] = dh
        dux = duv * xn
        _accum(dsh_ref, _colsum(duv), first)
        _accum(dsc_ref, _colsum(dux * n_ref[...]), first)
        _accum(dn_ref, _colsum(dux * one_sc), first)
        if gated:
            dt_ref[...] = (dh * g_ref[...]).astype(BF16)
            _accum(dg_ref, _colsum(dh * t_ref[...]), first)

    full, row = _T(tr, D), _P(1, D)
    rowo = _sds((1, D), F32)
    if gated:
        return _call(name, body, (1, S // tr), [full, full, full, row, row, full, row],
                     [full, row, row, row, full, row],
                     [_sds((S, D), F32), rowo, rowo, rowo, _sds((S, D), BF16), rowo])(h, du, dres, norm, scale, t, gate)
    return _call(name, body, (1, S // tr), [full, full, full, row, row], [full, row, row, row],
                 [_sds((S, D), F32), rowo, rowo, rowo])(h, du, dres, norm, scale)


def _final(h1, t2, gate2, norm_f, tgt, *, name):
    S, D = h1.shape
    tr = _pick(S, 256, 16)

    def body(h_ref, t_ref, g_ref, n_ref, y_ref, dh_ref, dt_ref, loss_ref, dn_ref, dg_ref):
        first = pl.program_id(1) == 0
        tv = t_ref[...]
        h2 = h_ref[...] + g_ref[...] * tv
        r = lax.rsqrt(jnp.mean(h2 * h2, axis=-1, keepdims=True) + EPS)
        xn = h2 * r
        e = xn * n_ref[...] - y_ref[...]
        part = 0.5 * jnp.sum(jnp.mean(e * e, axis=-1, keepdims=True), axis=0, keepdims=True)
        _accum(loss_ref, jnp.broadcast_to(part, (SUBLANES, LANES)), first)
        dy = e * (1.0 / D)
        _accum(dn_ref, _colsum(dy * xn), first)
        dxn = dy * n_ref[...]
        dh2 = r * (dxn - xn * jnp.mean(dxn * xn, axis=-1, keepdims=True))
        dh_ref[...] = dh2
        dt_ref[...] = (dh2 * g_ref[...]).astype(BF16)
        _accum(dg_ref, _colsum(dh2 * tv), first)

    full, row = _T(tr, D), _P(1, D)
    rowo = _sds((1, D), F32)
    return _call(name, body, (1, S // tr), [full, full, row, row, full],
                 [full, full, _P(SUBLANES, LANES), row, row],
                 [_sds((S, D), F32), _sds((S, D), BF16), _sds((SUBLANES, LANES), F32), rowo, rowo])(h1, t2, gate2, norm_f, tgt)


HALO = 16


def _halo_prev(tr, tc, off=0):
    return pl.BlockSpec((HALO, tc), lambda j, i: (jnp.maximum(i * (tr // HALO) - 1, 0), j + off))


def _halo_next(tr, tc, n_slabs, off=0):
    return pl.BlockSpec((HALO, tc), lambda j, i: (jnp.minimum((i + 1) * (tr // HALO), n_slabs - 1), j + off))


def _fill_past(buf, prev_ref, cur_ref, first, tr):
    buf[0:HALO, :] = jnp.where(first, 0.0, prev_ref[...].astype(F32))
    buf[HALO:HALO + tr, :] = cur_ref[...].astype(F32)


def _fill_future(buf, cur_ref, next_ref, last, tr):
    buf[0:tr, :] = cur_ref[...].astype(F32)
    buf[tr:tr + HALO, :] = jnp.where(last, 0.0, next_ref[...].astype(F32))


def _al8(r):
    return r if isinstance(r, int) else pl.multiple_of(r, SUBLANES)


def _past_rows(buf, r0, taps):
    xe = buf[pl.ds(_al8(r0 + HALO - SUBLANES), CHUNK + SUBLANES), :]
    return [_shift_down(xe, d, CHUNK) for d in range(taps)]


def _future_rows(buf, r0, taps):
    xe = buf[pl.ds(_al8(r0), CHUNK + SUBLANES), :]
    return [_shift_up(xe, d, CHUNK) for d in range(taps)]


def _taps(w_ref):
    return [w_ref[k:k + 1, :] for k in range(w_ref.shape[0])]


def _conv(xs, w, b):
    taps = len(xs)
    y = w[taps - 1] * xs[0] if b is None else b + w[taps - 1] * xs[0]
    for d in range(1, taps):
        y = y + w[taps - 1 - d] * xs[d]
    return y


def _conv_fwd(x, off, C, w, b, *, name):
    S = x.shape[0]
    taps = w.shape[0]
    tc = _pick(math.gcd(C, off) if off else C, 512, LANES)
    tr = _pick(S, 256, 16)

    def body(x_ref, p_ref, w_ref, b_ref, y_ref, buf):
        _fill_past(buf, p_ref, x_ref, pl.program_id(1) == 0, tr)
        wt, bv = _taps(w_ref), b_ref[...]

        def step(r0):
            y_ref[pl.ds(r0, CHUNK), :] = _conv(_past_rows(buf, r0, taps), wt, bv)

        _chunks(tr, CHUNK, step)

    return _call(name, body, (C // tc, S // tr),
                 [_T(tr, tc, off // tc), _halo_prev(tr, tc, off // tc), _P(taps, tc), _P(1, tc)],
                 _T(tr, tc), _sds((S, C), F32), scratch=[pltpu.VMEM((tr + HALO, tc), F32)],
                 sem=("parallel", "parallel"))(x, x, w, b)


def _finish_conv_grads(acc, base, taps, dwb_ref, first):
    @pl.when(first)
    def _():
        dwb_ref[...] = jnp.zeros_like(dwb_ref)

    for d in range(taps):
        dwb_ref[taps - 1 - d:taps - d, :] += _colsum(acc[base + SUBLANES * d:base + SUBLANES * (d + 1), :])
    dwb_ref[taps:taps + 1, :] += _colsum(acc[base + SUBLANES * taps:base + SUBLANES * (taps + 1), :])


def _add_conv_grads(acc, base, dy, xs):
    taps = len(xs)
    for d in range(taps):
        acc[base + SUBLANES * d:base + SUBLANES * (d + 1), :] += _fold(dy * xs[d])
    acc[base + SUBLANES * taps:base + SUBLANES * (taps + 1), :] += _fold(dy)


def _conv_bwd(dy, x, off, C, w, *, name):
    S = dy.shape[0]
    taps = w.shape[0]
    assert taps < SUBLANES
    tc = _pick(math.gcd(C, off) if off else C, 512, LANES)
    tr = _pick(S, 256, 16)
    n_slabs = S // HALO

    def body(dy_ref, nx_ref, x_ref, p_ref, w_ref, dx_ref, dwb_ref, fbuf, pbuf, acc):
        i = pl.program_id(1)
        first = i == 0
        _fill_future(fbuf, dy_ref, nx_ref, i == S // tr - 1, tr)
        _fill_past(pbuf, p_ref, x_ref, first, tr)
        acc[...] = jnp.zeros_like(acc)
        wt = _taps(w_ref)

        def step(r0):
            dys = _future_rows(fbuf, r0, taps)
            dx_ref[pl.ds(r0, CHUNK), :] = _conv(dys, wt, None).astype(BF16)
            _add_conv_grads(acc, 0, dys[0], _past_rows(pbuf, r0, taps))

        _chunks(tr, CHUNK, step)
        _finish_conv_grads(acc, 0, taps, dwb_ref, first)

    return _call(name, body, (C // tc, S // tr),
                 [_T(tr, tc), _halo_next(tr, tc, n_slabs), _T(tr, tc, off // tc), _halo_prev(tr, tc, off // tc),
                  _P(taps, tc)],
                 [_T(tr, tc), _P(SUBLANES, tc)],
                 [_sds((S, C), BF16), _sds((SUBLANES, C), F32)],
                 scratch=[pltpu.VMEM((tr + HALO, tc), F32), pltpu.VMEM((tr + HALO, tc), F32),
                          pltpu.VMEM((SUBLANES * (taps + 1), tc), F32)])(dy, dy, x, x, w)


def _geglu_fwd(up, w, b, F, *, name):
    S = up.shape[0]
    taps = w.shape[0]
    tc = _pick(F, 512, LANES)
    tr = _pick(S, 256, 16)
    nf = F // tc

    def body(g_ref, gp_ref, v_ref, vp_ref, wg_ref, wv_ref, bg_ref, bv_ref, a_ref, ge_ref, dge_ref, vo_ref, gbuf, vbuf):
        first = pl.program_id(1) == 0
        _fill_past(gbuf, gp_ref, g_ref, first, tr)
        _fill_past(vbuf, vp_ref, v_ref, first, tr)
        wg, wv, bg, bv = _taps(wg_ref), _taps(wv_ref), bg_ref[...], bv_ref[...]

        def step(r0):
            rows = pl.ds(r0, CHUNK)
            ge, dge = _gelu_and_grad(_conv(_past_rows(gbuf, r0, taps), wg, bg))
            v = _conv(_past_rows(vbuf, r0, taps), wv, bv)
            a_ref[rows, :] = (ge * v).astype(BF16)
            ge_ref[rows, :] = ge.astype(BF16)
            dge_ref[rows, :] = dge.astype(BF16)
            vo_ref[rows, :] = v.astype(BF16)

        _chunks(tr, CHUNK, step)

    buf = pltpu.VMEM((tr + HALO, tc), F32)
    t, o = _T(tr, tc), _sds((S, F), BF16)
    return _call(name, body, (nf, S // tr),
                 [t, _halo_prev(tr, tc), _T(tr, tc, nf), _halo_prev(tr, tc, nf),
                  _P(taps, tc), _P(taps, tc, nf), _P(1, tc), _P(1, tc, nf)],
                 [t, t, t, t], [o, o, o, o], scratch=[buf, buf], sem=("parallel", "parallel"))(up, up, up, up, w, w, b, b)


def _ffn_bwd(up, w, ge, dge, vv, da, F, *, name):
    S = up.shape[0]
    taps = w.shape[0]
    tc = _pick(F, 512, LANES)
    tr = _pick(S, 256, 16)
    nf, nr = F // tc, S // tr
    ext = tr + HALO
    n_slabs = S // HALO

    assert HALO == CHUNK
    v_base = SUBLANES * (taps + 1)

    def body(g_ref, gp_ref, v_ref, vp_ref, wg_ref, wv_ref, ge_ref, gen_ref, dge_ref, dgen_ref, vo_ref, von_ref,
             da_ref, dan_ref, dxg_ref, dxv_ref, dwbg_ref, dwbv_ref, gbuf, vbuf, dgbuf, dvbuf, acc):
        i = pl.program_id(1)
        first, last = i == 0, i == nr - 1
        _fill_past(gbuf, gp_ref, g_ref, first, tr)
        _fill_past(vbuf, vp_ref, v_ref, first, tr)
        acc[...] = jnp.zeros_like(acc)
        wg, wv = _taps(wg_ref), _taps(wv_ref)

        def grads(r0, da, gev, dgev, vov, in_tile):
            dg = da * vov.astype(F32) * dgev.astype(F32)
            dv = da * gev.astype(F32)
            dgbuf[pl.ds(_al8(r0), CHUNK), :] = dg
            dvbuf[pl.ds(_al8(r0), CHUNK), :] = dv
            if in_tile:
                _add_conv_grads(acc, 0, dg, _past_rows(gbuf, r0, taps))
                _add_conv_grads(acc, v_base, dv, _past_rows(vbuf, r0, taps))

        def in_tile(r0):
            rows = pl.ds(r0, CHUNK)
            grads(r0, da_ref[rows, :], ge_ref[rows, :], dge_ref[rows, :], vo_ref[rows, :], True)

        _chunks(tr, CHUNK, in_tile)
        grads(tr, jnp.where(last, 0.0, dan_ref[...]), gen_ref[...], dgen_ref[...], von_ref[...], False)

        def back(r0):
            dxg_ref[pl.ds(r0, CHUNK), :] = _conv(_future_rows(dgbuf, r0, taps), wg, None).astype(BF16)
            dxv_ref[pl.ds(r0, CHUNK), :] = _conv(_future_rows(dvbuf, r0, taps), wv, None).astype(BF16)

        _chunks(tr, CHUNK, back)
        _finish_conv_grads(acc, 0, taps, dwbg_ref, first)
        _finish_conv_grads(acc, v_base, taps, dwbv_ref, first)

    xbuf = pltpu.VMEM((HALO + tr, tc), F32)
    dbuf = pltpu.VMEM((ext, tc), F32)
    t, tw, t8, nx = _T(tr, tc), _P(taps, tc), _P(SUBLANES, tc), _halo_next(tr, tc, n_slabs)
    return _call(name, body, (nf, nr),
                 [t, _halo_prev(tr, tc), _T(tr, tc, nf), _halo_prev(tr, tc, nf), tw, _P(taps, tc, nf),
                  t, nx, t, nx, t, nx, t, nx],
                 [t, t, t8, t8],
                 [_sds((S, F), BF16), _sds((S, F), BF16), _sds((SUBLANES, F), F32), _sds((SUBLANES, F), F32)],
                 scratch=[xbuf, xbuf, dbuf, dbuf, pltpu.VMEM((2 * v_base, tc), F32)])(
                     up, up, up, up, w, w, ge, ge, dge, dge, vv, vv, da, da)


def _softplus_neg(lam):
    z = -lam
    e = jnp.exp(-jnp.abs(z))
    u = 1.0 + e
    log1p_e = jnp.where(u == 1.0, e, jnp.log(u) * e / jnp.where(u == 1.0, 1.0, u - 1.0))
    sp = jnp.maximum(z, 0.0) + log1p_e
    sg = jnp.where(z >= 0, 1.0 / u, e / u)
    return sp, sg


def _one_minus_exp(x):
    series = -x * (1.0 + x * (0.5 + x * (1.0 / 6.0 + x * (1.0 / 24.0))))
    return jnp.where(x > -0.01, series, 1.0 - jnp.exp(x))


def _gate_values(xc, wa_ref, wi_ref, ba_ref, bi_ref, lam_ref, is_t0):
    xb = xc.astype(BF16)
    ra = _sigmoid(jnp.dot(xb, wa_ref[0].astype(BF16), preferred_element_type=F32) + ba_ref[...])
    ia = _sigmoid(jnp.dot(xb, wi_ref[0].astype(BF16), preferred_element_type=F32) + bi_ref[...])
    sp, sg = _softplus_neg(lam_ref[...])
    log_a = -RG_C * ra * sp
    a = jnp.exp(log_a)
    mult = jnp.where(is_t0, 1.0, jnp.sqrt(_one_minus_exp(2.0 * log_a)))
    return ra, ia, sp, sg, a, mult


def _rnn_blockspecs(tr):
    x = pl.BlockSpec((tr, LANES), lambda n, i: (i, n))
    w = pl.BlockSpec((1, LANES, LANES), lambda n, i: (n, 0, 0))
    p = pl.BlockSpec((1, LANES), lambda n, i: (0, n))
    return x, w, p


def _is_t0(tr):
    rows = lax.broadcasted_iota(jnp.int32, (tr, LANES), 0)
    return jnp.logical_and(pl.program_id(1) == 0, rows == 0)


def _gates_fwd(xc, w_a, b_a, w_i, b_i, lam, *, name):
    S, C = xc.shape
    tr = _pick(S, 1024, 16)

    def body(x_ref, wa_ref, wi_ref, ba_ref, bi_ref, lam_ref, a_ref, inp_ref):
        xv = x_ref[...]
        _, ia, _, _, a, mult = _gate_values(xv, wa_ref, wi_ref, ba_ref, bi_ref, lam_ref, _is_t0(tr))
        a_ref[...] = a
        inp_ref[...] = mult * ia * xv

    x, w, p = _rnn_blockspecs(tr)
    return _call(name, body, (C // LANES, S // tr), [x, w, w, p, p, p], [x, x],
                 [_sds((S, C), F32), _sds((S, C), F32)], sem=("parallel", "parallel"))(xc, w_a, w_i, b_a, b_i, lam)


def _gates_bwd(xc, dacc, hprev, w_a, b_a, w_i, b_i, lam, *, name, ride=None):
    S, C = xc.shape
    nb = C // LANES
    tr = _pick(S, 1024, 16)

    def body(x_ref, d_ref, hp_ref, wa_ref, wi_ref, ba_ref, bi_ref, lam_ref,
             dx_ref, dwa_ref, dwi_ref, dba_ref, dbi_ref, dlam_ref):
        first = pl.program_id(1) == 0
        t0 = _is_t0(tr)
        xv, dv = x_ref[...], d_ref[...]
        ra, ia, sp, sg, a, mult = _gate_values(xv, wa_ref, wi_ref, ba_ref, bi_ref, lam_ref, t0)
        d_a = dv * hp_ref[...]
        d_mult = dv * ia * xv
        d_ia = dv * mult * xv
        d_log = d_a * a + jnp.where(t0, 0.0, -d_mult * (a * a) / mult)
        d_pa = d_log * (-RG_C * sp) * ra * (1.0 - ra)
        d_pi = d_ia * ia * (1.0 - ia)
        xb, dab, dib = xv.astype(BF16), d_pa.astype(BF16), d_pi.astype(BF16)
        nt = (((1,), (1,)), ((), ()))
        tn = (((0,), (0,)), ((), ()))
        dx_ref[...] = (dv * mult * ia
                       + lax.dot_general(dab, wa_ref[0].astype(BF16), nt, preferred_element_type=F32)
                       + lax.dot_general(dib, wi_ref[0].astype(BF16), nt, preferred_element_type=F32))
        _accum(dwa_ref, lax.dot_general(xb, dab, tn, preferred_element_type=F32)[None], first)
        _accum(dwi_ref, lax.dot_general(xb, dib, tn, preferred_element_type=F32)[None], first)
        _accum(dba_ref, _colsum(d_pa), first)
        _accum(dbi_ref, _colsum(d_pi), first)
        _accum(dlam_ref, _colsum(d_log * (-RG_C) * ra) * (-sg), first)

    x, w, p = _rnn_blockspecs(tr)
    row = _sds((1, C), F32)
    return _call(name, body, (nb, S // tr), [x, x, x, w, w, p, p, p], [x, w, w, p, p, p],
                 [_sds((S, C), F32), _sds((nb, LANES, LANES), F32), _sds((nb, LANES, LANES), F32), row, row, row],
                 ride=ride)(xc, dacc, hprev, w_a, w_i, b_a, b_i, lam)


def _scan_fwd(a, inp, *, name):
    S, C = a.shape
    tc = _pick(C, 1280, LANES)
    tr = _pick(S, 256, SUBLANES)

    def body(a_ref, b_ref, h_ref, hp_ref, carry):
        @pl.when(pl.program_id(1) == 0)
        def _():
            carry[...] = jnp.zeros_like(carry)

        rows = lax.broadcasted_iota(jnp.int32, (SUBLANES, tc), 0)

        def slab(s, h):
            base = pl.multiple_of(s * SUBLANES, SUBLANES)
            av, bv = a_ref[pl.ds(base, SUBLANES), :], b_ref[pl.ds(base, SUBLANES), :]
            ho = jnp.zeros((SUBLANES, tc), F32)
            po = jnp.zeros((SUBLANES, tc), F32)
            for r in range(SUBLANES):
                po = jnp.where(rows == r, h, po)
                h = av[r:r + 1, :] * h + bv[r:r + 1, :]
                ho = jnp.where(rows == r, h, ho)
            h_ref[pl.ds(base, SUBLANES), :] = ho
            hp_ref[pl.ds(base, SUBLANES), :] = po
            return h

        carry[...] = lax.fori_loop(0, tr // SUBLANES, slab, carry[...])

    t = _T(tr, tc)
    return _call(name, body, (C // tc, S // tr), [t, t], [t, t], [_sds((S, C), F32), _sds((S, C), F32)],
                 scratch=[pltpu.VMEM((1, tc), F32)])(a, inp)


def _scan_bwd(a, dh, *, name):
    S, C = a.shape
    tc = _pick(C, 1280, LANES)
    tr = _pick(S, 256, SUBLANES)
    nr = S // tr

    def body(a_ref, d_ref, o_ref, carry):
        @pl.when(pl.program_id(1) == 0)
        def _():
            carry[...] = jnp.zeros_like(carry)

        rows = lax.broadcasted_iota(jnp.int32, (SUBLANES, tc), 0)
        n_slabs = tr // SUBLANES

        def slab(s, g):
            base = pl.multiple_of((n_slabs - 1 - s) * SUBLANES, SUBLANES)
            av, dv = a_ref[pl.ds(base, SUBLANES), :], d_ref[pl.ds(base, SUBLANES), :]
            out = jnp.zeros((SUBLANES, tc), F32)
            for r in range(SUBLANES - 1, -1, -1):
                acc = dv[r:r + 1, :] + g
                out = jnp.where(rows == r, acc, out)
                g = av[r:r + 1, :] * acc
            o_ref[pl.ds(base, SUBLANES), :] = out
            return g

        carry[...] = lax.fori_loop(0, n_slabs, slab, carry[...])

    t = pl.BlockSpec((tr, tc), lambda j, i: (nr - 1 - i, j))
    return _call(name, body, (C // tc, nr), [t, t], t, _sds((S, C), F32), scratch=[pltpu.VMEM((1, tc), F32)])(a, dh)


SCALE = HEAD_DIM ** -0.5


def _bucket_table(num_buckets):
    qi = np.arange(BLOCK)[:, None]
    j = np.arange(BLOCK)[None, :]
    dist = np.where(j <= qi, qi - j, qi + BLOCK - j)
    d = np.maximum(dist, 1).astype(np.float64)
    large = NUM_EXACT + (np.log(d / NUM_EXACT) / math.log(MAX_DISTANCE / NUM_EXACT) * (num_buckets - NUM_EXACT)).astype(np.int32)
    large = np.minimum(large, num_buckets - 1)
    return np.where(dist < NUM_EXACT, dist, large).astype(np.int32)


def _bias_table(rel_bias, n_heads, *, name):
    nbk = rel_bias.shape[0]
    bucket = jnp.asarray(_bucket_table(nbk))

    def body(rb_ref, bk_ref, o_ref):
        t, h = pl.program_id(0), pl.program_id(1)
        bk = bk_ref[...]
        acc = jnp.zeros((BLOCK, BLOCK), F32)
        for b in range(nbk):
            acc = jnp.where(bk == b, rb_ref[b, h], acc)
        o_ref[0, 0] = jnp.where(jnp.logical_and(t == 0, jnp.logical_not(_own_block())), NEG_INF, acc)

    return _call(name, body, (2, n_heads),
                 [pl.BlockSpec(memory_space=pltpu.SMEM), pl.BlockSpec((BLOCK, BLOCK), lambda t, h: (0, 0))],
                 pl.BlockSpec((1, 1, BLOCK, BLOCK), lambda t, h: (t, h, 0, 0)),
                 _sds((2, n_heads, BLOCK, BLOCK), F32), sem=("arbitrary", "arbitrary"))(rel_bias, bucket)


def _bias_table_bwd(dbias, nbk, *, name):
    n_heads = dbias.shape[0]
    bucket = jnp.asarray(_bucket_table(nbk))

    def body(db_ref, bk_ref, o_ref):
        h = pl.program_id(0)
        bk = bk_ref[...]
        dv = db_ref[0]
        rows = lax.broadcasted_iota(jnp.int32, (nbk, LANES), 0)
        lanes = lax.broadcasted_iota(jnp.int32, (nbk, LANES), 1)
        acc = jnp.zeros((nbk, LANES), F32)
        for b in range(nbk):
            s = jnp.sum(jnp.sum(jnp.where(bk == b, dv, 0.0), axis=0, keepdims=True), axis=1, keepdims=True)
            acc = jnp.where(jnp.logical_and(rows == b, lanes == h), s, acc)
        _accum(o_ref, acc, h == 0)

    return _call(name, body, (n_heads,),
                 [pl.BlockSpec((1, BLOCK, BLOCK), lambda h: (h, 0, 0)), pl.BlockSpec((BLOCK, BLOCK), lambda h: (0, 0))],
                 pl.BlockSpec((nbk, LANES), lambda h: (0, 0)), _sds((nbk, LANES), F32), sem=("arbitrary",))(dbias, bucket)


def _own_block(heads=1):
    qi = lax.broadcasted_iota(jnp.int32, (heads * BLOCK, BLOCK), 0) % BLOCK
    j = lax.broadcasted_iota(jnp.int32, (heads * BLOCK, BLOCK), 1)
    return j <= qi


def _to_window(band, own):
    return jnp.where(own, band[:, BLOCK:], band[:, :BLOCK])


def _to_band(win, own):
    return jnp.concatenate([jnp.where(own, 0.0, win), jnp.where(own, win, 0.0)], axis=1)


def _stack_pairs(ref, group):
    return jnp.concatenate([ref[:, (g // 2) * LANES:(g // 2 + 1) * LANES] for g in range(group)], axis=0).astype(F32)


def _own_lanes(group):
    rows = lax.broadcasted_iota(jnp.int32, (group * BLOCK, LANES), 0)
    lanes = lax.broadcasted_iota(jnp.int32, (group * BLOCK, LANES), 1)
    return (lanes < HEAD_DIM) == ((rows // BLOCK) % 2 == 0)


def _unstack_pairs(stacked, o_ref, group):
    lo = lax.broadcasted_iota(jnp.int32, (BLOCK, LANES), 1) < HEAD_DIM
    for p in range(group // 2):
        even = stacked[2 * p * BLOCK:(2 * p + 1) * BLOCK, :]
        odd = stacked[(2 * p + 1) * BLOCK:(2 * p + 2) * BLOCK, :]
        o_ref[:, p * LANES:(p + 1) * LANES] = jnp.where(lo, even, odd).astype(o_ref.dtype)


def _group_probs(qm, kb, b_ref, s_ref, own, group):
    band = lax.dot_general(qm, kb, (((1,), (1,)), ((), ())), preferred_element_type=F32)
    bias = jnp.concatenate([b_ref[0, g] for g in range(group)], axis=0)
    sink = jnp.concatenate([jnp.broadcast_to(s_ref[g][:, 0:1], (BLOCK, 1)) for g in range(group)], axis=0)
    s = _to_window(band, own) + bias
    m = jnp.maximum(jnp.max(s, axis=-1, keepdims=True), sink)
    p = jnp.exp(s - m)
    es = jnp.exp(sink - m)
    inv = 1.0 / (jnp.sum(p, axis=-1, keepdims=True) + es)
    return p, inv, es


def _attn_fwd(proj, bias, sinks_b, n_kv, group, q_off, k_off, v_off, *, name, ride=None):
    S = proj.shape[0]
    nblk = S // BLOCK
    gw = group * HEAD_DIM
    pairs = group // 2

    def body(q_ref, kc_ref, kp_ref, vc_ref, vp_ref, b_ref, s_ref, o_ref):
        kb = jnp.concatenate([kp_ref[...], kc_ref[...]], axis=0).astype(BF16)
        vb = jnp.concatenate([vp_ref[...], vc_ref[...]], axis=0).astype(BF16)
        own = _own_block(group)
        qm = jnp.where(_own_lanes(group), _stack_pairs(q_ref, group) * SCALE, 0.0).astype(BF16)
        pu, inv, _ = _group_probs(qm, kb, b_ref, s_ref, own, group)
        o = jnp.dot(_to_band(pu, own).astype(BF16), vb, preferred_element_type=F32) * inv
        _unstack_pairs(o, o_ref, group)

    qb, kb0, vb0 = q_off // gw, k_off // LANES, v_off // LANES
    cur = lambda b0: pl.BlockSpec((BLOCK, LANES), lambda j, n: (n, b0 + j))
    prev = lambda b0: pl.BlockSpec((BLOCK, LANES), lambda j, n: (jnp.maximum(n - 1, 0), b0 + j))
    return _call(name, body, (n_kv, nblk),
                 [pl.BlockSpec((BLOCK, gw), lambda j, n: (n, qb + j)), cur(kb0), prev(kb0), cur(vb0), prev(vb0),
                  pl.BlockSpec((1, group, BLOCK, BLOCK), lambda j, n: (jnp.minimum(n, 1), j, 0, 0)),
                  pl.BlockSpec((group, 1, LANES), lambda j, n: (j, 0, 0))],
                 pl.BlockSpec((BLOCK, gw), lambda j, n: (n, j)), _sds((S, n_kv * gw), BF16),
                 sem=("parallel", "parallel"), ride=ride)(proj, proj, proj, proj, proj, bias, sinks_b)


def _attn_bwd(proj, o, do, bias, sinks_b, n_kv, group, q_off, k_off, v_off, *, name, ride=None):
    S = proj.shape[0]
    nblk = S // BLOCK
    gw = group * HEAD_DIM
    rows_all = group * BLOCK
    nt = (((1,), (1,)), ((), ()))
    tn = (((0,), (0,)), ((), ()))

    def body(q_ref, kc_ref, kp_ref, vc_ref, vp_ref, b_ref, s_ref, o_ref, do_ref,
             dq_ref, dk_ref, dv_ref, db_ref, ds_ref, ck, cv, sacc):
        n = pl.program_id(1)
        lo2 = lax.broadcasted_iota(jnp.int32, (2 * BLOCK, LANES), 1) < HEAD_DIM

        @pl.when(n == 0)
        def _():
            ck[...] = jnp.zeros_like(ck)
            cv[...] = jnp.zeros_like(cv)
            sacc[...] = jnp.zeros_like(sacc)
            db_ref[...] = jnp.zeros_like(db_ref)

        @pl.when(n < nblk)
        def _():
            kb = jnp.concatenate([kp_ref[...], kc_ref[...]], axis=0).astype(BF16)
            vb = jnp.concatenate([vp_ref[...], vc_ref[...]], axis=0).astype(BF16)
            own = _own_block(group)
            mine = _own_lanes(group)
            qm = jnp.where(mine, _stack_pairs(q_ref, group) * SCALE, 0.0).astype(BF16)
            dom = jnp.where(mine, _stack_pairs(do_ref, group), 0.0)
            pu, inv, es = _group_probs(qm, kb, b_ref, s_ref, own, group)
            pr = pu * inv
            delta = jnp.sum(dom * _stack_pairs(o_ref, group), axis=-1, keepdims=True)
            domb = dom.astype(BF16)
            dp = _to_window(lax.dot_general(domb, vb, nt, preferred_element_type=F32), own)
            ds = pr * (dp - delta)
            for g in range(group):
                db_ref[g] += ds[g * BLOCK:(g + 1) * BLOCK, :]
            sacc[...] += -(es * inv) * delta
            dsb = _to_band(ds, own).astype(BF16)
            _unstack_pairs(jnp.dot(dsb, kb, preferred_element_type=F32) * SCALE, dq_ref, group)
            dk_acc = lax.dot_general(dsb, qm, tn, preferred_element_type=F32)
            dv_acc = lax.dot_general(_to_band(pr, own).astype(BF16), domb, tn, preferred_element_type=F32)
            dkf = jnp.where(lo2, dk_acc + pltpu.roll(dk_acc, HEAD_DIM, 1), 0.0)
            dvf = jnp.where(lo2, dv_acc + pltpu.roll(dv_acc, HEAD_DIM, 1), 0.0)

            @pl.when(n > 0)
            def _():
                dk_ref[...] = (ck[...] + dkf[0:BLOCK, :]).astype(BF16)
                dv_ref[...] = (cv[...] + dvf[0:BLOCK, :]).astype(BF16)

            ck[...] = dkf[BLOCK:2 * BLOCK, :]
            cv[...] = dvf[BLOCK:2 * BLOCK, :]

        @pl.when(n == nblk)
        def _():
            dk_ref[...] = ck[...].astype(BF16)
            dv_ref[...] = cv[...].astype(BF16)
            lane = lax.broadcasted_iota(jnp.int32, (1, LANES), 1)
            tot = jnp.zeros((1, LANES), F32)
            for g in range(group):
                tot = jnp.where(lane == g, _colsum(sacc[g * BLOCK:(g + 1) * BLOCK, :]), tot)
            ds_ref[0] = tot

    qb, kb0, vb0 = q_off // gw, k_off // LANES, v_off // LANES
    last = nblk - 1
    cur = lambda b0: pl.BlockSpec((BLOCK, LANES), lambda j, n: (jnp.minimum(n, last), b0 + j))
    prev = lambda b0: pl.BlockSpec((BLOCK, LANES), lambda j, n: (jnp.clip(n - 1, 0, last), b0 + j))
    qspec = lambda b0: pl.BlockSpec((BLOCK, gw), lambda j, n: (jnp.minimum(n, last), b0 + j))
    kvout = pl.BlockSpec((BLOCK, LANES), lambda j, n: (jnp.maximum(n - 1, 0), j))
    return _call(name, body, (n_kv, nblk + 1),
                 [qspec(qb), cur(kb0), prev(kb0), cur(vb0), prev(vb0),
                  pl.BlockSpec((1, group, BLOCK, BLOCK), lambda j, n: (jnp.minimum(n, 1), j, 0, 0)),
                  pl.BlockSpec((group, 1, LANES), lambda j, n: (j, 0, 0)), qspec(0), qspec(0)],
                 [qspec(0), kvout, kvout, pl.BlockSpec((group, BLOCK, BLOCK), lambda j, n: (j, 0, 0)),
                  pl.BlockSpec((1, 1, LANES), lambda j, n: (j, 0, 0))],
                 [_sds((S, n_kv * gw), BF16), _sds((S, n_kv * LANES), BF16), _sds((S, n_kv * LANES), BF16),
                  _sds((n_kv * group, BLOCK, BLOCK), F32), _sds((n_kv, 1, LANES), F32)],
                 scratch=[pltpu.VMEM((BLOCK, LANES), F32), pltpu.VMEM((BLOCK, LANES), F32),
                          pltpu.VMEM((rows_all, 1), F32)], ride=ride)(proj, proj, proj, proj, proj, bias, sinks_b, o, do)


def _adamw(w, g, m, v, *, name, ride=None):
    R, C = w.shape
    tr = _pick(R, 128, SUBLANES)
    bc1 = 1.0 - ADAM_B1 ** ADAM_STEP
    bc2 = 1.0 - ADAM_B2 ** ADAM_STEP

    def body(w_ref, g_ref, m_ref, v_ref, d_ref, nm_ref, nv_ref):
        gv = g_ref[...]
        nm = ADAM_B1 * m_ref[...] + (1.0 - ADAM_B1) * gv
        nv = ADAM_B2 * v_ref[...] + (1.0 - ADAM_B2) * (gv * gv)
        d_ref[...] = -ADAM_LR * ((nm / bc1) / (jnp.sqrt(nv / bc2) + ADAM_EPS) + ADAM_WD * w_ref[...])
        nm_ref[...] = nm
        nv_ref[...] = nv

    t = pl.BlockSpec((tr, C), lambda i: (i, 0))
    o = _sds((R, C), F32)
    return _call(name, body, (R // tr,), [t, t, t, t], [t, t, t], [o, o, o], sem=("parallel",), ride=ride)(w, g, m, v)


def _sum_devices(packs, *, name):
    _, R, C = packs.shape
    tr = _pick(R, 512, SUBLANES)

    def body(p_ref, o_ref):
        acc = p_ref[0]
        for d in range(1, N_DEV):
            acc = acc + p_ref[d]
        o_ref[...] = acc

    return _call(name, body, (R // tr,), [pl.BlockSpec((N_DEV, tr, C), lambda i: (0, i, 0))],
                 pl.BlockSpec((tr, C), lambda i: (i, 0)), _sds((R, C), F32), sem=("parallel",))(packs)


class _Big:
    def __init__(self, kind, R, C):
        self.kind, self.R, self.C = kind, R, C
        self.hr = R // 2
        self.full = (R, N_CHIPS * C) if kind == "col" else (N_CHIPS * R, C)

    def region(self, ref, k, c):
        if self.kind == "col":
            return ref.at[pl.ds(c * self.hr, self.hr), pl.ds(k * self.C, self.C)]
        return ref.at[pl.ds(k * self.R + c * self.hr, self.hr), :]


def _pair_sum(spec, g_full, land, c_arr, *, name):
    hr, C = spec.hr, spec.C
    tr = _pick(hr, 128, 16)
    nr = hr // tr

    def body(c_ref, g_ref, l_ref, o_ref):
        o_ref[0] = (g_ref[...] + l_ref[0]).astype(BF16)

    if spec.kind == "col":
        gspec = pl.BlockSpec((tr, C), lambda k, i, c_ref: (c_ref[0] * nr + i, k))
    else:
        gspec = pl.BlockSpec((tr, C), lambda k, i, c_ref: (k * 2 * nr + c_ref[0] * nr + i, 0))
    lspec = pl.BlockSpec((1, tr, C), lambda k, i, c_ref: (k, i, 0))
    return _call(name, body, (N_CHIPS, nr), [gspec, lspec], lspec, _sds((N_CHIPS, hr, C), BF16), nsp=1,
                 sem=("parallel", "parallel"))(c_arr, g_full, land)


def _chip_sum(spec, chipsum, land, kc_arr, *, name):
    hr, C = spec.hr, spec.C
    tr = _pick(hr, 128, 16)
    nr = hr // tr

    def body(kc_ref, s_ref, l_ref, o_ref):
        acc = s_ref[0].astype(F32)
        for j in range(N_CHIPS - 1):
            acc = acc + l_ref[j].astype(F32)
        o_ref[...] = acc

    return _call(name, body, (nr,),
                 [pl.BlockSpec((1, tr, C), lambda i, kc_ref: (kc_ref[0], i, 0)),
                  pl.BlockSpec((N_CHIPS - 1, tr, C), lambda i, kc_ref: (0, i, 0))],
                 pl.BlockSpec((tr, C), lambda i, kc_ref: (kc_ref[1] * nr + i, 0)), _sds((spec.R, C), F32), nsp=1,
                 sem=("parallel",))(kc_arr, chipsum, land)


def _place():
    x, y, c = lax.axis_index("x"), lax.axis_index("y"), lax.axis_index("c")
    return x, y, c, [(1 - x, y), (x, 1 - y), (1 - x, 1 - y)]


def _remote(src, dst, send_sem, recv_sem, dev):
    return pltpu.make_async_remote_copy(src_ref=src, dst_ref=dst, send_sem=send_sem, recv_sem=recv_sem,
                                        device_id=dev, device_id_type=MESH)


def _placed(block, dev):
    m, n = block.shape
    return lax.dynamic_update_slice(jnp.zeros((N_DEV * m, n), block.dtype), block, (dev * m, 0))


def _all_gather(bufs):
    n = len(bufs)
    per = 7

    def copier(outs, send_sems, recv_sems, w):
        m = outs[w].shape[0] // N_DEV

        def rows(px, py, pc):
            return outs[w].at[pl.ds((4 * px + 2 * py + pc) * m, m), :]

        def copy(k, block, to):
            return _remote(rows(*block), rows(*block), send_sems.at[w * per + k], recv_sems.at[w * per + k], to)

        return copy

    def own(outs, send_sems, recv_sems, w):
        x, y, c, chips = _place()
        copy = copier(outs, send_sems, recv_sems, w)
        return [copy(0, (x, y, c), (x, y, 1 - c))] + [copy(1 + j, (x, y, c), (*chip, c)) for j, chip in enumerate(chips)]

    def start(ins, outs, send_sems, recv_sems):
        for w in range(n):
            for cp in own(outs, send_sems, recv_sems, w):
                cp.start()

    def finish(ins, outs, send_sems, recv_sems):
        x, y, c, chips = _place()
        me, sibling = (x, y, c), (x, y, 1 - c)
        passed = []
        for w in range(n):
            copy = copier(outs, send_sems, recv_sems, w)
            for j, chip in enumerate(chips):
                copy(1 + j, (*chip, c), me).wait_recv()
                cp = copy(4 + j, (*chip, c), sibling)
                cp.start()
                passed.append(cp)
        for w in range(n):
            copy = copier(outs, send_sems, recv_sems, w)
            copy(0, sibling, me).wait_recv()
            for j, chip in enumerate(chips):
                copy(4 + j, (*chip, 1 - c), me).wait_recv()
            for cp in own(outs, send_sems, recv_sems, w):
                cp.wait_send()
        for cp in passed:
            cp.wait_send()

    return _Exchange(bufs, [_sds(a.shape, a.dtype) for a in bufs], per * n, start, finish, in_place=True)


def _cast_into(spec, w, k_arr, *, name):
    R, C = spec.R, spec.C
    tr = _pick(R, 256, 16)
    nr = R // tr

    def body(k_ref, w_ref, o_ref):
        o_ref[...] = w_ref[...].astype(BF16)

    if spec.kind == "col":
        ospec = pl.BlockSpec((tr, C), lambda i, k_ref: (i, k_ref[0]))
    else:
        ospec = pl.BlockSpec((tr, C), lambda i, k_ref: (k_ref[0] * nr + i, 0))
    return _call(name, body, (nr,), [pl.BlockSpec((tr, C), lambda i, k_ref: (i, 0))], ospec,
                 _sds(spec.full, BF16), nsp=1, sem=("parallel",))(k_arr, w)


def _weight_gather(bufs, specs):
    per = 6

    def to_chips(outs, send_sems, recv_sems):
        x, y, c, chips = _place()
        k_me = 2 * x + y
        cps = []
        for w, sp in enumerate(specs):
            mine = sp.region(outs[w], k_me, c)
            cps += [_remote(mine, mine, send_sems.at[w * per + j], recv_sems.at[w * per + j], (*chip, c))
                    for j, chip in enumerate(chips)]
        return cps

    def start(ins, outs, send_sems, recv_sems):
        for cp in to_chips(outs, send_sems, recv_sems):
            cp.start()

    def finish(ins, outs, send_sems, recv_sems):
        x, y, c, chips = _place()
        sibling = (x, y, 1 - c)
        passed = []
        for w, sp in enumerate(specs):
            for j, chip in enumerate(chips):
                got = sp.region(outs[w], 2 * chip[0] + chip[1], c)
                _remote(got, got, send_sems.at[w * per + j], recv_sems.at[w * per + j], (*chip, c)).wait_recv()
                cp = _remote(got, got, send_sems.at[w * per + 3 + j], recv_sems.at[w * per + 3 + j], sibling)
                cp.start()
                passed.append(cp)
        for w, sp in enumerate(specs):
            for j, chip in enumerate(chips):
                got = sp.region(outs[w], 2 * chip[0] + chip[1], 1 - c)
                _remote(got, got, send_sems.at[w * per + 3 + j], recv_sems.at[w * per + 3 + j], sibling).wait_recv()
        for cp in passed + to_chips(outs, send_sems, recv_sems):
            cp.wait_send()

    return _Exchange(bufs, [_sds(sp.full, BF16) for sp in specs], per * len(specs), start, finish, in_place=True)


def _pair_exchange(grads, specs):
    def copies(ins, outs, send_sems, recv_sems):
        x, y, c, _ = _place()
        return [_remote(sp.region(ins[w], k, 1 - c), outs[w].at[k], send_sems.at[w * N_CHIPS + k],
                        recv_sems.at[w * N_CHIPS + k], (x, y, 1 - c))
                for w, sp in enumerate(specs) for k in range(N_CHIPS)]

    def start(ins, outs, send_sems, recv_sems):
        for cp in copies(ins, outs, send_sems, recv_sems):
            cp.start()

    def finish(ins, outs, send_sems, recv_sems):
        for cp in copies(ins, outs, send_sems, recv_sems):
            cp.wait()

    return _Exchange(grads, [_sds((N_CHIPS, sp.hr, sp.C), F32) for sp in specs], N_CHIPS * len(specs), start, finish,
                     in_place=False)


def _chip_exchange(chipsums, specs):
    per = N_CHIPS - 1

    def copies(ins, outs, send_sems, recv_sems):
        x, y, c, chips = _place()
        return [_remote(ins[w].at[2 * chip[0] + chip[1]], outs[w].at[j], send_sems.at[w * per + j],
                        recv_sems.at[w * per + j], (*chip, c))
                for w in range(len(specs)) for j, chip in enumerate(chips)]

    def start(ins, outs, send_sems, recv_sems):
        for cp in copies(ins, outs, send_sems, recv_sems):
            cp.start()

    def finish(ins, outs, send_sems, recv_sems):
        for cp in copies(ins, outs, send_sems, recv_sems):
            cp.wait()

    return _Exchange(chipsums, [_sds((per, sp.hr, sp.C), BF16) for sp in specs], per * len(specs), start, finish,
                     in_place=False)


def _pair_share(bufs, specs):
    def mine(outs, send_sems, recv_sems):
        x, y, c, _ = _place()
        rows = [outs[w].at[pl.ds(c * sp.hr, sp.hr), :] for w, sp in enumerate(specs)]
        return [_remote(r, r, send_sems.at[w], recv_sems.at[w], (x, y, 1 - c)) for w, r in enumerate(rows)]

    def start(ins, outs, send_sems, recv_sems):
        for cp in mine(outs, send_sems, recv_sems):
            cp.start()

    def finish(ins, outs, send_sems, recv_sems):
        x, y, c, _ = _place()
        for w, sp in enumerate(specs):
            theirs = outs[w].at[pl.ds((1 - c) * sp.hr, sp.hr), :]
            _remote(theirs, theirs, send_sems.at[w], recv_sems.at[w], (x, y, 1 - c)).wait_recv()
        for cp in mine(outs, send_sems, recv_sems):
            cp.wait_send()

    return _Exchange(bufs, [_sds((sp.R, sp.C), F32) for sp in specs], len(specs), start, finish, in_place=True)


PACK_ALIGN = SUBLANES * LANES


def _pack(arrs):
    flat = jnp.concatenate([a.reshape(-1) for a in arrs])
    pad = (-flat.shape[0]) % PACK_ALIGN
    return jnp.pad(flat, (0, pad)).reshape(-1, LANES)


def _unpack(packed, shapes):
    flat = packed.reshape(-1)
    out, pos = [], 0
    for s in shapes:
        n = int(np.prod(s))
        out.append(flat[pos:pos + n].reshape(s))
        pos += n
    return out


def kernel(x, c, w_ada, b_ada, norm1, w_in, rnn_conv_w, rnn_conv_b, w_rg_a, b_rg_a, w_rg_i, b_rg_i, rg_lambda, w_o_rnn, w_o_attn, attn_sinks, rel_bias, w_out, norm2, w_up, ffn_conv_w, ffn_conv_b, w_down, norm_f, loss_target, m_w_ada, m_b_ada, m_norm1, m_w_in, m_rnn_conv_w, m_rnn_conv_b, m_w_rg_a, m_b_rg_a, m_w_rg_i, m_b_rg_i, m_rg_lambda, m_w_o_rnn, m_w_o_attn, m_attn_sinks, m_rel_bias, m_w_out, m_norm2, m_w_up, m_ffn_conv_w, m_ffn_conv_b, m_w_down, m_norm_f, v_w_ada, v_b_ada, v_norm1, v_w_in, v_rnn_conv_w, v_rnn_conv_b, v_w_rg_a, v_b_rg_a, v_w_rg_i, v_b_rg_i, v_rg_lambda, v_w_o_rnn, v_w_o_attn, v_attn_sinks, v_rel_bias, v_w_out, v_norm2, v_w_up, v_ffn_conv_w, v_ffn_conv_b, v_w_down, v_norm_f):
    S, D = x.shape[1], x.shape[2]
    d_attn = N_CHIPS * w_o_attn.shape[1]
    d_rnn = N_CHIPS * w_o_rnn.shape[1]
    d_ff = N_CHIPS * w_down.shape[1]
    d_in = N_CHIPS * w_in.shape[2]
    n_heads = attn_sinks.shape[1]
    d_kv = (d_in - d_attn - 2 * d_rnn - 2 * D) // 2
    n_kv = d_kv // HEAD_DIM
    group = n_heads // n_kv
    nbk = rel_bias.shape[0]
    assert d_attn == n_heads * HEAD_DIM and group % 2 == 0 and S % BLOCK == 0

    mx, my, mc = lax.axis_index("x"), lax.axis_index("y"), lax.axis_index("c")
    k_me = 2 * mx + my
    dev = 2 * k_me + mc
    c_arr = jnp.reshape(mc, (1,)).astype(jnp.int32)
    k_arr = jnp.reshape(k_me, (1,)).astype(jnp.int32)
    kc_arr = jnp.stack([k_me, mc]).astype(jnp.int32)

    xs, tgt = x[0], loss_target[0]

    big_names = ["w_in", "w_o_rnn", "w_o_attn", "w_out", "w_up", "w_down"]
    big_w = dict(w_in=w_in[0], w_o_rnn=w_o_rnn[0], w_o_attn=w_o_attn[0], w_out=w_out[0], w_up=w_up[0], w_down=w_down[0])
    big_kind = dict(w_in="col", w_o_rnn="row", w_o_attn="row", w_out="row", w_up="col", w_down="row")
    specs = {k: _Big(big_kind[k], *big_w[k].shape) for k in big_names}
    placed = {k: _cast_into(specs[k], big_w[k], k_arr, name="cast_" + k) for k in big_names}
    later = big_names[1:]
    W = dict(w_in=_weight_gather([placed["w_in"]], [specs["w_in"]]).alone("gather_w_in")[0])
    c_all, cw4, cw3 = _all_gather([_placed(jnp.broadcast_to(c, (SUBLANES, D)), dev),
                                   _placed(jnp.pad(rnn_conv_w[0], ((0, SUBLANES - rnn_conv_w.shape[1]), (0, 0))), dev),
                                   _placed(jnp.pad(ffn_conv_w[0], ((0, SUBLANES - ffn_conv_w.shape[1]), (0, 0))), dev)]
                                  ).alone("gather_cond")
    c_all = c_all.reshape(N_DEV, SUBLANES, D)[:, 0]

    def from_chips(g, taps):
        cs = g.shape[1]
        g = g.reshape(N_CHIPS, 2, SUBLANES, cs)[:, 0, :taps]
        return jnp.transpose(g, (1, 0, 2)).reshape(taps, N_CHIPS * cs)

    conv4_w = from_chips(cw4, rnn_conv_w.shape[1])
    conv3_w = from_chips(cw3, ffn_conv_w.shape[1])

    (silu_c,) = _ew("silu_c", lambda v: (v * _sigmoid(v),), [(c_all, 0)], [F32], N_DEV, D)
    mod_sh = _mm(silu_c, w_ada[0], name="mod", exact=True, bias=lax.dynamic_slice_in_dim(b_ada, k_me * w_ada.shape[2], w_ada.shape[2], 1))
    (mod_g,) = _all_gather([_placed(mod_sh, dev)]).alone("gather_mod")
    mod_all = jnp.transpose(mod_g.reshape(N_CHIPS, 2, N_DEV, -1)[:, 0], (1, 0, 2)).reshape(N_DEV, 6 * D)
    mod = lax.dynamic_slice_in_dim(mod_all, dev, 1, 0)
    shift1, scale1, gate1, shift2, scale2, gate2 = [mod[:, i * D:(i + 1) * D] for i in range(6)]

    o_k, o_v, o_xr = d_attn, d_attn + d_kv, d_attn + 2 * d_kv
    wi = W["w_in"]

    spread = np.zeros((d_kv, n_kv * LANES), np.float32)
    for col in range(d_kv):
        spread[col, (col // HEAD_DIM) * LANES + col % HEAD_DIM] = 1.0
        spread[col, (col // HEAD_DIM) * LANES + HEAD_DIM + col % HEAD_DIM] = 1.0
    w_in_x = jnp.concatenate([wi[:, :o_k],
                              _mm(wi[:, o_k:o_v], jnp.asarray(spread, BF16), name="spread_k", out_dtype=BF16),
                              _mm(wi[:, o_v:o_xr], jnp.asarray(spread, BF16), name="spread_v", out_dtype=BF16),
                              wi[:, o_xr:]], axis=1)
    e_k = d_attn
    e_v = e_k + n_kv * LANES
    e_xr = e_v + n_kv * LANES
    e_gr = e_xr + d_rnn
    e_ga = e_gr + d_rnn
    e_gl = e_ga + D
    d_ext = e_gl + D

    u = _adaln_fwd(xs, norm1, scale1, shift1, name="adaln1")
    proj = _mm(u, w_in_x, name="proj", out_dtype=BF16)
    bias = _bias_table(rel_bias, n_heads, name="bias_table")
    sinks_b = jnp.broadcast_to(attn_sinks.reshape(n_heads, 1, 1), (n_heads, 1, LANES))
    o_attn, gathered = _attn_fwd(proj, bias, sinks_b, n_kv, group, 0, e_k, e_v, name="attn_fwd",
                                 ride=_weight_gather([placed[k] for k in later], [specs[k] for k in later]))
    W.update(zip(later, gathered))
    y_attn = _mm(o_attn, W["w_o_attn"], name="y_attn")
    xc = _conv_fwd(proj, e_xr, d_rnn, conv4_w, rnn_conv_b, name="conv4")
    a_t, inp = _gates_fwd(xc, w_rg_a[0], b_rg_a, w_rg_i[0], b_rg_i, rg_lambda, name="gates")
    h_rnn, h_prev = _scan_fwd(a_t, inp, name="scan")
    (z,) = _ew("rnn_gate", lambda h, g: (h * _gelu(g),), [(h_rnn, 0), (proj, e_gr)], [BF16], S, d_rnn)
    y_rnn = _mm(z, W["w_o_rnn"], name="y_rnn")
    (merged,) = _ew("merge", lambda ya, yr, ga, gl: (_sigmoid(ga) * ya + _sigmoid(gl) * yr,),
                    [(y_attn, 0), (y_rnn, 0), (proj, e_ga), (proj, e_gl)], [BF16], S, D)
    t1 = _mm(merged, W["w_out"], name="t1")
    u2, h1 = _adaln_fwd(xs, norm2, scale2, shift2, name="adaln2", t=t1, gate=gate1)
    up = _mm(u2, W["w_up"], name="up", out_dtype=BF16)
    a2, gelu_g, gelu_dg, conv_v = _geglu_fwd(up, conv3_w, ffn_conv_b, d_ff, name="geglu")
    t2 = _mm(a2, W["w_down"], name="t2")
    dh2, dt2, loss_tile, g_norm_f, d_gate2 = _final(h1, t2, gate2, norm_f.reshape(1, D), tgt, name="final")

    da2 = _mm(dt2, W["w_down"], name="da2", tb=True)
    g_w_down = _mm(a2, dt2, name="g_w_down", ta=True)
    dupg, dupv, g_c3g, g_c3v = _ffn_bwd(up, conv3_w, gelu_g, gelu_dg, conv_v, da2, d_ff, name="ffn_bwd")
    dup = jnp.concatenate([dupg, dupv], axis=1)
    g_conv3 = jnp.concatenate([g_c3g, g_c3v], axis=1)
    du2 = _mm(dup, W["w_up"], name="du2", tb=True)
    g_w_up = _mm(u2, dup, name="g_w_up", ta=True)
    dh1, d_shift2, d_scale2, g_norm2, dt1, d_gate1 = _adaln_bwd(h1, du2, dh2, norm2, scale2, name="adaln2_bwd", t=t1, gate=gate1)

    dmerged = _mm(dt1, W["w_out"], name="dmerged", tb=True)
    g_w_out = _mm(merged, dt1, name="g_w_out", ta=True)

    def merge_bwd(dm, ya, yr, ga, gl):
        sa, sl = _sigmoid(ga), _sigmoid(gl)
        return dm * sa, dm * sl, dm * ya * sa * (1.0 - sa), dm * yr * sl * (1.0 - sl)

    dy_attn, dy_rnn, d_ga, d_gl = _ew("merge_bwd", merge_bwd, [(dmerged, 0), (y_attn, 0), (y_rnn, 0), (proj, e_ga), (proj, e_gl)],
                                      [BF16, BF16, BF16, BF16], S, D)
    do = _mm(dy_attn, W["w_o_attn"], name="do", tb=True, out_dtype=BF16)
    g_w_o_attn = _mm(o_attn, dy_attn, name="g_w_o_attn", ta=True)
    dz = _mm(dy_rnn, W["w_o_rnn"], name="dz", tb=True)
    g_w_o_rnn = _mm(z, dy_rnn, name="g_w_o_rnn", ta=True)

    def rnn_gate_bwd(dzv, h, g):
        ge, dge = _gelu_and_grad(g)
        return dzv * ge, dzv * h * dge

    dh_rnn, d_gr = _ew("rnn_gate_bwd", rnn_gate_bwd, [(dz, 0), (h_rnn, 0), (proj, e_gr)], [F32, BF16], S, d_rnn)
    dacc = _scan_bwd(a_t, dh_rnn, name="scan_bwd")
    big_g = dict(w_o_rnn=g_w_o_rnn, w_o_attn=g_w_o_attn, w_out=g_w_out, w_up=g_w_up, w_down=g_w_down)
    (dxc, g_w_rg_a, g_w_rg_i, g_b_rg_a, g_b_rg_i, g_lam), landed = _gates_bwd(
        xc, dacc, h_prev, w_rg_a[0], b_rg_a, w_rg_i[0], b_rg_i, rg_lambda, name="gates_bwd",
        ride=_pair_exchange([big_g[k] for k in later], [specs[k] for k in later]))
    d_xr, g_conv4 = _conv_bwd(dxc, proj, e_xr, d_rnn, conv4_w, name="conv4_bwd")
    chipsums = {k: _pair_sum(specs[k], big_g[k], l, c_arr, name="pair_sum_" + k) for k, l in zip(later, landed)}
    (dq, dk, dv, dbias, dsink), landed2 = _attn_bwd(
        proj, o_attn, do, bias, sinks_b, n_kv, group, 0, e_k, e_v, name="attn_bwd",
        ride=_chip_exchange([chipsums[k] for k in later], [specs[k] for k in later]))
    landed2 = dict(zip(later, landed2))
    g_rel = _bias_table_bwd(dbias, nbk, name="bias_table_bwd")[:, :n_heads]
    g_sinks = dsink[:, 0, :group].reshape(1, n_heads)
    dproj = jnp.concatenate([dq, dk, dv, d_xr, d_gr, d_ga, d_gl], axis=1)

    early = [loss_tile[0:1, 0:1], g_conv4, g_b_rg_a, g_b_rg_i, g_lam, g_sinks, g_rel, g_norm2, g_conv3, g_norm_f,
             d_gate1, d_shift2, d_scale2, d_gate2]
    early_shapes = [a.shape for a in early]
    rg_rows = g_w_rg_a.shape[0] * LANES
    g_w_in_x, (early_all, rg_a_all, rg_i_all) = _mm(
        u, dproj, name="g_w_in", ta=True,
        ride=_all_gather([_placed(_pack(early), dev), _placed(g_w_rg_a.reshape(rg_rows, LANES), dev),
                          _placed(g_w_rg_i.reshape(rg_rows, LANES), dev)]))

    gather_m = jnp.asarray(spread.T, F32)
    g_w_in = jnp.concatenate([g_w_in_x[:, :e_k],
                              _mm(g_w_in_x[:, e_k:e_v], gather_m, name="gather_k", exact=True),
                              _mm(g_w_in_x[:, e_v:e_xr], gather_m, name="gather_v", exact=True),
                              g_w_in_x[:, e_xr:]], axis=1)
    big_g["w_in"] = g_w_in
    (landed_in,) = _pair_exchange([g_w_in], [specs["w_in"]]).alone("pair_exchange_w_in")
    chipsums["w_in"] = _pair_sum(specs["w_in"], g_w_in, landed_in, c_arr, name="pair_sum_w_in")
    du, (landed2["w_in"],) = _mm(dproj, w_in_x, name="du", tb=True, ride=_chip_exchange([chipsums["w_in"]], [specs["w_in"]]))
    grad_x, d_shift1, d_scale1, g_norm1 = _adaln_bwd(xs, du, dh1, norm1, scale1, name="adaln1_bwd")

    late = [d_shift1, d_scale1, g_norm1]
    late_shapes = [a.shape for a in late]
    (late_all,) = _all_gather([_placed(_pack(late), dev)]).alone("gather_late")
    early_all = early_all.reshape(N_DEV, -1, LANES)
    late_all = late_all.reshape(N_DEV, -1, LANES)
    (loss_s, g_conv4, g_b_rg_a, g_b_rg_i, g_lam, g_sinks, g_rel, g_norm2, g_conv3, g_norm_f,
     s_gate1, s_shift2, s_scale2, s_gate2) = _unpack(_sum_devices(early_all, name="sum_early"), early_shapes)
    s_shift1, s_scale1, g_norm1 = _unpack(_sum_devices(late_all, name="sum_late"), late_shapes)
    g_w_rg_a = _sum_devices(rg_a_all.reshape(N_DEV, rg_rows, LANES), name="sum_w_rg_a")
    g_w_rg_i = _sum_devices(rg_i_all.reshape(N_DEV, rg_rows, LANES), name="sum_w_rg_i")
    loss = loss_s.reshape(())
    g_b_ada = jnp.concatenate([s_shift1, s_scale1, s_gate1, s_shift2, s_scale2, s_gate2], axis=1)
    taps4, taps3 = rnn_conv_w.shape[1], ffn_conv_w.shape[1]
    g_conv4_w, g_conv4_b = g_conv4[:taps4], g_conv4[taps4:taps4 + 1]
    g_conv3_w, g_conv3_b = g_conv3[:taps3], g_conv3[taps3:taps3 + 1]

    def per_device(all_packs, shapes, pick):
        flat = all_packs.reshape(N_DEV, -1)
        offs = np.cumsum([0] + [int(np.prod(s)) for s in shapes])
        return [flat[:, offs[i]:offs[i + 1]] for i in pick]

    dmod_all = jnp.concatenate(per_device(late_all, late_shapes, [0, 1]) + per_device(early_all, early_shapes, [10, 11, 12, 13]),
                               axis=1)
    cs_ada = w_ada.shape[2]
    g_w_ada = _mm(silu_c, lax.dynamic_slice_in_dim(dmod_all, k_me * cs_ada, cs_ada, 1), name="g_w_ada", ta=True, exact=True)
    cs4, cs3 = rnn_conv_w.shape[2], ffn_conv_w.shape[2]
    g_conv4_sh = lax.dynamic_slice_in_dim(g_conv4_w, k_me * cs4, cs4, 1)
    g_conv3_sh = lax.dynamic_slice_in_dim(g_conv3_w, k_me * cs3, cs3, 1)

    halves = [_chip_sum(specs[k], chipsums[k], landed2[k], kc_arr, name="chip_sum_" + k) for k in big_names]
    share = _pair_share(halves, [specs[k] for k in big_names])

    (d_ada, m_ada, v_ada), shared = _adamw(w_ada[0], g_w_ada, m_w_ada[0], v_w_ada[0], name="adamw_w_ada", ride=share)
    shards = dict(zip(big_names, shared))
    grads = dict(w_ada=g_w_ada[None], b_ada=g_b_ada, norm1=g_norm1, w_in=shards["w_in"][None], rnn_conv_w=g_conv4_sh[None],
                 rnn_conv_b=g_conv4_b, w_rg_a=g_w_rg_a[None], b_rg_a=g_b_rg_a, w_rg_i=g_w_rg_i[None], b_rg_i=g_b_rg_i,
                 rg_lambda=g_lam, w_o_rnn=shards["w_o_rnn"][None], w_o_attn=shards["w_o_attn"][None], attn_sinks=g_sinks,
                 rel_bias=g_rel, w_out=shards["w_out"][None], norm2=g_norm2, w_up=shards["w_up"][None],
                 ffn_conv_w=g_conv3_sh[None], ffn_conv_b=g_conv3_b, w_down=shards["w_down"][None], norm_f=g_norm_f.reshape(D))
    weights = dict(w_ada=w_ada, b_ada=b_ada, norm1=norm1, w_in=w_in, rnn_conv_w=rnn_conv_w, rnn_conv_b=rnn_conv_b, w_rg_a=w_rg_a,
                   b_rg_a=b_rg_a, w_rg_i=w_rg_i, b_rg_i=b_rg_i, rg_lambda=rg_lambda, w_o_rnn=w_o_rnn, w_o_attn=w_o_attn,
                   attn_sinks=attn_sinks, rel_bias=rel_bias, w_out=w_out, norm2=norm2, w_up=w_up, ffn_conv_w=ffn_conv_w,
                   ffn_conv_b=ffn_conv_b, w_down=w_down, norm_f=norm_f)
    moms = dict(w_ada=(m_w_ada, v_w_ada), b_ada=(m_b_ada, v_b_ada), norm1=(m_norm1, v_norm1), w_in=(m_w_in, v_w_in),
                rnn_conv_w=(m_rnn_conv_w, v_rnn_conv_w), rnn_conv_b=(m_rnn_conv_b, v_rnn_conv_b), w_rg_a=(m_w_rg_a, v_w_rg_a),
                b_rg_a=(m_b_rg_a, v_b_rg_a), w_rg_i=(m_w_rg_i, v_w_rg_i), b_rg_i=(m_b_rg_i, v_b_rg_i),
                rg_lambda=(m_rg_lambda, v_rg_lambda), w_o_rnn=(m_w_o_rnn, v_w_o_rnn), w_o_attn=(m_w_o_attn, v_w_o_attn),
                attn_sinks=(m_attn_sinks, v_attn_sinks), rel_bias=(m_rel_bias, v_rel_bias), w_out=(m_w_out, v_w_out),
                norm2=(m_norm2, v_norm2), w_up=(m_w_up, v_w_up), ffn_conv_w=(m_ffn_conv_w, v_ffn_conv_w),
                ffn_conv_b=(m_ffn_conv_b, v_ffn_conv_b), w_down=(m_w_down, v_w_down), norm_f=(m_norm_f, v_norm_f))
    names = list(weights)
    grads = {k: grads[k].reshape(weights[k].shape) for k in names}
    large = ["w_ada"] + big_names + ["w_rg_a", "w_rg_i"]
    delta, new_m, new_v = dict(w_ada=d_ada[None]), dict(w_ada=m_ada[None]), dict(w_ada=v_ada[None])
    for k in large[1:]:
        shp = weights[k].shape
        two = lambda a: a.reshape(-1, shp[-1])
        d_, m_, v_ = _adamw(two(weights[k]), two(grads[k]), two(moms[k][0]), two(moms[k][1]), name="adamw_" + k)
        delta[k], new_m[k], new_v[k] = d_.reshape(shp), m_.reshape(shp), v_.reshape(shp)
    rest = [k for k in names if k not in large]
    rest_shapes = [weights[k].shape for k in rest]
    d_, m_, v_ = _adamw(_pack([weights[k] for k in rest]), _pack([grads[k] for k in rest]),
                        _pack([moms[k][0] for k in rest]), _pack([moms[k][1] for k in rest]), name="adamw_small")
    for k, dd, mm_, vv in zip(rest, _unpack(d_, rest_shapes), _unpack(m_, rest_shapes), _unpack(v_, rest_shapes)):
        delta[k], new_m[k], new_v[k] = dd, mm_, vv

    return (loss, grad_x[None], *[grads[k] for k in names], *[delta[k] for k in names],
            *[new_m[k] for k in names], *[new_v[k] for k in names])
```

```python
import math

import numpy as np
import jax
import jax.numpy as jnp
from jax import lax
from jax.experimental import pallas as pl
from jax.experimental.pallas import tpu as pltpu

F32 = jnp.float32
BF16 = jnp.bfloat16
MESH = pl.DeviceIdType.MESH
ANY = pl.BlockSpec(memory_space=pl.ANY)

EPS = 1e-6
NEG_INF = -1e30
HEAD_DIM = 64
BLOCK = 128
NUM_EXACT = 16
MAX_DISTANCE = 128
RG_C = 8.0
ADAM_LR, ADAM_B1, ADAM_B2, ADAM_EPS, ADAM_WD, ADAM_STEP = 0.001, 0.9, 0.999, 1e-08, 0.01, 10
N_CHIPS = 4
N_DEV = 8
SUBLANES = 8
LANES = 128
VMEM_LIMIT_BYTES = 56 * 1024 * 1024
GELU_C0 = math.sqrt(2.0 / math.pi)
GELU_C1 = 0.044715


def _pick(dim, pref, align):
    if dim <= pref:
        return dim
    t = (pref // align) * align
    while t >= align:
        if dim % t == 0:
            return t
        t -= align
    return dim


def _params(sem):
    return pltpu.CompilerParams(dimension_semantics=sem, vmem_limit_bytes=VMEM_LIMIT_BYTES)


def _call(name, body, grid, in_specs, out_specs, out_shape, scratch=(), nsp=0, sem=None, ride=None):
    if ride is not None:
        return _call_with_ride(name, body, grid, in_specs, out_specs, out_shape, scratch, ride)
    sem = sem or ("parallel",) * (len(grid) - 1) + ("arbitrary",)
    if nsp:
        gs = pltpu.PrefetchScalarGridSpec(num_scalar_prefetch=nsp, grid=grid, in_specs=in_specs,
                                          out_specs=out_specs, scratch_shapes=list(scratch))
        return pl.pallas_call(body, name=name, grid_spec=gs, out_shape=out_shape, compiler_params=_params(sem))
    return pl.pallas_call(body, name=name, grid=grid, in_specs=in_specs, out_specs=out_specs,
                          out_shape=out_shape, scratch_shapes=list(scratch), compiler_params=_params(sem))


def _sds(shape, dtype):
    return jax.ShapeDtypeStruct(shape, dtype)


class _Exchange:
    def __init__(self, ins, outs, n_sems, start, finish, in_place):
        self.ins, self.outs, self.n_sems = list(ins), list(outs), n_sems
        self.start, self.finish, self.in_place = start, finish, in_place

    def alone(self, name):
        n_in, n_out = len(self.ins), len(self.outs)

        def body(*refs):
            ins, outs = refs[:n_in], refs[n_in:n_in + n_out]
            send_sems, recv_sems = refs[n_in + n_out:]
            self.start(ins, outs, send_sems, recv_sems)
            self.finish(ins, outs, send_sems, recv_sems)

        sems = [pltpu.SemaphoreType.DMA((self.n_sems,)), pltpu.SemaphoreType.DMA((self.n_sems,))]
        aliases = {i: i for i in range(n_in)} if self.in_place else {}
        return pl.pallas_call(body, name=name, out_shape=self.outs, in_specs=[ANY] * n_in, out_specs=[ANY] * n_out,
                              scratch_shapes=sems, input_output_aliases=aliases)(*self.ins)


def _call_with_ride(name, body, grid, in_specs, out_specs, out_shape, scratch, ride):
    single = not isinstance(out_specs, (list, tuple))
    out_specs = [out_specs] if single else list(out_specs)
    out_shape = [out_shape] if single else list(out_shape)
    n_in, n_out, n_sc = len(in_specs), len(out_specs), len(scratch)
    r_in, r_out = len(ride.ins), len(ride.outs)

    def wrapped(*refs):
        pos = 0
        parts = []
        for n in (n_in, r_in, n_out, r_out, n_sc, 2):
            parts.append(refs[pos:pos + n])
            pos += n
        core_in, ride_in, core_out, ride_out, core_scratch, (send_sems, recv_sems) = parts
        first = pl.program_id(0) == 0
        last = pl.program_id(0) == grid[0] - 1
        for axis in range(1, len(grid)):
            first = jnp.logical_and(first, pl.program_id(axis) == 0)
            last = jnp.logical_and(last, pl.program_id(axis) == grid[axis] - 1)

        @pl.when(first)
        def _():
            ride.start(ride_in, ride_out, send_sems, recv_sems)

        body(*core_in, *core_out, *core_scratch)

        @pl.when(last)
        def _():
            ride.finish(ride_in, ride_out, send_sems, recv_sems)

    sems = [pltpu.SemaphoreType.DMA((ride.n_sems,)), pltpu.SemaphoreType.DMA((ride.n_sems,))]
    aliases = {n_in + k: n_out + k for k in range(r_in)} if ride.in_place else {}
    call = pl.pallas_call(wrapped, name=name, grid=grid, in_specs=list(in_specs) + [ANY] * r_in,
                          out_specs=out_specs + [ANY] * r_out, out_shape=out_shape + ride.outs,
                          scratch_shapes=list(scratch) + sems, input_output_aliases=aliases,
                          compiler_params=_params(("arbitrary",) * len(grid)))

    def run(*args):
        res = call(*args, *ride.ins)
        core = res[0] if single else list(res[:n_out])
        return core, list(res[n_out:])

    return run


def _T(tr, tc, off=0):
    return pl.BlockSpec((tr, tc), lambda j, i: (i, j + off))


def _P(rows, tc, off=0):
    return pl.BlockSpec((rows, tc), lambda j, i: (0, j + off))


def _gelu(x):
    t = jnp.tanh(GELU_C0 * (x + GELU_C1 * x * x * x))
    return 0.5 * x * (1.0 + t)


def _gelu_and_grad(x):
    t = jnp.tanh(GELU_C0 * (x + GELU_C1 * x * x * x))
    g = 0.5 * x * (1.0 + t)
    dg = 0.5 * (1.0 + t) + 0.5 * x * (1.0 - t * t) * GELU_C0 * (1.0 + 3.0 * GELU_C1 * x * x)
    return g, dg


def _sigmoid(x):
    return 1.0 / (1.0 + jnp.exp(-x))


def _accum(ref, val, first):
    @pl.when(first)
    def _():
        ref[...] = val

    @pl.when(jnp.logical_not(first))
    def _():
        ref[...] += val


def _colsum(v):
    return jnp.sum(v, axis=0, keepdims=True)


CHUNK = 16


def _chunks(n_rows, ch, step):
    def it(i, carry):
        step(pl.multiple_of(i * ch, ch))
        return carry

    lax.fori_loop(0, n_rows // ch, it, 0)


def _fold(v):
    return v[0:SUBLANES, :] + v[SUBLANES:2 * SUBLANES, :]


def _shift_down(xe, d, ch):
    return xe[SUBLANES:SUBLANES + ch, :] if d == 0 else pltpu.roll(xe, d, 0)[SUBLANES:SUBLANES + ch, :]


def _shift_up(xe, d, ch):
    return xe[0:ch, :] if d == 0 else pltpu.roll(xe, ch + SUBLANES - d, 0)[0:ch, :]


def _mm(a, b, *, name, ta=False, tb=False, out_dtype=F32, tm=1024, tn=1024, tk=4096, exact=False, bias=None, ride=None):
    if ta:
        K, M = a.shape
    else:
        M, K = a.shape
    if tb:
        N, K2 = b.shape
    else:
        K2, N = b.shape
    assert K == K2, (a.shape, b.shape, ta, tb)
    tm, tn, tk = _pick(M, tm, LANES), _pick(N, tn, LANES), _pick(K, tk, LANES)
    nk = K // tk
    cdt = F32 if exact else BF16
    prec = lax.Precision.HIGHEST if exact else None
    dims = (((0 if ta else 1,), (1 if tb else 0,)), ((), ()))

    def body(*refs):
        a_ref, b_ref = refs[0], refs[1]
        bias_ref = refs[2] if bias is not None else None
        o_ref = refs[3] if bias is not None else refs[2]
        part = lax.dot_general(a_ref[...].astype(cdt), b_ref[...].astype(cdt), dims,
                               preferred_element_type=F32, precision=prec)

        def finish(r):
            if bias is not None:
                r = r + bias_ref[...]
            o_ref[...] = r.astype(out_dtype)

        if nk == 1:
            finish(part)
            return
        acc_ref = refs[-1]
        k = pl.program_id(2)

        @pl.when(k == 0)
        def _():
            acc_ref[...] = part

        @pl.when(jnp.logical_and(k > 0, k < nk - 1))
        def _():
            acc_ref[...] += part

        @pl.when(k == nk - 1)
        def _():
            finish(acc_ref[...] + part)

    a_spec = pl.BlockSpec((tk, tm), lambda i, j, k: (k, i)) if ta else pl.BlockSpec((tm, tk), lambda i, j, k: (i, k))
    b_spec = pl.BlockSpec((tn, tk), lambda i, j, k: (j, k)) if tb else pl.BlockSpec((tk, tn), lambda i, j, k: (k, j))
    in_specs, args = [a_spec, b_spec], [a, b]
    if bias is not None:
        in_specs.append(pl.BlockSpec((1, tn), lambda i, j, k: (0, j)))
        args.append(bias)
    scratch = [pltpu.VMEM((tm, tn), F32)] if nk > 1 else []
    return _call(name, body, (M // tm, N // tn, nk), in_specs, pl.BlockSpec((tm, tn), lambda i, j, k: (i, j)),
                 _sds((M, N), out_dtype), scratch=scratch, sem=("parallel", "parallel", "arbitrary"), ride=ride)(*args)


def _ew(name, fn, ins, out_dtypes, S, C, rows=(), tr=256, tc=512):
    g = C
    for _, off in ins:
        g = math.gcd(g, off) if off else g
    tc = _pick(g, tc, LANES)
    tr = _pick(S, tr, 16)
    n_in, n_row, n_out = len(ins), len(rows), len(out_dtypes)

    def body(*refs):
        vals = [r[...].astype(F32) for r in refs[:n_in + n_row]]
        for o_ref, o in zip(refs[n_in + n_row:], fn(*vals)):
            o_ref[...] = o.astype(o_ref.dtype)

    in_specs = [_T(tr, tc, off // tc) for _, off in ins] + [_P(1, tc) for _ in rows]
    res = _call(name, body, (C // tc, S // tr), in_specs, [_T(tr, tc) for _ in out_dtypes],
                [_sds((S, C), d) for d in out_dtypes], sem=("parallel", "parallel"))(*[a for a, _ in ins], *rows)
    return res


def _adaln_fwd(x, norm, scale, shift, *, name, t=None, gate=None):
    S, D = x.shape
    tr = _pick(S, 256, 16)
    resid = t is not None

    def body(*refs):
        if resid:
            x_ref, t_ref, g_ref, n_ref, sc_ref, sh_ref, u_ref, h_ref = refs
            h = x_ref[...] + g_ref[...] * t_ref[...]
            h_ref[...] = h
        else:
            x_ref, n_ref, sc_ref, sh_ref, u_ref = refs
            h = x_ref[...]
        r = lax.rsqrt(jnp.mean(h * h, axis=-1, keepdims=True) + EPS)
        u_ref[...] = (h * r * (n_ref[...] * (1.0 + sc_ref[...])) + sh_ref[...]).astype(BF16)

    full, row = _T(tr, D), _P(1, D)
    if resid:
        return _call(name, body, (1, S // tr), [full, full, row, row, row, row], [full, full],
                     [_sds((S, D), BF16), _sds((S, D), F32)])(x, t, gate, norm, scale, shift)
    return _call(name, body, (1, S // tr), [full, row, row, row], full, _sds((S, D), BF16))(x, norm, scale, shift)


def _adaln_bwd(h, du, dres, norm, scale, *, name, t=None, gate=None):
    S, D = h.shape
    tr = _pick(S, 256, 16)
    gated = t is not None

    def body(*refs):
        if gated:
            h_ref, du_ref, dr_ref, n_ref, sc_ref, t_ref, g_ref, dh_ref, dsh_ref, dsc_ref, dn_ref, dt_ref, dg_ref = refs
        else:
            h_ref, du_ref, dr_ref, n_ref, sc_ref, dh_ref, dsh_ref, dsc_ref, dn_ref = refs
        first = pl.program_id(1) == 0
        hv, duv = h_ref[...], du_ref[...]
        r = lax.rsqrt(jnp.mean(hv * hv, axis=-1, keepdims=True) + EPS)
        xn = hv * r
        one_sc = 1.0 + sc_ref[...]
        dxn = duv * (n_ref[...] * one_sc)
        dh = dr_ref[...] + r * (dxn - xn * jnp.mean(dxn * xn, axis=-1, keepdims=True))
        dh_ref[...] = dh
        dux = duv * xn
        _accum(dsh_ref, _colsum(duv), first)
        _accum(dsc_ref, _colsum(dux * n_ref[...]), first)
        _accum(dn_ref, _colsum(dux * one_sc), first)
        if gated:
            dt_ref[...] = (dh * g_ref[...]).astype(BF16)
            _accum(dg_ref, _colsum(dh * t_ref[...]), first)

    full, row = _T(tr, D), _P(1, D)
    rowo = _sds((1, D), F32)
    if gated:
        return _call(name, body, (1, S // tr), [full, full, full, row, row, full, row],
                     [full, row, row, row, full, row],
                     [_sds((S, D), F32), rowo, rowo, rowo, _sds((S, D), BF16), rowo])(h, du, dres, norm, scale, t, gate)
    return _call(name, body, (1, S // tr), [full, full, full, row, row], [full, row, row, row],
                 [_sds((S, D), F32), rowo, rowo, rowo])(h, du, dres, norm, scale)


def _final(h1, t2, gate2, norm_f, tgt, *, name):
    S, D = h1.shape
    tr = _pick(S, 256, 16)

    def body(h_ref, t_ref, g_ref, n_ref, y_ref, dh_ref, dt_ref, loss_ref, dn_ref, dg_ref):
        first = pl.program_id(1) == 0
        tv = t_ref[...]
        h2 = h_ref[...] + g_ref[...] * tv
        r = lax.rsqrt(jnp.mean(h2 * h2, axis=-1, keepdims=True) + EPS)
        xn = h2 * r
        e = xn * n_ref[...] - y_ref[...]
        part = 0.5 * jnp.sum(jnp.mean(e * e, axis=-1, keepdims=True), axis=0, keepdims=True)
        _accum(loss_ref, jnp.broadcast_to(part, (SUBLANES, LANES)), first)
        dy = e * (1.0 / D)
        _accum(dn_ref, _colsum(dy * xn), first)
        dxn = dy * n_ref[...]
        dh2 = r * (dxn - xn * jnp.mean(dxn * xn, axis=-1, keepdims=True))
        dh_ref[...] = dh2
        dt_ref[...] = (dh2 * g_ref[...]).astype(BF16)
        _accum(dg_ref, _colsum(dh2 * tv), first)

    full, row = _T(tr, D), _P(1, D)
    rowo = _sds((1, D), F32)
    return _call(name, body, (1, S // tr), [full, full, row, row, full],
                 [full, full, _P(SUBLANES, LANES), row, row],
                 [_sds((S, D), F32), _sds((S, D), BF16), _sds((SUBLANES, LANES), F32), rowo, rowo])(h1, t2, gate2, norm_f, tgt)


HALO = 16


def _halo_prev(tr, tc, off=0):
    return pl.BlockSpec((HALO, tc), lambda j, i: (jnp.maximum(i * (tr // HALO) - 1, 0), j + off))


def _halo_next(tr, tc, n_slabs, off=0):
    return pl.BlockSpec((HALO, tc), lambda j, i: (jnp.minimum((i + 1) * (tr // HALO), n_slabs - 1), j + off))


def _fill_past(buf, prev_ref, cur_ref, first, tr):
    buf[0:HALO, :] = jnp.where(first, 0.0, prev_ref[...].astype(F32))
    buf[HALO:HALO + tr, :] = cur_ref[...].astype(F32)


def _fill_future(buf, cur_ref, next_ref, last, tr):
    buf[0:tr, :] = cur_ref[...].astype(F32)
    buf[tr:tr + HALO, :] = jnp.where(last, 0.0, next_ref[...].astype(F32))


def _al8(r):
    return r if isinstance(r, int) else pl.multiple_of(r, SUBLANES)


def _past_rows(buf, r0, taps):
    xe = buf[pl.ds(_al8(r0 + HALO - SUBLANES), CHUNK + SUBLANES), :]
    return [_shift_down(xe, d, CHUNK) for d in range(taps)]


def _future_rows(buf, r0, taps):
    xe = buf[pl.ds(_al8(r0), CHUNK + SUBLANES), :]
    return [_shift_up(xe, d, CHUNK) for d in range(taps)]


def _taps(w_ref):
    return [w_ref[k:k + 1, :] for k in range(w_ref.shape[0])]


def _conv(xs, w, b):
    taps = len(xs)
    y = w[taps - 1] * xs[0] if b is None else b + w[taps - 1] * xs[0]
    for d in range(1, taps):
        y = y + w[taps - 1 - d] * xs[d]
    return y


def _conv_fwd(x, off, C, w, b, *, name):
    S = x.shape[0]
    taps = w.shape[0]
    tc = _pick(math.gcd(C, off) if off else C, 512, LANES)
    tr = _pick(S, 256, 16)

    def body(x_ref, p_ref, w_ref, b_ref, y_ref, buf):
        _fill_past(buf, p_ref, x_ref, pl.program_id(1) == 0, tr)
        wt, bv = _taps(w_ref), b_ref[...]

        def step(r0):
            y_ref[pl.ds(r0, CHUNK), :] = _conv(_past_rows(buf, r0, taps), wt, bv)

        _chunks(tr, CHUNK, step)

    return _call(name, body, (C // tc, S // tr),
                 [_T(tr, tc, off // tc), _halo_prev(tr, tc, off // tc), _P(taps, tc), _P(1, tc)],
                 _T(tr, tc), _sds((S, C), F32), scratch=[pltpu.VMEM((tr + HALO, tc), F32)],
                 sem=("parallel", "parallel"))(x, x, w, b)


def _finish_conv_grads(acc, base, taps, dwb_ref, first):
    @pl.when(first)
    def _():
        dwb_ref[...] = jnp.zeros_like(dwb_ref)

    for d in range(taps):
        dwb_ref[taps - 1 - d:taps - d, :] += _colsum(acc[base + SUBLANES * d:base + SUBLANES * (d + 1), :])
    dwb_ref[taps:taps + 1, :] += _colsum(acc[base + SUBLANES * taps:base + SUBLANES * (taps + 1), :])


def _add_conv_grads(acc, base, dy, xs):
    taps = len(xs)
    for d in range(taps):
        acc[base + SUBLANES * d:base + SUBLANES * (d + 1), :] += _fold(dy * xs[d])
    acc[base + SUBLANES * taps:base + SUBLANES * (taps + 1), :] += _fold(dy)


def _conv_bwd(dy, x, off, C, w, *, name):
    S = dy.shape[0]
    taps = w.shape[0]
    assert taps < SUBLANES
    tc = _pick(math.gcd(C, off) if off else C, 512, LANES)
    tr = _pick(S, 256, 16)
    n_slabs = S // HALO

    def body(dy_ref, nx_ref, x_ref, p_ref, w_ref, dx_ref, dwb_ref, fbuf, pbuf, acc):
        i = pl.program_id(1)
        first = i == 0
        _fill_future(fbuf, dy_ref, nx_ref, i == S // tr - 1, tr)
        _fill_past(pbuf, p_ref, x_ref, first, tr)
        acc[...] = jnp.zeros_like(acc)
        wt = _taps(w_ref)

        def step(r0):
            dys = _future_rows(fbuf, r0, taps)
            dx_ref[pl.ds(r0, CHUNK), :] = _conv(dys, wt, None).astype(BF16)
            _add_conv_grads(acc, 0, dys[0], _past_rows(pbuf, r0, taps))

        _chunks(tr, CHUNK, step)
        _finish_conv_grads(acc, 0, taps, dwb_ref, first)

    return _call(name, body, (C // tc, S // tr),
                 [_T(tr, tc), _halo_next(tr, tc, n_slabs), _T(tr, tc, off // tc), _halo_prev(tr, tc, off // tc),
                  _P(taps, tc)],
                 [_T(tr, tc), _P(SUBLANES, tc)],
                 [_sds((S, C), BF16), _sds((SUBLANES, C), F32)],
                 scratch=[pltpu.VMEM((tr + HALO, tc), F32), pltpu.VMEM((tr + HALO, tc), F32),
                          pltpu.VMEM((SUBLANES * (taps + 1), tc), F32)])(dy, dy, x, x, w)


def _geglu_fwd(up, w, b, F, *, name):
    S = up.shape[0]
    taps = w.shape[0]
    tc = _pick(F, 512, LANES)
    tr = _pick(S, 256, 16)
    nf = F // tc

    def body(g_ref, gp_ref, v_ref, vp_ref, wg_ref, wv_ref, bg_ref, bv_ref, a_ref, ge_ref, dge_ref, vo_ref, gbuf, vbuf):
        first = pl.program_id(1) == 0
        _fill_past(gbuf, gp_ref, g_ref, first, tr)
        _fill_past(vbuf, vp_ref, v_ref, first, tr)
        wg, wv, bg, bv = _taps(wg_ref), _taps(wv_ref), bg_ref[...], bv_ref[...]

        def step(r0):
            rows = pl.ds(r0, CHUNK)
            ge, dge = _gelu_and_grad(_conv(_past_rows(gbuf, r0, taps), wg, bg))
            v = _conv(_past_rows(vbuf, r0, taps), wv, bv)
            a_ref[rows, :] = (ge * v).astype(BF16)
            ge_ref[rows, :] = ge.astype(BF16)
            dge_ref[rows, :] = dge.astype(BF16)
            vo_ref[rows, :] = v.astype(BF16)

        _chunks(tr, CHUNK, step)

    buf = pltpu.VMEM((tr + HALO, tc), F32)
    t, o = _T(tr, tc), _sds((S, F), BF16)
    return _call(name, body, (nf, S // tr),
                 [t, _halo_prev(tr, tc), _T(tr, tc, nf), _halo_prev(tr, tc, nf),
                  _P(taps, tc), _P(taps, tc, nf), _P(1, tc), _P(1, tc, nf)],
                 [t, t, t, t], [o, o, o, o], scratch=[buf, buf], sem=("parallel", "parallel"))(up, up, up, up, w, w, b, b)


def _ffn_bwd(up, w, ge, dge, vv, da, F, *, name):
    S = up.shape[0]
    taps = w.shape[0]
    tc = _pick(F, 512, LANES)
    tr = _pick(S, 256, 16)
    nf, nr = F // tc, S // tr
    ext = tr + HALO
    n_slabs = S // HALO

    assert HALO == CHUNK
    v_base = SUBLANES * (taps + 1)

    def body(g_ref, gp_ref, v_ref, vp_ref, wg_ref, wv_ref, ge_ref, gen_ref, dge_ref, dgen_ref, vo_ref, von_ref,
             da_ref, dan_ref, dxg_ref, dxv_ref, dwbg_ref, dwbv_ref, gbuf, vbuf, dgbuf, dvbuf, acc):
        i = pl.program_id(1)
        first, last = i == 0, i == nr - 1
        _fill_past(gbuf, gp_ref, g_ref, first, tr)
        _fill_past(vbuf, vp_ref, v_ref, first, tr)
        acc[...] = jnp.zeros_like(acc)
        wg, wv = _taps(wg_ref), _taps(wv_ref)

        def grads(r0, da, gev, dgev, vov, in_tile):
            dg = da * vov.astype(F32) * dgev.astype(F32)
            dv = da * gev.astype(F32)
            dgbuf[pl.ds(_al8(r0), CHUNK), :] = dg
            dvbuf[pl.ds(_al8(r0), CHUNK), :] = dv
            if in_tile:
                _add_conv_grads(acc, 0, dg, _past_rows(gbuf, r0, taps))
                _add_conv_grads(acc, v_base, dv, _past_rows(vbuf, r0, taps))

        def in_tile(r0):
            rows = pl.ds(r0, CHUNK)
            grads(r0, da_ref[rows, :], ge_ref[rows, :], dge_ref[rows, :], vo_ref[rows, :], True)

        _chunks(tr, CHUNK, in_tile)
        grads(tr, jnp.where(last, 0.0, dan_ref[...]), gen_ref[...], dgen_ref[...], von_ref[...], False)

        def back(r0):
            dxg_ref[pl.ds(r0, CHUNK), :] = _conv(_future_rows(dgbuf, r0, taps), wg, None).astype(BF16)
            dxv_ref[pl.ds(r0, CHUNK), :] = _conv(_future_rows(dvbuf, r0, taps), wv, None).astype(BF16)

        _chunks(tr, CHUNK, back)
        _finish_conv_grads(acc, 0, taps, dwbg_ref, first)
        _finish_conv_grads(acc, v_base, taps, dwbv_ref, first)

    xbuf = pltpu.VMEM((HALO + tr, tc), F32)
    dbuf = pltpu.VMEM((ext, tc), F32)
    t, tw, t8, nx = _T(tr, tc), _P(taps, tc), _P(SUBLANES, tc), _halo_next(tr, tc, n_slabs)
    return _call(name, body, (nf, nr),
                 [t, _halo_prev(tr, tc), _T(tr, tc, nf), _halo_prev(tr, tc, nf), tw, _P(taps, tc, nf),
                  t, nx, t, nx, t, nx, t, nx],
                 [t, t, t8, t8],
                 [_sds((S, F), BF16), _sds((S, F), BF16), _sds((SUBLANES, F), F32), _sds((SUBLANES, F), F32)],
                 scratch=[xbuf, xbuf, dbuf, dbuf, pltpu.VMEM((2 * v_base, tc), F32)])(
                     up, up, up, up, w, w, ge, ge, dge, dge, vv, vv, da, da)


def _softplus_neg(lam):
    z = -lam
    e = jnp.exp(-jnp.abs(z))
    u = 1.0 + e
    log1p_e = jnp.where(u == 1.0, e, jnp.log(u) * e / jnp.where(u == 1.0, 1.0, u - 1.0))
    sp = jnp.maximum(z, 0.0) + log1p_e
    sg = jnp.where(z >= 0, 1.0 / u, e / u)
    return sp, sg


def _one_minus_exp(x):
    series = -x * (1.0 + x * (0.5 + x * (1.0 / 6.0 + x * (1.0 / 24.0))))
    return jnp.where(x > -0.01, series, 1.0 - jnp.exp(x))


def _gate_values(xc, wa_ref, wi_ref, ba_ref, bi_ref, lam_ref, is_t0):
    xb = xc.astype(BF16)
    ra = _sigmoid(jnp.dot(xb, wa_ref[0].astype(BF16), preferred_element_type=F32) + ba_ref[...])
    ia = _sigmoid(jnp.dot(xb, wi_ref[0].astype(BF16), preferred_element_type=F32) + bi_ref[...])
    sp, sg = _softplus_neg(lam_ref[...])
    log_a = -RG_C * ra * sp
    a = jnp.exp(log_a)
    mult = jnp.where(is_t0, 1.0, jnp.sqrt(_one_minus_exp(2.0 * log_a)))
    return ra, ia, sp, sg, a, mult


def _rnn_blockspecs(tr):
    x = pl.BlockSpec((tr, LANES), lambda n, i: (i, n))
    w = pl.BlockSpec((1, LANES, LANES), lambda n, i: (n, 0, 0))
    p = pl.BlockSpec((1, LANES), lambda n, i: (0, n))
    return x, w, p


def _is_t0(tr):
    rows = lax.broadcasted_iota(jnp.int32, (tr, LANES), 0)
    return jnp.logical_and(pl.program_id(1) == 0, rows == 0)


def _gates_fwd(xc, w_a, b_a, w_i, b_i, lam, *, name):
    S, C = xc.shape
    tr = _pick(S, 1024, 16)

    def body(x_ref, wa_ref, wi_ref, ba_ref, bi_ref, lam_ref, a_ref, inp_ref):
        xv = x_ref[...]
        _, ia, _, _, a, mult = _gate_values(xv, wa_ref, wi_ref, ba_ref, bi_ref, lam_ref, _is_t0(tr))
        a_ref[...] = a
        inp_ref[...] = mult * ia * xv

    x, w, p = _rnn_blockspecs(tr)
    return _call(name, body, (C // LANES, S // tr), [x, w, w, p, p, p], [x, x],
                 [_sds((S, C), F32), _sds((S, C), F32)], sem=("parallel", "parallel"))(xc, w_a, w_i, b_a, b_i, lam)


def _gates_bwd(xc, dacc, hprev, w_a, b_a, w_i, b_i, lam, *, name, ride=None):
    S, C = xc.shape
    nb = C // LANES
    tr = _pick(S, 1024, 16)

    def body(x_ref, d_ref, hp_ref, wa_ref, wi_ref, ba_ref, bi_ref, lam_ref,
             dx_ref, dwa_ref, dwi_ref, dba_ref, dbi_ref, dlam_ref):
        first = pl.program_id(1) == 0
        t0 = _is_t0(tr)
        xv, dv = x_ref[...], d_ref[...]
        ra, ia, sp, sg, a, mult = _gate_values(xv, wa_ref, wi_ref, ba_ref, bi_ref, lam_ref, t0)
        d_a = dv * hp_ref[...]
        d_mult = dv * ia * xv
        d_ia = dv * mult * xv
        d_log = d_a * a + jnp.where(t0, 0.0, -d_mult * (a * a) / mult)
        d_pa = d_log * (-RG_C * sp) * ra * (1.0 - ra)
        d_pi = d_ia * ia * (1.0 - ia)
        xb, dab, dib = xv.astype(BF16), d_pa.astype(BF16), d_pi.astype(BF16)
        nt = (((1,), (1,)), ((), ()))
        tn = (((0,), (0,)), ((), ()))
        dx_ref[...] = (dv * mult * ia
                       + lax.dot_general(dab, wa_ref[0].astype(BF16), nt, preferred_element_type=F32)
                       + lax.dot_general(dib, wi_ref[0].astype(BF16), nt, preferred_element_type=F32))
        _accum(dwa_ref, lax.dot_general(xb, dab, tn, preferred_element_type=F32)[None], first)
        _accum(dwi_ref, lax.dot_general(xb, dib, tn, preferred_element_type=F32)[None], first)
        _accum(dba_ref, _colsum(d_pa), first)
        _accum(dbi_ref, _colsum(d_pi), first)
        _accum(dlam_ref, _colsum(d_log * (-RG_C) * ra) * (-sg), first)

    x, w, p = _rnn_blockspecs(tr)
    row = _sds((1, C), F32)
    return _call(name, body, (nb, S // tr), [x, x, x, w, w, p, p, p], [x, w, w, p, p, p],
                 [_sds((S, C), F32), _sds((nb, LANES, LANES), F32), _sds((nb, LANES, LANES), F32), row, row, row],
                 ride=ride)(xc, dacc, hprev, w_a, w_i, b_a, b_i, lam)


def _scan_fwd(a, inp, *, name):
    S, C = a.shape
    tc = _pick(C, 1280, LANES)
    tr = _pick(S, 256, SUBLANES)

    def body(a_ref, b_ref, h_ref, hp_ref, carry):
        @pl.when(pl.program_id(1) == 0)
        def _():
            carry[...] = jnp.zeros_like(carry)

        rows = lax.broadcasted_iota(jnp.int32, (SUBLANES, tc), 0)

        def slab(s, h):
            base = pl.multiple_of(s * SUBLANES, SUBLANES)
            av, bv = a_ref[pl.ds(base, SUBLANES), :], b_ref[pl.ds(base, SUBLANES), :]
            ho = jnp.zeros((SUBLANES, tc), F32)
            po = jnp.zeros((SUBLANES, tc), F32)
            for r in range(SUBLANES):
                po = jnp.where(rows == r, h, po)
                h = av[r:r + 1, :] * h + bv[r:r + 1, :]
                ho = jnp.where(rows == r, h, ho)
            h_ref[pl.ds(base, SUBLANES), :] = ho
            hp_ref[pl.ds(base, SUBLANES), :] = po
            return h

        carry[...] = lax.fori_loop(0, tr // SUBLANES, slab, carry[...])

    t = _T(tr, tc)
    return _call(name, body, (C // tc, S // tr), [t, t], [t, t], [_sds((S, C), F32), _sds((S, C), F32)],
                 scratch=[pltpu.VMEM((1, tc), F32)])(a, inp)


def _scan_bwd(a, dh, *, name):
    S, C = a.shape
    tc = _pick(C, 1280, LANES)
    tr = _pick(S, 256, SUBLANES)
    nr = S // tr

    def body(a_ref, d_ref, o_ref, carry):
        @pl.when(pl.program_id(1) == 0)
        def _():
            carry[...] = jnp.zeros_like(carry)

        rows = lax.broadcasted_iota(jnp.int32, (SUBLANES, tc), 0)
        n_slabs = tr // SUBLANES

        def slab(s, g):
            base = pl.multiple_of((n_slabs - 1 - s) * SUBLANES, SUBLANES)
            av, dv = a_ref[pl.ds(base, SUBLANES), :], d_ref[pl.ds(base, SUBLANES), :]
            out = jnp.zeros((SUBLANES, tc), F32)
            for r in range(SUBLANES - 1, -1, -1):
                acc = dv[r:r + 1, :] + g
                out = jnp.where(rows == r, acc, out)
                g = av[r:r + 1, :] * acc
            o_ref[pl.ds(base, SUBLANES), :] = out
            return g

        carry[...] = lax.fori_loop(0, n_slabs, slab, carry[...])

    t = pl.BlockSpec((tr, tc), lambda j, i: (nr - 1 - i, j))
    return _call(name, body, (C // tc, nr), [t, t], t, _sds((S, C), F32), scratch=[pltpu.VMEM((1, tc), F32)])(a, dh)


SCALE = HEAD_DIM ** -0.5


def _bucket_table(num_buckets):
    qi = np.arange(BLOCK)[:, None]
    j = np.arange(BLOCK)[None, :]
    dist = np.where(j <= qi, qi - j, qi + BLOCK - j)
    d = np.maximum(dist, 1).astype(np.float64)
    large = NUM_EXACT + (np.log(d / NUM_EXACT) / math.log(MAX_DISTANCE / NUM_EXACT) * (num_buckets - NUM_EXACT)).astype(np.int32)
    large = np.minimum(large, num_buckets - 1)
    return np.where(dist < NUM_EXACT, dist, large).astype(np.int32)


def _bias_table(rel_bias, n_heads, *, name):
    nbk = rel_bias.shape[0]
    bucket = jnp.asarray(_bucket_table(nbk))

    def body(rb_ref, bk_ref, o_ref):
        t, h = pl.program_id(0), pl.program_id(1)
        bk = bk_ref[...]
        acc = jnp.zeros((BLOCK, BLOCK), F32)
        for b in range(nbk):
            acc = jnp.where(bk == b, rb_ref[b, h], acc)
        o_ref[0, 0] = jnp.where(jnp.logical_and(t == 0, jnp.logical_not(_own_block())), NEG_INF, acc)

    return _call(name, body, (2, n_heads),
                 [pl.BlockSpec(memory_space=pltpu.SMEM), pl.BlockSpec((BLOCK, BLOCK), lambda t, h: (0, 0))],
                 pl.BlockSpec((1, 1, BLOCK, BLOCK), lambda t, h: (t, h, 0, 0)),
                 _sds((2, n_heads, BLOCK, BLOCK), F32), sem=("arbitrary", "arbitrary"))(rel_bias, bucket)


def _bias_table_bwd(dbias, nbk, *, name):
    n_heads = dbias.shape[0]
    bucket = jnp.asarray(_bucket_table(nbk))

    def body(db_ref, bk_ref, o_ref):
        h = pl.program_id(0)
        bk = bk_ref[...]
        dv = db_ref[0]
        rows = lax.broadcasted_iota(jnp.int32, (nbk, LANES), 0)
        lanes = lax.broadcasted_iota(jnp.int32, (nbk, LANES), 1)
        acc = jnp.zeros((nbk, LANES), F32)
        for b in range(nbk):
            s = jnp.sum(jnp.sum(jnp.where(bk == b, dv, 0.0), axis=0, keepdims=True), axis=1, keepdims=True)
            acc = jnp.where(jnp.logical_and(rows == b, lanes == h), s, acc)
        _accum(o_ref, acc, h == 0)

    return _call(name, body, (n_heads,),
                 [pl.BlockSpec((1, BLOCK, BLOCK), lambda h: (h, 0, 0)), pl.BlockSpec((BLOCK, BLOCK), lambda h: (0, 0))],
                 pl.BlockSpec((nbk, LANES), lambda h: (0, 0)), _sds((nbk, LANES), F32), sem=("arbitrary",))(dbias, bucket)


def _own_block(heads=1):
    qi = lax.broadcasted_iota(jnp.int32, (heads * BLOCK, BLOCK), 0) % BLOCK
    j = lax.broadcasted_iota(jnp.int32, (heads * BLOCK, BLOCK), 1)
    return j <= qi


def _to_window(band, own):
    return jnp.where(own, band[:, BLOCK:], band[:, :BLOCK])


def _to_band(win, own):
    return jnp.concatenate([jnp.where(own, 0.0, win), jnp.where(own, win, 0.0)], axis=1)


def _stack_pairs(ref, group):
    return jnp.concatenate([ref[:, (g // 2) * LANES:(g // 2 + 1) * LANES] for g in range(group)], axis=0).astype(F32)


def _own_lanes(group):
    rows = lax.broadcasted_iota(jnp.int32, (group * BLOCK, LANES), 0)
    lanes = lax.broadcasted_iota(jnp.int32, (group * BLOCK, LANES), 1)
    return (lanes < HEAD_DIM) == ((rows // BLOCK) % 2 == 0)


def _unstack_pairs(stacked, o_ref, group):
    lo = lax.broadcasted_iota(jnp.int32, (BLOCK, LANES), 1) < HEAD_DIM
    for p in range(group // 2):
        even = stacked[2 * p * BLOCK:(2 * p + 1) * BLOCK, :]
        odd = stacked[(2 * p + 1) * BLOCK:(2 * p + 2) * BLOCK, :]
        o_ref[:, p * LANES:(p + 1) * LANES] = jnp.where(lo, even, odd).astype(o_ref.dtype)


def _group_probs(qm, kb, b_ref, s_ref, own, group):
    band = lax.dot_general(qm, kb, (((1,), (1,)), ((), ())), preferred_element_type=F32)
    bias = jnp.concatenate([b_ref[0, g] for g in range(group)], axis=0)
    sink = jnp.concatenate([jnp.broadcast_to(s_ref[g][:, 0:1], (BLOCK, 1)) for g in range(group)], axis=0)
    s = _to_window(band, own) + bias
    m = jnp.maximum(jnp.max(s, axis=-1, keepdims=True), sink)
    p = jnp.exp(s - m)
    es = jnp.exp(sink - m)
    inv = 1.0 / (jnp.sum(p, axis=-1, keepdims=True) + es)
    return p, inv, es


def _attn_fwd(proj, bias, sinks_b, n_kv, group, q_off, k_off, v_off, *, name, ride=None):
    S = proj.shape[0]
    nblk = S // BLOCK
    gw = group * HEAD_DIM
    pairs = group // 2

    def body(q_ref, kc_ref, kp_ref, vc_ref, vp_ref, b_ref, s_ref, o_ref):
        kb = jnp.concatenate([kp_ref[...], kc_ref[...]], axis=0).astype(BF16)
        vb = jnp.concatenate([vp_ref[...], vc_ref[...]], axis=0).astype(BF16)
        own = _own_block(group)
        qm = jnp.where(_own_lanes(group), _stack_pairs(q_ref, group) * SCALE, 0.0).astype(BF16)
        pu, inv, _ = _group_probs(qm, kb, b_ref, s_ref, own, group)
        o = jnp.dot(_to_band(pu, own).astype(BF16), vb, preferred_element_type=F32) * inv
        _unstack_pairs(o, o_ref, group)

    qb, kb0, vb0 = q_off // gw, k_off // LANES, v_off // LANES
    cur = lambda b0: pl.BlockSpec((BLOCK, LANES), lambda j, n: (n, b0 + j))
    prev = lambda b0: pl.BlockSpec((BLOCK, LANES), lambda j, n: (jnp.maximum(n - 1, 0), b0 + j))
    return _call(name, body, (n_kv, nblk),
                 [pl.BlockSpec((BLOCK, gw), lambda j, n: (n, qb + j)), cur(kb0), prev(kb0), cur(vb0), prev(vb0),
                  pl.BlockSpec((1, group, BLOCK, BLOCK), lambda j, n: (jnp.minimum(n, 1), j, 0, 0)),
                  pl.BlockSpec((group, 1, LANES), lambda j, n: (j, 0, 0))],
                 pl.BlockSpec((BLOCK, gw), lambda j, n: (n, j)), _sds((S, n_kv * gw), BF16),
                 sem=("parallel", "parallel"), ride=ride)(proj, proj, proj, proj, proj, bias, sinks_b)


def _attn_bwd(proj, o, do, bias, sinks_b, n_kv, group, q_off, k_off, v_off, *, name, ride=None):
    S = proj.shape[0]
    nblk = S // BLOCK
    gw = group * HEAD_DIM
    rows_all = group * BLOCK
    nt = (((1,), (1,)), ((), ()))
    tn = (((0,), (0,)), ((), ()))

    def body(q_ref, kc_ref, kp_ref, vc_ref, vp_ref, b_ref, s_ref, o_ref, do_ref,
             dq_ref, dk_ref, dv_ref, db_ref, ds_ref, ck, cv, sacc):
        n = pl.program_id(1)
        lo2 = lax.broadcasted_iota(jnp.int32, (2 * BLOCK, LANES), 1) < HEAD_DIM

        @pl.when(n == 0)
        def _():
            ck[...] = jnp.zeros_like(ck)
            cv[...] = jnp.zeros_like(cv)
            sacc[...] = jnp.zeros_like(sacc)
            db_ref[...] = jnp.zeros_like(db_ref)

        @pl.when(n < nblk)
        def _():
            kb = jnp.concatenate([kp_ref[...], kc_ref[...]], axis=0).astype(BF16)
            vb = jnp.concatenate([vp_ref[...], vc_ref[...]], axis=0).astype(BF16)
            own = _own_block(group)
            mine = _own_lanes(group)
            qm = jnp.where(mine, _stack_pairs(q_ref, group) * SCALE, 0.0).astype(BF16)
            dom = jnp.where(mine, _stack_pairs(do_ref, group), 0.0)
            pu, inv, es = _group_probs(qm, kb, b_ref, s_ref, own, group)
            pr = pu * inv
            delta = jnp.sum(dom * _stack_pairs(o_ref, group), axis=-1, keepdims=True)
            domb = dom.astype(BF16)
            dp = _to_window(lax.dot_general(domb, vb, nt, preferred_element_type=F32), own)
            ds = pr * (dp - delta)
            for g in range(group):
                db_ref[g] += ds[g * BLOCK:(g + 1) * BLOCK, :]
            sacc[...] += -(es * inv) * delta
            dsb = _to_band(ds, own).astype(BF16)
            _unstack_pairs(jnp.dot(dsb, kb, preferred_element_type=F32) * SCALE, dq_ref, group)
            dk_acc = lax.dot_general(dsb, qm, tn, preferred_element_type=F32)
            dv_acc = lax.dot_general(_to_band(pr, own).astype(BF16), domb, tn, preferred_element_type=F32)
            dkf = jnp.where(lo2, dk_acc + pltpu.roll(dk_acc, HEAD_DIM, 1), 0.0)
            dvf = jnp.where(lo2, dv_acc + pltpu.roll(dv_acc, HEAD_DIM, 1), 0.0)

            @pl.when(n > 0)
            def _():
                dk_ref[...] = (ck[...] + dkf[0:BLOCK, :]).astype(BF16)
                dv_ref[...] = (cv[...] + dvf[0:BLOCK, :]).astype(BF16)

            ck[...] = dkf[BLOCK:2 * BLOCK, :]
            cv[...] = dvf[BLOCK:2 * BLOCK, :]

        @pl.when(n == nblk)
        def _():
            dk_ref[...] = ck[...].astype(BF16)
            dv_ref[...] = cv[...].astype(BF16)
            lane = lax.broadcasted_iota(jnp.int32, (1, LANES), 1)
            tot = jnp.zeros((1, LANES), F32)
            for g in range(group):
                tot = jnp.where(lane == g, _colsum(sacc[g * BLOCK:(g + 1) * BLOCK, :]), tot)
            ds_ref[0] = tot

    qb, kb0, vb0 = q_off // gw, k_off // LANES, v_off // LANES
    last = nblk - 1
    cur = lambda b0: pl.BlockSpec((BLOCK, LANES), lambda j, n: (jnp.minimum(n, last), b0 + j))
    prev = lambda b0: pl.BlockSpec((BLOCK, LANES), lambda j, n: (jnp.clip(n - 1, 0, last), b0 + j))
    qspec = lambda b0: pl.BlockSpec((BLOCK, gw), lambda j, n: (jnp.minimum(n, last), b0 + j))
    kvout = pl.BlockSpec((BLOCK, LANES), lambda j, n: (jnp.maximum(n - 1, 0), j))
    return _call(name, body, (n_kv, nblk + 1),
                 [qspec(qb), cur(kb0), prev(kb0), cur(vb0), prev(vb0),
                  pl.BlockSpec((1, group, BLOCK, BLOCK), lambda j, n: (jnp.minimum(n, 1), j, 0, 0)),
                  pl.BlockSpec((group, 1, LANES), lambda j, n: (j, 0, 0)), qspec(0), qspec(0)],
                 [qspec(0), kvout, kvout, pl.BlockSpec((group, BLOCK, BLOCK), lambda j, n: (j, 0, 0)),
                  pl.BlockSpec((1, 1, LANES), lambda j, n: (j, 0, 0))],
                 [_sds((S, n_kv * gw), BF16), _sds((S, n_kv * LANES), BF16), _sds((S, n_kv * LANES), BF16),
                  _sds((n_kv * group, BLOCK, BLOCK), F32), _sds((n_kv, 1, LANES), F32)],
                 scratch=[pltpu.VMEM((BLOCK, LANES), F32), pltpu.VMEM((BLOCK, LANES), F32),
                          pltpu.VMEM((rows_all, 1), F32)], ride=ride)(proj, proj, proj, proj, proj, bias, sinks_b, o, do)


def _adamw(w, g, m, v, *, name, ride=None):
    R, C = w.shape
    tr = _pick(R, 128, SUBLANES)
    bc1 = 1.0 - ADAM_B1 ** ADAM_STEP
    bc2 = 1.0 - ADAM_B2 ** ADAM_STEP

    def body(w_ref, g_ref, m_ref, v_ref, d_ref, nm_ref, nv_ref):
        gv = g_ref[...]
        nm = ADAM_B1 * m_ref[...] + (1.0 - ADAM_B1) * gv
        nv = ADAM_B2 * v_ref[...] + (1.0 - ADAM_B2) * (gv * gv)
        d_ref[...] = -ADAM_LR * ((nm / bc1) / (jnp.sqrt(nv / bc2) + ADAM_EPS) + ADAM_WD * w_ref[...])
        nm_ref[...] = nm
        nv_ref[...] = nv

    t = pl.BlockSpec((tr, C), lambda i: (i, 0))
    o = _sds((R, C), F32)
    return _call(name, body, (R // tr,), [t, t, t, t], [t, t, t], [o, o, o], sem=("parallel",), ride=ride)(w, g, m, v)


def _sum_devices(packs, *, name):
    _, R, C = packs.shape
    tr = _pick(R, 512, SUBLANES)

    def body(p_ref, o_ref):
        acc = p_ref[0]
        for d in range(1, N_DEV):
            acc = acc + p_ref[d]
        o_ref[...] = acc

    return _call(name, body, (R // tr,), [pl.BlockSpec((N_DEV, tr, C), lambda i: (0, i, 0))],
                 pl.BlockSpec((tr, C), lambda i: (i, 0)), _sds((R, C), F32), sem=("parallel",))(packs)


class _Big:
    def __init__(self, kind, R, C):
        self.kind, self.R, self.C = kind, R, C
        self.hr = R // 2
        self.full = (R, N_CHIPS * C) if kind == "col" else (N_CHIPS * R, C)

    def region(self, ref, k, c):
        if self.kind == "col":
            return ref.at[pl.ds(c * self.hr, self.hr), pl.ds(k * self.C, self.C)]
        return ref.at[pl.ds(k * self.R + c * self.hr, self.hr), :]


def _pair_sum(spec, g_full, land, c_arr, *, name):
    hr, C = spec.hr, spec.C
    tr = _pick(hr, 128, 16)
    nr = hr // tr

    def body(c_ref, g_ref, l_ref, o_ref):
        o_ref[0] = (g_ref[...] + l_ref[0]).astype(BF16)

    if spec.kind == "col":
        gspec = pl.BlockSpec((tr, C), lambda k, i, c_ref: (c_ref[0] * nr + i, k))
    else:
        gspec = pl.BlockSpec((tr, C), lambda k, i, c_ref: (k * 2 * nr + c_ref[0] * nr + i, 0))
    lspec = pl.BlockSpec((1, tr, C), lambda k, i, c_ref: (k, i, 0))
    return _call(name, body, (N_CHIPS, nr), [gspec, lspec], lspec, _sds((N_CHIPS, hr, C), BF16), nsp=1,
                 sem=("parallel", "parallel"))(c_arr, g_full, land)


def _chip_sum(spec, chipsum, land, kc_arr, *, name):
    hr, C = spec.hr, spec.C
    tr = _pick(hr, 128, 16)
    nr = hr // tr

    def body(kc_ref, s_ref, l_ref, o_ref):
        acc = s_ref[0].astype(F32)
        for j in range(N_CHIPS - 1):
            acc = acc + l_ref[j].astype(F32)
        o_ref[...] = acc

    return _call(name, body, (nr,),
                 [pl.BlockSpec((1, tr, C), lambda i, kc_ref: (kc_ref[0], i, 0)),
                  pl.BlockSpec((N_CHIPS - 1, tr, C), lambda i, kc_ref: (0, i, 0))],
                 pl.BlockSpec((tr, C), lambda i, kc_ref: (kc_ref[1] * nr + i, 0)), _sds((spec.R, C), F32), nsp=1,
                 sem=("parallel",))(kc_arr, chipsum, land)


def _place():
    x, y, c = lax.axis_index("x"), lax.axis_index("y"), lax.axis_index("c")
    return x, y, c, [(1 - x, y), (x, 1 - y), (1 - x, 1 - y)]


def _remote(src, dst, send_sem, recv_sem, dev):
    return pltpu.make_async_remote_copy(src_ref=src, dst_ref=dst, send_sem=send_sem, recv_sem=recv_sem,
                                        device_id=dev, device_id_type=MESH)


def _placed(block, dev):
    m, n = block.shape
    return lax.dynamic_update_slice(jnp.zeros((N_DEV * m, n), block.dtype), block, (dev * m, 0))


def _all_gather(bufs):
    n = len(bufs)
    per = 7

    def copier(outs, send_sems, recv_sems, w):
        m = outs[w].shape[0] // N_DEV

        def rows(px, py, pc):
            return outs[w].at[pl.ds((4 * px + 2 * py + pc) * m, m), :]

        def copy(k, block, to):
            return _remote(rows(*block), rows(*block), send_sems.at[w * per + k], recv_sems.at[w * per + k], to)

        return copy

    def own(outs, send_sems, recv_sems, w):
        x, y, c, chips = _place()
        copy = copier(outs, send_sems, recv_sems, w)
        return [copy(0, (x, y, c), (x, y, 1 - c))] + [copy(1 + j, (x, y, c), (*chip, c)) for j, chip in enumerate(chips)]

    def start(ins, outs, send_sems, recv_sems):
        for w in range(n):
            for cp in own(outs, send_sems, recv_sems, w):
                cp.start()

    def finish(ins, outs, send_sems, recv_sems):
        x, y, c, chips = _place()
        me, sibling = (x, y, c), (x, y, 1 - c)
        passed = []
        for w in range(n):
            copy = copier(outs, send_sems, recv_sems, w)
            for j, chip in enumerate(chips):
                copy(1 + j, (*chip, c), me).wait_recv()
                cp = copy(4 + j, (*chip, c), sibling)
                cp.start()
                passed.append(cp)
        for w in range(n):
            copy = copier(outs, send_sems, recv_sems, w)
            copy(0, sibling, me).wait_recv()
            for j, chip in enumerate(chips):
                copy(4 + j, (*chip, 1 - c), me).wait_recv()
            for cp in own(outs, send_sems, recv_sems, w):
                cp.wait_send()
        for cp in passed:
            cp.wait_send()

    return _Exchange(bufs, [_sds(a.shape, a.dtype) for a in bufs], per * n, start, finish, in_place=True)


def _cast_into(spec, w, k_arr, *, name):
    R, C = spec.R, spec.C
    tr = _pick(R, 256, 16)
    nr = R // tr

    def body(k_ref, w_ref, o_ref):
        o_ref[...] = w_ref[...].astype(BF16)

    if spec.kind == "col":
        ospec = pl.BlockSpec((tr, C), lambda i, k_ref: (i, k_ref[0]))
    else:
        ospec = pl.BlockSpec((tr, C), lambda i, k_ref: (k_ref[0] * nr + i, 0))
    return _call(name, body, (nr,), [pl.BlockSpec((tr, C), lambda i, k_ref: (i, 0))], ospec,
                 _sds(spec.full, BF16), nsp=1, sem=("parallel",))(k_arr, w)


def _weight_gather(bufs, specs):
    per = 6

    def to_chips(outs, send_sems, recv_sems):
        x, y, c, chips = _place()
        k_me = 2 * x + y
        cps = []
        for w, sp in enumerate(specs):
            mine = sp.region(outs[w], k_me, c)
            cps += [_remote(mine, mine, send_sems.at[w * per + j], recv_sems.at[w * per + j], (*chip, c))
                    for j, chip in enumerate(chips)]
        return cps

    def start(ins, outs, send_sems, recv_sems):
        for cp in to_chips(outs, send_sems, recv_sems):
            cp.start()

    def finish(ins, outs, send_sems, recv_sems):
        x, y, c, chips = _place()
        sibling = (x, y, 1 - c)
        passed = []
        for w, sp in enumerate(specs):
            for j, chip in enumerate(chips):
                got = sp.region(outs[w], 2 * chip[0] + chip[1], c)
                _remote(got, got, send_sems.at[w * per + j], recv_sems.at[w * per + j], (*chip, c)).wait_recv()
                cp = _remote(got, got, send_sems.at[w * per + 3 + j], recv_sems.at[w * per + 3 + j], sibling)
                cp.start()
                passed.append(cp)
        for w, sp in enumerate(specs):
            for j, chip in enumerate(chips):
                got = sp.region(outs[w], 2 * chip[0] + chip[1], 1 - c)
                _remote(got, got, send_sems.at[w * per + 3 + j], recv_sems.at[w * per + 3 + j], sibling).wait_recv()
        for cp in passed + to_chips(outs, send_sems, recv_sems):
            cp.wait_send()

    return _Exchange(bufs, [_sds(sp.full, BF16) for sp in specs], per * len(specs), start, finish, in_place=True)


def _pair_exchange(grads, specs):
    def copies(ins, outs, send_sems, recv_sems):
        x, y, c, _ = _place()
        return [_remote(sp.region(ins[w], k, 1 - c), outs[w].at[k], send_sems.at[w * N_CHIPS + k],
                        recv_sems.at[w * N_CHIPS + k], (x, y, 1 - c))
                for w, sp in enumerate(specs) for k in range(N_CHIPS)]

    def start(ins, outs, send_sems, recv_sems):
        for cp in copies(ins, outs, send_sems, recv_sems):
            cp.start()

    def finish(ins, outs, send_sems, recv_sems):
        for cp in copies(ins, outs, send_sems, recv_sems):
            cp.wait()

    return _Exchange(grads, [_sds((N_CHIPS, sp.hr, sp.C), F32) for sp in specs], N_CHIPS * len(specs), start, finish,
                     in_place=False)


def _chip_exchange(chipsums, specs):
    per = N_CHIPS - 1

    def copies(ins, outs, send_sems, recv_sems):
        x, y, c, chips = _place()
        return [_remote(ins[w].at[2 * chip[0] + chip[1]], outs[w].at[j], send_sems.at[w * per + j],
                        recv_sems.at[w * per + j], (*chip, c))
                for w in range(len(specs)) for j, chip in enumerate(chips)]

    def start(ins, outs, send_sems, recv_sems):
        for cp in copies(ins, outs, send_sems, recv_sems):
            cp.start()

    def finish(ins, outs, send_sems, recv_sems):
        for cp in copies(ins, outs, send_sems, recv_sems):
            cp.wait()

    return _Exchange(chipsums, [_sds((per, sp.hr, sp.C), BF16) for sp in specs], per * len(specs), start, finish,
                     in_place=False)


def _pair_share(bufs, specs):
    def mine(outs, send_sems, recv_sems):
        x, y, c, _ = _place()
        rows = [outs[w].at[pl.ds(c * sp.hr, sp.hr), :] for w, sp in enumerate(specs)]
        return [_remote(r, r, send_sems.at[w], recv_sems.at[w], (x, y, 1 - c)) for w, r in enumerate(rows)]

    def start(ins, outs, send_sems, recv_sems):
        for cp in mine(outs, send_sems, recv_sems):
            cp.start()

    def finish(ins, outs, send_sems, recv_sems):
        x, y, c, _ = _place()
        for w, sp in enumerate(specs):
            theirs = outs[w].at[pl.ds((1 - c) * sp.hr, sp.hr), :]
            _remote(theirs, theirs, send_sems.at[w], recv_sems.at[w], (x, y, 1 - c)).wait_recv()
        for cp in mine(outs, send_sems, recv_sems):
            cp.wait_send()

    return _Exchange(bufs, [_sds((sp.R, sp.C), F32) for sp in specs], len(specs), start, finish, in_place=True)


PACK_ALIGN = SUBLANES * LANES


def _pack(arrs):
    flat = jnp.concatenate([a.reshape(-1) for a in arrs])
    pad = (-flat.shape[0]) % PACK_ALIGN
    return jnp.pad(flat, (0, pad)).reshape(-1, LANES)


def _unpack(packed, shapes):
    flat = packed.reshape(-1)
    out, pos = [], 0
    for s in shapes:
        n = int(np.prod(s))
        out.append(flat[pos:pos + n].reshape(s))
        pos += n
    return out


def kernel(x, c, w_ada, b_ada, norm1, w_in, rnn_conv_w, rnn_conv_b, w_rg_a, b_rg_a, w_rg_i, b_rg_i, rg_lambda, w_o_rnn, w_o_attn, attn_sinks, rel_bias, w_out, norm2, w_up, ffn_conv_w, ffn_conv_b, w_down, norm_f, loss_target, m_w_ada, m_b_ada, m_norm1, m_w_in, m_rnn_conv_w, m_rnn_conv_b, m_w_rg_a, m_b_rg_a, m_w_rg_i, m_b_rg_i, m_rg_lambda, m_w_o_rnn, m_w_o_attn, m_attn_sinks, m_rel_bias, m_w_out, m_norm2, m_w_up, m_ffn_conv_w, m_ffn_conv_b, m_w_down, m_norm_f, v_w_ada, v_b_ada, v_norm1, v_w_in, v_rnn_conv_w, v_rnn_conv_b, v_w_rg_a, v_b_rg_a, v_w_rg_i, v_b_rg_i, v_rg_lambda, v_w_o_rnn, v_w_o_attn, v_attn_sinks, v_rel_bias, v_w_out, v_norm2, v_w_up, v_ffn_conv_w, v_ffn_conv_b, v_w_down, v_norm_f):
    S, D = x.shape[1], x.shape[2]
    d_attn = N_CHIPS * w_o_attn.shape[1]
    d_rnn = N_CHIPS * w_o_rnn.shape[1]
    d_ff = N_CHIPS * w_down.shape[1]
    d_in = N_CHIPS * w_in.shape[2]
    n_heads = attn_sinks.shape[1]
    d_kv = (d_in - d_attn - 2 * d_rnn - 2 * D) // 2
    n_kv = d_kv // HEAD_DIM
    group = n_heads // n_kv
    nbk = rel_bias.shape[0]
    assert d_attn == n_heads * HEAD_DIM and group % 2 == 0 and S % BLOCK == 0

    mx, my, mc = lax.axis_index("x"), lax.axis_index("y"), lax.axis_index("c")
    k_me = 2 * mx + my
    dev = 2 * k_me + mc
    c_arr = jnp.reshape(mc, (1,)).astype(jnp.int32)
    k_arr = jnp.reshape(k_me, (1,)).astype(jnp.int32)
    kc_arr = jnp.stack([k_me, mc]).astype(jnp.int32)

    xs, tgt = x[0], loss_target[0]

    big_names = ["w_in", "w_o_rnn", "w_o_attn", "w_out", "w_up", "w_down"]
    big_w = dict(w_in=w_in[0], w_o_rnn=w_o_rnn[0], w_o_attn=w_o_attn[0], w_out=w_out[0], w_up=w_up[0], w_down=w_down[0])
    big_kind = dict(w_in="col", w_o_rnn="row", w_o_attn="row", w_out="row", w_up="col", w_down="row")
    specs = {k: _Big(big_kind[k], *big_w[k].shape) for k in big_names}
    placed = {k: _cast_into(specs[k], big_w[k], k_arr, name="cast_" + k) for k in big_names}
    later = big_names[1:]
    W = dict(w_in=_weight_gather([placed["w_in"]], [specs["w_in"]]).alone("gather_w_in")[0])
    c_all, cw4, cw3 = _all_gather([_placed(jnp.broadcast_to(c, (SUBLANES, D)), dev),
                                   _placed(jnp.pad(rnn_conv_w[0], ((0, SUBLANES - rnn_conv_w.shape[1]), (0, 0))), dev),
                                   _placed(jnp.pad(ffn_conv_w[0], ((0, SUBLANES - ffn_conv_w.shape[1]), (0, 0))), dev)]
                                  ).alone("gather_cond")
    c_all = c_all.reshape(N_DEV, SUBLANES, D)[:, 0]

    def from_chips(g, taps):
        cs = g.shape[1]
        g = g.reshape(N_CHIPS, 2, SUBLANES, cs)[:, 0, :taps]
        return jnp.transpose(g, (1, 0, 2)).reshape(taps, N_CHIPS * cs)

    conv4_w = from_chips(cw4, rnn_conv_w.shape[1])
    conv3_w = from_chips(cw3, ffn_conv_w.shape[1])

    (silu_c,) = _ew("silu_c", lambda v: (v * _sigmoid(v),), [(c_all, 0)], [F32], N_DEV, D)
    mod_sh = _mm(silu_c, w_ada[0], name="mod", exact=True, bias=lax.dynamic_slice_in_dim(b_ada, k_me * w_ada.shape[2], w_ada.shape[2], 1))
    (mod_g,) = _all_gather([_placed(mod_sh, dev)]).alone("gather_mod")
    mod_all = jnp.transpose(mod_g.reshape(N_CHIPS, 2, N_DEV, -1)[:, 0], (1, 0, 2)).reshape(N_DEV, 6 * D)
    mod = lax.dynamic_slice_in_dim(mod_all, dev, 1, 0)
    shift1, scale1, gate1, shift2, scale2, gate2 = [mod[:, i * D:(i + 1) * D] for i in range(6)]

    o_k, o_v, o_xr = d_attn, d_attn + d_kv, d_attn + 2 * d_kv
    wi = W["w_in"]

    spread = np.zeros((d_kv, n_kv * LANES), np.float32)
    for col in range(d_kv):
        spread[col, (col // HEAD_DIM) * LANES + col % HEAD_DIM] = 1.0
        spread[col, (col // HEAD_DIM) * LANES + HEAD_DIM + col % HEAD_DIM] = 1.0
    w_in_x = jnp.concatenate([wi[:, :o_k],
                              _mm(wi[:, o_k:o_v], jnp.asarray(spread, BF16), name="spread_k", out_dtype=BF16),
                              _mm(wi[:, o_v:o_xr], jnp.asarray(spread, BF16), name="spread_v", out_dtype=BF16),
                              wi[:, o_xr:]], axis=1)
    e_k = d_attn
    e_v = e_k + n_kv * LANES
    e_xr = e_v + n_kv * LANES
    e_gr = e_xr + d_rnn
    e_ga = e_gr + d_rnn
    e_gl = e_ga + D
    d_ext = e_gl + D

    u = _adaln_fwd(xs, norm1, scale1, shift1, name="adaln1")
    proj = _mm(u, w_in_x, name="proj", out_dtype=BF16)
    bias = _bias_table(rel_bias, n_heads, name="bias_table")
    sinks_b = jnp.broadcast_to(attn_sinks.reshape(n_heads, 1, 1), (n_heads, 1, LANES))
    o_attn, gathered = _attn_fwd(proj, bias, sinks_b, n_kv, group, 0, e_k, e_v, name="attn_fwd",
                                 ride=_weight_gather([placed[k] for k in later], [specs[k] for k in later]))
    W.update(zip(later, gathered))
    y_attn = _mm(o_attn, W["w_o_attn"], name="y_attn", out_dtype=BF16)
    xc = _conv_fwd(proj, e_xr, d_rnn, conv4_w, rnn_conv_b, name="conv4")
    a_t, inp = _gates_fwd(xc, w_rg_a[0], b_rg_a, w_rg_i[0], b_rg_i, rg_lambda, name="gates")
    h_rnn, h_prev = _scan_fwd(a_t, inp, name="scan")
    (z,) = _ew("rnn_gate", lambda h, g: (h * _gelu(g),), [(h_rnn, 0), (proj, e_gr)], [BF16], S, d_rnn)
    y_rnn = _mm(z, W["w_o_rnn"], name="y_rnn", out_dtype=BF16)
    (merged,) = _ew("merge", lambda ya, yr, ga, gl: (_sigmoid(ga) * ya + _sigmoid(gl) * yr,),
                    [(y_attn, 0), (y_rnn, 0), (proj, e_ga), (proj, e_gl)], [BF16], S, D)
    t1 = _mm(merged, W["w_out"], name="t1")
    u2, h1 = _adaln_fwd(xs, norm2, scale2, shift2, name="adaln2", t=t1, gate=gate1)
    up = _mm(u2, W["w_up"], name="up", out_dtype=BF16)
    a2, gelu_g, gelu_dg, conv_v = _geglu_fwd(up, conv3_w, ffn_conv_b, d_ff, name="geglu")
    t2 = _mm(a2, W["w_down"], name="t2")
    dh2, dt2, loss_tile, g_norm_f, d_gate2 = _final(h1, t2, gate2, norm_f.reshape(1, D), tgt, name="final")

    da2 = _mm(dt2, W["w_down"], name="da2", tb=True)
    g_w_down = _mm(a2, dt2, name="g_w_down", ta=True)
    dupg, dupv, g_c3g, g_c3v = _ffn_bwd(up, conv3_w, gelu_g, gelu_dg, conv_v, da2, d_ff, name="ffn_bwd")
    dup = jnp.concatenate([dupg, dupv], axis=1)
    g_conv3 = jnp.concatenate([g_c3g, g_c3v], axis=1)
    du2 = _mm(dup, W["w_up"], name="du2", tb=True)
    g_w_up = _mm(u2, dup, name="g_w_up", ta=True)
    dh1, d_shift2, d_scale2, g_norm2, dt1, d_gate1 = _adaln_bwd(h1, du2, dh2, norm2, scale2, name="adaln2_bwd", t=t1, gate=gate1)

    dmerged = _mm(dt1, W["w_out"], name="dmerged", tb=True, out_dtype=BF16)
    g_w_out = _mm(merged, dt1, name="g_w_out", ta=True)

    def merge_bwd(dm, ya, yr, ga, gl):
        sa, sl = _sigmoid(ga), _sigmoid(gl)
        return dm * sa, dm * sl, dm * ya * sa * (1.0 - sa), dm * yr * sl * (1.0 - sl)

    dy_attn, dy_rnn, d_ga, d_gl = _ew("merge_bwd", merge_bwd, [(dmerged, 0), (y_attn, 0), (y_rnn, 0), (proj, e_ga), (proj, e_gl)],
                                      [BF16, BF16, BF16, BF16], S, D)
    do = _mm(dy_attn, W["w_o_attn"], name="do", tb=True, out_dtype=BF16)
    g_w_o_attn = _mm(o_attn, dy_attn, name="g_w_o_attn", ta=True)
    dz = _mm(dy_rnn, W["w_o_rnn"], name="dz", tb=True, out_dtype=BF16)
    g_w_o_rnn = _mm(z, dy_rnn, name="g_w_o_rnn", ta=True)

    def rnn_gate_bwd(dzv, h, g):
        ge, dge = _gelu_and_grad(g)
        return dzv * ge, dzv * h * dge

    dh_rnn, d_gr = _ew("rnn_gate_bwd", rnn_gate_bwd, [(dz, 0), (h_rnn, 0), (proj, e_gr)], [F32, BF16], S, d_rnn)
    dacc = _scan_bwd(a_t, dh_rnn, name="scan_bwd")
    big_g = dict(w_o_rnn=g_w_o_rnn, w_o_attn=g_w_o_attn, w_out=g_w_out, w_up=g_w_up, w_down=g_w_down)
    (dxc, g_w_rg_a, g_w_rg_i, g_b_rg_a, g_b_rg_i, g_lam), landed = _gates_bwd(
        xc, dacc, h_prev, w_rg_a[0], b_rg_a, w_rg_i[0], b_rg_i, rg_lambda, name="gates_bwd",
        ride=_pair_exchange([big_g[k] for k in later], [specs[k] for k in later]))
    d_xr, g_conv4 = _conv_bwd(dxc, proj, e_xr, d_rnn, conv4_w, name="conv4_bwd")
    chipsums = {k: _pair_sum(specs[k], big_g[k], l, c_arr, name="pair_sum_" + k) for k, l in zip(later, landed)}
    (dq, dk, dv, dbias, dsink), landed2 = _attn_bwd(
        proj, o_attn, do, bias, sinks_b, n_kv, group, 0, e_k, e_v, name="attn_bwd",
        ride=_chip_exchange([chipsums[k] for k in later], [specs[k] for k in later]))
    landed2 = dict(zip(later, landed2))
    g_rel = _bias_table_bwd(dbias, nbk, name="bias_table_bwd")[:, :n_heads]
    g_sinks = dsink[:, 0, :group].reshape(1, n_heads)
    dproj = jnp.concatenate([dq, dk, dv, d_xr, d_gr, d_ga, d_gl], axis=1)

    early = [loss_tile[0:1, 0:1], g_conv4, g_b_rg_a, g_b_rg_i, g_lam, g_sinks, g_rel, g_norm2, g_conv3, g_norm_f,
             d_gate1, d_shift2, d_scale2, d_gate2]
    early_shapes = [a.shape for a in early]
    rg_rows = g_w_rg_a.shape[0] * LANES
    g_w_in_x, (early_all, rg_a_all, rg_i_all) = _mm(
        u, dproj, name="g_w_in", ta=True,
        ride=_all_gather([_placed(_pack(early), dev), _placed(g_w_rg_a.reshape(rg_rows, LANES), dev),
                          _placed(g_w_rg_i.reshape(rg_rows, LANES), dev)]))

    gather_m = jnp.asarray(spread.T, F32)
    g_w_in = jnp.concatenate([g_w_in_x[:, :e_k],
                              _mm(g_w_in_x[:, e_k:e_v], gather_m, name="gather_k", exact=True),
                              _mm(g_w_in_x[:, e_v:e_xr], gather_m, name="gather_v", exact=True),
                              g_w_in_x[:, e_xr:]], axis=1)
    big_g["w_in"] = g_w_in
    (landed_in,) = _pair_exchange([g_w_in], [specs["w_in"]]).alone("pair_exchange_w_in")
    chipsums["w_in"] = _pair_sum(specs["w_in"], g_w_in, landed_in, c_arr, name="pair_sum_w_in")
    du, (landed2["w_in"],) = _mm(dproj, w_in_x, name="du", tb=True, ride=_chip_exchange([chipsums["w_in"]], [specs["w_in"]]))
    grad_x, d_shift1, d_scale1, g_norm1 = _adaln_bwd(xs, du, dh1, norm1, scale1, name="adaln1_bwd")

    late = [d_shift1, d_scale1, g_norm1]
    late_shapes = [a.shape for a in late]
    (late_all,) = _all_gather([_placed(_pack(late), dev)]).alone("gather_late")
    early_all = early_all.reshape(N_DEV, -1, LANES)
    late_all = late_all.reshape(N_DEV, -1, LANES)
    (loss_s, g_conv4, g_b_rg_a, g_b_rg_i, g_lam, g_sinks, g_rel, g_norm2, g_conv3, g_norm_f,
     s_gate1, s_shift2, s_scale2, s_gate2) = _unpack(_sum_devices(early_all, name="sum_early"), early_shapes)
    s_shift1, s_scale1, g_norm1 = _unpack(_sum_devices(late_all, name="sum_late"), late_shapes)
    g_w_rg_a = _sum_devices(rg_a_all.reshape(N_DEV, rg_rows, LANES), name="sum_w_rg_a")
    g_w_rg_i = _sum_devices(rg_i_all.reshape(N_DEV, rg_rows, LANES), name="sum_w_rg_i")
    loss = loss_s.reshape(())
    g_b_ada = jnp.concatenate([s_shift1, s_scale1, s_gate1, s_shift2, s_scale2, s_gate2], axis=1)
    taps4, taps3 = rnn_conv_w.shape[1], ffn_conv_w.shape[1]
    g_conv4_w, g_conv4_b = g_conv4[:taps4], g_conv4[taps4:taps4 + 1]
    g_conv3_w, g_conv3_b = g_conv3[:taps3], g_conv3[taps3:taps3 + 1]

    def per_device(all_packs, shapes, pick):
        flat = all_packs.reshape(N_DEV, -1)
        offs = np.cumsum([0] + [int(np.prod(s)) for s in shapes])
        return [flat[:, offs[i]:offs[i + 1]] for i in pick]

    dmod_all = jnp.concatenate(per_device(late_all, late_shapes, [0, 1]) + per_device(early_all, early_shapes, [10, 11, 12, 13]),
                               axis=1)
    cs_ada = w_ada.shape[2]
    g_w_ada = _mm(silu_c, lax.dynamic_slice_in_dim(dmod_all, k_me * cs_ada, cs_ada, 1), name="g_w_ada", ta=True, exact=True)
    cs4, cs3 = rnn_conv_w.shape[2], ffn_conv_w.shape[2]
    g_conv4_sh = lax.dynamic_slice_in_dim(g_conv4_w, k_me * cs4, cs4, 1)
    g_conv3_sh = lax.dynamic_slice_in_dim(g_conv3_w, k_me * cs3, cs3, 1)

    halves = [_chip_sum(specs[k], chipsums[k], landed2[k], kc_arr, name="chip_sum_" + k) for k in big_names]
    share = _pair_share(halves, [specs[k] for k in big_names])

    (d_ada, m_ada, v_ada), shared = _adamw(w_ada[0], g_w_ada, m_w_ada[0], v_w_ada[0], name="adamw_w_ada", ride=share)
    shards = dict(zip(big_names, shared))
    grads = dict(w_ada=g_w_ada[None], b_ada=g_b_ada, norm1=g_norm1, w_in=shards["w_in"][None], rnn_conv_w=g_conv4_sh[None],
                 rnn_conv_b=g_conv4_b, w_rg_a=g_w_rg_a[None], b_rg_a=g_b_rg_a, w_rg_i=g_w_rg_i[None], b_rg_i=g_b_rg_i,
                 rg_lambda=g_lam, w_o_rnn=shards["w_o_rnn"][None], w_o_attn=shards["w_o_attn"][None], attn_sinks=g_sinks,
                 rel_bias=g_rel, w_out=shards["w_out"][None], norm2=g_norm2, w_up=shards["w_up"][None],
                 ffn_conv_w=g_conv3_sh[None], ffn_conv_b=g_conv3_b, w_down=shards["w_down"][None], norm_f=g_norm_f.reshape(D))
    weights = dict(w_ada=w_ada, b_ada=b_ada, norm1=norm1, w_in=w_in, rnn_conv_w=rnn_conv_w, rnn_conv_b=rnn_conv_b, w_rg_a=w_rg_a,
                   b_rg_a=b_rg_a, w_rg_i=w_rg_i, b_rg_i=b_rg_i, rg_lambda=rg_lambda, w_o_rnn=w_o_rnn, w_o_attn=w_o_attn,
                   attn_sinks=attn_sinks, rel_bias=rel_bias, w_out=w_out, norm2=norm2, w_up=w_up, ffn_conv_w=ffn_conv_w,
                   ffn_conv_b=ffn_conv_b, w_down=w_down, norm_f=norm_f)
    moms = dict(w_ada=(m_w_ada, v_w_ada), b_ada=(m_b_ada, v_b_ada), norm1=(m_norm1, v_norm1), w_in=(m_w_in, v_w_in),
                rnn_conv_w=(m_rnn_conv_w, v_rnn_conv_w), rnn_conv_b=(m_rnn_conv_b, v_rnn_conv_b), w_rg_a=(m_w_rg_a, v_w_rg_a),
                b_rg_a=(m_b_rg_a, v_b_rg_a), w_rg_i=(m_w_rg_i, v_w_rg_i), b_rg_i=(m_b_rg_i, v_b_rg_i),
                rg_lambda=(m_rg_lambda, v_rg_lambda), w_o_rnn=(m_w_o_rnn, v_w_o_rnn), w_o_attn=(m_w_o_attn, v_w_o_attn),
                attn_sinks=(m_attn_sinks, v_attn_sinks), rel_bias=(m_rel_bias, v_rel_bias), w_out=(m_w_out, v_w_out),
                norm2=(m_norm2, v_norm2), w_up=(m_w_up, v_w_up), ffn_conv_w=(m_ffn_conv_w, v_ffn_conv_w),
                ffn_conv_b=(m_ffn_conv_b, v_ffn_conv_b), w_down=(m_w_down, v_w_down), norm_f=(m_norm_f, v_norm_f))
    names = list(weights)
    grads = {k: grads[k].reshape(weights[k].shape) for k in names}
    large = ["w_ada"] + big_names + ["w_rg_a", "w_rg_i"]
    delta, new_m, new_v = dict(w_ada=d_ada[None]), dict(w_ada=m_ada[None]), dict(w_ada=v_ada[None])
    for k in large[1:]:
        shp = weights[k].shape
        two = lambda a: a.reshape(-1, shp[-1])
        d_, m_, v_ = _adamw(two(weights[k]), two(grads[k]), two(moms[k][0]), two(moms[k][1]), name="adamw_" + k)
        delta[k], new_m[k], new_v[k] = d_.reshape(shp), m_.reshape(shp), v_.reshape(shp)
    rest = [k for k in names if k not in large]
    rest_shapes = [weights[k].shape for k in rest]
    d_, m_, v_ = _adamw(_pack([weights[k] for k in rest]), _pack([grads[k] for k in rest]),
                        _pack([moms[k][0] for k in rest]), _pack([moms[k][1] for k in rest]), name="adamw_small")
    for k, dd, mm_, vv in zip(rest, _unpack(d_, rest_shapes), _unpack(m_, rest_shapes), _unpack(v_, rest_shapes)):
        delta[k], new_m[k], new_v[k] = dd, mm_, vv

    return (loss, grad_x[None], *[grads[k] for k in names], *[delta[k] for k in names],
            *[new_m[k] for k in names], *[new_v[k] for k in names])
```

```python
import math

import numpy as np
import jax
import jax.numpy as jnp
from jax import lax
from jax.experimental import pallas as pl
from jax.experimental.pallas import tpu as pltpu

F32 = jnp.float32
BF16 = jnp.bfloat16
MESH = pl.DeviceIdType.MESH
ANY = pl.BlockSpec(memory_space=pl.ANY)

EPS = 1e-6
NEG_INF = -1e30
HEAD_DIM = 64
BLOCK = 128
NUM_EXACT = 16
MAX_DISTANCE = 128
RG_C = 8.0
ADAM_LR, ADAM_B1, ADAM_B2, ADAM_EPS, ADAM_WD, ADAM_STEP = 0.001, 0.9, 0.999, 1e-08, 0.01, 10
N_CHIPS = 4
N_DEV = 8
SUBLANES = 8
LANES = 128
VMEM_LIMIT_BYTES = 56 * 1024 * 1024
GELU_C0 = math.sqrt(2.0 / math.pi)
GELU_C1 = 0.044715


def _pick(dim, pref, align):
    if dim <= pref:
        return dim
    t = (pref // align) * align
    while t >= align:
        if dim % t == 0:
            return t
        t -= align
    return dim


def _params(sem):
    return pltpu.CompilerParams(dimension_semantics=sem, vmem_limit_bytes=VMEM_LIMIT_BYTES)


def _call(name, body, grid, in_specs, out_specs, out_shape, scratch=(), nsp=0, sem=None, ride=None):
    if ride is not None:
        return _call_with_ride(name, body, grid, in_specs, out_specs, out_shape, scratch, ride)
    sem = sem or ("parallel",) * (len(grid) - 1) + ("arbitrary",)
    if nsp:
        gs = pltpu.PrefetchScalarGridSpec(num_scalar_prefetch=nsp, grid=grid, in_specs=in_specs,
                                          out_specs=out_specs, scratch_shapes=list(scratch))
        return pl.pallas_call(body, name=name, grid_spec=gs, out_shape=out_shape, compiler_params=_params(sem))
    return pl.pallas_call(body, name=name, grid=grid, in_specs=in_specs, out_specs=out_specs,
                          out_shape=out_shape, scratch_shapes=list(scratch), compiler_params=_params(sem))


def _sds(shape, dtype):
    return jax.ShapeDtypeStruct(shape, dtype)


class _Exchange:
    def __init__(self, ins, outs, n_sems, start, finish, in_place):
        self.ins, self.outs, self.n_sems = list(ins), list(outs), n_sems
        self.start, self.finish, self.in_place = start, finish, in_place

    def alone(self, name):
        n_in, n_out = len(self.ins), len(self.outs)

        def body(*refs):
            ins, outs = refs[:n_in], refs[n_in:n_in + n_out]
            send_sems, recv_sems = refs[n_in + n_out:]
            self.start(ins, outs, send_sems, recv_sems)
            self.finish(ins, outs, send_sems, recv_sems)

        sems = [pltpu.SemaphoreType.DMA((self.n_sems,)), pltpu.SemaphoreType.DMA((self.n_sems,))]
        aliases = {i: i for i in range(n_in)} if self.in_place else {}
        return pl.pallas_call(body, name=name, out_shape=self.outs, in_specs=[ANY] * n_in, out_specs=[ANY] * n_out,
                              scratch_shapes=sems, input_output_aliases=aliases)(*self.ins)


def _call_with_ride(name, body, grid, in_specs, out_specs, out_shape, scratch, ride):
    single = not isinstance(out_specs, (list, tuple))
    out_specs = [out_specs] if single else list(out_specs)
    out_shape = [out_shape] if single else list(out_shape)
    n_in, n_out, n_sc = len(in_specs), len(out_specs), len(scratch)
    r_in, r_out = len(ride.ins), len(ride.outs)

    def wrapped(*refs):
        pos = 0
        parts = []
        for n in (n_in, r_in, n_out, r_out, n_sc, 2):
            parts.append(refs[pos:pos + n])
            pos += n
        core_in, ride_in, core_out, ride_out, core_scratch, (send_sems, recv_sems) = parts
        first = pl.program_id(0) == 0
        last = pl.program_id(0) == grid[0] - 1
        for axis in range(1, len(grid)):
            first = jnp.logical_and(first, pl.program_id(axis) == 0)
            last = jnp.logical_and(last, pl.program_id(axis) == grid[axis] - 1)

        @pl.when(first)
        def _():
            ride.start(ride_in, ride_out, send_sems, recv_sems)

        body(*core_in, *core_out, *core_scratch)

        @pl.when(last)
        def _():
            ride.finish(ride_in, ride_out, send_sems, recv_sems)

    sems = [pltpu.SemaphoreType.DMA((ride.n_sems,)), pltpu.SemaphoreType.DMA((ride.n_sems,))]
    aliases = {n_in + k: n_out + k for k in range(r_in)} if ride.in_place else {}
    call = pl.pallas_call(wrapped, name=name, grid=grid, in_specs=list(in_specs) + [ANY] * r_in,
                          out_specs=out_specs + [ANY] * r_out, out_shape=out_shape + ride.outs,
                          scratch_shapes=list(scratch) + sems, input_output_aliases=aliases,
                          compiler_params=_params(("arbitrary",) * len(grid)))

    def run(*args):
        res = call(*args, *ride.ins)
        core = res[0] if single else list(res[:n_out])
        return core, list(res[n_out:])

    return run


def _T(tr, tc, off=0):
    return pl.BlockSpec((tr, tc), lambda j, i: (i, j + off))


def _P(rows, tc, off=0):
    return pl.BlockSpec((rows, tc), lambda j, i: (0, j + off))


def _gelu(x):
    t = jnp.tanh(GELU_C0 * (x + GELU_C1 * x * x * x))
    return 0.5 * x * (1.0 + t)


def _gelu_and_grad(x):
    t = jnp.tanh(GELU_C0 * (x + GELU_C1 * x * x * x))
    g = 0.5 * x * (1.0 + t)
    dg = 0.5 * (1.0 + t) + 0.5 * x * (1.0 - t * t) * GELU_C0 * (1.0 + 3.0 * GELU_C1 * x * x)
    return g, dg


def _sigmoid(x):
    return 1.0 / (1.0 + jnp.exp(-x))


def _accum(ref, val, first):
    @pl.when(first)
    def _():
        ref[...] = val

    @pl.when(jnp.logical_not(first))
    def _():
        ref[...] += val


def _colsum(v):
    return jnp.sum(v, axis=0, keepdims=True)


CHUNK = 16


def _chunks(n_rows, ch, step):
    def it(i, carry):
        step(pl.multiple_of(i * ch, ch))
        return carry

    lax.fori_loop(0, n_rows // ch, it, 0)


def _fold(v):
    return v[0:SUBLANES, :] + v[SUBLANES:2 * SUBLANES, :]


def _shift_down(xe, d, ch):
    return xe[SUBLANES:SUBLANES + ch, :] if d == 0 else pltpu.roll(xe, d, 0)[SUBLANES:SUBLANES + ch, :]


def _shift_up(xe, d, ch):
    return xe[0:ch, :] if d == 0 else pltpu.roll(xe, ch + SUBLANES - d, 0)[0:ch, :]


def _mm(a, b, *, name, ta=False, tb=False, out_dtype=F32, tm=1024, tn=1024, tk=4096, exact=False, bias=None, ride=None):
    if ta:
        K, M = a.shape
    else:
        M, K = a.shape
    if tb:
        N, K2 = b.shape
    else:
        K2, N = b.shape
    assert K == K2, (a.shape, b.shape, ta, tb)
    tm, tn, tk = _pick(M, tm, LANES), _pick(N, tn, LANES), _pick(K, tk, LANES)
    nk = K // tk
    cdt = F32 if exact else BF16
    prec = lax.Precision.HIGHEST if exact else None
    dims = (((0 if ta else 1,), (1 if tb else 0,)), ((), ()))

    def body(*refs):
        a_ref, b_ref = refs[0], refs[1]
        bias_ref = refs[2] if bias is not None else None
        o_ref = refs[3] if bias is not None else refs[2]
        part = lax.dot_general(a_ref[...].astype(cdt), b_ref[...].astype(cdt), dims,
                               preferred_element_type=F32, precision=prec)

        def finish(r):
            if bias is not None:
                r = r + bias_ref[...]
            o_ref[...] = r.astype(out_dtype)

        if nk == 1:
            finish(part)
            return
        acc_ref = refs[-1]
        k = pl.program_id(2)

        @pl.when(k == 0)
        def _():
            acc_ref[...] = part

        @pl.when(jnp.logical_and(k > 0, k < nk - 1))
        def _():
            acc_ref[...] += part

        @pl.when(k == nk - 1)
        def _():
            finish(acc_ref[...] + part)

    a_spec = pl.BlockSpec((tk, tm), lambda i, j, k: (k, i)) if ta else pl.BlockSpec((tm, tk), lambda i, j, k: (i, k))
    b_spec = pl.BlockSpec((tn, tk), lambda i, j, k: (j, k)) if tb else pl.BlockSpec((tk, tn), lambda i, j, k: (k, j))
    in_specs, args = [a_spec, b_spec], [a, b]
    if bias is not None:
        in_specs.append(pl.BlockSpec((1, tn), lambda i, j, k: (0, j)))
        args.append(bias)
    scratch = [pltpu.VMEM((tm, tn), F32)] if nk > 1 else []
    return _call(name, body, (M // tm, N // tn, nk), in_specs, pl.BlockSpec((tm, tn), lambda i, j, k: (i, j)),
                 _sds((M, N), out_dtype), scratch=scratch, sem=("parallel", "parallel", "arbitrary"), ride=ride)(*args)


def _ew(name, fn, ins, out_dtypes, S, C, rows=(), tr=256, tc=512):
    g = C
    for _, off in ins:
        g = math.gcd(g, off) if off else g
    tc = _pick(g, tc, LANES)
    tr = _pick(S, tr, 16)
    n_in, n_row, n_out = len(ins), len(rows), len(out_dtypes)

    def body(*refs):
        vals = [r[...].astype(F32) for r in refs[:n_in + n_row]]
        for o_ref, o in zip(refs[n_in + n_row:], fn(*vals)):
            o_ref[...] = o.astype(o_ref.dtype)

    in_specs = [_T(tr, tc, off // tc) for _, off in ins] + [_P(1, tc) for _ in rows]
    res = _call(name, body, (C // tc, S // tr), in_specs, [_T(tr, tc) for _ in out_dtypes],
                [_sds((S, C), d) for d in out_dtypes], sem=("parallel", "parallel"))(*[a for a, _ in ins], *rows)
    return res


def _adaln_fwd(x, norm, scale, shift, *, name, t=None, gate=None):
    S, D = x.shape
    tr = _pick(S, 256, 16)
    resid = t is not None

    def body(*refs):
        if resid:
            x_ref, t_ref, g_ref, n_ref, sc_ref, sh_ref, u_ref, h_ref = refs
            h = x_ref[...] + g_ref[...] * t_ref[...]
            h_ref[...] = h
        else:
            x_ref, n_ref, sc_ref, sh_ref, u_ref = refs
            h = x_ref[...]
        r = lax.rsqrt(jnp.mean(h * h, axis=-1, keepdims=True) + EPS)
        u_ref[...] = (h * r * (n_ref[...] * (1.0 + sc_ref[...])) + sh_ref[...]).astype(BF16)

    full, row = _T(tr, D), _P(1, D)
    if resid:
        return _call(name, body, (1, S // tr), [full, full, row, row, row, row], [full, full],
                     [_sds((S, D), BF16), _sds((S, D), F32)])(x, t, gate, norm, scale, shift)
    return _call(name, body, (1, S // tr), [full, row, row, row], full, _sds((S, D), BF16))(x, norm, scale, shift)


def _adaln_bwd(h, du, dres, norm, scale, *, name, t=None, gate=None):
    S, D = h.shape
    tr = _pick(S, 256, 16)
    gated = t is not None

    def body(*refs):
        if gated:
            h_ref, du_ref, dr_ref, n_ref, sc_ref, t_ref, g_ref, dh_ref, dsh_ref, dsc_ref, dn_ref, dt_ref, dg_ref = refs
        else:
            h_ref, du_ref, dr_ref, n_ref, sc_ref, dh_ref, dsh_ref, dsc_ref, dn_ref = refs
        first = pl.program_id(1) == 0
        hv, duv = h_ref[...], du_ref[...]
        r = lax.rsqrt(jnp.mean(hv * hv, axis=-1, keepdims=True) + EPS)
        xn = hv * r
        one_sc = 1.0 + sc_ref[...]
        dxn = duv * (n_ref[...] * one_sc)
        dh = dr_ref[...] + r * (dxn - xn * jnp.mean(dxn * xn, axis=-1, keepdims=True))
        dh_ref[...] = dh
        dux = duv * xn
        _accum(dsh_ref, _colsum(duv), first)
        _accum(dsc_ref, _colsum(dux * n_ref[...]), first)
        _accum(dn_ref, _colsum(dux * one_sc), first)
        if gated:
            dt_ref[...] = (dh * g_ref[...]).astype(BF16)
            _accum(dg_ref, _colsum(dh * t_ref[...]), first)

    full, row = _T(tr, D), _P(1, D)
    rowo = _sds((1, D), F32)
    if gated:
        return _call(name, body, (1, S // tr), [full, full, full, row, row, full, row],
                     [full, row, row, row, full, row],
                     [_sds((S, D), F32), rowo, rowo, rowo, _sds((S, D), BF16), rowo])(h, du, dres, norm, scale, t, gate)
    return _call(name, body, (1, S // tr), [full, full, full, row, row], [full, row, row, row],
                 [_sds((S, D), F32), rowo, rowo, rowo])(h, du, dres, norm, scale)


def _final(h1, t2, gate2, norm_f, tgt, *, name):
    S, D = h1.shape
    tr = _pick(S, 256, 16)

    def body(h_ref, t_ref, g_ref, n_ref, y_ref, dh_ref, dt_ref, loss_ref, dn_ref, dg_ref):
        first = pl.program_id(1) == 0
        tv = t_ref[...]
        h2 = h_ref[...] + g_ref[...] * tv
        r = lax.rsqrt(jnp.mean(h2 * h2, axis=-1, keepdims=True) + EPS)
        xn = h2 * r
        e = xn * n_ref[...] - y_ref[...]
        part = 0.5 * jnp.sum(jnp.mean(e * e, axis=-1, keepdims=True), axis=0, keepdims=True)
        _accum(loss_ref, jnp.broadcast_to(part, (SUBLANES, LANES)), first)
        dy = e * (1.0 / D)
        _accum(dn_ref, _colsum(dy * xn), first)
        dxn = dy * n_ref[...]
        dh2 = r * (dxn - xn * jnp.mean(dxn * xn, axis=-1, keepdims=True))
        dh_ref[...] = dh2
        dt_ref[...] = (dh2 * g_ref[...]).astype(BF16)
        _accum(dg_ref, _colsum(dh2 * tv), first)

    full, row = _T(tr, D), _P(1, D)
    rowo = _sds((1, D), F32)
    return _call(name, body, (1, S // tr), [full, full, row, row, full],
                 [full, full, _P(SUBLANES, LANES), row, row],
                 [_sds((S, D), F32), _sds((S, D), BF16), _sds((SUBLANES, LANES), F32), rowo, rowo])(h1, t2, gate2, norm_f, tgt)


HALO = 16


def _halo_prev(tr, tc, off=0):
    return pl.BlockSpec((HALO, tc), lambda j, i: (jnp.maximum(i * (tr // HALO) - 1, 0), j + off))


def _halo_next(tr, tc, n_slabs, off=0):
    return pl.BlockSpec((HALO, tc), lambda j, i: (jnp.minimum((i + 1) * (tr // HALO), n_slabs - 1), j + off))


def _fill_past(buf, prev_ref, cur_ref, first, tr):
    buf[0:HALO, :] = jnp.where(first, 0.0, prev_ref[...].astype(F32))
    buf[HALO:HALO + tr, :] = cur_ref[...].astype(F32)


def _fill_future(buf, cur_ref, next_ref, last, tr):
    buf[0:tr, :] = cur_ref[...].astype(F32)
    buf[tr:tr + HALO, :] = jnp.where(last, 0.0, next_ref[...].astype(F32))


def _al8(r):
    return r if isinstance(r, int) else pl.multiple_of(r, SUBLANES)


def _past_rows(buf, r0, taps):
    xe = buf[pl.ds(_al8(r0 + HALO - SUBLANES), CHUNK + SUBLANES), :]
    return [_shift_down(xe, d, CHUNK) for d in range(taps)]


def _future_rows(buf, r0, taps):
    xe = buf[pl.ds(_al8(r0), CHUNK + SUBLANES), :]
    return [_shift_up(xe, d, CHUNK) for d in range(taps)]


def _taps(w_ref):
    return [w_ref[k:k + 1, :] for k in range(w_ref.shape[0])]


def _conv(xs, w, b):
    taps = len(xs)
    y = w[taps - 1] * xs[0] if b is None else b + w[taps - 1] * xs[0]
    for d in range(1, taps):
        y = y + w[taps - 1 - d] * xs[d]
    return y


def _conv_fwd(x, off, C, w, b, *, name):
    S = x.shape[0]
    taps = w.shape[0]
    tc = _pick(math.gcd(C, off) if off else C, 512, LANES)
    tr = _pick(S, 256, 16)

    def body(x_ref, p_ref, w_ref, b_ref, y_ref, buf):
        _fill_past(buf, p_ref, x_ref, pl.program_id(1) == 0, tr)
        wt, bv = _taps(w_ref), b_ref[...]

        def step(r0):
            y_ref[pl.ds(r0, CHUNK), :] = _conv(_past_rows(buf, r0, taps), wt, bv)

        _chunks(tr, CHUNK, step)

    return _call(name, body, (C // tc, S // tr),
                 [_T(tr, tc, off // tc), _halo_prev(tr, tc, off // tc), _P(taps, tc), _P(1, tc)],
                 _T(tr, tc), _sds((S, C), F32), scratch=[pltpu.VMEM((tr + HALO, tc), F32)],
                 sem=("parallel", "parallel"))(x, x, w, b)


def _finish_conv_grads(acc, base, taps, dwb_ref, first):
    @pl.when(first)
    def _():
        dwb_ref[...] = jnp.zeros_like(dwb_ref)

    for d in range(taps):
        dwb_ref[taps - 1 - d:taps - d, :] += _colsum(acc[base + SUBLANES * d:base + SUBLANES * (d + 1), :])
    dwb_ref[taps:taps + 1, :] += _colsum(acc[base + SUBLANES * taps:base + SUBLANES * (taps + 1), :])


def _add_conv_grads(acc, base, dy, xs):
    taps = len(xs)
    for d in range(taps):
        acc[base + SUBLANES * d:base + SUBLANES * (d + 1), :] += _fold(dy * xs[d])
    acc[base + SUBLANES * taps:base + SUBLANES * (taps + 1), :] += _fold(dy)


def _conv_bwd(dy, x, off, C, w, *, name):
    S = dy.shape[0]
    taps = w.shape[0]
    assert taps < SUBLANES
    tc = _pick(math.gcd(C, off) if off else C, 512, LANES)
    tr = _pick(S, 256, 16)
    n_slabs = S // HALO

    def body(dy_ref, nx_ref, x_ref, p_ref, w_ref, dx_ref, dwb_ref, fbuf, pbuf, acc):
        i = pl.program_id(1)
        first = i == 0
        _fill_future(fbuf, dy_ref, nx_ref, i == S // tr - 1, tr)
        _fill_past(pbuf, p_ref, x_ref, first, tr)
        acc[...] = jnp.zeros_like(acc)
        wt = _taps(w_ref)

        def step(r0):
            dys = _future_rows(fbuf, r0, taps)
            dx_ref[pl.ds(r0, CHUNK), :] = _conv(dys, wt, None).astype(BF16)
            _add_conv_grads(acc, 0, dys[0], _past_rows(pbuf, r0, taps))

        _chunks(tr, CHUNK, step)
        _finish_conv_grads(acc, 0, taps, dwb_ref, first)

    return _call(name, body, (C // tc, S // tr),
                 [_T(tr, tc), _halo_next(tr, tc, n_slabs), _T(tr, tc, off // tc), _halo_prev(tr, tc, off // tc),
                  _P(taps, tc)],
                 [_T(tr, tc), _P(SUBLANES, tc)],
                 [_sds((S, C), BF16), _sds((SUBLANES, C), F32)],
                 scratch=[pltpu.VMEM((tr + HALO, tc), F32), pltpu.VMEM((tr + HALO, tc), F32),
                          pltpu.VMEM((SUBLANES * (taps + 1), tc), F32)])(dy, dy, x, x, w)


def _geglu_fwd(up, w, b, F, *, name):
    S = up.shape[0]
    taps = w.shape[0]
    tc = _pick(F, 512, LANES)
    tr = _pick(S, 256, 16)
    nf = F // tc

    def body(g_ref, gp_ref, v_ref, vp_ref, wg_ref, wv_ref, bg_ref, bv_ref, a_ref, ge_ref, dge_ref, vo_ref, gbuf, vbuf):
        first = pl.program_id(1) == 0
        _fill_past(gbuf, gp_ref, g_ref, first, tr)
        _fill_past(vbuf, vp_ref, v_ref, first, tr)
        wg, wv, bg, bv = _taps(wg_ref), _taps(wv_ref), bg_ref[...], bv_ref[...]

        def step(r0):
            rows = pl.ds(r0, CHUNK)
            ge, dge = _gelu_and_grad(_conv(_past_rows(gbuf, r0, taps), wg, bg))
            v = _conv(_past_rows(vbuf, r0, taps), wv, bv)
            a_ref[rows, :] = (ge * v).astype(BF16)
            ge_ref[rows, :] = ge.astype(BF16)
            dge_ref[rows, :] = dge.astype(BF16)
            vo_ref[rows, :] = v.astype(BF16)

        _chunks(tr, CHUNK, step)

    buf = pltpu.VMEM((tr + HALO, tc), F32)
    t, o = _T(tr, tc), _sds((S, F), BF16)
    return _call(name, body, (nf, S // tr),
                 [t, _halo_prev(tr, tc), _T(tr, tc, nf), _halo_prev(tr, tc, nf),
                  _P(taps, tc), _P(taps, tc, nf), _P(1, tc), _P(1, tc, nf)],
                 [t, t, t, t], [o, o, o, o], scratch=[buf, buf], sem=("parallel", "parallel"))(up, up, up, up, w, w, b, b)


def _ffn_bwd(up, w, ge, dge, vv, da, F, *, name):
    S = up.shape[0]
    taps = w.shape[0]
    tc = _pick(F, 512, LANES)
    tr = _pick(S, 256, 16)
    nf, nr = F // tc, S // tr
    ext = tr + HALO
    n_slabs = S // HALO

    assert HALO == CHUNK
    v_base = SUBLANES * (taps + 1)

    def body(g_ref, gp_ref, v_ref, vp_ref, wg_ref, wv_ref, ge_ref, gen_ref, dge_ref, dgen_ref, vo_ref, von_ref,
             da_ref, dan_ref, dxg_ref, dxv_ref, dwbg_ref, dwbv_ref, gbuf, vbuf, dgbuf, dvbuf, acc):
        i = pl.program_id(1)
        first, last = i == 0, i == nr - 1
        _fill_past(gbuf, gp_ref, g_ref, first, tr)
        _fill_past(vbuf, vp_ref, v_ref, first, tr)
        acc[...] = jnp.zeros_like(acc)
        wg, wv = _taps(wg_ref), _taps(wv_ref)

        def grads(r0, da, gev, dgev, vov, in_tile):
            dg = da * vov.astype(F32) * dgev.astype(F32)
            dv = da * gev.astype(F32)
            dgbuf[pl.ds(_al8(r0), CHUNK), :] = dg
            dvbuf[pl.ds(_al8(r0), CHUNK), :] = dv
            if in_tile:
                _add_conv_grads(acc, 0, dg, _past_rows(gbuf, r0, taps))
                _add_conv_grads(acc, v_base, dv, _past_rows(vbuf, r0, taps))

        def in_tile(r0):
            rows = pl.ds(r0, CHUNK)
            grads(r0, da_ref[rows, :], ge_ref[rows, :], dge_ref[rows, :], vo_ref[rows, :], True)

        _chunks(tr, CHUNK, in_tile)
        grads(tr, jnp.where(last, 0.0, dan_ref[...]), gen_ref[...], dgen_ref[...], von_ref[...], False)

        def back(r0):
            dxg_ref[pl.ds(r0, CHUNK), :] = _conv(_future_rows(dgbuf, r0, taps), wg, None).astype(BF16)
            dxv_ref[pl.ds(r0, CHUNK), :] = _conv(_future_rows(dvbuf, r0, taps), wv, None).astype(BF16)

        _chunks(tr, CHUNK, back)
        _finish_conv_grads(acc, 0, taps, dwbg_ref, first)
        _finish_conv_grads(acc, v_base, taps, dwbv_ref, first)

    xbuf = pltpu.VMEM((HALO + tr, tc), F32)
    dbuf = pltpu.VMEM((ext, tc), F32)
    t, tw, t8, nx = _T(tr, tc), _P(taps, tc), _P(SUBLANES, tc), _halo_next(tr, tc, n_slabs)
    return _call(name, body, (nf, nr),
                 [t, _halo_prev(tr, tc), _T(tr, tc, nf), _halo_prev(tr, tc, nf), tw, _P(taps, tc, nf),
                  t, nx, t, nx, t, nx, t, nx],
                 [t, t, t8, t8],
                 [_sds((S, F), BF16), _sds((S, F), BF16), _sds((SUBLANES, F), F32), _sds((SUBLANES, F), F32)],
                 scratch=[xbuf, xbuf, dbuf, dbuf, pltpu.VMEM((2 * v_base, tc), F32)])(
                     up, up, up, up, w, w, ge, ge, dge, dge, vv, vv, da, da)


def _softplus_neg(lam):
    z = -lam
    e = jnp.exp(-jnp.abs(z))
    u = 1.0 + e
    log1p_e = jnp.where(u == 1.0, e, jnp.log(u) * e / jnp.where(u == 1.0, 1.0, u - 1.0))
    sp = jnp.maximum(z, 0.0) + log1p_e
    sg = jnp.where(z >= 0, 1.0 / u, e / u)
    return sp, sg


def _one_minus_exp(x):
    series = -x * (1.0 + x * (0.5 + x * (1.0 / 6.0 + x * (1.0 / 24.0))))
    return jnp.where(x > -0.01, series, 1.0 - jnp.exp(x))


def _gate_values(xc, wa_ref, wi_ref, ba_ref, bi_ref, lam_ref, is_t0):
    xb = xc.astype(BF16)
    ra = _sigmoid(jnp.dot(xb, wa_ref[0].astype(BF16), preferred_element_type=F32) + ba_ref[...])
    ia = _sigmoid(jnp.dot(xb, wi_ref[0].astype(BF16), preferred_element_type=F32) + bi_ref[...])
    sp, sg = _softplus_neg(lam_ref[...])
    log_a = -RG_C * ra * sp
    a = jnp.exp(log_a)
    mult = jnp.where(is_t0, 1.0, jnp.sqrt(_one_minus_exp(2.0 * log_a)))
    return ra, ia, sp, sg, a, mult


def _rnn_blockspecs(tr):
    x = pl.BlockSpec((tr, LANES), lambda n, i: (i, n))
    w = pl.BlockSpec((1, LANES, LANES), lambda n, i: (n, 0, 0))
    p = pl.BlockSpec((1, LANES), lambda n, i: (0, n))
    return x, w, p


def _is_t0(tr):
    rows = lax.broadcasted_iota(jnp.int32, (tr, LANES), 0)
    return jnp.logical_and(pl.program_id(1) == 0, rows == 0)


def _gates_fwd(xc, w_a, b_a, w_i, b_i, lam, *, name):
    S, C = xc.shape
    tr = _pick(S, 1024, 16)

    def body(x_ref, wa_ref, wi_ref, ba_ref, bi_ref, lam_ref, a_ref, inp_ref):
        xv = x_ref[...]
        _, ia, _, _, a, mult = _gate_values(xv, wa_ref, wi_ref, ba_ref, bi_ref, lam_ref, _is_t0(tr))
        a_ref[...] = a
        inp_ref[...] = mult * ia * xv

    x, w, p = _rnn_blockspecs(tr)
    return _call(name, body, (C // LANES, S // tr), [x, w, w, p, p, p], [x, x],
                 [_sds((S, C), F32), _sds((S, C), F32)], sem=("parallel", "parallel"))(xc, w_a, w_i, b_a, b_i, lam)


def _gates_bwd(xc, dacc, hprev, w_a, b_a, w_i, b_i, lam, *, name, ride=None):
    S, C = xc.shape
    nb = C // LANES
    tr = _pick(S, 1024, 16)

    def body(x_ref, d_ref, hp_ref, wa_ref, wi_ref, ba_ref, bi_ref, lam_ref,
             dx_ref, dwa_ref, dwi_ref, dba_ref, dbi_ref, dlam_ref):
        first = pl.program_id(1) == 0
        t0 = _is_t0(tr)
        xv, dv = x_ref[...], d_ref[...]
        ra, ia, sp, sg, a, mult = _gate_values(xv, wa_ref, wi_ref, ba_ref, bi_ref, lam_ref, t0)
        d_a = dv * hp_ref[...]
        d_mult = dv * ia * xv
        d_ia = dv * mult * xv
        d_log = d_a * a + jnp.where(t0, 0.0, -d_mult * (a * a) / mult)
        d_pa = d_log * (-RG_C * sp) * ra * (1.0 - ra)
        d_pi = d_ia * ia * (1.0 - ia)
        xb, dab, dib = xv.astype(BF16), d_pa.astype(BF16), d_pi.astype(BF16)
        nt = (((1,), (1,)), ((), ()))
        tn = (((0,), (0,)), ((), ()))
        dx_ref[...] = (dv * mult * ia
                       + lax.dot_general(dab, wa_ref[0].astype(BF16), nt, preferred_element_type=F32)
                       + lax.dot_general(dib, wi_ref[0].astype(BF16), nt, preferred_element_type=F32))
        _accum(dwa_ref, lax.dot_general(xb, dab, tn, preferred_element_type=F32)[None], first)
        _accum(dwi_ref, lax.dot_general(xb, dib, tn, preferred_element_type=F32)[None], first)
        _accum(dba_ref, _colsum(d_pa), first)
        _accum(dbi_ref, _colsum(d_pi), first)
        _accum(dlam_ref, _colsum(d_log * (-RG_C) * ra) * (-sg), first)

    x, w, p = _rnn_blockspecs(tr)
    row = _sds((1, C), F32)
    return _call(name, body, (nb, S // tr), [x, x, x, w, w, p, p, p], [x, w, w, p, p, p],
                 [_sds((S, C), F32), _sds((nb, LANES, LANES), F32), _sds((nb, LANES, LANES), F32), row, row, row],
                 ride=ride)(xc, dacc, hprev, w_a, w_i, b_a, b_i, lam)


def _scan_fwd(a, inp, *, name):
    S, C = a.shape
    tc = _pick(C, 1280, LANES)
    tr = _pick(S, 256, SUBLANES)

    def body(a_ref, b_ref, h_ref, hp_ref, carry):
        @pl.when(pl.program_id(1) == 0)
        def _():
            carry[...] = jnp.zeros_like(carry)

        rows = lax.broadcasted_iota(jnp.int32, (SUBLANES, tc), 0)

        def slab(s, h_in):
            base = pl.multiple_of(s * SUBLANES, SUBLANES)
            pa, pb = a_ref[pl.ds(base, SUBLANES), :], b_ref[pl.ds(base, SUBLANES), :]
            for d in (1, 2, 4):
                reach = rows >= d
                pb = jnp.where(reach, pb + pa * pltpu.roll(pb, d, 0), pb)
                pa = jnp.where(reach, pa * pltpu.roll(pa, d, 0), pa)
            h = pb + pa * h_in
            h_ref[pl.ds(base, SUBLANES), :] = h
            hp_ref[pl.ds(base, SUBLANES), :] = jnp.where(rows == 0, h_in, pltpu.roll(h, 1, 0))
            return h[SUBLANES - 1:SUBLANES, :]

        carry[...] = lax.fori_loop(0, tr // SUBLANES, slab, carry[...])

    t = _T(tr, tc)
    return _call(name, body, (C // tc, S // tr), [t, t], [t, t], [_sds((S, C), F32), _sds((S, C), F32)],
                 scratch=[pltpu.VMEM((1, tc), F32)])(a, inp)


def _scan_bwd(a, dh, *, name):
    S, C = a.shape
    tc = _pick(C, 1280, LANES)
    tr = _pick(S, 256, SUBLANES)
    nr = S // tr

    def body(a_ref, d_ref, o_ref, carry):
        @pl.when(pl.program_id(1) == 0)
        def _():
            carry[...] = jnp.zeros_like(carry)

        rows = lax.broadcasted_iota(jnp.int32, (SUBLANES, tc), 0)
        n_slabs = tr // SUBLANES

        def slab(s, g_in):
            base = pl.multiple_of((n_slabs - 1 - s) * SUBLANES, SUBLANES)
            pa, dv = a_ref[pl.ds(base, SUBLANES), :], d_ref[pl.ds(base, SUBLANES), :]
            pb = pa * dv
            for d in (1, 2, 4):
                reach = rows < SUBLANES - d
                pb = jnp.where(reach, pb + pa * pltpu.roll(pb, SUBLANES - d, 0), pb)
                pa = jnp.where(reach, pa * pltpu.roll(pa, SUBLANES - d, 0), pa)
            g = pb + pa * g_in
            o_ref[pl.ds(base, SUBLANES), :] = dv + jnp.where(rows == SUBLANES - 1, g_in, pltpu.roll(g, SUBLANES - 1, 0))
            return g[0:1, :]

        carry[...] = lax.fori_loop(0, n_slabs, slab, carry[...])

    t = pl.BlockSpec((tr, tc), lambda j, i: (nr - 1 - i, j))
    return _call(name, body, (C // tc, nr), [t, t], t, _sds((S, C), F32), scratch=[pltpu.VMEM((1, tc), F32)])(a, dh)


SCALE = HEAD_DIM ** -0.5


def _bucket_table(num_buckets):
    qi = np.arange(BLOCK)[:, None]
    j = np.arange(BLOCK)[None, :]
    dist = np.where(j <= qi, qi - j, qi + BLOCK - j)
    d = np.maximum(dist, 1).astype(np.float64)
    large = NUM_EXACT + (np.log(d / NUM_EXACT) / math.log(MAX_DISTANCE / NUM_EXACT) * (num_buckets - NUM_EXACT)).astype(np.int32)
    large = np.minimum(large, num_buckets - 1)
    return np.where(dist < NUM_EXACT, dist, large).astype(np.int32)


def _bias_table(rel_bias, n_heads, *, name):
    nbk = rel_bias.shape[0]
    bucket = jnp.asarray(_bucket_table(nbk))

    def body(rb_ref, bk_ref, o_ref):
        t, h = pl.program_id(0), pl.program_id(1)
        bk = bk_ref[...]
        acc = jnp.zeros((BLOCK, BLOCK), F32)
        for b in range(nbk):
            acc = jnp.where(bk == b, rb_ref[b, h], acc)
        o_ref[0, 0] = jnp.where(jnp.logical_and(t == 0, jnp.logical_not(_own_block())), NEG_INF, acc)

    return _call(name, body, (2, n_heads),
                 [pl.BlockSpec(memory_space=pltpu.SMEM), pl.BlockSpec((BLOCK, BLOCK), lambda t, h: (0, 0))],
                 pl.BlockSpec((1, 1, BLOCK, BLOCK), lambda t, h: (t, h, 0, 0)),
                 _sds((2, n_heads, BLOCK, BLOCK), F32), sem=("arbitrary", "arbitrary"))(rel_bias, bucket)


def _bias_table_bwd(dbias, nbk, *, name):
    n_heads = dbias.shape[0]
    bucket = jnp.asarray(_bucket_table(nbk))

    def body(db_ref, bk_ref, o_ref):
        h = pl.program_id(0)
        bk = bk_ref[...]
        dv = db_ref[0]
        rows = lax.broadcasted_iota(jnp.int32, (nbk, LANES), 0)
        lanes = lax.broadcasted_iota(jnp.int32, (nbk, LANES), 1)
        acc = jnp.zeros((nbk, LANES), F32)
        for b in range(nbk):
            s = jnp.sum(jnp.sum(jnp.where(bk == b, dv, 0.0), axis=0, keepdims=True), axis=1, keepdims=True)
            acc = jnp.where(jnp.logical_and(rows == b, lanes == h), s, acc)
        _accum(o_ref, acc, h == 0)

    return _call(name, body, (n_heads,),
                 [pl.BlockSpec((1, BLOCK, BLOCK), lambda h: (h, 0, 0)), pl.BlockSpec((BLOCK, BLOCK), lambda h: (0, 0))],
                 pl.BlockSpec((nbk, LANES), lambda h: (0, 0)), _sds((nbk, LANES), F32), sem=("arbitrary",))(dbias, bucket)


def _own_block(heads=1):
    qi = lax.broadcasted_iota(jnp.int32, (heads * BLOCK, BLOCK), 0) % BLOCK
    j = lax.broadcasted_iota(jnp.int32, (heads * BLOCK, BLOCK), 1)
    return j <= qi


def _to_window(band, own):
    return jnp.where(own, band[:, BLOCK:], band[:, :BLOCK])


def _to_band(win, own):
    return jnp.concatenate([jnp.where(own, 0.0, win), jnp.where(own, win, 0.0)], axis=1)


def _stack_pairs(ref, group):
    return jnp.concatenate([ref[:, (g // 2) * LANES:(g // 2 + 1) * LANES] for g in range(group)], axis=0).astype(F32)


def _own_lanes(group):
    rows = lax.broadcasted_iota(jnp.int32, (group * BLOCK, LANES), 0)
    lanes = lax.broadcasted_iota(jnp.int32, (group * BLOCK, LANES), 1)
    return (lanes < HEAD_DIM) == ((rows // BLOCK) % 2 == 0)


def _unstack_pairs(stacked, o_ref, group):
    lo = lax.broadcasted_iota(jnp.int32, (BLOCK, LANES), 1) < HEAD_DIM
    for p in range(group // 2):
        even = stacked[2 * p * BLOCK:(2 * p + 1) * BLOCK, :]
        odd = stacked[(2 * p + 1) * BLOCK:(2 * p + 2) * BLOCK, :]
        o_ref[:, p * LANES:(p + 1) * LANES] = jnp.where(lo, even, odd).astype(o_ref.dtype)


def _group_probs(qm, kb, b_ref, s_ref, own, group):
    band = lax.dot_general(qm, kb, (((1,), (1,)), ((), ())), preferred_element_type=F32)
    bias = jnp.concatenate([b_ref[0, g] for g in range(group)], axis=0)
    sink = jnp.concatenate([jnp.broadcast_to(s_ref[g][:, 0:1], (BLOCK, 1)) for g in range(group)], axis=0)
    s = _to_window(band, own) + bias
    m = jnp.maximum(jnp.max(s, axis=-1, keepdims=True), sink)
    p = jnp.exp(s - m)
    es = jnp.exp(sink - m)
    inv = 1.0 / (jnp.sum(p, axis=-1, keepdims=True) + es)
    return p, inv, es


def _attn_fwd(proj, bias, sinks_b, n_kv, group, q_off, k_off, v_off, *, name, ride=None):
    S = proj.shape[0]
    nblk = S // BLOCK
    gw = group * HEAD_DIM
    pairs = group // 2

    def body(q_ref, kc_ref, kp_ref, vc_ref, vp_ref, b_ref, s_ref, o_ref):
        kb = jnp.concatenate([kp_ref[...], kc_ref[...]], axis=0).astype(BF16)
        vb = jnp.concatenate([vp_ref[...], vc_ref[...]], axis=0).astype(BF16)
        own = _own_block(group)
        qm = jnp.where(_own_lanes(group), _stack_pairs(q_ref, group) * SCALE, 0.0).astype(BF16)
        pu, inv, _ = _group_probs(qm, kb, b_ref, s_ref, own, group)
        o = jnp.dot(_to_band(pu, own).astype(BF16), vb, preferred_element_type=F32) * inv
        _unstack_pairs(o, o_ref, group)

    qb, kb0, vb0 = q_off // gw, k_off // LANES, v_off // LANES
    cur = lambda b0: pl.BlockSpec((BLOCK, LANES), lambda j, n: (n, b0 + j))
    prev = lambda b0: pl.BlockSpec((BLOCK, LANES), lambda j, n: (jnp.maximum(n - 1, 0), b0 + j))
    return _call(name, body, (n_kv, nblk),
                 [pl.BlockSpec((BLOCK, gw), lambda j, n: (n, qb + j)), cur(kb0), prev(kb0), cur(vb0), prev(vb0),
                  pl.BlockSpec((1, group, BLOCK, BLOCK), lambda j, n: (jnp.minimum(n, 1), j, 0, 0)),
                  pl.BlockSpec((group, 1, LANES), lambda j, n: (j, 0, 0))],
                 pl.BlockSpec((BLOCK, gw), lambda j, n: (n, j)), _sds((S, n_kv * gw), BF16),
                 sem=("parallel", "parallel"), ride=ride)(proj, proj, proj, proj, proj, bias, sinks_b)


def _attn_bwd(proj, o, do, bias, sinks_b, n_kv, group, q_off, k_off, v_off, *, name, ride=None):
    S = proj.shape[0]
    nblk = S // BLOCK
    gw = group * HEAD_DIM
    rows_all = group * BLOCK
    nt = (((1,), (1,)), ((), ()))
    tn = (((0,), (0,)), ((), ()))

    def body(q_ref, kc_ref, kp_ref, vc_ref, vp_ref, b_ref, s_ref, o_ref, do_ref,
             dq_ref, dk_ref, dv_ref, db_ref, ds_ref, ck, cv, sacc):
        n = pl.program_id(1)
        lo2 = lax.broadcasted_iota(jnp.int32, (2 * BLOCK, LANES), 1) < HEAD_DIM

        @pl.when(n == 0)
        def _():
            ck[...] = jnp.zeros_like(ck)
            cv[...] = jnp.zeros_like(cv)
            sacc[...] = jnp.zeros_like(sacc)
            db_ref[...] = jnp.zeros_like(db_ref)

        @pl.when(n < nblk)
        def _():
            kb = jnp.concatenate([kp_ref[...], kc_ref[...]], axis=0).astype(BF16)
            vb = jnp.concatenate([vp_ref[...], vc_ref[...]], axis=0).astype(BF16)
            own = _own_block(group)
            mine = _own_lanes(group)
            qm = jnp.where(mine, _stack_pairs(q_ref, group) * SCALE, 0.0).astype(BF16)
            dom = jnp.where(mine, _stack_pairs(do_ref, group), 0.0)
            pu, inv, es = _group_probs(qm, kb, b_ref, s_ref, own, group)
            pr = pu * inv
            delta = jnp.sum(dom * _stack_pairs(o_ref, group), axis=-1, keepdims=True)
            domb = dom.astype(BF16)
            dp = _to_window(lax.dot_general(domb, vb, nt, preferred_element_type=F32), own)
            ds = pr * (dp - delta)
            for g in range(group):
                db_ref[g] += ds[g * BLOCK:(g + 1) * BLOCK, :]
            sacc[...] += -(es * inv) * delta
            dsb = _to_band(ds, own).astype(BF16)
            _unstack_pairs(jnp.dot(dsb, kb, preferred_element_type=F32) * SCALE, dq_ref, group)
            dk_acc = lax.dot_general(dsb, qm, tn, preferred_element_type=F32)
            dv_acc = lax.dot_general(_to_band(pr, own).astype(BF16), domb, tn, preferred_element_type=F32)
            dkf = jnp.where(lo2, dk_acc + pltpu.roll(dk_acc, HEAD_DIM, 1), 0.0)
            dvf = jnp.where(lo2, dv_acc + pltpu.roll(dv_acc, HEAD_DIM, 1), 0.0)

            @pl.when(n > 0)
            def _():
                dk_ref[...] = (ck[...] + dkf[0:BLOCK, :]).astype(BF16)
                dv_ref[...] = (cv[...] + dvf[0:BLOCK, :]).astype(BF16)

            ck[...] = dkf[BLOCK:2 * BLOCK, :]
            cv[...] = dvf[BLOCK:2 * BLOCK, :]

        @pl.when(n == nblk)
        def _():
            dk_ref[...] = ck[...].astype(BF16)
            dv_ref[...] = cv[...].astype(BF16)
            lane = lax.broadcasted_iota(jnp.int32, (1, LANES), 1)
            tot = jnp.zeros((1, LANES), F32)
            for g in range(group):
                tot = jnp.where(lane == g, _colsum(sacc[g * BLOCK:(g + 1) * BLOCK, :]), tot)
            ds_ref[0] = tot

    qb, kb0, vb0 = q_off // gw, k_off // LANES, v_off // LANES
    last = nblk - 1
    cur = lambda b0: pl.BlockSpec((BLOCK, LANES), lambda j, n: (jnp.minimum(n, last), b0 + j))
    prev = lambda b0: pl.BlockSpec((BLOCK, LANES), lambda j, n: (jnp.clip(n - 1, 0, last), b0 + j))
    qspec = lambda b0: pl.BlockSpec((BLOCK, gw), lambda j, n: (jnp.minimum(n, last), b0 + j))
    kvout = pl.BlockSpec((BLOCK, LANES), lambda j, n: (jnp.maximum(n - 1, 0), j))
    return _call(name, body, (n_kv, nblk + 1),
                 [qspec(qb), cur(kb0), prev(kb0), cur(vb0), prev(vb0),
                  pl.BlockSpec((1, group, BLOCK, BLOCK), lambda j, n: (jnp.minimum(n, 1), j, 0, 0)),
                  pl.BlockSpec((group, 1, LANES), lambda j, n: (j, 0, 0)), qspec(0), qspec(0)],
                 [qspec(0), kvout, kvout, pl.BlockSpec((group, BLOCK, BLOCK), lambda j, n: (j, 0, 0)),
                  pl.BlockSpec((1, 1, LANES), lambda j, n: (j, 0, 0))],
                 [_sds((S, n_kv * gw), BF16), _sds((S, n_kv * LANES), BF16), _sds((S, n_kv * LANES), BF16),
                  _sds((n_kv * group, BLOCK, BLOCK), F32), _sds((n_kv, 1, LANES), F32)],
                 scratch=[pltpu.VMEM((BLOCK, LANES), F32), pltpu.VMEM((BLOCK, LANES), F32),
                          pltpu.VMEM((rows_all, 1), F32)], ride=ride)(proj, proj, proj, proj, proj, bias, sinks_b, o, do)


def _adamw(w, g, m, v, *, name, ride=None):
    R, C = w.shape
    tr = _pick(R, 128, SUBLANES)
    bc1 = 1.0 - ADAM_B1 ** ADAM_STEP
    bc2 = 1.0 - ADAM_B2 ** ADAM_STEP

    def body(w_ref, g_ref, m_ref, v_ref, d_ref, nm_ref, nv_ref):
        gv = g_ref[...]
        nm = ADAM_B1 * m_ref[...] + (1.0 - ADAM_B1) * gv
        nv = ADAM_B2 * v_ref[...] + (1.0 - ADAM_B2) * (gv * gv)
        d_ref[...] = -ADAM_LR * ((nm / bc1) / (jnp.sqrt(nv / bc2) + ADAM_EPS) + ADAM_WD * w_ref[...])
        nm_ref[...] = nm
        nv_ref[...] = nv

    t = pl.BlockSpec((tr, C), lambda i: (i, 0))
    o = _sds((R, C), F32)
    return _call(name, body, (R // tr,), [t, t, t, t], [t, t, t], [o, o, o], sem=("parallel",), ride=ride)(w, g, m, v)


def _sum_devices(packs, *, name):
    _, R, C = packs.shape
    tr = _pick(R, 512, SUBLANES)

    def body(p_ref, o_ref):
        acc = p_ref[0]
        for d in range(1, N_DEV):
            acc = acc + p_ref[d]
        o_ref[...] = acc

    return _call(name, body, (R // tr,), [pl.BlockSpec((N_DEV, tr, C), lambda i: (0, i, 0))],
                 pl.BlockSpec((tr, C), lambda i: (i, 0)), _sds((R, C), F32), sem=("parallel",))(packs)


class _Big:
    def __init__(self, kind, R, C):
        self.kind, self.R, self.C = kind, R, C
        self.hr = R // 2
        self.full = (R, N_CHIPS * C) if kind == "col" else (N_CHIPS * R, C)

    def region(self, ref, k, c):
        if self.kind == "col":
            return ref.at[pl.ds(c * self.hr, self.hr), pl.ds(k * self.C, self.C)]
        return ref.at[pl.ds(k * self.R + c * self.hr, self.hr), :]


def _pair_sum(spec, g_full, land, c_arr, *, name):
    hr, C = spec.hr, spec.C
    tr = _pick(hr, 128, 16)
    nr = hr // tr

    def body(c_ref, g_ref, l_ref, o_ref):
        o_ref[0] = (g_ref[...] + l_ref[0]).astype(BF16)

    if spec.kind == "col":
        gspec = pl.BlockSpec((tr, C), lambda k, i, c_ref: (c_ref[0] * nr + i, k))
    else:
        gspec = pl.BlockSpec((tr, C), lambda k, i, c_ref: (k * 2 * nr + c_ref[0] * nr + i, 0))
    lspec = pl.BlockSpec((1, tr, C), lambda k, i, c_ref: (k, i, 0))
    return _call(name, body, (N_CHIPS, nr), [gspec, lspec], lspec, _sds((N_CHIPS, hr, C), BF16), nsp=1,
                 sem=("parallel", "parallel"))(c_arr, g_full, land)


def _chip_sum(spec, chipsum, land, kc_arr, *, name):
    hr, C = spec.hr, spec.C
    tr = _pick(hr, 128, 16)
    nr = hr // tr

    def body(kc_ref, s_ref, l_ref, o_ref):
        acc = s_ref[0].astype(F32)
        for j in range(N_CHIPS - 1):
            acc = acc + l_ref[j].astype(F32)
        o_ref[...] = acc

    return _call(name, body, (nr,),
                 [pl.BlockSpec((1, tr, C), lambda i, kc_ref: (kc_ref[0], i, 0)),
                  pl.BlockSpec((N_CHIPS - 1, tr, C), lambda i, kc_ref: (0, i, 0))],
                 pl.BlockSpec((tr, C), lambda i, kc_ref: (kc_ref[1] * nr + i, 0)), _sds((spec.R, C), F32), nsp=1,
                 sem=("parallel",))(kc_arr, chipsum, land)


def _place():
    x, y, c = lax.axis_index("x"), lax.axis_index("y"), lax.axis_index("c")
    return x, y, c, [(1 - x, y), (x, 1 - y), (1 - x, 1 - y)]


def _remote(src, dst, send_sem, recv_sem, dev):
    return pltpu.make_async_remote_copy(src_ref=src, dst_ref=dst, send_sem=send_sem, recv_sem=recv_sem,
                                        device_id=dev, device_id_type=MESH)


def _placed(block, dev):
    m, n = block.shape
    return lax.dynamic_update_slice(jnp.zeros((N_DEV * m, n), block.dtype), block, (dev * m, 0))


def _all_gather(bufs):
    n = len(bufs)
    per = 7

    def copier(outs, send_sems, recv_sems, w):
        m = outs[w].shape[0] // N_DEV

        def rows(px, py, pc):
            return outs[w].at[pl.ds((4 * px + 2 * py + pc) * m, m), :]

        def copy(k, block, to):
            return _remote(rows(*block), rows(*block), send_sems.at[w * per + k], recv_sems.at[w * per + k], to)

        return copy

    def own(outs, send_sems, recv_sems, w):
        x, y, c, chips = _place()
        copy = copier(outs, send_sems, recv_sems, w)
        return [copy(0, (x, y, c), (x, y, 1 - c))] + [copy(1 + j, (x, y, c), (*chip, c)) for j, chip in enumerate(chips)]

    def start(ins, outs, send_sems, recv_sems):
        for w in range(n):
            for cp in own(outs, send_sems, recv_sems, w):
                cp.start()

    def finish(ins, outs, send_sems, recv_sems):
        x, y, c, chips = _place()
        me, sibling = (x, y, c), (x, y, 1 - c)
        passed = []
        for w in range(n):
            copy = copier(outs, send_sems, recv_sems, w)
            for j, chip in enumerate(chips):
                copy(1 + j, (*chip, c), me).wait_recv()
                cp = copy(4 + j, (*chip, c), sibling)
                cp.start()
                passed.append(cp)
        for w in range(n):
            copy = copier(outs, send_sems, recv_sems, w)
            copy(0, sibling, me).wait_recv()
            for j, chip in enumerate(chips):
                copy(4 + j, (*chip, 1 - c), me).wait_recv()
            for cp in own(outs, send_sems, recv_sems, w):
                cp.wait_send()
        for cp in passed:
            cp.wait_send()

    return _Exchange(bufs, [_sds(a.shape, a.dtype) for a in bufs], per * n, start, finish, in_place=True)


def _cast_into(spec, w, k_arr, *, name):
    R, C = spec.R, spec.C
    tr = _pick(R, 256, 16)
    nr = R // tr

    def body(k_ref, w_ref, o_ref):
        o_ref[...] = w_ref[...].astype(BF16)

    if spec.kind == "col":
        ospec = pl.BlockSpec((tr, C), lambda i, k_ref: (i, k_ref[0]))
    else:
        ospec = pl.BlockSpec((tr, C), lambda i, k_ref: (k_ref[0] * nr + i, 0))
    return _call(name, body, (nr,), [pl.BlockSpec((tr, C), lambda i, k_ref: (i, 0))], ospec,
                 _sds(spec.full, BF16), nsp=1, sem=("parallel",))(k_arr, w)


def _weight_gather(bufs, specs):
    per = 6

    def to_chips(outs, send_sems, recv_sems):
        x, y, c, chips = _place()
        k_me = 2 * x + y
        cps = []
        for w, sp in enumerate(specs):
            mine = sp.region(outs[w], k_me, c)
            cps += [_remote(mine, mine, send_sems.at[w * per + j], recv_sems.at[w * per + j], (*chip, c))
                    for j, chip in enumerate(chips)]
        return cps

    def start(ins, outs, send_sems, recv_sems):
        for cp in to_chips(outs, send_sems, recv_sems):
            cp.start()

    def finish(ins, outs, send_sems, recv_sems):
        x, y, c, chips = _place()
        sibling = (x, y, 1 - c)
        passed = []
        for w, sp in enumerate(specs):
            for j, chip in enumerate(chips):
                got = sp.region(outs[w], 2 * chip[0] + chip[1], c)
                _remote(got, got, send_sems.at[w * per + j], recv_sems.at[w * per + j], (*chip, c)).wait_recv()
                cp = _remote(got, got, send_sems.at[w * per + 3 + j], recv_sems.at[w * per + 3 + j], sibling)
                cp.start()
                passed.append(cp)
        for w, sp in enumerate(specs):
            for j, chip in enumerate(chips):
                got = sp.region(outs[w], 2 * chip[0] + chip[1], 1 - c)
                _remote(got, got, send_sems.at[w * per + 3 + j], recv_sems.at[w * per + 3 + j], sibling).wait_recv()
        for cp in passed + to_chips(outs, send_sems, recv_sems):
            cp.wait_send()

    return _Exchange(bufs, [_sds(sp.full, BF16) for sp in specs], per * len(specs), start, finish, in_place=True)


def _pair_exchange(grads, specs):
    def copies(ins, outs, send_sems, recv_sems):
        x, y, c, _ = _place()
        return [_remote(sp.region(ins[w], k, 1 - c), outs[w].at[k], send_sems.at[w * N_CHIPS + k],
                        recv_sems.at[w * N_CHIPS + k], (x, y, 1 - c))
                for w, sp in enumerate(specs) for k in range(N_CHIPS)]

    def start(ins, outs, send_sems, recv_sems):
        for cp in copies(ins, outs, send_sems, recv_sems):
            cp.start()

    def finish(ins, outs, send_sems, recv_sems):
        for cp in copies(ins, outs, send_sems, recv_sems):
            cp.wait()

    return _Exchange(grads, [_sds((N_CHIPS, sp.hr, sp.C), F32) for sp in specs], N_CHIPS * len(specs), start, finish,
                     in_place=False)


def _chip_exchange(chipsums, specs):
    per = N_CHIPS - 1

    def copies(ins, outs, send_sems, recv_sems):
        x, y, c, chips = _place()
        return [_remote(ins[w].at[2 * chip[0] + chip[1]], outs[w].at[j], send_sems.at[w * per + j],
                        recv_sems.at[w * per + j], (*chip, c))
                for w in range(len(specs)) for j, chip in enumerate(chips)]

    def start(ins, outs, send_sems, recv_sems):
        for cp in copies(ins, outs, send_sems, recv_sems):
            cp.start()

    def finish(ins, outs, send_sems, recv_sems):
        for cp in copies(ins, outs, send_sems, recv_sems):
            cp.wait()

    return _Exchange(chipsums, [_sds((per, sp.hr, sp.C), BF16) for sp in specs], per * len(specs), start, finish,
                     in_place=False)


def _pair_share(bufs, specs):
    def mine(outs, send_sems, recv_sems):
        x, y, c, _ = _place()
        rows = [outs[w].at[pl.ds(c * sp.hr, sp.hr), :] for w, sp in enumerate(specs)]
        return [_remote(r, r, send_sems.at[w], recv_sems.at[w], (x, y, 1 - c)) for w, r in enumerate(rows)]

    def start(ins, outs, send_sems, recv_sems):
        for cp in mine(outs, send_sems, recv_sems):
            cp.start()

    def finish(ins, outs, send_sems, recv_sems):
        x, y, c, _ = _place()
        for w, sp in enumerate(specs):
            theirs = outs[w].at[pl.ds((1 - c) * sp.hr, sp.hr), :]
            _remote(theirs, theirs, send_sems.at[w], recv_sems.at[w], (x, y, 1 - c)).wait_recv()
        for cp in mine(outs, send_sems, recv_sems):
            cp.wait_send()

    return _Exchange(bufs, [_sds((sp.R, sp.C), F32) for sp in specs], len(specs), start, finish, in_place=True)


PACK_ALIGN = SUBLANES * LANES


def _pack(arrs):
    flat = jnp.concatenate([a.reshape(-1) for a in arrs])
    pad = (-flat.shape[0]) % PACK_ALIGN
    return jnp.pad(flat, (0, pad)).reshape(-1, LANES)


def _unpack(packed, shapes):
    flat = packed.reshape(-1)
    out, pos = [], 0
    for s in shapes:
        n = int(np.prod(s))
        out.append(flat[pos:pos + n].reshape(s))
        pos += n
    return out


def kernel(x, c, w_ada, b_ada, norm1, w_in, rnn_conv_w, rnn_conv_b, w_rg_a, b_rg_a, w_rg_i, b_rg_i, rg_lambda, w_o_rnn, w_o_attn, attn_sinks, rel_bias, w_out, norm2, w_up, ffn_conv_w, ffn_conv_b, w_down, norm_f, loss_target, m_w_ada, m_b_ada, m_norm1, m_w_in, m_rnn_conv_w, m_rnn_conv_b, m_w_rg_a, m_b_rg_a, m_w_rg_i, m_b_rg_i, m_rg_lambda, m_w_o_rnn, m_w_o_attn, m_attn_sinks, m_rel_bias, m_w_out, m_norm2, m_w_up, m_ffn_conv_w, m_ffn_conv_b, m_w_down, m_norm_f, v_w_ada, v_b_ada, v_norm1, v_w_in, v_rnn_conv_w, v_rnn_conv_b, v_w_rg_a, v_b_rg_a, v_w_rg_i, v_b_rg_i, v_rg_lambda, v_w_o_rnn, v_w_o_attn, v_attn_sinks, v_rel_bias, v_w_out, v_norm2, v_w_up, v_ffn_conv_w, v_ffn_conv_b, v_w_down, v_norm_f):
    S, D = x.shape[1], x.shape[2]
    d_attn = N_CHIPS * w_o_attn.shape[1]
    d_rnn = N_CHIPS * w_o_rnn.shape[1]
    d_ff = N_CHIPS * w_down.shape[1]
    d_in = N_CHIPS * w_in.shape[2]
    n_heads = attn_sinks.shape[1]
    d_kv = (d_in - d_attn - 2 * d_rnn - 2 * D) // 2
    n_kv = d_kv // HEAD_DIM
    group = n_heads // n_kv
    nbk = rel_bias.shape[0]
    assert d_attn == n_heads * HEAD_DIM and group % 2 == 0 and S % BLOCK == 0

    mx, my, mc = lax.axis_index("x"), lax.axis_index("y"), lax.axis_index("c")
    k_me = 2 * mx + my
    dev = 2 * k_me + mc
    c_arr = jnp.reshape(mc, (1,)).astype(jnp.int32)
    k_arr = jnp.reshape(k_me, (1,)).astype(jnp.int32)
    kc_arr = jnp.stack([k_me, mc]).astype(jnp.int32)

    xs, tgt = x[0], loss_target[0]

    big_names = ["w_in", "w_o_rnn", "w_o_attn", "w_out", "w_up", "w_down"]
    big_w = dict(w_in=w_in[0], w_o_rnn=w_o_rnn[0], w_o_attn=w_o_attn[0], w_out=w_out[0], w_up=w_up[0], w_down=w_down[0])
    big_kind = dict(w_in="col", w_o_rnn="row", w_o_attn="row", w_out="row", w_up="col", w_down="row")
    specs = {k: _Big(big_kind[k], *big_w[k].shape) for k in big_names}
    placed = {k: _cast_into(specs[k], big_w[k], k_arr, name="cast_" + k) for k in big_names}
    later = big_names[1:]
    W = dict(w_in=_weight_gather([placed["w_in"]], [specs["w_in"]]).alone("gather_w_in")[0])
    c_all, cw4, cw3 = _all_gather([_placed(jnp.broadcast_to(c, (SUBLANES, D)), dev),
                                   _placed(jnp.pad(rnn_conv_w[0], ((0, SUBLANES - rnn_conv_w.shape[1]), (0, 0))), dev),
                                   _placed(jnp.pad(ffn_conv_w[0], ((0, SUBLANES - ffn_conv_w.shape[1]), (0, 0))), dev)]
                                  ).alone("gather_cond")
    c_all = c_all.reshape(N_DEV, SUBLANES, D)[:, 0]

    def from_chips(g, taps):
        cs = g.shape[1]
        g = g.reshape(N_CHIPS, 2, SUBLANES, cs)[:, 0, :taps]
        return jnp.transpose(g, (1, 0, 2)).reshape(taps, N_CHIPS * cs)

    conv4_w = from_chips(cw4, rnn_conv_w.shape[1])
    conv3_w = from_chips(cw3, ffn_conv_w.shape[1])

    (silu_c,) = _ew("silu_c", lambda v: (v * _sigmoid(v),), [(c_all, 0)], [F32], N_DEV, D)
    mod_sh = _mm(silu_c, w_ada[0], name="mod", exact=True, bias=lax.dynamic_slice_in_dim(b_ada, k_me * w_ada.shape[2], w_ada.shape[2], 1))
    (mod_g,) = _all_gather([_placed(mod_sh, dev)]).alone("gather_mod")
    mod_all = jnp.transpose(mod_g.reshape(N_CHIPS, 2, N_DEV, -1)[:, 0], (1, 0, 2)).reshape(N_DEV, 6 * D)
    mod = lax.dynamic_slice_in_dim(mod_all, dev, 1, 0)
    shift1, scale1, gate1, shift2, scale2, gate2 = [mod[:, i * D:(i + 1) * D] for i in range(6)]

    o_k, o_v, o_xr = d_attn, d_attn + d_kv, d_attn + 2 * d_kv
    wi = W["w_in"]

    spread = np.zeros((d_kv, n_kv * LANES), np.float32)
    for col in range(d_kv):
        spread[col, (col // HEAD_DIM) * LANES + col % HEAD_DIM] = 1.0
        spread[col, (col // HEAD_DIM) * LANES + HEAD_DIM + col % HEAD_DIM] = 1.0
    w_in_x = jnp.concatenate([wi[:, :o_k],
                              _mm(wi[:, o_k:o_v], jnp.asarray(spread, BF16), name="spread_k", out_dtype=BF16),
                              _mm(wi[:, o_v:o_xr], jnp.asarray(spread, BF16), name="spread_v", out_dtype=BF16),
                              wi[:, o_xr:]], axis=1)
    e_k = d_attn
    e_v = e_k + n_kv * LANES
    e_xr = e_v + n_kv * LANES
    e_gr = e_xr + d_rnn
    e_ga = e_gr + d_rnn
    e_gl = e_ga + D
    d_ext = e_gl + D

    u = _adaln_fwd(xs, norm1, scale1, shift1, name="adaln1")
    proj = _mm(u, w_in_x, name="proj", out_dtype=BF16)
    bias = _bias_table(rel_bias, n_heads, name="bias_table")
    sinks_b = jnp.broadcast_to(attn_sinks.reshape(n_heads, 1, 1), (n_heads, 1, LANES))
    o_attn, gathered = _attn_fwd(proj, bias, sinks_b, n_kv, group, 0, e_k, e_v, name="attn_fwd",
                                 ride=_weight_gather([placed[k] for k in later], [specs[k] for k in later]))
    W.update(zip(later, gathered))
    y_attn = _mm(o_attn, W["w_o_attn"], name="y_attn")
    xc = _conv_fwd(proj, e_xr, d_rnn, conv4_w, rnn_conv_b, name="conv4")
    a_t, inp = _gates_fwd(xc, w_rg_a[0], b_rg_a, w_rg_i[0], b_rg_i, rg_lambda, name="gates")
    h_rnn, h_prev = _scan_fwd(a_t, inp, name="scan")
    (z,) = _ew("rnn_gate", lambda h, g: (h * _gelu(g),), [(h_rnn, 0), (proj, e_gr)], [BF16], S, d_rnn)
    y_rnn = _mm(z, W["w_o_rnn"], name="y_rnn")
    (merged,) = _ew("merge", lambda ya, yr, ga, gl: (_sigmoid(ga) * ya + _sigmoid(gl) * yr,),
                    [(y_attn, 0), (y_rnn, 0), (proj, e_ga), (proj, e_gl)], [BF16], S, D)
    t1 = _mm(merged, W["w_out"], name="t1")
    u2, h1 = _adaln_fwd(xs, norm2, scale2, shift2, name="adaln2", t=t1, gate=gate1)
    up = _mm(u2, W["w_up"], name="up", out_dtype=BF16)
    a2, gelu_g, gelu_dg, conv_v = _geglu_fwd(up, conv3_w, ffn_conv_b, d_ff, name="geglu")
    t2 = _mm(a2, W["w_down"], name="t2")
    dh2, dt2, loss_tile, g_norm_f, d_gate2 = _final(h1, t2, gate2, norm_f.reshape(1, D), tgt, name="final")

    da2 = _mm(dt2, W["w_down"], name="da2", tb=True)
    g_w_down = _mm(a2, dt2, name="g_w_down", ta=True)
    dupg, dupv, g_c3g, g_c3v = _ffn_bwd(up, conv3_w, gelu_g, gelu_dg, conv_v, da2, d_ff, name="ffn_bwd")
    dup = jnp.concatenate([dupg, dupv], axis=1)
    g_conv3 = jnp.concatenate([g_c3g, g_c3v], axis=1)
    du2 = _mm(dup, W["w_up"], name="du2", tb=True)
    g_w_up = _mm(u2, dup, name="g_w_up", ta=True)
    dh1, d_shift2, d_scale2, g_norm2, dt1, d_gate1 = _adaln_bwd(h1, du2, dh2, norm2, scale2, name="adaln2_bwd", t=t1, gate=gate1)

    dmerged = _mm(dt1, W["w_out"], name="dmerged", tb=True)
    g_w_out = _mm(merged, dt1, name="g_w_out", ta=True)

    def merge_bwd(dm, ya, yr, ga, gl):
        sa, sl = _sigmoid(ga), _sigmoid(gl)
        return dm * sa, dm * sl, dm * ya * sa * (1.0 - sa), dm * yr * sl * (1.0 - sl)

    dy_attn, dy_rnn, d_ga, d_gl = _ew("merge_bwd", merge_bwd, [(dmerged, 0), (y_attn, 0), (y_rnn, 0), (proj, e_ga), (proj, e_gl)],
                                      [BF16, BF16, BF16, BF16], S, D)
    do = _mm(dy_attn, W["w_o_attn"], name="do", tb=True, out_dtype=BF16)
    g_w_o_attn = _mm(o_attn, dy_attn, name="g_w_o_attn", ta=True)
    dz = _mm(dy_rnn, W["w_o_rnn"], name="dz", tb=True)
    g_w_o_rnn = _mm(z, dy_rnn, name="g_w_o_rnn", ta=True)

    def rnn_gate_bwd(dzv, h, g):
        ge, dge = _gelu_and_grad(g)
        return dzv * ge, dzv * h * dge

    dh_rnn, d_gr = _ew("rnn_gate_bwd", rnn_gate_bwd, [(dz, 0), (h_rnn, 0), (proj, e_gr)], [F32, BF16], S, d_rnn)
    dacc = _scan_bwd(a_t, dh_rnn, name="scan_bwd")
    big_g = dict(w_o_rnn=g_w_o_rnn, w_o_attn=g_w_o_attn, w_out=g_w_out, w_up=g_w_up, w_down=g_w_down)
    (dxc, g_w_rg_a, g_w_rg_i, g_b_rg_a, g_b_rg_i, g_lam), landed = _gates_bwd(
        xc, dacc, h_prev, w_rg_a[0], b_rg_a, w_rg_i[0], b_rg_i, rg_lambda, name="gates_bwd",
        ride=_pair_exchange([big_g[k] for k in later], [specs[k] for k in later]))
    d_xr, g_conv4 = _conv_bwd(dxc, proj, e_xr, d_rnn, conv4_w, name="conv4_bwd")
    chipsums = {k: _pair_sum(specs[k], big_g[k], l, c_arr, name="pair_sum_" + k) for k, l in zip(later, landed)}
    (dq, dk, dv, dbias, dsink), landed2 = _attn_bwd(
        proj, o_attn, do, bias, sinks_b, n_kv, group, 0, e_k, e_v, name="attn_bwd",
        ride=_chip_exchange([chipsums[k] for k in later], [specs[k] for k in later]))
    landed2 = dict(zip(later, landed2))
    g_rel = _bias_table_bwd(dbias, nbk, name="bias_table_bwd")[:, :n_heads]
    g_sinks = dsink[:, 0, :group].reshape(1, n_heads)
    dproj = jnp.concatenate([dq, dk, dv, d_xr, d_gr, d_ga, d_gl], axis=1)

    early = [loss_tile[0:1, 0:1], g_conv4, g_b_rg_a, g_b_rg_i, g_lam, g_sinks, g_rel, g_norm2, g_conv3, g_norm_f,
             d_gate1, d_shift2, d_scale2, d_gate2]
    early_shapes = [a.shape for a in early]
    rg_rows = g_w_rg_a.shape[0] * LANES
    g_w_in_x, (early_all, rg_a_all, rg_i_all) = _mm(
        u, dproj, name="g_w_in", ta=True,
        ride=_all_gather([_placed(_pack(early), dev), _placed(g_w_rg_a.reshape(rg_rows, LANES), dev),
                          _placed(g_w_rg_i.reshape(rg_rows, LANES), dev)]))

    gather_m = jnp.asarray(spread.T, F32)
    g_w_in = jnp.concatenate([g_w_in_x[:, :e_k],
                              _mm(g_w_in_x[:, e_k:e_v], gather_m, name="gather_k", exact=True),
                              _mm(g_w_in_x[:, e_v:e_xr], gather_m, name="gather_v", exact=True),
                              g_w_in_x[:, e_xr:]], axis=1)
    big_g["w_in"] = g_w_in
    (landed_in,) = _pair_exchange([g_w_in], [specs["w_in"]]).alone("pair_exchange_w_in")
    chipsums["w_in"] = _pair_sum(specs["w_in"], g_w_in, landed_in, c_arr, name="pair_sum_w_in")
    du, (landed2["w_in"],) = _mm(dproj, w_in_x, name="du", tb=True, ride=_chip_exchange([chipsums["w_in"]], [specs["w_in"]]))
    grad_x, d_shift1, d_scale1, g_norm1 = _adaln_bwd(xs, du, dh1, norm1, scale1, name="adaln1_bwd")

    late = [d_shift1, d_scale1, g_norm1]
    late_shapes = [a.shape for a in late]
    (late_all,) = _all_gather([_placed(_pack(late), dev)]).alone("gather_late")
    early_all = early_all.reshape(N_DEV, -1, LANES)
    late_all = late_all.reshape(N_DEV, -1, LANES)
    (loss_s, g_conv4, g_b_rg_a, g_b_rg_i, g_lam, g_sinks, g_rel, g_norm2, g_conv3, g_norm_f,
     s_gate1, s_shift2, s_scale2, s_gate2) = _unpack(_sum_devices(early_all, name="sum_early"), early_shapes)
    s_shift1, s_scale1, g_norm1 = _unpack(_sum_devices(late_all, name="sum_late"), late_shapes)
    g_w_rg_a = _sum_devices(rg_a_all.reshape(N_DEV, rg_rows, LANES), name="sum_w_rg_a")
    g_w_rg_i = _sum_devices(rg_i_all.reshape(N_DEV, rg_rows, LANES), name="sum_w_rg_i")
    loss = loss_s.reshape(())
    g_b_ada = jnp.concatenate([s_shift1, s_scale1, s_gate1, s_shift2, s_scale2, s_gate2], axis=1)
    taps4, taps3 = rnn_conv_w.shape[1], ffn_conv_w.shape[1]
    g_conv4_w, g_conv4_b = g_conv4[:taps4], g_conv4[taps4:taps4 + 1]
    g_conv3_w, g_conv3_b = g_conv3[:taps3], g_conv3[taps3:taps3 + 1]

    def per_device(all_packs, shapes, pick):
        flat = all_packs.reshape(N_DEV, -1)
        offs = np.cumsum([0] + [int(np.prod(s)) for s in shapes])
        return [flat[:, offs[i]:offs[i + 1]] for i in pick]

    dmod_all = jnp.concatenate(per_device(late_all, late_shapes, [0, 1]) + per_device(early_all, early_shapes, [10, 11, 12, 13]),
                               axis=1)
    cs_ada = w_ada.shape[2]
    g_w_ada = _mm(silu_c, lax.dynamic_slice_in_dim(dmod_all, k_me * cs_ada, cs_ada, 1), name="g_w_ada", ta=True, exact=True)
    cs4, cs3 = rnn_conv_w.shape[2], ffn_conv_w.shape[2]
    g_conv4_sh = lax.dynamic_slice_in_dim(g_conv4_w, k_me * cs4, cs4, 1)
    g_conv3_sh = lax.dynamic_slice_in_dim(g_conv3_w, k_me * cs3, cs3, 1)

    halves = [_chip_sum(specs[k], chipsums[k], landed2[k], kc_arr, name="chip_sum_" + k) for k in big_names]
    share = _pair_share(halves, [specs[k] for k in big_names])

    (d_ada, m_ada, v_ada), shared = _adamw(w_ada[0], g_w_ada, m_w_ada[0], v_w_ada[0], name="adamw_w_ada", ride=share)
    shards = dict(zip(big_names, shared))
    grads = dict(w_ada=g_w_ada[None], b_ada=g_b_ada, norm1=g_norm1, w_in=shards["w_in"][None], rnn_conv_w=g_conv4_sh[None],
                 rnn_conv_b=g_conv4_b, w_rg_a=g_w_rg_a[None], b_rg_a=g_b_rg_a, w_rg_i=g_w_rg_i[None], b_rg_i=g_b_rg_i,
                 rg_lambda=g_lam, w_o_rnn=shards["w_o_rnn"][None], w_o_attn=shards["w_o_attn"][None], attn_sinks=g_sinks,
                 rel_bias=g_rel, w_out=shards["w_out"][None], norm2=g_norm2, w_up=shards["w_up"][None],
                 ffn_conv_w=g_conv3_sh[None], ffn_conv_b=g_conv3_b, w_down=shards["w_down"][None], norm_f=g_norm_f.reshape(D))
    weights = dict(w_ada=w_ada, b_ada=b_ada, norm1=norm1, w_in=w_in, rnn_conv_w=rnn_conv_w, rnn_conv_b=rnn_conv_b, w_rg_a=w_rg_a,
                   b_rg_a=b_rg_a, w_rg_i=w_rg_i, b_rg_i=b_rg_i, rg_lambda=rg_lambda, w_o_rnn=w_o_rnn, w_o_attn=w_o_attn,
                   attn_sinks=attn_sinks, rel_bias=rel_bias, w_out=w_out, norm2=norm2, w_up=w_up, ffn_conv_w=ffn_conv_w,
                   ffn_conv_b=ffn_conv_b, w_down=w_down, norm_f=norm_f)
    moms = dict(w_ada=(m_w_ada, v_w_ada), b_ada=(m_b_ada, v_b_ada), norm1=(m_norm1, v_norm1), w_in=(m_w_in, v_w_in),
                rnn_conv_w=(m_rnn_conv_w, v_rnn_conv_w), rnn_conv_b=(m_rnn_conv_b, v_rnn_conv_b), w_rg_a=(m_w_rg_a, v_w_rg_a),
                b_rg_a=(m_b_rg_a, v_b_rg_a), w_rg_i=(m_w_rg_i, v_w_rg_i), b_rg_i=(m_b_rg_i, v_b_rg_i),
                rg_lambda=(m_rg_lambda, v_rg_lambda), w_o_rnn=(m_w_o_rnn, v_w_o_rnn), w_o_attn=(m_w_o_attn, v_w_o_attn),
                attn_sinks=(m_attn_sinks, v_attn_sinks), rel_bias=(m_rel_bias, v_rel_bias), w_out=(m_w_out, v_w_out),
                norm2=(m_norm2, v_norm2), w_up=(m_w_up, v_w_up), ffn_conv_w=(m_ffn_conv_w, v_ffn_conv_w),
                ffn_conv_b=(m_ffn_conv_b, v_ffn_conv_b), w_down=(m_w_down, v_w_down), norm_f=(m_norm_f, v_norm_f))
    names = list(weights)
    grads = {k: grads[k].reshape(weights[k].shape) for k in names}
    large = ["w_ada"] + big_names + ["w_rg_a", "w_rg_i"]
    delta, new_m, new_v = dict(w_ada=d_ada[None]), dict(w_ada=m_ada[None]), dict(w_ada=v_ada[None])
    for k in large[1:]:
        shp = weights[k].shape
        two = lambda a: a.reshape(-1, shp[-1])
        d_, m_, v_ = _adamw(two(weights[k]), two(grads[k]), two(moms[k][0]), two(moms[k][1]), name="adamw_" + k)
        delta[k], new_m[k], new_v[k] = d_.reshape(shp), m_.reshape(shp), v_.reshape(shp)
    rest = [k for k in names if k not in large]
    rest_shapes = [weights[k].shape for k in rest]
    d_, m_, v_ = _adamw(_pack([weights[k] for k in rest]), _pack([grads[k] for k in rest]),
                        _pack([moms[k][0] for k in rest]), _pack([moms[k][1] for k in rest]), name="adamw_small")
    for k, dd, mm_, vv in zip(rest, _unpack(d_, rest_shapes), _unpack(m_, rest_shapes), _unpack(v_, rest_shapes)):
        delta[k], new_m[k], new_v[k] = dd, mm_, vv

    return (loss, grad_x[None], *[grads[k] for k in names], *[delta[k] for k in names],
            *[new_m[k] for k in names], *[new_v[k] for k in names])
```
